```python
import math
import jax
import jax.numpy as jnp
from jax import lax
import numpy as np

D_MODEL = 2048
BATCH = 4
SEQ = 2048
DEPTH = 1
DEC_BATCH = 128
DEC_SEQ = 8
PAST_LEN = 16384
PAGE_SIZE = 128

HG_HEADS = 8
HG_DK = 128
HG_DV = 128
HG_KDIM = HG_HEADS * HG_DK
HG_VDIM = HG_HEADS * HG_DV
HG_CHUNK = 32
SSM_WIDTH = D_MODEL // 2
SSM_GROUP = 16
SSM_GROUPS = SSM_WIDTH // SSM_GROUP
SSM_STATE = 64
N_EXPERTS = 32
TOP_K = 4
D_FF = D_MODEL
SWIGLU_LIMIT = 7.0
SWIGLU_ALPHA = 1.702
MOE_BLOCK = 128
PLE_DIM = 256
EPS = 1e-6
IN_SPLITS = (HG_KDIM, 2 * HG_KDIM, 2 * HG_KDIM + HG_VDIM, 2 * HG_KDIM + 2 * HG_VDIM,
             2 * HG_KDIM + 2 * HG_VDIM + SSM_WIDTH, 2 * HG_KDIM + 2 * HG_VDIM + SSM_WIDTH + D_MODEL)
IN_WIDTH = 2 * HG_KDIM + 2 * HG_VDIM + SSM_WIDTH + 2 * D_MODEL

kernel_name = 'hgrn2_s5_moe_hybrid_step'


def rmsnorm(x, w):
    xf = x.astype(jnp.float32)
    y = xf * lax.rsqrt(jnp.mean(xf * xf, axis=-1, keepdims=True) + EPS)
    return (y * w.astype(jnp.float32)).astype(x.dtype)


def hgrn2_chunked(q, k, v, log_f, s0):
    Bt, L, H, DK = q.shape
    DV = v.shape[-1]
    C = math.gcd(L, HG_CHUNK)
    n = L // C

    def to_chunks(a):
        return jnp.moveaxis(a.astype(jnp.float32).reshape(Bt, n, C, H, a.shape[-1]), 1, 0)

    qc, kc, vc, gc = to_chunks(q), to_chunks(k), to_chunks(v), to_chunks(log_f)
    mask = jnp.tril(jnp.ones((C, C), dtype=bool))

    def step(S, xs):
        qb, kb, vb, gb = xs
        b = jnp.cumsum(gb, axis=1)
        b_last = b[:, -1]
        q_t = qb * jnp.exp(b)
        k_t = kb * jnp.exp(-b)
        att = jnp.where(mask, jnp.einsum('bthk,bshk->bhts', q_t, k_t), 0.0)
        o = jnp.einsum('bhts,bshv->bthv', att, vb) + jnp.einsum('bthk,bhkv->bthv', q_t, S)
        k_end = kb * jnp.exp(b_last[:, None] - b)
        S_new = jnp.exp(b_last)[..., None] * S + jnp.einsum('bshk,bshv->bhkv', k_end, vb)
        return S_new, o

    S_fin, o = lax.scan(step, s0.astype(jnp.float32), (qc, kc, vc, gc))
    o = jnp.moveaxis(o, 0, 1).reshape(Bt, L, H, DV)
    return o, S_fin


def s5_scan(u, x0, a_re, a_im, log_dt, b_re, b_im, c_re, c_im, d_skip):
    Bt, L, _ = u.shape
    f32 = jnp.float32
    uf = u.astype(f32).reshape(Bt, L, SSM_GROUPS, SSM_GROUP)
    A = lax.complex(a_re.astype(f32), a_im.astype(f32))
    dt = jnp.exp(log_dt.astype(f32))[:, None]
    A_bar = jnp.exp(A * dt)
    Bm = lax.complex(b_re.astype(f32), b_im.astype(f32))
    B_bar = ((A_bar - 1.0) / A)[..., None] * Bm
    Bu = jnp.einsum('gpc,blgc->blgp', B_bar, uf.astype(jnp.complex64))
    Bu = Bu.at[:, 0].add(A_bar * x0)
    a = jnp.broadcast_to(A_bar, Bu.shape)

    def combine(e1, e2):
        a1, b1 = e1
        a2, b2 = e2
        return a1 * a2, a2 * b1 + b2

    _, xs = lax.associative_scan(combine, (a, Bu), axis=1)
    Cm = lax.complex(c_re.astype(f32), c_im.astype(f32))
    y = jnp.einsum('gcp,blgp->blgc', Cm, xs).real + d_skip.astype(f32).reshape(SSM_GROUPS, SSM_GROUP) * uf
    return y.reshape(Bt, L, SSM_WIDTH), xs[:, -1]


def token_mixers(xn, s_hg, s_ssm, lb, w_in, hg_gnorm, w_branch_a, a_re, a_im, log_dt,
                 b_re, b_im, c_re, c_im, d_skip, w_glu, w_branch_b, w_out):
    Bt, L, _ = xn.shape
    f32 = jnp.float32
    z = xn @ w_in
    q, f, v, g, u, gate_a, gate_b = jnp.split(z, IN_SPLITS, axis=-1)
    q = jax.nn.silu(q.astype(f32)).reshape(Bt, L, HG_HEADS, HG_DK)
    f_eff = (lb + (1.0 - lb) * jax.nn.sigmoid(f.astype(f32))).reshape(Bt, L, HG_HEADS, HG_DK)
    o, s_hg_new = hgrn2_chunked(q, 1.0 - f_eff, v.reshape(Bt, L, HG_HEADS, HG_DV), jnp.log(f_eff), s_hg)
    o = rmsnorm(o, hg_gnorm) * jax.nn.silu(g.astype(f32).reshape(Bt, L, HG_HEADS, HG_DV))
    y_a = o.reshape(Bt, L, HG_VDIM).astype(xn.dtype) @ w_branch_a
    y_ssm, s_ssm_new = s5_scan(u, s_ssm, a_re, a_im, log_dt, b_re, b_im, c_re, c_im, d_skip)
    zg = jax.nn.gelu(y_ssm)
    y_b = (zg * jax.nn.sigmoid(zg @ w_glu.astype(f32))).astype(xn.dtype) @ w_branch_b
    merged = jax.nn.sigmoid(gate_a) * y_a + jax.nn.sigmoid(gate_b) * y_b
    return merged @ w_out, s_hg_new, s_ssm_new


def moe(xn, w_router, b_router, w_gate_up, b_gate_up, w_down, b_down):
    Bt, L, D = xn.shape
    xt = xn.reshape(-1, D)
    T = xt.shape[0]
    logits = (xt @ w_router + b_router).astype(jnp.float32)
    top_vals, top_idx = lax.top_k(logits, TOP_K)
    top_w = jax.nn.softmax(top_vals, axis=-1)
    M = T * TOP_K
    e_flat = top_idx.reshape(-1)
    w_flat = top_w.reshape(-1)
    tok_flat = jnp.repeat(jnp.arange(T, dtype=jnp.int32), TOP_K)
    order = jnp.argsort(e_flat)
    e_sorted = e_flat[order]
    sizes = jnp.bincount(e_flat, length=N_EXPERTS)
    starts = jnp.cumsum(sizes) - sizes
    padded = (sizes + MOE_BLOCK - 1) // MOE_BLOCK * MOE_BLOCK
    pad_ends = jnp.cumsum(padded)
    pad_starts = pad_ends - padded
    dest = pad_starts[e_sorted] + (jnp.arange(M, dtype=jnp.int32) - starts[e_sorted])
    n_blocks = -(-M // MOE_BLOCK) + N_EXPERTS
    R = n_blocks * MOE_BLOCK
    tok_buf = jnp.full((R,), T, dtype=jnp.int32).at[dest].set(tok_flat[order])
    w_buf = jnp.zeros((R,), jnp.float32).at[dest].set(w_flat[order])
    x_pad = jnp.concatenate([xt, jnp.zeros((1, D), xt.dtype)], axis=0)
    x_buf = x_pad[tok_buf].reshape(n_blocks, MOE_BLOCK, D)
    block_e = jnp.clip(jnp.searchsorted(pad_ends, jnp.arange(n_blocks, dtype=jnp.int32) * MOE_BLOCK, side='right'),
                       0, N_EXPERTS - 1)

    def run_block(args):
        xb, e = args
        gu = (xb @ w_gate_up[e] + b_gate_up[e]).astype(jnp.float32)
        gate = jnp.minimum(gu[:, :D_FF], SWIGLU_LIMIT)
        lin = jnp.clip(gu[:, D_FF:], -SWIGLU_LIMIT, SWIGLU_LIMIT)
        act = (gate * jax.nn.sigmoid(SWIGLU_ALPHA * gate) * (lin + 1.0)).astype(xb.dtype)
        return act @ w_down[e] + b_down[e]

    out_buf = lax.map(run_block, (x_buf, block_e)).reshape(R, D).astype(jnp.float32) * w_buf[:, None]
    y = jnp.zeros((T + 1, D), jnp.float32).at[tok_buf].add(out_buf)[:T]
    return y.reshape(Bt, L, D).astype(xn.dtype)


def run_trunk(x, p, s_hg, s_re, s_im, weights):
    (norm_mix, w_in, hg_lb, hg_gnorm, w_branch_a, ssm_a_re, ssm_a_im, ssm_log_dt,
     ssm_b_re, ssm_b_im, ssm_c_re, ssm_c_im, ssm_d, w_glu, w_branch_b, w_out,
     norm_moe, w_router, b_router, w_gate_up, b_gate_up, w_down, b_down,
     norm_ple, w_ple, w_ple_gate, norm_final) = weights
    f32 = jnp.float32
    lb_all = jnp.cumsum(jax.nn.softmax(hg_lb.astype(f32), axis=0), axis=0)
    h = x
    new_hg, new_re, new_im = [], [], []
    for i in range(DEPTH):
        s_ssm = lax.complex(s_re[i].astype(f32), s_im[i].astype(f32))
        mix, s_hg_i, s_ssm_i = token_mixers(
            rmsnorm(h, norm_mix[i]), s_hg[i], s_ssm, lb_all[i], w_in[i], hg_gnorm[i], w_branch_a[i],
            ssm_a_re[i], ssm_a_im[i], ssm_log_dt[i], ssm_b_re[i], ssm_b_im[i], ssm_c_re[i], ssm_c_im[i],
            ssm_d[i], w_glu[i], w_branch_b[i], w_out[i])
        h = h + mix
        h = h + moe(rmsnorm(h, norm_moe[i]), w_router[i], b_router[i], w_gate_up[i], b_gate_up[i],
                    w_down[i], b_down[i])
        h = h + (p[i].astype(h.dtype) @ w_ple[i]) * jax.nn.sigmoid(rmsnorm(h, norm_ple[i]) @ w_ple_gate[i])
        new_hg.append(s_hg_i)
        new_re.append(s_ssm_i.real)
        new_im.append(s_ssm_i.imag)
    return rmsnorm(h, norm_final), jnp.stack(new_hg), jnp.stack(new_re), jnp.stack(new_im)


def setup_inputs(seed: int = 0) -> dict:
    key = jax.random.key(seed)
    keys = jax.random.split(key, 48)
    counter = [0]

    def nxt():
        k = keys[counter[0]]
        counter[0] += 1
        return k

    def nrm(shape, scale):
        return jax.random.normal(nxt(), shape, jnp.float32) * scale

    def gain(shape):
        return 1.0 + nrm(shape, 0.05)

    inp = {}
    inp['x_prompt'] = nrm((BATCH, SEQ, D_MODEL), 1.0)
    inp['x_sample'] = nrm((DEC_BATCH, DEC_SEQ, D_MODEL), 1.0)
    inp['p_prompt'] = nrm((DEPTH, BATCH, SEQ, PLE_DIM), 1.0)
    inp['p_sample'] = nrm((DEPTH, DEC_BATCH, DEC_SEQ, PLE_DIM), 1.0)
    inp['state_hgrn'] = nrm((DEPTH, DEC_BATCH, HG_HEADS, HG_DK, HG_DV), 0.5)
    inp['state_ssm_re'] = nrm((DEPTH, DEC_BATCH, SSM_GROUPS, SSM_STATE), 0.3)
    inp['state_ssm_im'] = nrm((DEPTH, DEC_BATCH, SSM_GROUPS, SSM_STATE), 0.3)
    inp['norm_mix'] = gain((DEPTH, D_MODEL))
    inp['w_in'] = nrm((DEPTH, D_MODEL, IN_WIDTH), D_MODEL ** -0.5)
    inp['hg_lb'] = nrm((DEPTH + 1, HG_KDIM), 0.1)
    inp['hg_gnorm'] = gain((DEPTH, HG_DV))
    inp['w_branch_a'] = nrm((DEPTH, HG_VDIM, D_MODEL), HG_VDIM ** -0.5)
    n_idx = jnp.arange(SSM_STATE, dtype=jnp.float32)
    inp['ssm_a_re'] = -0.5 + nrm((DEPTH, SSM_GROUPS, SSM_STATE), 0.01)
    inp['ssm_a_im'] = math.pi * n_idx + nrm((DEPTH, SSM_GROUPS, SSM_STATE), 0.01)
    inp['ssm_log_dt'] = jax.random.uniform(nxt(), (DEPTH, SSM_GROUPS), jnp.float32,
                                           math.log(1e-3), math.log(1e-1))
    inp['ssm_b_re'] = nrm((DEPTH, SSM_GROUPS, SSM_STATE, SSM_GROUP), SSM_GROUP ** -0.5)
    inp['ssm_b_im'] = nrm((DEPTH, SSM_GROUPS, SSM_STATE, SSM_GROUP), SSM_GROUP ** -0.5)
    inp['ssm_c_re'] = nrm((DEPTH, SSM_GROUPS, SSM_GROUP, SSM_STATE), SSM_STATE ** -0.5)
    inp['ssm_c_im'] = nrm((DEPTH, SSM_GROUPS, SSM_GROUP, SSM_STATE), SSM_STATE ** -0.5)
    inp['ssm_d'] = nrm((DEPTH, SSM_WIDTH), 1.0)
    inp['w_glu'] = nrm((DEPTH, SSM_WIDTH, SSM_WIDTH), SSM_WIDTH ** -0.5)
    inp['w_branch_b'] = nrm((DEPTH, SSM_WIDTH, D_MODEL), SSM_WIDTH ** -0.5)
    inp['w_out'] = nrm((DEPTH, D_MODEL, D_MODEL), D_MODEL ** -0.5)
    inp['norm_moe'] = gain((DEPTH, D_MODEL))
    inp['w_router'] = nrm((DEPTH, D_MODEL, N_EXPERTS), D_MODEL ** -0.5)
    inp['b_router'] = nrm((DEPTH, N_EXPERTS), 0.01)
    inp['w_gate_up'] = nrm((DEPTH, N_EXPERTS, D_MODEL, 2 * D_FF), D_MODEL ** -0.5)
    inp['b_gate_up'] = nrm((DEPTH, N_EXPERTS, 2 * D_FF), 0.01)
    inp['w_down'] = nrm((DEPTH, N_EXPERTS, D_FF, D_MODEL), D_FF ** -0.5)
    inp['b_down'] = nrm((DEPTH, N_EXPERTS, D_MODEL), 0.01)
    inp['norm_ple'] = gain((DEPTH, D_MODEL))
    inp['w_ple'] = nrm((DEPTH, PLE_DIM, D_MODEL), PLE_DIM ** -0.5)
    inp['w_ple_gate'] = nrm((DEPTH, D_MODEL, D_MODEL), D_MODEL ** -0.5)
    inp['norm_final'] = gain((D_MODEL,))
    return inp


def reference(x_prompt, x_sample, p_prompt, p_sample, state_hgrn, state_ssm_re, state_ssm_im,
              norm_mix, w_in, hg_lb, hg_gnorm, w_branch_a, ssm_a_re, ssm_a_im, ssm_log_dt,
              ssm_b_re, ssm_b_im, ssm_c_re, ssm_c_im, ssm_d, w_glu, w_branch_b, w_out,
              norm_moe, w_router, b_router, w_gate_up, b_gate_up, w_down, b_down,
              norm_ple, w_ple, w_ple_gate, norm_final):
    weights = (norm_mix, w_in, hg_lb, hg_gnorm, w_branch_a, ssm_a_re, ssm_a_im, ssm_log_dt,
               ssm_b_re, ssm_b_im, ssm_c_re, ssm_c_im, ssm_d, w_glu, w_branch_b, w_out,
               norm_moe, w_router, b_router, w_gate_up, b_gate_up, w_down, b_down,
               norm_ple, w_ple, w_ple_gate, norm_final)
    nb = x_prompt.shape[0]
    zero_hg = jnp.zeros((DEPTH, nb, HG_HEADS, HG_DK, HG_DV), jnp.float32)
    zero_ssm = jnp.zeros((DEPTH, nb, SSM_GROUPS, SSM_STATE), jnp.float32)
    y_prompt, hg_p, re_p, im_p = run_trunk(x_prompt, p_prompt, zero_hg, zero_ssm, zero_ssm, weights)
    y_sample, hg_s, re_s, im_s = run_trunk(x_sample, p_sample, state_hgrn, state_ssm_re, state_ssm_im, weights)
    return (y_prompt, y_sample, hg_p, re_p, im_p, hg_s, re_s, im_s)
```

```python
import functools

import jax
import jax.numpy as jnp
from jax import lax
from jax.experimental import pallas as pl
from jax.experimental.pallas import tpu as pltpu

F32 = jnp.float32
BF16 = jnp.bfloat16
EPS = 1e-6

HG_HEADS = 8
HG_DK = 128
HG_DV = 128
HG_CHUNK = 32
SSM_GROUP = 16
SSM_STATE = 64
N_EXPERTS = 32
TOP_K = 4
SWIGLU_LIMIT = 7.0
SWIGLU_ALPHA = 1.702

LANES = 128
VMEM_LIMIT_BYTES = 56 * 1024 * 1024

MOE_TM = 256
MOE_UNIT_TILES = 4
MOE_TF = 256
NEG_BIG = -1e30


def _cparams(sem):
    return pltpu.CompilerParams(dimension_semantics=sem, vmem_limit_bytes=VMEM_LIMIT_BYTES)


def _dot(a, b):
    return jnp.dot(a, b, preferred_element_type=F32)


def _split2(a):
    hi = a.astype(BF16)
    lo = (a - hi.astype(F32)).astype(BF16)
    return hi, lo


def _dot_x3(a, b_split):
    a1, a2 = _split2(a)
    b1, b2 = b_split
    return _dot(a1, b1) + (_dot(a1, b2) + _dot(a2, b1))


def _rms_kernel(x_ref, w_ref, o_ref):
    x = x_ref[...]
    ms = jnp.mean(x * x, axis=-1, keepdims=True)
    o_ref[...] = (x * lax.rsqrt(ms + EPS) * w_ref[...]).astype(o_ref.dtype)


def rmsnorm_rows(x, w, out_dtype, tm=512, name="rmsnorm"):
    T, D = x.shape
    return pl.pallas_call(
        _rms_kernel,
        grid=(T // tm,),
        in_specs=[pl.BlockSpec((tm, D), lambda i: (i, 0)), pl.BlockSpec((1, D), lambda i: (0, 0))],
        out_specs=pl.BlockSpec((tm, D), lambda i: (i, 0)),
        out_shape=jax.ShapeDtypeStruct((T, D), out_dtype),
        compiler_params=_cparams(("arbitrary",)),
        name=name,
    )(x, w.reshape(1, D))


def _proj_kernel(x_ref, w_ref, o_ref, wb_ref):
    @pl.when(pl.program_id(1) == 0)
    def _():
        wb_ref[...] = w_ref[...].astype(BF16)

    o_ref[...] = _dot(x_ref[...], wb_ref[...]).astype(o_ref.dtype)


def project(xn, w, row_blk0, n_row_blks, col_blk0, n_col_blks, tm, tn, name):
    K = xn.shape[1]
    return pl.pallas_call(
        _proj_kernel,
        grid=(n_col_blks, n_row_blks),
        in_specs=[
            pl.BlockSpec((tm, K), lambda j, i: (row_blk0 + i, 0)),
            pl.BlockSpec((K, tn), lambda j, i: (0, col_blk0 + j)),
        ],
        out_specs=pl.BlockSpec((tm, tn), lambda j, i: (i, j)),
        out_shape=jax.ShapeDtypeStruct((n_row_blks * tm, n_col_blks * tn), F32),
        scratch_shapes=[pltpu.VMEM((K, tn), BF16)],
        compiler_params=_cparams(("arbitrary", "arbitrary")),
        name=name,
    )(xn, w)


def _hgrn_kernel(*refs, C, nchunk, nb, tok_step, has_s0):
    if has_s0:
        q_ref, f_ref, v_ref, g_ref, lb_ref, gn_ref, s0_ref, og_ref, so_ref, st_ref = refs
    else:
        q_ref, f_ref, v_ref, g_ref, lb_ref, gn_ref, og_ref, so_ref, st_ref = refs
        s0_ref = None
    H, DK, DV = HG_HEADS, HG_DK, HG_DV
    step = pl.program_id(1)

    @pl.when(step == 0)
    def _():
        for i in range(nb):
            for h in range(H):
                if has_s0:
                    st_ref[i, h] = s0_ref[i, h].T
                else:
                    st_ref[i, h] = jnp.zeros((DV, DK), F32)

    lbv = lb_ref[...]
    gnv = gn_ref[...]
    rr = lax.broadcasted_iota(jnp.int32, (C, C), 0)
    cc = lax.broadcasted_iota(jnp.int32, (C, C), 1)
    causal = rr >= cc
    tri = causal.astype(BF16)

    def chunk(i, r0):
        rows = pl.ds(r0, C)
        zq = q_ref[rows, :]
        zf = f_ref[rows, :]
        zv = v_ref[rows, :]
        zg = g_ref[rows, :]
        q = jax.nn.silu(zq)
        fe = lbv + (1.0 - lbv) * jax.nn.sigmoid(zf)
        k = 1.0 - fe
        gl = jnp.log(fe)
        g1 = gl.astype(BF16)
        r1 = gl - g1.astype(F32)
        g2 = r1.astype(BF16)
        g3 = (r1 - g2.astype(F32)).astype(BF16)
        b = _dot(tri, g1) + (_dot(tri, g2) + _dot(tri, g3))
        blast = b[C - 1:C, :]
        qt = (q * jnp.exp(b)).astype(BF16)
        kt = (k * jnp.exp(-b)).astype(BF16)
        kend = (k * jnp.exp(blast - b)).astype(BF16)
        eblast = jnp.exp(blast)
        vb = zv.astype(BF16)
        gate = jax.nn.silu(zg)
        outs = []
        for h in range(H):
            sl = slice(h * DK, (h + 1) * DK)
            qth, kth, keh, vh = qt[:, sl], kt[:, sl], kend[:, sl], vb[:, sl]
            att = lax.dot_general(qth, kth, (((1,), (1,)), ((), ())), preferred_element_type=F32)
            att = jnp.where(causal, att, 0.0).astype(BF16)
            st = st_ref[i, h]
            o = _dot(att, vh) + lax.dot_general(qth, st.astype(BF16), (((1,), (1,)), ((), ())),
                                                preferred_element_type=F32)
            st_ref[i, h] = st * eblast[:, sl] + lax.dot_general(vh, keh, (((0,), (0,)), ((), ())),
                                                                preferred_element_type=F32)
            ms = jnp.mean(o * o, axis=-1, keepdims=True)
            outs.append(o * lax.rsqrt(ms + EPS) * gnv * gate[:, sl])
        return jnp.concatenate(outs, axis=1)

    if nchunk == 1:
        og_ref[...] = jnp.concatenate([chunk(i, i * tok_step) for i in range(nb)], axis=0).astype(og_ref.dtype)
    else:
        assert nb == 1

        def body(c, carry):
            r0 = pl.multiple_of(c * C, C)
            og_ref[pl.ds(r0, C), :] = chunk(0, r0).astype(og_ref.dtype)
            return carry

        lax.fori_loop(0, nchunk, body, 0)

    @pl.when(step == pl.num_programs(1) - 1)
    def _():
        for i in range(nb):
            for h in range(H):
                so_ref[i, h] = st_ref[i, h].T


def hgrn2(z_hg, lb, gn, s0, *, row0, B, L, C, nb, tok_step, name):
    H, DK, DV = HG_HEADS, HG_DK, HG_DV
    W = H * DK
    steps = L // tok_step
    rows = nb * tok_step
    nchunk = tok_step // C
    blk0 = row0 // rows
    has_s0 = s0 is not None

    def zspec(col):
        return pl.BlockSpec((rows, W), lambda bb, s: (blk0 + bb * steps + s, col))

    in_specs = [zspec(0), zspec(1), zspec(2), zspec(3),
                pl.BlockSpec((1, W), lambda bb, s: (0, 0)),
                pl.BlockSpec((1, DV), lambda bb, s: (0, 0))]
    args = [z_hg, z_hg, z_hg, z_hg, lb.reshape(1, W), gn.reshape(1, DV)]
    if has_s0:
        in_specs.append(pl.BlockSpec((nb, H, DK, DV), lambda bb, s: (bb, 0, 0, 0)))
        args.append(s0)
    kern = functools.partial(_hgrn_kernel, C=C, nchunk=nchunk, nb=nb, tok_step=tok_step, has_s0=has_s0)
    return pl.pallas_call(
        kern,
        grid=(B // nb, steps),
        in_specs=in_specs,
        out_specs=[pl.BlockSpec((rows, W), lambda bb, s: (bb * steps + s, 0)),
                   pl.BlockSpec((nb, H, DK, DV), lambda bb, s: (bb, 0, 0, 0))],
        out_shape=[jax.ShapeDtypeStruct((B * L, W), BF16),
                   jax.ShapeDtypeStruct((B, H, DK, DV), F32)],
        scratch_shapes=[pltpu.VMEM((nb, H, DV, DK), F32)],
        compiler_params=_cparams(("arbitrary", "arbitrary")),
        name=name,
    )(*args)


def _s5_step(ar, ai, xr, xi, br, bi):
    return ar * xr - ai * xi + br, ar * xi + ai * xr + bi


def _s5_prompt_kernel(u_ref, bd_ref, cd_ref, a_ref, d_ref, y_ref, xr_ref, xi_ref, X_ref, st_ref, *, B, TC):
    NC = X_ref.shape[0]
    NH = NC // 2
    half = NH * LANES
    tc = pl.program_id(1)

    @pl.when(tc == 0)
    def _():
        st_ref[...] = jnp.zeros_like(st_ref)

    bd = _split2(bd_ref[...])
    for b in range(B):
        bu = _dot_x3(u_ref[b], bd)
        for c in range(NC):
            X_ref[c, pl.ds(b, TC, stride=B), :] = bu[:, c * LANES:(c + 1) * LANES]

    per = 8 // B

    def body(i, carry):
        r0 = pl.multiple_of(i * 8, 8)
        new = []
        for c in range(NH):
            xr, xi = carry[c]
            ar = a_ref[0:1, c * LANES:(c + 1) * LANES]
            ai = a_ref[1:2, c * LANES:(c + 1) * LANES]
            tr = X_ref[c, pl.ds(r0, 8), :]
            ti = X_ref[NH + c, pl.ds(r0, 8), :]
            outr, outi = [], []
            for j in range(per):
                xr, xi = _s5_step(ar, ai, xr, xi, tr[j * B:(j + 1) * B, :], ti[j * B:(j + 1) * B, :])
                outr.append(xr)
                outi.append(xi)
            X_ref[c, pl.ds(r0, 8), :] = jnp.concatenate(outr, axis=0)
            X_ref[NH + c, pl.ds(r0, 8), :] = jnp.concatenate(outi, axis=0)
            new.append((xr, xi))
        return tuple(new)

    init = tuple((st_ref[0:B, c * LANES:(c + 1) * LANES], st_ref[0:B, half + c * LANES:half + (c + 1) * LANES])
                 for c in range(NH))
    fin = lax.fori_loop(0, TC // per, body, init)
    xr = jnp.concatenate([f[0] for f in fin], axis=1)
    xi = jnp.concatenate([f[1] for f in fin], axis=1)
    st_ref[0:B, :] = jnp.concatenate([xr, xi], axis=1)

    cd = cd_ref[...].astype(BF16)
    dv = d_ref[...]
    for b in range(B):
        xb = jnp.concatenate([X_ref[c, pl.ds(b, TC, stride=B), :] for c in range(NC)], axis=1).astype(BF16)
        y_ref[b] = _dot(xb, cd) + dv * u_ref[b]

    @pl.when(tc == pl.num_programs(1) - 1)
    def _():
        xr_ref[...] = xr
        xi_ref[...] = xi


def s5_prompt(u, bd, cd, ab, dskip, *, TC, name):
    B, L, W = u.shape
    GB = W // LANES
    SW = bd.shape[2]
    half = SW // 2
    kern = functools.partial(_s5_prompt_kernel, B=B, TC=TC)
    return pl.pallas_call(
        kern,
        grid=(GB, L // TC),
        in_specs=[
            pl.BlockSpec((B, TC, LANES), lambda g, t: (0, t, g)),
            pl.BlockSpec((None, LANES, SW), lambda g, t: (g, 0, 0)),
            pl.BlockSpec((None, SW, LANES), lambda g, t: (g, 0, 0)),
            pl.BlockSpec((None, 2, half), lambda g, t: (g, 0, 0)),
            pl.BlockSpec((1, LANES), lambda g, t: (0, g)),
        ],
        out_specs=[
            pl.BlockSpec((B, TC, LANES), lambda g, t: (0, t, g)),
            pl.BlockSpec((B, half), lambda g, t: (0, g)),
            pl.BlockSpec((B, half), lambda g, t: (0, g)),
        ],
        out_shape=[jax.ShapeDtypeStruct((B, L, W), F32),
                   jax.ShapeDtypeStruct((B, GB * half), F32),
                   jax.ShapeDtypeStruct((B, GB * half), F32)],
        scratch_shapes=[pltpu.VMEM((SW // LANES, B * TC, LANES), F32), pltpu.VMEM((8, SW), F32)],
        compiler_params=_cparams(("arbitrary", "arbitrary")),
        name=name,
    )(u, bd, cd, ab, dskip.reshape(1, W))


def _s5_sample_kernel(u_ref, bd_ref, cd_ref, a_ref, d_ref, x0r_ref, x0i_ref, y_ref, xr_ref, xi_ref, X_ref, *, B, L):
    NC = X_ref.shape[0]
    NH = NC // 2
    u = u_ref[...]
    bu = _dot_x3(u, _split2(bd_ref[...]))
    for c in range(NC):
        X_ref[c] = bu[:, c * LANES:(c + 1) * LANES]
    for c in range(NH):
        cols = slice(c * LANES, (c + 1) * LANES)
        ar = a_ref[0:1, cols]
        ai = a_ref[1:2, cols]
        xr = x0r_ref[:, cols]
        xi = x0i_ref[:, cols]
        for t in range(L):
            rows = pl.ds(t, B, stride=L)
            xr, xi = _s5_step(ar, ai, xr, xi, X_ref[c, rows, :], X_ref[NH + c, rows, :])
            X_ref[c, rows, :] = xr
            X_ref[NH + c, rows, :] = xi
        xr_ref[:, cols] = xr
        xi_ref[:, cols] = xi
    xs = jnp.concatenate([X_ref[c] for c in range(NC)], axis=1).astype(BF16)
    y_ref[...] = _dot(xs, cd_ref[...].astype(BF16)) + d_ref[...] * u


def s5_sample(u, bd, cd, ab, dskip, x0r, x0i, *, B, L, name):
    T, W = u.shape
    GB = W // LANES
    SW = bd.shape[2]
    half = SW // 2
    kern = functools.partial(_s5_sample_kernel, B=B, L=L)
    return pl.pallas_call(
        kern,
        grid=(GB,),
        in_specs=[
            pl.BlockSpec((T, LANES), lambda g: (0, g)),
            pl.BlockSpec((None, LANES, SW), lambda g: (g, 0, 0)),
            pl.BlockSpec((None, SW, LANES), lambda g: (g, 0, 0)),
            pl.BlockSpec((None, 2, half), lambda g: (g, 0, 0)),
            pl.BlockSpec((1, LANES), lambda g: (0, g)),
            pl.BlockSpec((B, half), lambda g: (0, g)),
            pl.BlockSpec((B, half), lambda g: (0, g)),
        ],
        out_specs=[
            pl.BlockSpec((T, LANES), lambda g: (0, g)),
            pl.BlockSpec((B, half), lambda g: (0, g)),
            pl.BlockSpec((B, half), lambda g: (0, g)),
        ],
        out_shape=[jax.ShapeDtypeStruct((T, W), F32),
                   jax.ShapeDtypeStruct((B, GB * half), F32),
                   jax.ShapeDtypeStruct((B, GB * half), F32)],
        scratch_shapes=[pltpu.VMEM((SW // LANES, T, LANES), F32)],
        compiler_params=_cparams(("arbitrary",)),
        name=name,
    )(u, bd, cd, ab, dskip.reshape(1, W), x0r, x0i)


def s5_params(a_re, a_im, log_dt, b_re, b_im, c_re, c_im):
    G, P = a_re.shape
    gpb = LANES // SSM_GROUP
    GB = G // gpb
    A = lax.complex(a_re, a_im)
    dt = jnp.exp(log_dt)[:, None]
    A_bar = jnp.exp(A * dt)
    Bm = lax.complex(b_re, b_im)
    B_bar = ((A_bar - 1.0) / A)[..., None] * Bm
    eye = jnp.eye(gpb, dtype=F32)

    def in_map(m):
        m = m.reshape(GB, gpb, P, SSM_GROUP)
        return jnp.einsum('bgpc,gh->bgchp', m, eye).reshape(GB, gpb * SSM_GROUP, gpb * P)

    def out_map(m):
        m = m.reshape(GB, gpb, SSM_GROUP, P)
        return jnp.einsum('bgcp,gh->bgphc', m, eye).reshape(GB, gpb * P, gpb * SSM_GROUP)

    bd = jnp.concatenate([in_map(B_bar.real), in_map(B_bar.imag)], axis=2)
    cd = jnp.concatenate([out_map(c_re), out_map(-c_im)], axis=1)
    ab = jnp.stack([A_bar.real.reshape(GB, gpb * P), A_bar.imag.reshape(GB, gpb * P)], axis=1)
    return bd, cd, ab


def _glu_kernel(y_ref, w_ref, o_ref, wb_ref):
    @pl.when(pl.program_id(0) == 0)
    def _():
        wb_ref[...] = w_ref[...].astype(BF16)

    zg = jax.nn.gelu(y_ref[...])
    o_ref[...] = (zg * jax.nn.sigmoid(_dot(zg.astype(BF16), wb_ref[...]))).astype(o_ref.dtype)


def gelu_glu(y, w, tm=512, name="gelu_glu"):
    T, W = y.shape
    return pl.pallas_call(
        _glu_kernel,
        grid=(T // tm,),
        in_specs=[pl.BlockSpec((tm, W), lambda i: (i, 0)), pl.BlockSpec((W, W), lambda i: (0, 0))],
        out_specs=pl.BlockSpec((tm, W), lambda i: (i, 0)),
        out_shape=jax.ShapeDtypeStruct((T, W), BF16),
        scratch_shapes=[pltpu.VMEM((W, W), BF16)],
        compiler_params=_cparams(("arbitrary",)),
        name=name,
    )(y, w)


def _merge_kernel(oa_ref, ob_ref, ga_ref, gb_ref, wa_ref, wb_ref, o_ref, wab_ref, wbb_ref):
    @pl.when(pl.program_id(1) == 0)
    def _():
        wab_ref[...] = wa_ref[...].astype(BF16)
        wbb_ref[...] = wb_ref[...].astype(BF16)

    ya = _dot(oa_ref[...], wab_ref[...])
    yb = _dot(ob_ref[...], wbb_ref[...])
    o_ref[...] = (jax.nn.sigmoid(ga_ref[...]) * ya + jax.nn.sigmoid(gb_ref[...]) * yb).astype(o_ref.dtype)


def gated_merge(oa, ob, z_gate, wa, wb, tm=1024, tn=1024, name="gated_merge"):
    T, K = oa.shape
    N = wa.shape[1]
    nj = N // tn
    return pl.pallas_call(
        _merge_kernel,
        grid=(nj, T // tm),
        in_specs=[
            pl.BlockSpec((tm, K), lambda j, i: (i, 0)),
            pl.BlockSpec((tm, K), lambda j, i: (i, 0)),
            pl.BlockSpec((tm, tn), lambda j, i: (i, j)),
            pl.BlockSpec((tm, tn), lambda j, i: (i, nj + j)),
            pl.BlockSpec((K, tn), lambda j, i: (0, j)),
            pl.BlockSpec((K, tn), lambda j, i: (0, j)),
        ],
        out_specs=pl.BlockSpec((tm, tn), lambda j, i: (i, j)),
        out_shape=jax.ShapeDtypeStruct((T, N), BF16),
        scratch_shapes=[pltpu.VMEM((K, tn), BF16), pltpu.VMEM((K, tn), BF16)],
        compiler_params=_cparams(("arbitrary", "arbitrary")),
        name=name,
    )(oa, ob, z_gate, z_gate, wa, wb)


def _resid_kernel(x_ref, h_ref, w_ref, o_ref, wb_ref):
    @pl.when(pl.program_id(1) == 0)
    def _():
        wb_ref[...] = w_ref[...].astype(BF16)

    o_ref[...] = h_ref[...] + _dot(x_ref[...], wb_ref[...])


def resid_matmul(x, h, w, tm=1024, tn=1024, name="resid_matmul"):
    T, K = x.shape
    N = w.shape[1]
    return pl.pallas_call(
        _resid_kernel,
        grid=(N // tn, T // tm),
        in_specs=[
            pl.BlockSpec((tm, K), lambda j, i: (i, 0)),
            pl.BlockSpec((tm, tn), lambda j, i: (i, j)),
            pl.BlockSpec((K, tn), lambda j, i: (0, j)),
        ],
        out_specs=pl.BlockSpec((tm, tn), lambda j, i: (i, j)),
        out_shape=jax.ShapeDtypeStruct((T, N), F32),
        scratch_shapes=[pltpu.VMEM((K, tn), BF16)],
        compiler_params=_cparams(("arbitrary", "arbitrary")),
        name=name,
    )(x, h, w)


def _router_kernel(h_ref, nw_ref, wr_ref, br_ref, xn_ref, route_ref, cnt_ref, tri_ref, carry_ref, *, tm):
    i = pl.program_id(0)

    @pl.when(i == 0)
    def _():
        rr = lax.broadcasted_iota(jnp.int32, (tm, tm), 0)
        cc = lax.broadcasted_iota(jnp.int32, (tm, tm), 1)
        tri_ref[...] = (rr > cc).astype(BF16)
        carry_ref[...] = jnp.zeros_like(carry_ref)

    x = h_ref[...]
    ms = jnp.mean(x * x, axis=-1, keepdims=True)
    xn = x * lax.rsqrt(ms + EPS) * nw_ref[...]
    xn_ref[...] = xn
    logits = _dot_x3(xn, _split2(wr_ref[...])) + br_ref[...]
    lane = lax.broadcasted_iota(jnp.int32, (tm, LANES), 1)
    lanef = lane.astype(F32)
    cur = logits
    vals, hots, eids = [], [], []
    for _ in range(TOP_K):
        m = jnp.max(cur, axis=-1, keepdims=True)
        idx = jnp.min(jnp.where(cur == m, lanef, float(LANES)), axis=-1, keepdims=True)
        hot = lanef == idx
        vals.append(m)
        hots.append(hot)
        eids.append(idx)
        cur = jnp.where(hot, -jnp.inf, cur)
    es = [jnp.exp(v - vals[0]) for v in vals]
    den = es[0] + es[1] + es[2] + es[3]
    multi = jnp.zeros((tm, LANES), F32)
    for hot in hots:
        multi = multi + hot.astype(F32)
    base = carry_ref[0:1, :] + _dot(tri_ref[...], multi.astype(BF16))
    route = jnp.zeros((tm, LANES), F32)
    for k in range(TOP_K):
        e_k = eids[k]
        r_k = jnp.sum(jnp.where(hots[k], base, 0.0), axis=-1, keepdims=True)
        w_k = es[k] / den
        route = route + jnp.where(lane == k, e_k, 0.0) + jnp.where(lane == TOP_K + k, r_k, 0.0) \
            + jnp.where(lane == 2 * TOP_K + k, w_k, 0.0)
    route_ref[...] = route
    carry = carry_ref[0:1, :] + jnp.sum(multi, axis=0, keepdims=True)
    carry_ref[...] = jnp.broadcast_to(carry, carry_ref.shape)
    cnt_ref[...] = jnp.broadcast_to(carry, cnt_ref.shape)


def moe_router(h, norm_w, w_router, b_router, tm=512, name="moe_router"):
    T, D = h.shape
    E = w_router.shape[1]
    wr = jnp.pad(w_router, ((0, 0), (0, LANES - E)))
    br = jnp.pad(b_router.reshape(1, E), ((0, 0), (0, LANES - E)), constant_values=NEG_BIG)
    kern = functools.partial(_router_kernel, tm=tm)
    return pl.pallas_call(
        kern,
        grid=(T // tm,),
        in_specs=[
            pl.BlockSpec((tm, D), lambda i: (i, 0)),
            pl.BlockSpec((1, D), lambda i: (0, 0)),
            pl.BlockSpec((D, LANES), lambda i: (0, 0)),
            pl.BlockSpec((1, LANES), lambda i: (0, 0)),
        ],
        out_specs=[
            pl.BlockSpec((tm, D), lambda i: (i, 0)),
            pl.BlockSpec((tm, LANES), lambda i: (i, 0)),
            pl.BlockSpec((8, LANES), lambda i: (0, 0)),
        ],
        out_shape=[jax.ShapeDtypeStruct((T, D), F32),
                   jax.ShapeDtypeStruct((T, LANES), F32),
                   jax.ShapeDtypeStruct((8, LANES), F32)],
        scratch_shapes=[pltpu.VMEM((tm, tm), BF16), pltpu.VMEM((8, LANES), F32)],
        compiler_params=_cparams(("arbitrary",)),
        name=name,
    )(h, norm_w.reshape(1, D), wr, br)


def _dispatch_kernel(dest_ref, x_ref, buf_in_ref, buf_ref, sem, *, tm):
    del buf_in_ref
    i = pl.program_id(0)

    def row_copy(r, k):
        d = dest_ref[(i * tm + r) * TOP_K + k]
        return pltpu.make_async_copy(x_ref.at[pl.ds(r, 1)], buf_ref.at[pl.ds(d, 1)], sem)

    def start(r, c):
        for k in range(TOP_K):
            row_copy(r, k).start()
        return c

    lax.fori_loop(0, tm, start, 0)

    def wait(r, c):
        for k in range(TOP_K):
            row_copy(r, k).wait()
        return c

    lax.fori_loop(0, tm, wait, 0)


def moe_dispatch(xn, dest_flat, n_rows, tm=256, name="moe_dispatch"):
    T, D = xn.shape
    buf0 = jnp.zeros((n_rows, D), F32)
    kern = functools.partial(_dispatch_kernel, tm=tm)
    return pl.pallas_call(
        kern,
        grid_spec=pltpu.PrefetchScalarGridSpec(
            num_scalar_prefetch=1,
            grid=(T // tm,),
            in_specs=[pl.BlockSpec((tm, D), lambda i, d: (i, 0)),
                      pl.BlockSpec(memory_space=pl.ANY)],
            out_specs=pl.BlockSpec(memory_space=pl.ANY),
            scratch_shapes=[pltpu.SemaphoreType.DMA(())],
        ),
        out_shape=jax.ShapeDtypeStruct((n_rows, D), F32),
        input_output_aliases={2: 0},
        compiler_params=_cparams(("arbitrary",)),
        name=name,
    )(dest_flat, xn, buf0)


def _experts_kernel(ue_ref, r0_ref, nt_ref, na_ref, x_hbm, wg_ref, wu_ref, bg_ref, bu_ref, wd_ref, bd_ref,
                    o_hbm, xs_ref, xb_ref, acc_ref, wgb_ref, wub_ref, wdb_ref, sem_in, sem_out, *, TM, UT, NF):
    u = pl.program_id(0)
    f = pl.program_id(1)
    nt = nt_ref[u]
    r0 = r0_ref[u]

    def x_copy(i):
        return pltpu.make_async_copy(x_hbm.at[pl.ds(pl.multiple_of(r0 + i * TM, TM), TM)], xs_ref.at[i], sem_in.at[i])

    def o_copy(i):
        return pltpu.make_async_copy(acc_ref.at[i], o_hbm.at[pl.ds(pl.multiple_of(r0 + i * TM, TM), TM)], sem_out.at[i])

    @pl.when(u < na_ref[0])
    def _():
        @pl.when(f == 0)
        def _():
            for i in range(UT):
                @pl.when(i < nt)
                def _():
                    x_copy(i).start()
            for i in range(UT):
                @pl.when(i < nt)
                def _():
                    x_copy(i).wait()
                    xb_ref[i] = xs_ref[i].astype(BF16)

        wgb_ref[...] = wg_ref[...].astype(BF16)
        wub_ref[...] = wu_ref[...].astype(BF16)
        wdb_ref[...] = wd_ref[...].astype(BF16)
        bg = bg_ref[...]
        bu = bu_ref[...]

        def tile_out(i):
            xt = xb_ref[i]
            g = _dot(xt, wgb_ref[...]) + bg
            up = _dot(xt, wub_ref[...]) + bu
            gate = jnp.minimum(g, SWIGLU_LIMIT)
            lin = jnp.clip(up, -SWIGLU_LIMIT, SWIGLU_LIMIT)
            act = (gate * jax.nn.sigmoid(SWIGLU_ALPHA * gate) * (lin + 1.0)).astype(BF16)
            return _dot(act, wdb_ref[...])

        @pl.when(f == 0)
        def _():
            def body(i, c):
                acc_ref[i] = tile_out(i) + bd_ref[...]
                return c
            lax.fori_loop(0, nt, body, 0)

        @pl.when(f > 0)
        def _():
            def body(i, c):
                acc_ref[i] = acc_ref[i] + tile_out(i)
                return c
            lax.fori_loop(0, nt, body, 0)

        @pl.when(f == NF - 1)
        def _():
            for i in range(UT):
                @pl.when(i < nt)
                def _():
                    o_copy(i).start()
            for i in range(UT):
                @pl.when(i < nt)
                def _():
                    o_copy(i).wait()


def moe_experts(x_buf, plan, w_gate_up, b_gate_up, w_down, b_down, name="moe_experts"):
    R, D = x_buf.shape
    E, _, F2 = w_gate_up.shape
    DF = F2 // 2
    TM, UT, TF = MOE_TM, MOE_UNIT_TILES, MOE_TF
    NF = DF // TF
    ue, r0, nt, na = plan
    U = ue.shape[0]

    def fsel(u, f, na_ref):
        return jnp.where(u < na_ref[0], f, NF - 1)

    kern = functools.partial(_experts_kernel, TM=TM, UT=UT, NF=NF)
    return pl.pallas_call(
        kern,
        grid_spec=pltpu.PrefetchScalarGridSpec(
            num_scalar_prefetch=4,
            grid=(U, NF),
            in_specs=[
                pl.BlockSpec(memory_space=pl.ANY),
                pl.BlockSpec((None, D, TF), lambda u, f, ue, r0, nt, na: (ue[u], 0, fsel(u, f, na))),
                pl.BlockSpec((None, D, TF), lambda u, f, ue, r0, nt, na: (ue[u], 0, NF + fsel(u, f, na))),
                pl.BlockSpec((None, 1, TF), lambda u, f, ue, r0, nt, na: (ue[u], 0, fsel(u, f, na))),
                pl.BlockSpec((None, 1, TF), lambda u, f, ue, r0, nt, na: (ue[u], 0, NF + fsel(u, f, na))),
                pl.BlockSpec((None, TF, D), lambda u, f, ue, r0, nt, na: (ue[u], fsel(u, f, na), 0)),
                pl.BlockSpec((None, 1, D), lambda u, f, ue, r0, nt, na: (ue[u], 0, 0)),
            ],
            out_specs=pl.BlockSpec(memory_space=pl.ANY),
            scratch_shapes=[
                pltpu.VMEM((UT, TM, D), F32),
                pltpu.VMEM((UT, TM, D), BF16),
                pltpu.VMEM((UT, TM, D), F32),
                pltpu.VMEM((D, TF), BF16),
                pltpu.VMEM((D, TF), BF16),
                pltpu.VMEM((TF, D), BF16),
                pltpu.SemaphoreType.DMA((UT,)),
                pltpu.SemaphoreType.DMA((UT,)),
            ],
        ),
        out_shape=jax.ShapeDtypeStruct((R, D), F32),
        input_output_aliases={4: 0},
        compiler_params=_cparams(("arbitrary", "arbitrary")),
        name=name,
    )(ue, r0, nt, na, x_buf, w_gate_up, w_gate_up, b_gate_up.reshape(E, 1, F2), b_gate_up.reshape(E, 1, F2),
      w_down, b_down.reshape(E, 1, D))


def moe_plan(counts, n_tokens):
    TM, UT = MOE_TM, MOE_UNIT_TILES
    E = counts.shape[0]
    nt = (counts + TM - 1) // TM
    padded = nt * TM
    pad_start = jnp.cumsum(padded) - padded
    nu = (nt + UT - 1) // UT
    cu = jnp.cumsum(nu)
    n_units = cu[-1]
    U = E + (n_tokens * TOP_K) // (TM * UT)
    u = jnp.arange(U, dtype=jnp.int32)
    uc = jnp.minimum(u, n_units - 1)
    ue = jnp.clip(jnp.searchsorted(cu, uc, side='right'), 0, E - 1).astype(jnp.int32)
    j = uc - (cu[ue] - nu[ue])
    r0 = (pad_start[ue] + j * (UT * TM)).astype(jnp.int32)
    ntl = jnp.where(u < n_units, jnp.clip(nt[ue] - j * UT, 0, UT), 0).astype(jnp.int32)
    return pad_start, (ue, r0, ntl, n_units.reshape(1).astype(jnp.int32))


def _combine_kernel(dest_ref, h_ref, route_ref, nw_ref, o_hbm, h2_ref, xn_ref, g_ref, sem, *, tm):
    i = pl.program_id(0)

    def row_copy(r, k):
        d = dest_ref[(i * tm + r) * TOP_K + k]
        return pltpu.make_async_copy(o_hbm.at[pl.ds(d, 1)], g_ref.at[k, pl.ds(r, 1)], sem)

    def start(r, c):
        for k in range(TOP_K):
            row_copy(r, k).start()
        return c

    lax.fori_loop(0, tm, start, 0)

    def wait(r, c):
        for k in range(TOP_K):
            row_copy(r, k).wait()
        return c

    lax.fori_loop(0, tm, wait, 0)

    route = route_ref[...]
    y = g_ref[0] * route[:, 2 * TOP_K:2 * TOP_K + 1]
    for k in range(1, TOP_K):
        y = y + g_ref[k] * route[:, 2 * TOP_K + k:2 * TOP_K + k + 1]
    h2 = h_ref[...] + y
    h2_ref[...] = h2
    ms = jnp.mean(h2 * h2, axis=-1, keepdims=True)
    xn_ref[...] = (h2 * lax.rsqrt(ms + EPS) * nw_ref[...]).astype(xn_ref.dtype)


def moe_combine(h, route, dest_flat, out_buf, norm_w, tm=256, name="moe_combine"):
    T, D = h.shape
    kern = functools.partial(_combine_kernel, tm=tm)
    return pl.pallas_call(
        kern,
        grid_spec=pltpu.PrefetchScalarGridSpec(
            num_scalar_prefetch=1,
            grid=(T // tm,),
            in_specs=[pl.BlockSpec((tm, D), lambda i, d: (i, 0)),
                      pl.BlockSpec((tm, LANES), lambda i, d: (i, 0)),
                      pl.BlockSpec((1, D), lambda i, d: (0, 0)),
                      pl.BlockSpec(memory_space=pl.ANY)],
            out_specs=[pl.BlockSpec((tm, D), lambda i, d: (i, 0)),
                       pl.BlockSpec((tm, D), lambda i, d: (i, 0))],
            scratch_shapes=[pltpu.VMEM((TOP_K, tm, D), F32), pltpu.SemaphoreType.DMA(())],
        ),
        out_shape=[jax.ShapeDtypeStruct((T, D), F32), jax.ShapeDtypeStruct((T, D), BF16)],
        compiler_params=_cparams(("arbitrary",)),
        name=name,
    )(dest_flat, h, route, norm_w.reshape(1, D), out_buf)


def _ple_kernel(xn_ref, p_ref, h_ref, wg_ref, wp_ref, o_ref, wgb_ref, wpb_ref):
    @pl.when(pl.program_id(1) == 0)
    def _():
        wgb_ref[...] = wg_ref[...].astype(BF16)
        wpb_ref[...] = wp_ref[...].astype(BF16)

    gate = jax.nn.sigmoid(_dot(xn_ref[...], wgb_ref[...]))
    o_ref[...] = h_ref[...] + _dot(p_ref[...].astype(BF16), wpb_ref[...]) * gate


def ple(xn, p, h, w_gate, w_ple, tm=1024, tn=512, name="ple"):
    T, D = h.shape
    P = p.shape[1]
    return pl.pallas_call(
        _ple_kernel,
        grid=(D // tn, T // tm),
        in_specs=[
            pl.BlockSpec((tm, D), lambda j, i: (i, 0)),
            pl.BlockSpec((tm, P), lambda j, i: (i, 0)),
            pl.BlockSpec((tm, tn), lambda j, i: (i, j)),
            pl.BlockSpec((D, tn), lambda j, i: (0, j)),
            pl.BlockSpec((P, tn), lambda j, i: (0, j)),
        ],
        out_specs=pl.BlockSpec((tm, tn), lambda j, i: (i, j)),
        out_shape=jax.ShapeDtypeStruct((T, D), F32),
        scratch_shapes=[pltpu.VMEM((D, tn), BF16), pltpu.VMEM((P, tn), BF16)],
        compiler_params=_cparams(("arbitrary", "arbitrary")),
        name=name,
    )(xn, p, h, w_gate, w_ple)


def kernel(x_prompt, x_sample, p_prompt, p_sample, state_hgrn, state_ssm_re, state_ssm_im, norm_mix, w_in, hg_lb, hg_gnorm, w_branch_a, ssm_a_re, ssm_a_im, ssm_log_dt, ssm_b_re, ssm_b_im, ssm_c_re, ssm_c_im, ssm_d, w_glu, w_branch_b, w_out, norm_moe, w_router, b_router, w_gate_up, b_gate_up, w_down, b_down, norm_ple, w_ple, w_ple_gate, norm_final):
    BP, LP, D = x_prompt.shape
    BS, LS, _ = x_sample.shape
    depth = w_in.shape[0]
    assert depth == 1
    TP, TS = BP * LP, BS * LS
    T = TP + TS
    G, P = ssm_a_re.shape[1:]
    KD = HG_HEADS * HG_DK
    W = SSM_GROUP * G

    h0 = jnp.concatenate([x_prompt.reshape(TP, D), x_sample.reshape(TS, D)], axis=0)
    p_all = jnp.concatenate([p_prompt[0].reshape(TP, -1), p_sample[0].reshape(TS, -1)], axis=0)

    xn = rmsnorm_rows(h0, norm_mix[0], BF16, name="norm_mix")
    tm = 1024
    w_in0 = w_in[0]
    z_hg = project(xn, w_in0, 0, T // tm, 0, 4, tm, 1024, "proj_hgrn")
    u_p = project(xn, w_in0, 0, TP // tm, 4, 1, tm, 1024, "proj_u_prompt")
    u_s = project(xn, w_in0, TP // tm, TS // tm, 4, 1, tm, 1024, "proj_u_sample")
    z_gate = project(xn, w_in0, 0, T // tm, 5, 4, tm, 1024, "proj_gates")

    lb = jax.nn.softmax(hg_lb.astype(F32), axis=0)[0]
    og_p, hg_p = hgrn2(z_hg, lb, hg_gnorm[0], None, row0=0, B=BP, L=LP, C=HG_CHUNK, nb=1, tok_step=256,
                       name="hgrn_prompt")
    og_s, hg_s = hgrn2(z_hg, lb, hg_gnorm[0], state_hgrn[0], row0=TP, B=BS, L=LS, C=LS, nb=4, tok_step=LS,
                       name="hgrn_sample")

    bd, cd, ab = s5_params(ssm_a_re[0], ssm_a_im[0], ssm_log_dt[0], ssm_b_re[0], ssm_b_im[0],
                           ssm_c_re[0], ssm_c_im[0])
    y_p, re_p, im_p = s5_prompt(u_p.reshape(BP, LP, W), bd, cd, ab, ssm_d[0], TC=256, name="s5_prompt")
    y_s, re_s, im_s = s5_sample(u_s, bd, cd, ab, ssm_d[0], state_ssm_re[0].reshape(BS, G * P),
                                state_ssm_im[0].reshape(BS, G * P), B=BS, L=LS, name="s5_sample")

    og = jnp.concatenate([og_p, og_s], axis=0)
    y_ssm = jnp.concatenate([y_p.reshape(TP, W), y_s], axis=0)
    glu = gelu_glu(y_ssm, w_glu[0])
    merged = gated_merge(og, glu, z_gate, w_branch_a[0], w_branch_b[0])
    h1 = resid_matmul(merged, h0, w_out[0], name="out_proj")

    xn2, route, cnt = moe_router(h1, norm_moe[0], w_router[0], b_router[0])
    e_idx = route[:, 0:TOP_K].astype(jnp.int32)
    rank = route[:, TOP_K:2 * TOP_K].astype(jnp.int32)
    counts = cnt[0, :N_EXPERTS].astype(jnp.int32)
    pad_start, plan = moe_plan(counts, T)
    dest = (pad_start[e_idx] + rank).astype(jnp.int32).reshape(T * TOP_K)
    n_rows = T * TOP_K + N_EXPERTS * MOE_TM
    x_buf = moe_dispatch(xn2, dest, n_rows)
    out_buf = moe_experts(x_buf, plan, w_gate_up[0], b_gate_up[0], w_down[0], b_down[0])
    h2, xn3 = moe_combine(h1, route, dest, out_buf, norm_ple[0])

    h3 = ple(xn3, p_all, h2, w_ple_gate[0], w_ple[0])
    y = rmsnorm_rows(h3, norm_final, F32, name="norm_final")

    y_prompt = y[:TP].reshape(BP, LP, D)
    y_sample = y[TP:].reshape(BS, LS, D)
    hs = (1, BP, HG_HEADS, HG_DK, HG_DV)
    return (y_prompt, y_sample,
            hg_p.reshape(hs), re_p.reshape(1, BP, G, P), im_p.reshape(1, BP, G, P),
            hg_s.reshape(1, BS, HG_HEADS, HG_DK, HG_DV), re_s.reshape(1, BS, G, P), im_s.reshape(1, BS, G, P))
```

```python
import functools

import jax
import jax.numpy as jnp
from jax import lax
from jax.experimental import pallas as pl
from jax.experimental.pallas import tpu as pltpu

F32 = jnp.float32
BF16 = jnp.bfloat16
EPS = 1e-6

HG_HEADS = 8
HG_DK = 128
HG_DV = 128
HG_CHUNK = 32
SSM_GROUP = 16
SSM_STATE = 64
N_EXPERTS = 32
TOP_K = 4
SWIGLU_LIMIT = 7.0
SWIGLU_ALPHA = 1.702

LANES = 128
VMEM_LIMIT_BYTES = 56 * 1024 * 1024

MOE_TM = 256
MOE_UNIT_TILES = 8
MOE_TF = 256
NEG_BIG = -1e30


def _cparams(sem):
    return pltpu.CompilerParams(dimension_semantics=sem, vmem_limit_bytes=VMEM_LIMIT_BYTES)


def _dot(a, b):
    return jnp.dot(a, b, preferred_element_type=F32)


def _split2(a):
    hi = a.astype(BF16)
    lo = (a - hi.astype(F32)).astype(BF16)
    return hi, lo


def _dot_x3(a, b_split):
    a1, a2 = _split2(a)
    b1, b2 = b_split
    return _dot(a1, b1) + (_dot(a1, b2) + _dot(a2, b1))


def _cat_specs(shape, nfirst, row_axis=0, col_axis=None):
    def col(ids):
        return 0 if col_axis is None else ids[col_axis]

    first = pl.BlockSpec(shape, lambda *ids: (jnp.minimum(ids[row_axis], nfirst - 1), col(ids)))
    second = pl.BlockSpec(shape, lambda *ids: (jnp.maximum(ids[row_axis] - nfirst, 0), col(ids)))
    return first, second


def _pick(i, nfirst, a_ref, b_ref):
    return jnp.where(i < nfirst, a_ref[...], b_ref[...])


def _rms(x, w):
    ms = jnp.mean(x * x, axis=-1, keepdims=True)
    return x * lax.rsqrt(ms + EPS) * w


def _rms_in2_kernel(a_ref, b_ref, w_ref, o_ref, *, nfirst):
    x = _pick(pl.program_id(0), nfirst, a_ref, b_ref)
    o_ref[...] = _rms(x, w_ref[...]).astype(o_ref.dtype)


def rmsnorm_cat(xa, xb, w, out_dtype, tm=512, name="rmsnorm"):
    (Ta, D), Tb = xa.shape, xb.shape[0]
    na = Ta // tm
    return pl.pallas_call(
        functools.partial(_rms_in2_kernel, nfirst=na),
        grid=((Ta + Tb) // tm,),
        in_specs=[*_cat_specs((tm, D), na), pl.BlockSpec((1, D), lambda i: (0, 0))],
        out_specs=pl.BlockSpec((tm, D), lambda i: (i, 0)),
        out_shape=jax.ShapeDtypeStruct((Ta + Tb, D), out_dtype),
        compiler_params=_cparams(("arbitrary",)),
        name=name,
    )(xa, xb, w.reshape(1, D))


def _rms_out2_kernel(x_ref, w_ref, oa_ref, ob_ref, *, nfirst):
    i = pl.program_id(0)
    y = _rms(x_ref[...], w_ref[...])

    @pl.when(i < nfirst)
    def _():
        oa_ref[...] = y

    @pl.when(i >= nfirst)
    def _():
        ob_ref[...] = y


def rmsnorm_split(x, w, Ta, tm=512, name="rmsnorm_split"):
    T, D = x.shape
    na = Ta // tm
    return pl.pallas_call(
        functools.partial(_rms_out2_kernel, nfirst=na),
        grid=(T // tm,),
        in_specs=[pl.BlockSpec((tm, D), lambda i: (i, 0)), pl.BlockSpec((1, D), lambda i: (0, 0))],
        out_specs=list(_cat_specs((tm, D), na)),
        out_shape=[jax.ShapeDtypeStruct((Ta, D), F32), jax.ShapeDtypeStruct((T - Ta, D), F32)],
        compiler_params=_cparams(("arbitrary",)),
        name=name,
    )(x, w.reshape(1, D))


def _proj_kernel(x_ref, w_ref, o_ref, wb_ref):
    @pl.when(pl.program_id(1) == 0)
    def _():
        wb_ref[...] = w_ref[...].astype(BF16)

    o_ref[...] = _dot(x_ref[...], wb_ref[...]).astype(o_ref.dtype)


def project(xn, w, row_blk0, n_row_blks, col_blk0, n_col_blks, tm, tn, name):
    K = xn.shape[1]
    return pl.pallas_call(
        _proj_kernel,
        grid=(n_col_blks, n_row_blks),
        in_specs=[
            pl.BlockSpec((tm, K), lambda j, i: (row_blk0 + i, 0)),
            pl.BlockSpec((K, tn), lambda j, i: (0, col_blk0 + j)),
        ],
        out_specs=pl.BlockSpec((tm, tn), lambda j, i: (i, j)),
        out_shape=jax.ShapeDtypeStruct((n_row_blks * tm, n_col_blks * tn), F32),
        scratch_shapes=[pltpu.VMEM((K, tn), BF16)],
        compiler_params=_cparams(("arbitrary", "arbitrary")),
        name=name,
    )(xn, w)


def _hgrn_kernel(*refs, C, nchunk, nb, tok_step, has_s0):
    if has_s0:
        q_ref, f_ref, v_ref, g_ref, lb_ref, gn_ref, s0_ref, og_ref, so_ref, st_ref = refs
    else:
        q_ref, f_ref, v_ref, g_ref, lb_ref, gn_ref, og_ref, so_ref, st_ref = refs
        s0_ref = None
    H, DK, DV = HG_HEADS, HG_DK, HG_DV
    step = pl.program_id(1)

    @pl.when(step == 0)
    def _():
        for i in range(nb):
            for h in range(H):
                if has_s0:
                    st_ref[i, h] = s0_ref[i, h].T
                else:
                    st_ref[i, h] = jnp.zeros((DV, DK), F32)

    lbv = lb_ref[...]
    gnv = gn_ref[...]
    rr = lax.broadcasted_iota(jnp.int32, (C, C), 0)
    cc = lax.broadcasted_iota(jnp.int32, (C, C), 1)
    causal = rr >= cc
    tri = causal.astype(BF16)

    def chunk(i, r0):
        rows = pl.ds(r0, C)
        zq = q_ref[rows, :]
        zf = f_ref[rows, :]
        zv = v_ref[rows, :]
        zg = g_ref[rows, :]
        q = jax.nn.silu(zq)
        fe = lbv + (1.0 - lbv) * jax.nn.sigmoid(zf)
        k = 1.0 - fe
        gl = jnp.log(fe)
        g1 = gl.astype(BF16)
        r1 = gl - g1.astype(F32)
        g2 = r1.astype(BF16)
        g3 = (r1 - g2.astype(F32)).astype(BF16)
        b = _dot(tri, g1) + (_dot(tri, g2) + _dot(tri, g3))
        blast = b[C - 1:C, :]
        qt = (q * jnp.exp(b)).astype(BF16)
        kt = (k * jnp.exp(-b)).astype(BF16)
        kend = (k * jnp.exp(blast - b)).astype(BF16)
        eblast = jnp.exp(blast)
        vb = zv.astype(BF16)
        gate = jax.nn.silu(zg)
        outs = []
        for h in range(H):
            sl = slice(h * DK, (h + 1) * DK)
            qth, kth, keh, vh = qt[:, sl], kt[:, sl], kend[:, sl], vb[:, sl]
            att = lax.dot_general(qth, kth, (((1,), (1,)), ((), ())), preferred_element_type=F32)
            att = jnp.where(causal, att, 0.0).astype(BF16)
            st = st_ref[i, h]
            o = _dot(att, vh) + lax.dot_general(qth, st.astype(BF16), (((1,), (1,)), ((), ())),
                                                preferred_element_type=F32)
            st_ref[i, h] = st * eblast[:, sl] + lax.dot_general(vh, keh, (((0,), (0,)), ((), ())),
                                                                preferred_element_type=F32)
            ms = jnp.mean(o * o, axis=-1, keepdims=True)
            outs.append(o * lax.rsqrt(ms + EPS) * gnv * gate[:, sl])
        return jnp.concatenate(outs, axis=1)

    if nchunk == 1:
        og_ref[...] = jnp.concatenate([chunk(i, i * tok_step) for i in range(nb)], axis=0).astype(og_ref.dtype)
    else:
        assert nb == 1

        def body(c, carry):
            r0 = pl.multiple_of(c * C, C)
            og_ref[pl.ds(r0, C), :] = chunk(0, r0).astype(og_ref.dtype)
            return carry

        lax.fori_loop(0, nchunk, body, 0)

    @pl.when(step == pl.num_programs(1) - 1)
    def _():
        for i in range(nb):
            for h in range(H):
                so_ref[i, h] = st_ref[i, h].T


def hgrn2(z_hg, lb, gn, s0, *, row0, B, L, C, nb, tok_step, name):
    H, DK, DV = HG_HEADS, HG_DK, HG_DV
    W = H * DK
    steps = L // tok_step
    rows = nb * tok_step
    nchunk = tok_step // C
    blk0 = row0 // rows
    has_s0 = s0 is not None

    def zspec(col):
        return pl.BlockSpec((rows, W), lambda bb, s: (blk0 + bb * steps + s, col))

    in_specs = [zspec(0), zspec(1), zspec(2), zspec(3),
                pl.BlockSpec((1, W), lambda bb, s: (0, 0)),
                pl.BlockSpec((1, DV), lambda bb, s: (0, 0))]
    args = [z_hg, z_hg, z_hg, z_hg, lb.reshape(1, W), gn.reshape(1, DV)]
    if has_s0:
        in_specs.append(pl.BlockSpec((nb, H, DK, DV), lambda bb, s: (bb, 0, 0, 0)))
        args.append(s0)
    kern = functools.partial(_hgrn_kernel, C=C, nchunk=nchunk, nb=nb, tok_step=tok_step, has_s0=has_s0)
    return pl.pallas_call(
        kern,
        grid=(B // nb, steps),
        in_specs=in_specs,
        out_specs=[pl.BlockSpec((rows, W), lambda bb, s: (bb * steps + s, 0)),
                   pl.BlockSpec((nb, H, DK, DV), lambda bb, s: (bb, 0, 0, 0))],
        out_shape=[jax.ShapeDtypeStruct((B * L, W), BF16),
                   jax.ShapeDtypeStruct((B, H, DK, DV), F32)],
        scratch_shapes=[pltpu.VMEM((nb, H, DV, DK), F32)],
        compiler_params=_cparams(("arbitrary", "arbitrary")),
        name=name,
    )(*args)


def _s5_step(ar, ai, xr, xi, br, bi):
    return ar * xr - ai * xi + br, ar * xi + ai * xr + bi


def _s5_prompt_kernel(u_ref, bd_ref, cd_ref, a_ref, d_ref, y_ref, xr_ref, xi_ref, X_ref, st_ref, *, B, TC):
    NC = X_ref.shape[0]
    NH = NC // 2
    half = NH * LANES
    tc = pl.program_id(1)

    @pl.when(tc == 0)
    def _():
        st_ref[...] = jnp.zeros_like(st_ref)

    bd = _split2(bd_ref[...])
    for b in range(B):
        bu = _dot_x3(u_ref[b], bd)
        for c in range(NC):
            X_ref[c, pl.ds(b, TC, stride=B), :] = bu[:, c * LANES:(c + 1) * LANES]

    per = 8 // B

    def body(i, carry):
        r0 = pl.multiple_of(i * 8, 8)
        new = []
        for c in range(NH):
            xr, xi = carry[c]
            ar = a_ref[0:1, c * LANES:(c + 1) * LANES]
            ai = a_ref[1:2, c * LANES:(c + 1) * LANES]
            tr = X_ref[c, pl.ds(r0, 8), :]
            ti = X_ref[NH + c, pl.ds(r0, 8), :]
            outr, outi = [], []
            for j in range(per):
                xr, xi = _s5_step(ar, ai, xr, xi, tr[j * B:(j + 1) * B, :], ti[j * B:(j + 1) * B, :])
                outr.append(xr)
                outi.append(xi)
            X_ref[c, pl.ds(r0, 8), :] = jnp.concatenate(outr, axis=0)
            X_ref[NH + c, pl.ds(r0, 8), :] = jnp.concatenate(outi, axis=0)
            new.append((xr, xi))
        return tuple(new)

    init = tuple((st_ref[0:B, c * LANES:(c + 1) * LANES], st_ref[0:B, half + c * LANES:half + (c + 1) * LANES])
                 for c in range(NH))
    fin = lax.fori_loop(0, TC // per, body, init)
    xr = jnp.concatenate([f[0] for f in fin], axis=1)
    xi = jnp.concatenate([f[1] for f in fin], axis=1)
    st_ref[0:B, :] = jnp.concatenate([xr, xi], axis=1)

    cd = cd_ref[...].astype(BF16)
    dv = d_ref[...]
    for b in range(B):
        xb = jnp.concatenate([X_ref[c, pl.ds(b, TC, stride=B), :] for c in range(NC)], axis=1).astype(BF16)
        y_ref[b] = _dot(xb, cd) + dv * u_ref[b]

    @pl.when(tc == pl.num_programs(1) - 1)
    def _():
        xr_ref[...] = xr
        xi_ref[...] = xi


def s5_prompt(u, bd, cd, ab, dskip, *, TC, name):
    B, L, W = u.shape
    GB = W // LANES
    SW = bd.shape[2]
    half = SW // 2
    kern = functools.partial(_s5_prompt_kernel, B=B, TC=TC)
    return pl.pallas_call(
        kern,
        grid=(GB, L // TC),
        in_specs=[
            pl.BlockSpec((B, TC, LANES), lambda g, t: (0, t, g)),
            pl.BlockSpec((None, LANES, SW), lambda g, t: (g, 0, 0)),
            pl.BlockSpec((None, SW, LANES), lambda g, t: (g, 0, 0)),
            pl.BlockSpec((None, 2, half), lambda g, t: (g, 0, 0)),
            pl.BlockSpec((1, LANES), lambda g, t: (0, g)),
        ],
        out_specs=[
            pl.BlockSpec((B, TC, LANES), lambda g, t: (0, t, g)),
            pl.BlockSpec((B, half), lambda g, t: (0, g)),
            pl.BlockSpec((B, half), lambda g, t: (0, g)),
        ],
        out_shape=[jax.ShapeDtypeStruct((B, L, W), F32),
                   jax.ShapeDtypeStruct((B, GB * half), F32),
                   jax.ShapeDtypeStruct((B, GB * half), F32)],
        scratch_shapes=[pltpu.VMEM((SW // LANES, B * TC, LANES), F32), pltpu.VMEM((8, SW), F32)],
        compiler_params=_cparams(("arbitrary", "arbitrary")),
        name=name,
    )(u, bd, cd, ab, dskip.reshape(1, W))


def _s5_sample_kernel(u_ref, bd_ref, cd_ref, a_ref, d_ref, x0r_ref, x0i_ref, y_ref, xr_ref, xi_ref, X_ref, *, B, L):
    NC = X_ref.shape[0]
    NH = NC // 2
    u = u_ref[...]
    bu = _dot_x3(u, _split2(bd_ref[...]))
    for c in range(NC):
        X_ref[c] = bu[:, c * LANES:(c + 1) * LANES]
    for c in range(NH):
        cols = slice(c * LANES, (c + 1) * LANES)
        ar = a_ref[0:1, cols]
        ai = a_ref[1:2, cols]
        xr = x0r_ref[:, cols]
        xi = x0i_ref[:, cols]
        for t in range(L):
            rows = pl.ds(t, B, stride=L)
            xr, xi = _s5_step(ar, ai, xr, xi, X_ref[c, rows, :], X_ref[NH + c, rows, :])
            X_ref[c, rows, :] = xr
            X_ref[NH + c, rows, :] = xi
        xr_ref[:, cols] = xr
        xi_ref[:, cols] = xi
    xs = jnp.concatenate([X_ref[c] for c in range(NC)], axis=1).astype(BF16)
    y_ref[...] = _dot(xs, cd_ref[...].astype(BF16)) + d_ref[...] * u


def s5_sample(u, bd, cd, ab, dskip, x0r, x0i, *, B, L, name):
    T, W = u.shape
    GB = W // LANES
    SW = bd.shape[2]
    half = SW // 2
    kern = functools.partial(_s5_sample_kernel, B=B, L=L)
    return pl.pallas_call(
        kern,
        grid=(GB,),
        in_specs=[
            pl.BlockSpec((T, LANES), lambda g: (0, g)),
            pl.BlockSpec((None, LANES, SW), lambda g: (g, 0, 0)),
            pl.BlockSpec((None, SW, LANES), lambda g: (g, 0, 0)),
            pl.BlockSpec((None, 2, half), lambda g: (g, 0, 0)),
            pl.BlockSpec((1, LANES), lambda g: (0, g)),
            pl.BlockSpec((B, half), lambda g: (0, g)),
            pl.BlockSpec((B, half), lambda g: (0, g)),
        ],
        out_specs=[
            pl.BlockSpec((T, LANES), lambda g: (0, g)),
            pl.BlockSpec((B, half), lambda g: (0, g)),
            pl.BlockSpec((B, half), lambda g: (0, g)),
        ],
        out_shape=[jax.ShapeDtypeStruct((T, W), F32),
                   jax.ShapeDtypeStruct((B, GB * half), F32),
                   jax.ShapeDtypeStruct((B, GB * half), F32)],
        scratch_shapes=[pltpu.VMEM((SW // LANES, T, LANES), F32)],
        compiler_params=_cparams(("arbitrary",)),
        name=name,
    )(u, bd, cd, ab, dskip.reshape(1, W), x0r, x0i)


def s5_params(a_re, a_im, log_dt, b_re, b_im, c_re, c_im):
    G, P = a_re.shape
    gpb = LANES // SSM_GROUP
    GB = G // gpb
    A = lax.complex(a_re, a_im)
    dt = jnp.exp(log_dt)[:, None]
    A_bar = jnp.exp(A * dt)
    Bm = lax.complex(b_re, b_im)
    B_bar = ((A_bar - 1.0) / A)[..., None] * Bm
    eye = jnp.eye(gpb, dtype=F32)

    def in_map(m):
        m = m.reshape(GB, gpb, P, SSM_GROUP)
        return jnp.einsum('bgpc,gh->bgchp', m, eye).reshape(GB, gpb * SSM_GROUP, gpb * P)

    def out_map(m):
        m = m.reshape(GB, gpb, SSM_GROUP, P)
        return jnp.einsum('bgcp,gh->bgphc', m, eye).reshape(GB, gpb * P, gpb * SSM_GROUP)

    bd = jnp.concatenate([in_map(B_bar.real), in_map(B_bar.imag)], axis=2)
    cd = jnp.concatenate([out_map(c_re), out_map(-c_im)], axis=1)
    ab = jnp.stack([A_bar.real.reshape(GB, gpb * P), A_bar.imag.reshape(GB, gpb * P)], axis=1)
    return bd, cd, ab


def _glu_kernel(ya_ref, yb_ref, w_ref, o_ref, wb_ref, *, nfirst):
    i = pl.program_id(0)

    @pl.when(i == 0)
    def _():
        wb_ref[...] = w_ref[...].astype(BF16)

    zg = jax.nn.gelu(_pick(i, nfirst, ya_ref, yb_ref))
    o_ref[...] = (zg * jax.nn.sigmoid(_dot(zg.astype(BF16), wb_ref[...]))).astype(o_ref.dtype)


def gelu_glu(ya, yb, w, tm=512, name="gelu_glu"):
    (Ta, W), Tb = ya.shape, yb.shape[0]
    na = Ta // tm
    return pl.pallas_call(
        functools.partial(_glu_kernel, nfirst=na),
        grid=((Ta + Tb) // tm,),
        in_specs=[*_cat_specs((tm, W), na), pl.BlockSpec((W, W), lambda i: (0, 0))],
        out_specs=pl.BlockSpec((tm, W), lambda i: (i, 0)),
        out_shape=jax.ShapeDtypeStruct((Ta + Tb, W), BF16),
        scratch_shapes=[pltpu.VMEM((W, W), BF16)],
        compiler_params=_cparams(("arbitrary",)),
        name=name,
    )(ya, yb, w)


def _merge_kernel(oa1_ref, oa2_ref, ob_ref, ga_ref, gb_ref, wa_ref, wb_ref, o_ref, wab_ref, wbb_ref, *, nfirst):
    i = pl.program_id(1)

    @pl.when(i == 0)
    def _():
        wab_ref[...] = wa_ref[...].astype(BF16)
        wbb_ref[...] = wb_ref[...].astype(BF16)

    ya = _dot(_pick(i, nfirst, oa1_ref, oa2_ref), wab_ref[...])
    yb = _dot(ob_ref[...], wbb_ref[...])
    o_ref[...] = (jax.nn.sigmoid(ga_ref[...]) * ya + jax.nn.sigmoid(gb_ref[...]) * yb).astype(o_ref.dtype)


def gated_merge(oa1, oa2, ob, z_gate, wa, wb, tm=512, tn=1024, name="gated_merge"):
    T, K = ob.shape
    N = wa.shape[1]
    nj = N // tn
    na = oa1.shape[0] // tm
    return pl.pallas_call(
        functools.partial(_merge_kernel, nfirst=na),
        grid=(nj, T // tm),
        in_specs=[
            *_cat_specs((tm, K), na, row_axis=1),
            pl.BlockSpec((tm, K), lambda j, i: (i, 0)),
            pl.BlockSpec((tm, tn), lambda j, i: (i, j)),
            pl.BlockSpec((tm, tn), lambda j, i: (i, nj + j)),
            pl.BlockSpec((K, tn), lambda j, i: (0, j)),
            pl.BlockSpec((K, tn), lambda j, i: (0, j)),
        ],
        out_specs=pl.BlockSpec((tm, tn), lambda j, i: (i, j)),
        out_shape=jax.ShapeDtypeStruct((T, N), BF16),
        scratch_shapes=[pltpu.VMEM((K, tn), BF16), pltpu.VMEM((K, tn), BF16)],
        compiler_params=_cparams(("arbitrary", "arbitrary")),
        name=name,
    )(oa1, oa2, ob, z_gate, z_gate, wa, wb)


def _resid_kernel(x_ref, h1_ref, h2_ref, w_ref, o_ref, wb_ref, *, nfirst):
    i = pl.program_id(1)

    @pl.when(i == 0)
    def _():
        wb_ref[...] = w_ref[...].astype(BF16)

    o_ref[...] = _pick(i, nfirst, h1_ref, h2_ref) + _dot(x_ref[...], wb_ref[...])


def resid_matmul(x, h1, h2, w, tm=1024, tn=1024, name="resid_matmul"):
    T, K = x.shape
    N = w.shape[1]
    na = h1.shape[0] // tm
    return pl.pallas_call(
        functools.partial(_resid_kernel, nfirst=na),
        grid=(N // tn, T // tm),
        in_specs=[
            pl.BlockSpec((tm, K), lambda j, i: (i, 0)),
            *_cat_specs((tm, tn), na, row_axis=1, col_axis=0),
            pl.BlockSpec((K, tn), lambda j, i: (0, j)),
        ],
        out_specs=pl.BlockSpec((tm, tn), lambda j, i: (i, j)),
        out_shape=jax.ShapeDtypeStruct((T, N), F32),
        scratch_shapes=[pltpu.VMEM((K, tn), BF16)],
        compiler_params=_cparams(("arbitrary", "arbitrary")),
        name=name,
    )(x, h1, h2, w)


def _router_kernel(h_ref, nw_ref, wr_ref, br_ref, xn_ref, route_ref, cnt_ref, tri_ref, carry_ref, *, tm):
    i = pl.program_id(0)

    @pl.when(i == 0)
    def _():
        rr = lax.broadcasted_iota(jnp.int32, (tm, tm), 0)
        cc = lax.broadcasted_iota(jnp.int32, (tm, tm), 1)
        tri_ref[...] = (rr > cc).astype(BF16)
        carry_ref[...] = jnp.zeros_like(carry_ref)

    x = h_ref[...]
    ms = jnp.mean(x * x, axis=-1, keepdims=True)
    xn = x * lax.rsqrt(ms + EPS) * nw_ref[...]
    xn_ref[...] = xn
    logits = _dot_x3(xn, _split2(wr_ref[...])) + br_ref[...]
    lane = lax.broadcasted_iota(jnp.int32, (tm, LANES), 1)
    lanef = lane.astype(F32)
    cur = logits
    vals, hots, eids = [], [], []
    for _ in range(TOP_K):
        m = jnp.max(cur, axis=-1, keepdims=True)
        idx = jnp.min(jnp.where(cur == m, lanef, float(LANES)), axis=-1, keepdims=True)
        hot = lanef == idx
        vals.append(m)
        hots.append(hot)
        eids.append(idx)
        cur = jnp.where(hot, -jnp.inf, cur)
    es = [jnp.exp(v - vals[0]) for v in vals]
    den = es[0] + es[1] + es[2] + es[3]
    multi = jnp.zeros((tm, LANES), F32)
    for hot in hots:
        multi = multi + hot.astype(F32)
    base = carry_ref[0:1, :] + _dot(tri_ref[...], multi.astype(BF16))
    route = jnp.zeros((tm, LANES), F32)
    for k in range(TOP_K):
        e_k = eids[k]
        r_k = jnp.sum(jnp.where(hots[k], base, 0.0), axis=-1, keepdims=True)
        w_k = es[k] / den
        route = route + jnp.where(lane == k, e_k, 0.0) + jnp.where(lane == TOP_K + k, r_k, 0.0) \
            + jnp.where(lane == 2 * TOP_K + k, w_k, 0.0)
    route_ref[...] = route
    carry = carry_ref[0:1, :] + jnp.sum(multi, axis=0, keepdims=True)
    carry_ref[...] = jnp.broadcast_to(carry, carry_ref.shape)
    cnt_ref[...] = jnp.broadcast_to(carry, cnt_ref.shape)


def moe_router(h, norm_w, w_router, b_router, tm=512, name="moe_router"):
    T, D = h.shape
    E = w_router.shape[1]
    wr = jnp.pad(w_router, ((0, 0), (0, LANES - E)))
    br = jnp.pad(b_router.reshape(1, E), ((0, 0), (0, LANES - E)), constant_values=NEG_BIG)
    kern = functools.partial(_router_kernel, tm=tm)
    return pl.pallas_call(
        kern,
        grid=(T // tm,),
        in_specs=[
            pl.BlockSpec((tm, D), lambda i: (i, 0)),
            pl.BlockSpec((1, D), lambda i: (0, 0)),
            pl.BlockSpec((D, LANES), lambda i: (0, 0)),
            pl.BlockSpec((1, LANES), lambda i: (0, 0)),
        ],
        out_specs=[
            pl.BlockSpec((tm, D), lambda i: (i, 0)),
            pl.BlockSpec((tm, LANES), lambda i: (i, 0)),
            pl.BlockSpec((8, LANES), lambda i: (0, 0)),
        ],
        out_shape=[jax.ShapeDtypeStruct((T, D), F32),
                   jax.ShapeDtypeStruct((T, LANES), F32),
                   jax.ShapeDtypeStruct((8, LANES), F32)],
        scratch_shapes=[pltpu.VMEM((tm, tm), BF16), pltpu.VMEM((8, LANES), F32)],
        compiler_params=_cparams(("arbitrary",)),
        name=name,
    )(h, norm_w.reshape(1, D), wr, br)


ROW_UNROLL = 8


def _dispatch_kernel(dest_ref, zrow_ref, x_ref, buf_ref, zero_ref, sem, zsem, *, tm, n_tail):
    i = pl.program_id(0)
    E = (zrow_ref.shape[0] - 1) // 2
    ZR = zero_ref.shape[0]
    R = buf_ref.shape[0]

    @pl.when(i == 0)
    def _():
        zero_ref[...] = jnp.zeros_like(zero_ref)

        def zero_copy(row, n):
            return pltpu.make_async_copy(zero_ref.at[pl.ds(0, n)], buf_ref.at[pl.ds(pl.multiple_of(row, 8), n)], zsem)

        def zero_rows(act):
            for e in range(E):
                off = zrow_ref[e]
                n = ZR
                while n >= 8:
                    @pl.when((zrow_ref[E + e] & n) != 0)
                    def _():
                        act(zero_copy(off, n))
                    off = off + (zrow_ref[E + e] & n)
                    n //= 2
            for j in range(n_tail):
                row = zrow_ref[2 * E] + j * ZR

                @pl.when(row < R)
                def _():
                    act(zero_copy(row, ZR))

        zero_rows(lambda c: c.start())
        zero_rows(lambda c: c.wait())

    def start(rb, c):
        for j in range(ROW_UNROLL):
            r = rb * ROW_UNROLL + j
            for k in range(TOP_K):
                d = dest_ref[(i * tm + r) * TOP_K + k]
                pltpu.make_async_copy(x_ref.at[pl.ds(r, 1)], buf_ref.at[pl.ds(d, 1)], sem).start(priority=k % 2)
        return c

    lax.fori_loop(0, tm // ROW_UNROLL, start, 0)
    for k in range(TOP_K):
        pltpu.make_async_copy(x_ref, buf_ref.at[pl.ds(0, tm)], sem).wait()


def moe_dispatch(xn, dest_flat, zrows, n_rows, tm=256, name="moe_dispatch"):
    T, D = xn.shape
    n_tail = (n_rows - T * TOP_K) // MOE_TM
    kern = functools.partial(_dispatch_kernel, tm=tm, n_tail=n_tail)
    return pl.pallas_call(
        kern,
        grid_spec=pltpu.PrefetchScalarGridSpec(
            num_scalar_prefetch=2,
            grid=(T // tm,),
            in_specs=[pl.BlockSpec((tm, D), lambda i, d, z: (i, 0))],
            out_specs=pl.BlockSpec(memory_space=pl.ANY),
            scratch_shapes=[pltpu.VMEM((MOE_TM, D), F32), pltpu.SemaphoreType.DMA(()),
                            pltpu.SemaphoreType.DMA(())],
        ),
        out_shape=jax.ShapeDtypeStruct((n_rows, D), F32),
        compiler_params=_cparams(("arbitrary",)),
        name=name,
    )(dest_flat, zrows, xn)


def _experts_kernel(ue_ref, r0_ref, nt_ref, na_ref, x_hbm, wg_ref, wu_ref, bg_ref, bu_ref, wd_ref, bd_ref,
                    o_hbm, xs_ref, xb_ref, acc_ref, sem_in, sem_out, *, TM, UT, NF):
    u = pl.program_id(0)
    f = pl.program_id(1)
    nt = nt_ref[u]
    r0 = r0_ref[u]

    def x_copy(i):
        rows = pl.ds(pl.multiple_of(r0 + i * TM, TM), TM)
        return pltpu.make_async_copy(x_hbm.at[rows], xs_ref.at[i % 2], sem_in.at[i % 2])

    def o_copy(base, i):
        rows = pl.ds(pl.multiple_of(base + i * TM, TM), TM)
        return pltpu.make_async_copy(acc_ref.at[i], o_hbm.at[rows], sem_out.at[i])

    def wait_outputs(un):
        for i in range(UT):
            @pl.when(i < nt_ref[un])
            def _():
                o_copy(r0_ref[un], i).wait()

    @pl.when(u < na_ref[0])
    def _():
        @pl.when(f == 0)
        def _():
            x_copy(0).start()

            @pl.when(u > 0)
            def _():
                wait_outputs(u - 1)

            for i in range(UT):
                @pl.when(i < nt)
                def _():
                    if i + 1 < UT:
                        @pl.when(i + 1 < nt)
                        def _():
                            x_copy(i + 1).start()
                    x_copy(i).wait()
                    xb_ref[i] = xs_ref[i % 2].astype(BF16)

        bg = bg_ref[...]
        bu = bu_ref[...]

        def tile_out(xt):
            g = _dot(xt, wg_ref[...].astype(BF16)) + bg
            up = _dot(xt, wu_ref[...].astype(BF16)) + bu
            gate = jnp.minimum(g, SWIGLU_LIMIT)
            lin = jnp.clip(up, -SWIGLU_LIMIT, SWIGLU_LIMIT)
            act = (gate * jax.nn.sigmoid(SWIGLU_ALPHA * gate) * (lin + 1.0)).astype(BF16)
            return _dot(act, wd_ref[...].astype(BF16))

        D = xb_ref.shape[2]

        def update(i, n, first):
            tiles = pl.ds(i, n)
            out = tile_out(xb_ref[tiles].reshape(n * TM, D))
            if first:
                out = out + bd_ref[...]
            else:
                out = out + acc_ref[tiles].reshape(n * TM, D)
            acc_ref[tiles] = out.reshape(n, TM, D)

        def all_tiles(first):
            def body(j, c):
                update(pl.multiple_of(4 * j, 4), 4, first)
                return c
            lax.fori_loop(0, nt // 4, body, 0)
            done4 = (nt // 4) * 4

            @pl.when((nt & 2) != 0)
            def _():
                update(pl.multiple_of(done4, 2), 2, first)

            @pl.when((nt & 1) != 0)
            def _():
                update(nt - 1, 1, first)

        @pl.when(f == 0)
        def _():
            all_tiles(True)

        @pl.when(f > 0)
        def _():
            all_tiles(False)

        @pl.when(f == NF - 1)
        def _():
            for i in range(UT):
                @pl.when(i < nt)
                def _():
                    o_copy(r0, i).start()

            @pl.when(u == na_ref[0] - 1)
            def _():
                wait_outputs(u)


def moe_experts(x_buf, plan, w_gate_up, b_gate_up, w_down, b_down, name="moe_experts"):
    R, D = x_buf.shape
    E, _, F2 = w_gate_up.shape
    DF = F2 // 2
    TM, UT, TF = MOE_TM, MOE_UNIT_TILES, MOE_TF
    NF = DF // TF
    ue, r0, nt, na = plan
    U = ue.shape[0]

    def fsel(u, f, na_ref):
        return jnp.where(u < na_ref[0], f, NF - 1)

    kern = functools.partial(_experts_kernel, TM=TM, UT=UT, NF=NF)
    return pl.pallas_call(
        kern,
        grid_spec=pltpu.PrefetchScalarGridSpec(
            num_scalar_prefetch=4,
            grid=(U, NF),
            in_specs=[
                pl.BlockSpec(memory_space=pl.ANY),
                pl.BlockSpec((None, D, TF), lambda u, f, ue, r0, nt, na: (ue[u], 0, fsel(u, f, na))),
                pl.BlockSpec((None, D, TF), lambda u, f, ue, r0, nt, na: (ue[u], 0, NF + fsel(u, f, na))),
                pl.BlockSpec((None, 1, TF), lambda u, f, ue, r0, nt, na: (ue[u], 0, fsel(u, f, na))),
                pl.BlockSpec((None, 1, TF), lambda u, f, ue, r0, nt, na: (ue[u], 0, NF + fsel(u, f, na))),
                pl.BlockSpec((None, TF, D), lambda u, f, ue, r0, nt, na: (ue[u], fsel(u, f, na), 0)),
                pl.BlockSpec((None, 1, D), lambda u, f, ue, r0, nt, na: (ue[u], 0, 0)),
            ],
            out_specs=pl.BlockSpec(memory_space=pl.ANY),
            scratch_shapes=[
                pltpu.VMEM((2, TM, D), F32),
                pltpu.VMEM((UT, TM, D), BF16),
                pltpu.VMEM((UT, TM, D), F32),
                pltpu.SemaphoreType.DMA((2,)),
                pltpu.SemaphoreType.DMA((UT,)),
            ],
        ),
        out_shape=jax.ShapeDtypeStruct((R, D), F32),
        input_output_aliases={4: 0},
        compiler_params=_cparams(("arbitrary", "arbitrary")),
        name=name,
    )(ue, r0, nt, na, x_buf, w_gate_up, w_gate_up, b_gate_up.reshape(E, 1, F2), b_gate_up.reshape(E, 1, F2),
      w_down, b_down.reshape(E, 1, D))


def moe_plan(counts, n_tokens):
    TM, UT = MOE_TM, MOE_UNIT_TILES
    E = counts.shape[0]
    nt = (counts + TM - 1) // TM
    padded = nt * TM
    pad_start = jnp.cumsum(padded) - padded
    nu = (nt + UT - 1) // UT
    cu = jnp.cumsum(nu)
    n_units = cu[-1]
    U = E + (n_tokens * TOP_K) // (TM * UT)
    u = jnp.arange(U, dtype=jnp.int32)
    uc = jnp.minimum(u, n_units - 1)
    ue = jnp.clip(jnp.searchsorted(cu, uc, side='right'), 0, E - 1).astype(jnp.int32)
    j = uc - (cu[ue] - nu[ue])
    r0 = (pad_start[ue] + j * (UT * TM)).astype(jnp.int32)
    ntl = jnp.where(u < n_units, jnp.clip(nt[ue] - j * UT, 0, UT), 0).astype(jnp.int32)
    zstart = (pad_start + counts) // 8 * 8
    pad_end = pad_start + padded
    zrows = jnp.concatenate([zstart, pad_end - zstart, pad_end[-1:]]).astype(jnp.int32)
    return pad_start, zrows, (ue, r0, ntl, n_units.reshape(1).astype(jnp.int32))


def _combine_kernel(dest_ref, h_ref, route_ref, nw_ref, o_hbm, h2_ref, xn_ref, g_ref, sem, *, tm):
    i = pl.program_id(0)

    def start(rb, c):
        for j in range(ROW_UNROLL):
            r = rb * ROW_UNROLL + j
            for k in range(TOP_K):
                d = dest_ref[(i * tm + r) * TOP_K + k]
                pltpu.make_async_copy(o_hbm.at[pl.ds(d, 1)], g_ref.at[k, pl.ds(r, 1)], sem).start(priority=k % 2)
        return c

    lax.fori_loop(0, tm // ROW_UNROLL, start, 0)
    for k in range(TOP_K):
        pltpu.make_async_copy(o_hbm.at[pl.ds(0, tm)], g_ref.at[k], sem).wait()

    route = route_ref[...]
    y = g_ref[0] * route[:, 2 * TOP_K:2 * TOP_K + 1]
    for k in range(1, TOP_K):
        y = y + g_ref[k] * route[:, 2 * TOP_K + k:2 * TOP_K + k + 1]
    h2 = h_ref[...] + y
    h2_ref[...] = h2
    ms = jnp.mean(h2 * h2, axis=-1, keepdims=True)
    xn_ref[...] = (h2 * lax.rsqrt(ms + EPS) * nw_ref[...]).astype(xn_ref.dtype)


def moe_combine(h, route, dest_flat, out_buf, norm_w, tm=256, name="moe_combine"):
    T, D = h.shape
    kern = functools.partial(_combine_kernel, tm=tm)
    return pl.pallas_call(
        kern,
        grid_spec=pltpu.PrefetchScalarGridSpec(
            num_scalar_prefetch=1,
            grid=(T // tm,),
            in_specs=[pl.BlockSpec((tm, D), lambda i, d: (i, 0)),
                      pl.BlockSpec((tm, LANES), lambda i, d: (i, 0)),
                      pl.BlockSpec((1, D), lambda i, d: (0, 0)),
                      pl.BlockSpec(memory_space=pl.ANY)],
            out_specs=[pl.BlockSpec((tm, D), lambda i, d: (i, 0)),
                       pl.BlockSpec((tm, D), lambda i, d: (i, 0))],
            scratch_shapes=[pltpu.VMEM((TOP_K, tm, D), F32), pltpu.SemaphoreType.DMA(())],
        ),
        out_shape=[jax.ShapeDtypeStruct((T, D), F32), jax.ShapeDtypeStruct((T, D), BF16)],
        compiler_params=_cparams(("arbitrary",)),
        name=name,
    )(dest_flat, h, route, norm_w.reshape(1, D), out_buf)


def _ple_kernel(xn_ref, p1_ref, p2_ref, h_ref, wg_ref, wp_ref, o_ref, wgb_ref, wpb_ref, *, nfirst):
    i = pl.program_id(1)

    @pl.when(i == 0)
    def _():
        wgb_ref[...] = wg_ref[...].astype(BF16)
        wpb_ref[...] = wp_ref[...].astype(BF16)

    gate = jax.nn.sigmoid(_dot(xn_ref[...], wgb_ref[...]))
    p = _pick(i, nfirst, p1_ref, p2_ref).astype(BF16)
    o_ref[...] = h_ref[...] + _dot(p, wpb_ref[...]) * gate


def ple(xn, p1, p2, h, w_gate, w_ple, tm=1024, tn=512, name="ple"):
    T, D = h.shape
    P = p1.shape[1]
    na = p1.shape[0] // tm
    return pl.pallas_call(
        functools.partial(_ple_kernel, nfirst=na),
        grid=(D // tn, T // tm),
        in_specs=[
            pl.BlockSpec((tm, D), lambda j, i: (i, 0)),
            *_cat_specs((tm, P), na, row_axis=1),
            pl.BlockSpec((tm, tn), lambda j, i: (i, j)),
            pl.BlockSpec((D, tn), lambda j, i: (0, j)),
            pl.BlockSpec((P, tn), lambda j, i: (0, j)),
        ],
        out_specs=pl.BlockSpec((tm, tn), lambda j, i: (i, j)),
        out_shape=jax.ShapeDtypeStruct((T, D), F32),
        scratch_shapes=[pltpu.VMEM((D, tn), BF16), pltpu.VMEM((P, tn), BF16)],
        compiler_params=_cparams(("arbitrary", "arbitrary")),
        name=name,
    )(xn, p1, p2, h, w_gate, w_ple)


def kernel(x_prompt, x_sample, p_prompt, p_sample, state_hgrn, state_ssm_re, state_ssm_im, norm_mix, w_in, hg_lb, hg_gnorm, w_branch_a, ssm_a_re, ssm_a_im, ssm_log_dt, ssm_b_re, ssm_b_im, ssm_c_re, ssm_c_im, ssm_d, w_glu, w_branch_b, w_out, norm_moe, w_router, b_router, w_gate_up, b_gate_up, w_down, b_down, norm_ple, w_ple, w_ple_gate, norm_final):
    BP, LP, D = x_prompt.shape
    BS, LS, _ = x_sample.shape
    depth = w_in.shape[0]
    assert depth == 1
    TP, TS = BP * LP, BS * LS
    T = TP + TS
    G, P = ssm_a_re.shape[1:]
    KD = HG_HEADS * HG_DK
    W = SSM_GROUP * G

    xp = x_prompt.reshape(TP, D)
    xs = x_sample.reshape(TS, D)

    xn = rmsnorm_cat(xp, xs, norm_mix[0], BF16, name="norm_mix")
    tm = 1024
    w_in0 = w_in[0]
    z_hg = project(xn, w_in0, 0, T // tm, 0, 4, tm, 1024, "proj_hgrn")
    u_p = project(xn, w_in0, 0, TP // tm, 4, 1, tm, 1024, "proj_u_prompt")
    u_s = project(xn, w_in0, TP // tm, TS // tm, 4, 1, tm, 1024, "proj_u_sample")
    z_gate = project(xn, w_in0, 0, T // tm, 5, 4, tm, 1024, "proj_gates")

    lb = jax.nn.softmax(hg_lb.astype(F32), axis=0)[0]
    og_p, hg_p = hgrn2(z_hg, lb, hg_gnorm[0], None, row0=0, B=BP, L=LP, C=HG_CHUNK, nb=1, tok_step=256,
                       name="hgrn_prompt")
    og_s, hg_s = hgrn2(z_hg, lb, hg_gnorm[0], state_hgrn[0], row0=TP, B=BS, L=LS, C=LS, nb=4, tok_step=LS,
                       name="hgrn_sample")

    bd, cd, ab = s5_params(ssm_a_re[0], ssm_a_im[0], ssm_log_dt[0], ssm_b_re[0], ssm_b_im[0],
                           ssm_c_re[0], ssm_c_im[0])
    y_p, re_p, im_p = s5_prompt(u_p.reshape(BP, LP, W), bd, cd, ab, ssm_d[0], TC=256, name="s5_prompt")
    y_s, re_s, im_s = s5_sample(u_s, bd, cd, ab, ssm_d[0], state_ssm_re[0].reshape(BS, G * P),
                                state_ssm_im[0].reshape(BS, G * P), B=BS, L=LS, name="s5_sample")

    glu = gelu_glu(y_p.reshape(TP, W), y_s, w_glu[0])
    merged = gated_merge(og_p, og_s, glu, z_gate, w_branch_a[0], w_branch_b[0])
    h1 = resid_matmul(merged, xp, xs, w_out[0], name="out_proj")

    xn2, route, cnt = moe_router(h1, norm_moe[0], w_router[0], b_router[0])
    e_idx = route[:, 0:TOP_K].astype(jnp.int32)
    rank = route[:, TOP_K:2 * TOP_K].astype(jnp.int32)
    counts = cnt[0, :N_EXPERTS].astype(jnp.int32)
    pad_start, zrows, plan = moe_plan(counts, T)
    dest = (pad_start[e_idx] + rank).astype(jnp.int32).reshape(T * TOP_K)
    n_rows = T * TOP_K + N_EXPERTS * MOE_TM
    x_buf = moe_dispatch(xn2, dest, zrows, n_rows)
    out_buf = moe_experts(x_buf, plan, w_gate_up[0], b_gate_up[0], w_down[0], b_down[0])
    h2, xn3 = moe_combine(h1, route, dest, out_buf, norm_ple[0])

    h3 = ple(xn3, p_prompt[0].reshape(TP, -1), p_sample[0].reshape(TS, -1), h2, w_ple_gate[0], w_ple[0])
    y_p2, y_s2 = rmsnorm_split(h3, norm_final, TP, name="norm_final")

    y_prompt = y_p2.reshape(BP, LP, D)
    y_sample = y_s2.reshape(BS, LS, D)
    hs = (1, BP, HG_HEADS, HG_DK, HG_DV)
    return (y_prompt, y_sample,
            hg_p.reshape(hs), re_p.reshape(1, BP, G, P), im_p.reshape(1, BP, G, P),
            hg_s.reshape(1, BS, HG_HEADS, HG_DK, HG_DV), re_s.reshape(1, BS, G, P), im_s.reshape(1, BS, G, P))
```

```python
import functools

import jax
import jax.numpy as jnp
from jax import lax
from jax.experimental import pallas as pl
from jax.experimental.pallas import tpu as pltpu

F32 = jnp.float32
BF16 = jnp.bfloat16
EPS = 1e-6

HG_HEADS = 8
HG_DK = 128
HG_DV = 128
HG_CHUNK = 32
SSM_GROUP = 16
SSM_STATE = 64
N_EXPERTS = 32
TOP_K = 4
SWIGLU_LIMIT = 7.0
SWIGLU_ALPHA = 1.702

LANES = 128
VMEM_LIMIT_BYTES = 56 * 1024 * 1024

MOE_TM = 256
MOE_UNIT_TILES = 8
MOE_TF = 256
NEG_BIG = -1e30


def _cparams(sem):
    return pltpu.CompilerParams(dimension_semantics=sem, vmem_limit_bytes=VMEM_LIMIT_BYTES)


def _dot(a, b):
    return jnp.dot(a, b, preferred_element_type=F32)


def _split2(a):
    hi = a.astype(BF16)
    lo = (a - hi.astype(F32)).astype(BF16)
    return hi, lo


def _dot_x3(a, b_split):
    a1, a2 = _split2(a)
    b1, b2 = b_split
    return _dot(a1, b1) + (_dot(a1, b2) + _dot(a2, b1))


def _cat_specs(shape, nfirst, row_axis=0, col_axis=None):
    def col(ids):
        return 0 if col_axis is None else ids[col_axis]

    first = pl.BlockSpec(shape, lambda *ids: (jnp.minimum(ids[row_axis], nfirst - 1), col(ids)))
    second = pl.BlockSpec(shape, lambda *ids: (jnp.maximum(ids[row_axis] - nfirst, 0), col(ids)))
    return first, second


def _pick(i, nfirst, a_ref, b_ref):
    return jnp.where(i < nfirst, a_ref[...], b_ref[...])


def _rms(x, w):
    ms = jnp.mean(x * x, axis=-1, keepdims=True)
    return x * lax.rsqrt(ms + EPS) * w


def _rms_in2_kernel(a_ref, b_ref, w_ref, o_ref, *, nfirst):
    x = _pick(pl.program_id(0), nfirst, a_ref, b_ref)
    o_ref[...] = _rms(x, w_ref[...]).astype(o_ref.dtype)


def rmsnorm_cat(xa, xb, w, out_dtype, tm=512, name="rmsnorm"):
    (Ta, D), Tb = xa.shape, xb.shape[0]
    na = Ta // tm
    return pl.pallas_call(
        functools.partial(_rms_in2_kernel, nfirst=na),
        grid=((Ta + Tb) // tm,),
        in_specs=[*_cat_specs((tm, D), na), pl.BlockSpec((1, D), lambda i: (0, 0))],
        out_specs=pl.BlockSpec((tm, D), lambda i: (i, 0)),
        out_shape=jax.ShapeDtypeStruct((Ta + Tb, D), out_dtype),
        compiler_params=_cparams(("arbitrary",)),
        name=name,
    )(xa, xb, w.reshape(1, D))


def _rms_out2_kernel(x_ref, w_ref, oa_ref, ob_ref, *, nfirst):
    i = pl.program_id(0)
    y = _rms(x_ref[...], w_ref[...])

    @pl.when(i < nfirst)
    def _():
        oa_ref[...] = y

    @pl.when(i >= nfirst)
    def _():
        ob_ref[...] = y


def rmsnorm_split(x, w, Ta, tm=512, name="rmsnorm_split"):
    T, D = x.shape
    na = Ta // tm
    return pl.pallas_call(
        functools.partial(_rms_out2_kernel, nfirst=na),
        grid=(T // tm,),
        in_specs=[pl.BlockSpec((tm, D), lambda i: (i, 0)), pl.BlockSpec((1, D), lambda i: (0, 0))],
        out_specs=list(_cat_specs((tm, D), na)),
        out_shape=[jax.ShapeDtypeStruct((Ta, D), F32), jax.ShapeDtypeStruct((T - Ta, D), F32)],
        compiler_params=_cparams(("arbitrary",)),
        name=name,
    )(x, w.reshape(1, D))


def _proj_kernel(x_ref, w_ref, o_ref, wb_ref):
    @pl.when(pl.program_id(1) == 0)
    def _():
        wb_ref[...] = w_ref[...].astype(BF16)

    o_ref[...] = _dot(x_ref[...], wb_ref[...]).astype(o_ref.dtype)


def project(xn, w, row_blk0, n_row_blks, col_blk0, n_col_blks, tm, tn, name):
    K = xn.shape[1]
    return pl.pallas_call(
        _proj_kernel,
        grid=(n_col_blks, n_row_blks),
        in_specs=[
            pl.BlockSpec((tm, K), lambda j, i: (row_blk0 + i, 0)),
            pl.BlockSpec((K, tn), lambda j, i: (0, col_blk0 + j)),
        ],
        out_specs=pl.BlockSpec((tm, tn), lambda j, i: (i, j)),
        out_shape=jax.ShapeDtypeStruct((n_row_blks * tm, n_col_blks * tn), F32),
        scratch_shapes=[pltpu.VMEM((K, tn), BF16)],
        compiler_params=_cparams(("arbitrary", "arbitrary")),
        name=name,
    )(xn, w)


HGRN_CHUNKS_PER_TRIP = 4


def _hgrn_kernel(*refs, C, nchunk, nb, tok_step, has_s0, state_t):
    if has_s0:
        q_ref, f_ref, v_ref, g_ref, lb_ref, gn_ref, s0_ref, og_ref, so_ref, st_ref = refs
    else:
        q_ref, f_ref, v_ref, g_ref, lb_ref, gn_ref, og_ref, so_ref, st_ref = refs
        s0_ref = None
    H, DK, DV = HG_HEADS, HG_DK, HG_DV
    step = pl.program_id(1)

    @pl.when(step == 0)
    def _():
        for i in range(nb):
            for h in range(H):
                if has_s0:
                    st_ref[i, h] = s0_ref[i, h].T if state_t else s0_ref[i, h]
                else:
                    st_ref[i, h] = jnp.zeros(st_ref.shape[2:], F32)

    lbv = lb_ref[...]
    gnv = gn_ref[...]
    rr = lax.broadcasted_iota(jnp.int32, (C, C), 0)
    cc = lax.broadcasted_iota(jnp.int32, (C, C), 1)
    causal = rr >= cc
    tri = causal.astype(BF16)
    sls =[slice(h * DK, (h + 1) * DK) for h in range(H)]
    nt_dims = (((1,), (1,)), ((), ()))
    tn_dims = (((0,), (0,)), ((), ()))

    def tn(a, b):
        return lax.dot_general(a, b, tn_dims, preferred_element_type=F32)

    def prepare(r0):
        rows = pl.ds(r0, C)
        zq = q_ref[rows, :]
        zf = f_ref[rows, :]
        q = jax.nn.silu(zq)
        fe = lbv + (1.0 - lbv) * jax.nn.sigmoid(zf)
        k = 1.0 - fe
        gl = jnp.log(fe)
        g1 = gl.astype(BF16)
        r1 = gl - g1.astype(F32)
        g2 = r1.astype(BF16)
        g3 = (r1 - g2.astype(F32)).astype(BF16)
        b = _dot(tri, g1) + (_dot(tri, g2) + _dot(tri, g3))
        blast = b[C - 1:C, :]
        return dict(
            qt=(q * jnp.exp(b)).astype(BF16),
            kt=(k * jnp.exp(-b)).astype(BF16),
            kend=(k * jnp.exp(blast - b)).astype(BF16),
            eblast=jnp.exp(blast),
            vb=v_ref[rows, :].astype(BF16),
            gate=jax.nn.silu(g_ref[rows, :]),
        )

    def run(jobs):
        P = [prepare(r0) for _, r0 in jobs]
        atts = [[lax.dot_general(p["qt"][:, sl], p["kt"][:, sl], nt_dims, preferred_element_type=F32)
                 for sl in sls] for p in P]
        if state_t:
            upds = [[tn(p["vb"][:, sl], p["kend"][:, sl]) for sl in sls] for p in P]
            decs = [[p["eblast"][:, sl] for sl in sls] for p in P]
        else:
            upds = [[tn(p["kend"][:, sl], p["vb"][:, sl]) for sl in sls] for p in P]
            decs = [[jnp.broadcast_to(p["eblast"][:, sl], (8, DK)).T[:, 0:1] for sl in sls] for p in P]
        state = {}
        inters = []
        for j, (i, _) in enumerate(jobs):
            row = []
            for h, sl in enumerate(sls):
                s = state[(i, h)] if (i, h) in state else st_ref[i, h]
                qth = P[j]["qt"][:, sl]
                if state_t:
                    row.append(lax.dot_general(qth, s.astype(BF16), nt_dims, preferred_element_type=F32))
                else:
                    row.append(_dot(qth, s.astype(BF16)))
                state[(i, h)] = s * decs[j][h] + upds[j][h]
            inters.append(row)
        for (i, h), s in state.items():
            st_ref[i, h] = s
        outs = []
        for j, p in enumerate(P):
            cols = []
            for h, sl in enumerate(sls):
                att = jnp.where(causal, atts[j][h], 0.0).astype(BF16)
                o = _dot(att, p["vb"][:, sl]) + inters[j][h]
                ms = jnp.mean(o * o, axis=-1, keepdims=True)
                cols.append(o * lax.rsqrt(ms + EPS) * gnv * p["gate"][:, sl])
            outs.append(jnp.concatenate(cols, axis=1))
        return outs

    if nchunk == 1:
        outs = run([(i, i * tok_step) for i in range(nb)])
        og_ref[...] = jnp.concatenate(outs, axis=0).astype(og_ref.dtype)
    else:
        assert nb == 1 and nchunk % HGRN_CHUNKS_PER_TRIP == 0

        def body(t, carry):
            r0s = [pl.multiple_of((HGRN_CHUNKS_PER_TRIP * t + j) * C, C) for j in range(HGRN_CHUNKS_PER_TRIP)]
            outs = run([(0, r0) for r0 in r0s])
            for r0, o in zip(r0s, outs):
                og_ref[pl.ds(r0, C), :] = o.astype(og_ref.dtype)
            return carry

        lax.fori_loop(0, nchunk // HGRN_CHUNKS_PER_TRIP, body, 0)

    @pl.when(step == pl.num_programs(1) - 1)
    def _():
        for i in range(nb):
            for h in range(H):
                so_ref[i, h] = st_ref[i, h].T if state_t else st_ref[i, h]


def hgrn2(z_hg, lb, gn, s0, *, row0, B, L, C, nb, tok_step, name):
    H, DK, DV = HG_HEADS, HG_DK, HG_DV
    W = H * DK
    steps = L // tok_step
    rows = nb * tok_step
    nchunk = tok_step // C
    blk0 = row0 // rows
    has_s0 = s0 is not None

    def zspec(col):
        return pl.BlockSpec((rows, W), lambda bb, s: (blk0 + bb * steps + s, col))

    in_specs = [zspec(0), zspec(1), zspec(2), zspec(3),
                pl.BlockSpec((1, W), lambda bb, s: (0, 0)),
                pl.BlockSpec((1, DV), lambda bb, s: (0, 0))]
    args = [z_hg, z_hg, z_hg, z_hg, lb.reshape(1, W), gn.reshape(1, DV)]
    if has_s0:
        in_specs.append(pl.BlockSpec((nb, H, DK, DV), lambda bb, s: (bb, 0, 0, 0)))
        args.append(s0)
    state_t = L > C
    kern = functools.partial(_hgrn_kernel, C=C, nchunk=nchunk, nb=nb, tok_step=tok_step, has_s0=has_s0,
                             state_t=state_t)
    return pl.pallas_call(
        kern,
        grid=(B // nb, steps),
        in_specs=in_specs,
        out_specs=[pl.BlockSpec((rows, W), lambda bb, s: (bb * steps + s, 0)),
                   pl.BlockSpec((nb, H, DK, DV), lambda bb, s: (bb, 0, 0, 0))],
        out_shape=[jax.ShapeDtypeStruct((B * L, W), BF16),
                   jax.ShapeDtypeStruct((B, H, DK, DV), F32)],
        scratch_shapes=[pltpu.VMEM((nb, H, DV, DK), F32)],
        compiler_params=_cparams(("arbitrary", "arbitrary")),
        name=name,
    )(*args)


def _s5_step(ar, ai, xr, xi, br, bi):
    return ar * xr - ai * xi + br, ar * xi + ai * xr + bi


S5_STEPS_PER_TRIP = 4


def _s5_prompt_kernel(u_ref, bd_ref, cd_ref, a_ref, d_ref, y_ref, xr_ref, xi_ref, X_ref, st_ref, *, B, TC):
    NP = X_ref.shape[0] // 2
    assert 2 * B == 8
    tc = pl.program_id(1)

    def rows_of(b, s):
        return pl.ds(2 * b + s, TC, stride=8)

    @pl.when(tc == 0)
    def _():
        st_ref[...] = jnp.zeros_like(st_ref)

    bd = bd_ref[...].astype(BF16)
    for b in range(B):
        bu = _dot(u_ref[b].astype(BF16), bd)
        for c in range(4 * NP):
            X_ref[c // 2, rows_of(b, c % 2), :] = bu[:, c * LANES:(c + 1) * LANES]

    odd = lax.broadcasted_iota(jnp.int32, (8, LANES), 0) % 2 == 1

    def pair_const(row, p):
        lo = a_ref[row:row + 1, (2 * p) * LANES:(2 * p + 1) * LANES]
        hi = a_ref[row:row + 1, (2 * p + 1) * LANES:(2 * p + 2) * LANES]
        return jnp.where(odd, hi, lo)

    ars = [pair_const(0, p) for p in range(NP)]
    ais = [pair_const(1, p) for p in range(NP)]

    def body(i, carry):
        carry = list(carry)
        for j in range(S5_STEPS_PER_TRIP):
            rows = pl.ds(pl.multiple_of((i * S5_STEPS_PER_TRIP + j) * 8, 8), 8)
            for p in range(NP):
                xr, xi = _s5_step(ars[p], ais[p], carry[p][0], carry[p][1], X_ref[p, rows, :], X_ref[NP + p, rows, :])
                X_ref[p, rows, :] = xr
                X_ref[NP + p, rows, :] = xi
                carry[p] = (xr, xi)
        return tuple(carry)

    init = tuple((st_ref[p], st_ref[NP + p]) for p in range(NP))
    fin = lax.fori_loop(0, TC // S5_STEPS_PER_TRIP, body, init)
    for p in range(NP):
        st_ref[p] = fin[p][0]
        st_ref[NP + p] = fin[p][1]

    cd = cd_ref[...].astype(BF16)
    dv = d_ref[...]
    for b in range(B):
        xb = jnp.concatenate([X_ref[c // 2, rows_of(b, c % 2), :] for c in range(4 * NP)], axis=1).astype(BF16)
        y_ref[b] = _dot(xb, cd) + dv * u_ref[b]

    @pl.when(tc == pl.num_programs(1) - 1)
    def _():
        for c in range(2 * NP):
            cols = slice(c * LANES, (c + 1) * LANES)
            xr_ref[:, cols] = st_ref[c // 2, pl.ds(c % 2, B, stride=2), :]
            xi_ref[:, cols] = st_ref[NP + c // 2, pl.ds(c % 2, B, stride=2), :]


def s5_prompt(u, bd, cd, ab, dskip, *, TC, name):
    B, L, W = u.shape
    GB = W // LANES
    SW = bd.shape[2]
    half = SW // 2
    kern = functools.partial(_s5_prompt_kernel, B=B, TC=TC)
    return pl.pallas_call(
        kern,
        grid=(GB, L // TC),
        in_specs=[
            pl.BlockSpec((B, TC, LANES), lambda g, t: (0, t, g)),
            pl.BlockSpec((None, LANES, SW), lambda g, t: (g, 0, 0)),
            pl.BlockSpec((None, SW, LANES), lambda g, t: (g, 0, 0)),
            pl.BlockSpec((None, 2, half), lambda g, t: (g, 0, 0)),
            pl.BlockSpec((1, LANES), lambda g, t: (0, g)),
        ],
        out_specs=[
            pl.BlockSpec((B, TC, LANES), lambda g, t: (0, t, g)),
            pl.BlockSpec((B, half), lambda g, t: (0, g)),
            pl.BlockSpec((B, half), lambda g, t: (0, g)),
        ],
        out_shape=[jax.ShapeDtypeStruct((B, L, W), F32),
                   jax.ShapeDtypeStruct((B, GB * half), F32),
                   jax.ShapeDtypeStruct((B, GB * half), F32)],
        scratch_shapes=[pltpu.VMEM((SW // LANES // 2, 8 * TC, LANES), F32),
                        pltpu.VMEM((SW // LANES // 2, 8, LANES), F32)],
        compiler_params=_cparams(("arbitrary", "arbitrary")),
        name=name,
    )(u, bd, cd, ab, dskip.reshape(1, W))


def _s5_sample_kernel(u_ref, bd_ref, cd_ref, a_ref, d_ref, x0r_ref, x0i_ref, y_ref, xr_ref, xi_ref, X_ref, *, B, L):
    NC = X_ref.shape[0]
    NH = NC // 2
    u = u_ref[...]
    bu = _dot_x3(u, _split2(bd_ref[...]))
    for c in range(NC):
        X_ref[c] = bu[:, c * LANES:(c + 1) * LANES]
    for c in range(NH):
        cols = slice(c * LANES, (c + 1) * LANES)
        ar = a_ref[0:1, cols]
        ai = a_ref[1:2, cols]
        xr = x0r_ref[:, cols]
        xi = x0i_ref[:, cols]
        for t in range(L):
            rows = pl.ds(t, B, stride=L)
            xr, xi = _s5_step(ar, ai, xr, xi, X_ref[c, rows, :], X_ref[NH + c, rows, :])
            X_ref[c, rows, :] = xr
            X_ref[NH + c, rows, :] = xi
        xr_ref[:, cols] = xr
        xi_ref[:, cols] = xi
    xs = jnp.concatenate([X_ref[c] for c in range(NC)], axis=1).astype(BF16)
    y_ref[...] = _dot(xs, cd_ref[...].astype(BF16)) + d_ref[...] * u


def s5_sample(u, bd, cd, ab, dskip, x0r, x0i, *, B, L, name):
    T, W = u.shape
    GB = W // LANES
    SW = bd.shape[2]
    half = SW // 2
    kern = functools.partial(_s5_sample_kernel, B=B, L=L)
    return pl.pallas_call(
        kern,
        grid=(GB,),
        in_specs=[
            pl.BlockSpec((T, LANES), lambda g: (0, g)),
            pl.BlockSpec((None, LANES, SW), lambda g: (g, 0, 0)),
            pl.BlockSpec((None, SW, LANES), lambda g: (g, 0, 0)),
            pl.BlockSpec((None, 2, half), lambda g: (g, 0, 0)),
            pl.BlockSpec((1, LANES), lambda g: (0, g)),
            pl.BlockSpec((B, half), lambda g: (0, g)),
            pl.BlockSpec((B, half), lambda g: (0, g)),
        ],
        out_specs=[
            pl.BlockSpec((T, LANES), lambda g: (0, g)),
            pl.BlockSpec((B, half), lambda g: (0, g)),
            pl.BlockSpec((B, half), lambda g: (0, g)),
        ],
        out_shape=[jax.ShapeDtypeStruct((T, W), F32),
                   jax.ShapeDtypeStruct((B, GB * half), F32),
                   jax.ShapeDtypeStruct((B, GB * half), F32)],
        scratch_shapes=[pltpu.VMEM((SW // LANES, T, LANES), F32)],
        compiler_params=_cparams(("arbitrary",)),
        name=name,
    )(u, bd, cd, ab, dskip.reshape(1, W), x0r, x0i)


def s5_params(a_re, a_im, log_dt, b_re, b_im, c_re, c_im):
    G, P = a_re.shape
    gpb = LANES // SSM_GROUP
    GB = G // gpb
    A = lax.complex(a_re, a_im)
    dt = jnp.exp(log_dt)[:, None]
    A_bar = jnp.exp(A * dt)
    Bm = lax.complex(b_re, b_im)
    B_bar = ((A_bar - 1.0) / A)[..., None] * Bm
    eye = jnp.eye(gpb, dtype=F32)

    def in_map(m):
        m = m.reshape(GB, gpb, P, SSM_GROUP)
        return jnp.einsum('bgpc,gh->bgchp', m, eye).reshape(GB, gpb * SSM_GROUP, gpb * P)

    def out_map(m):
        m = m.reshape(GB, gpb, SSM_GROUP, P)
        return jnp.einsum('bgcp,gh->bgphc', m, eye).reshape(GB, gpb * P, gpb * SSM_GROUP)

    bd = jnp.concatenate([in_map(B_bar.real), in_map(B_bar.imag)], axis=2)
    cd = jnp.concatenate([out_map(c_re), out_map(-c_im)], axis=1)
    ab = jnp.stack([A_bar.real.reshape(GB, gpb * P), A_bar.imag.reshape(GB, gpb * P)], axis=1)
    return bd, cd, ab


def _glu_kernel(ya_ref, yb_ref, w_ref, o_ref, wb_ref, *, nfirst):
    i = pl.program_id(0)

    @pl.when(i == 0)
    def _():
        wb_ref[...] = w_ref[...].astype(BF16)

    zg = jax.nn.gelu(_pick(i, nfirst, ya_ref, yb_ref))
    o_ref[...] = (zg * jax.nn.sigmoid(_dot(zg.astype(BF16), wb_ref[...]))).astype(o_ref.dtype)


def gelu_glu(ya, yb, w, tm=512, name="gelu_glu"):
    (Ta, W), Tb = ya.shape, yb.shape[0]
    na = Ta // tm
    return pl.pallas_call(
        functools.partial(_glu_kernel, nfirst=na),
        grid=((Ta + Tb) // tm,),
        in_specs=[*_cat_specs((tm, W), na), pl.BlockSpec((W, W), lambda i: (0, 0))],
        out_specs=pl.BlockSpec((tm, W), lambda i: (i, 0)),
        out_shape=jax.ShapeDtypeStruct((Ta + Tb, W), BF16),
        scratch_shapes=[pltpu.VMEM((W, W), BF16)],
        compiler_params=_cparams(("arbitrary",)),
        name=name,
    )(ya, yb, w)


def _merge_kernel(oa1_ref, oa2_ref, ob_ref, ga_ref, gb_ref, wa_ref, wb_ref, o_ref, wab_ref, wbb_ref, *, nfirst):
    i = pl.program_id(1)

    @pl.when(i == 0)
    def _():
        wab_ref[...] = wa_ref[...].astype(BF16)
        wbb_ref[...] = wb_ref[...].astype(BF16)

    ya = _dot(_pick(i, nfirst, oa1_ref, oa2_ref), wab_ref[...])
    yb = _dot(ob_ref[...], wbb_ref[...])
    o_ref[...] = (jax.nn.sigmoid(ga_ref[...]) * ya + jax.nn.sigmoid(gb_ref[...]) * yb).astype(o_ref.dtype)


def gated_merge(oa1, oa2, ob, z_gate, wa, wb, tm=512, tn=1024, name="gated_merge"):
    T, K = ob.shape
    N = wa.shape[1]
    nj = N // tn
    na = oa1.shape[0] // tm
    return pl.pallas_call(
        functools.partial(_merge_kernel, nfirst=na),
        grid=(nj, T // tm),
        in_specs=[
            *_cat_specs((tm, K), na, row_axis=1),
            pl.BlockSpec((tm, K), lambda j, i: (i, 0)),
            pl.BlockSpec((tm, tn), lambda j, i: (i, j)),
            pl.BlockSpec((tm, tn), lambda j, i: (i, nj + j)),
            pl.BlockSpec((K, tn), lambda j, i: (0, j)),
            pl.BlockSpec((K, tn), lambda j, i: (0, j)),
        ],
        out_specs=pl.BlockSpec((tm, tn), lambda j, i: (i, j)),
        out_shape=jax.ShapeDtypeStruct((T, N), BF16),
        scratch_shapes=[pltpu.VMEM((K, tn), BF16), pltpu.VMEM((K, tn), BF16)],
        compiler_params=_cparams(("arbitrary", "arbitrary")),
        name=name,
    )(oa1, oa2, ob, z_gate, z_gate, wa, wb)


def _resid_kernel(x_ref, h1_ref, h2_ref, w_ref, o_ref, wb_ref, *, nfirst):
    i = pl.program_id(1)

    @pl.when(i == 0)
    def _():
        wb_ref[...] = w_ref[...].astype(BF16)

    o_ref[...] = _pick(i, nfirst, h1_ref, h2_ref) + _dot(x_ref[...], wb_ref[...])


def resid_matmul(x, h1, h2, w, tm=1024, tn=1024, name="resid_matmul"):
    T, K = x.shape
    N = w.shape[1]
    na = h1.shape[0] // tm
    return pl.pallas_call(
        functools.partial(_resid_kernel, nfirst=na),
        grid=(N // tn, T // tm),
        in_specs=[
            pl.BlockSpec((tm, K), lambda j, i: (i, 0)),
            *_cat_specs((tm, tn), na, row_axis=1, col_axis=0),
            pl.BlockSpec((K, tn), lambda j, i: (0, j)),
        ],
        out_specs=pl.BlockSpec((tm, tn), lambda j, i: (i, j)),
        out_shape=jax.ShapeDtypeStruct((T, N), F32),
        scratch_shapes=[pltpu.VMEM((K, tn), BF16)],
        compiler_params=_cparams(("arbitrary", "arbitrary")),
        name=name,
    )(x, h1, h2, w)


def _router_kernel(h_ref, nw_ref, wr_ref, br_ref, xn_ref, route_ref, cnt_ref, tri_ref, carry_ref, *, tm):
    i = pl.program_id(0)

    @pl.when(i == 0)
    def _():
        rr = lax.broadcasted_iota(jnp.int32, (tm, tm), 0)
        cc = lax.broadcasted_iota(jnp.int32, (tm, tm), 1)
        tri_ref[...] = (rr > cc).astype(BF16)
        carry_ref[...] = jnp.zeros_like(carry_ref)

    x = h_ref[...]
    ms = jnp.mean(x * x, axis=-1, keepdims=True)
    xn = x * lax.rsqrt(ms + EPS) * nw_ref[...]
    xn_ref[...] = xn
    logits = _dot_x3(xn, _split2(wr_ref[...])) + br_ref[...]
    lane = lax.broadcasted_iota(jnp.int32, (tm, LANES), 1)
    lanef = lane.astype(F32)
    cur = logits
    vals, hots, eids = [], [], []
    for _ in range(TOP_K):
        m = jnp.max(cur, axis=-1, keepdims=True)
        idx = jnp.min(jnp.where(cur == m, lanef, float(LANES)), axis=-1, keepdims=True)
        hot = lanef == idx
        vals.append(m)
        hots.append(hot)
        eids.append(idx)
        cur = jnp.where(hot, -jnp.inf, cur)
    es = [jnp.exp(v - vals[0]) for v in vals]
    den = es[0] + es[1] + es[2] + es[3]
    multi = jnp.zeros((tm, LANES), F32)
    for hot in hots:
        multi = multi + hot.astype(F32)
    base = carry_ref[0:1, :] + _dot(tri_ref[...], multi.astype(BF16))
    route = jnp.zeros((tm, LANES), F32)
    for k in range(TOP_K):
        e_k = eids[k]
        r_k = jnp.sum(jnp.where(hots[k], base, 0.0), axis=-1, keepdims=True)
        w_k = es[k] / den
        route = route + jnp.where(lane == k, e_k, 0.0) + jnp.where(lane == TOP_K + k, r_k, 0.0) \
            + jnp.where(lane == 2 * TOP_K + k, w_k, 0.0)
    route_ref[...] = route
    carry = carry_ref[0:1, :] + jnp.sum(multi, axis=0, keepdims=True)
    carry_ref[...] = jnp.broadcast_to(carry, carry_ref.shape)
    cnt_ref[...] = jnp.broadcast_to(carry, cnt_ref.shape)


def moe_router(h, norm_w, w_router, b_router, tm=512, name="moe_router"):
    T, D = h.shape
    E = w_router.shape[1]
    wr = jnp.pad(w_router, ((0, 0), (0, LANES - E)))
    br = jnp.pad(b_router.reshape(1, E), ((0, 0), (0, LANES - E)), constant_values=NEG_BIG)
    kern = functools.partial(_router_kernel, tm=tm)
    return pl.pallas_call(
        kern,
        grid=(T // tm,),
        in_specs=[
            pl.BlockSpec((tm, D), lambda i: (i, 0)),
            pl.BlockSpec((1, D), lambda i: (0, 0)),
            pl.BlockSpec((D, LANES), lambda i: (0, 0)),
            pl.BlockSpec((1, LANES), lambda i: (0, 0)),
        ],
        out_specs=[
            pl.BlockSpec((tm, D), lambda i: (i, 0)),
            pl.BlockSpec((tm, LANES), lambda i: (i, 0)),
            pl.BlockSpec((8, LANES), lambda i: (0, 0)),
        ],
        out_shape=[jax.ShapeDtypeStruct((T, D), F32),
                   jax.ShapeDtypeStruct((T, LANES), F32),
                   jax.ShapeDtypeStruct((8, LANES), F32)],
        scratch_shapes=[pltpu.VMEM((tm, tm), BF16), pltpu.VMEM((8, LANES), F32)],
        compiler_params=_cparams(("arbitrary",)),
        name=name,
    )(h, norm_w.reshape(1, D), wr, br)


ROW_UNROLL = 8


def _dispatch_kernel(dest_ref, zrow_ref, x_ref, buf_ref, zero_ref, sem, zsem, *, tm, n_tail):
    i = pl.program_id(0)
    E = (zrow_ref.shape[0] - 1) // 2
    ZR = zero_ref.shape[0]
    R = buf_ref.shape[0]

    @pl.when(i == 0)
    def _():
        zero_ref[...] = jnp.zeros_like(zero_ref)

        def zero_copy(row, n):
            return pltpu.make_async_copy(zero_ref.at[pl.ds(0, n)], buf_ref.at[pl.ds(pl.multiple_of(row, 8), n)], zsem)

        def zero_rows(act):
            for e in range(E):
                off = zrow_ref[e]
                n = ZR
                while n >= 8:
                    @pl.when((zrow_ref[E + e] & n) != 0)
                    def _():
                        act(zero_copy(off, n))
                    off = off + (zrow_ref[E + e] & n)
                    n //= 2
            for j in range(n_tail):
                row = zrow_ref[2 * E] + j * ZR

                @pl.when(row < R)
                def _():
                    act(zero_copy(row, ZR))

        zero_rows(lambda c: c.start())
        zero_rows(lambda c: c.wait())

    def start(rb, c):
        for j in range(ROW_UNROLL):
            r = rb * ROW_UNROLL + j
            for k in range(TOP_K):
                d = dest_ref[(i * tm + r) * TOP_K + k]
                pltpu.make_async_copy(x_ref.at[pl.ds(r, 1)], buf_ref.at[pl.ds(d, 1)], sem).start(priority=k % 2)
        return c

    lax.fori_loop(0, tm // ROW_UNROLL, start, 0)
    for k in range(TOP_K):
        pltpu.make_async_copy(x_ref, buf_ref.at[pl.ds(0, tm)], sem).wait()


def moe_dispatch(xn, dest_flat, zrows, n_rows, tm=256, name="moe_dispatch"):
    T, D = xn.shape
    n_tail = (n_rows - T * TOP_K) // MOE_TM
    kern = functools.partial(_dispatch_kernel, tm=tm, n_tail=n_tail)
    return pl.pallas_call(
        kern,
        grid_spec=pltpu.PrefetchScalarGridSpec(
            num_scalar_prefetch=2,
            grid=(T // tm,),
            in_specs=[pl.BlockSpec((tm, D), lambda i, d, z: (i, 0))],
            out_specs=pl.BlockSpec(memory_space=pl.ANY),
            scratch_shapes=[pltpu.VMEM((MOE_TM, D), F32), pltpu.SemaphoreType.DMA(()),
                            pltpu.SemaphoreType.DMA(())],
        ),
        out_shape=jax.ShapeDtypeStruct((n_rows, D), F32),
        compiler_params=_cparams(("arbitrary",)),
        name=name,
    )(dest_flat, zrows, xn)


def _experts_kernel(ue_ref, r0_ref, nt_ref, na_ref, x_hbm, wg_ref, wu_ref, bg_ref, bu_ref, wd_ref, bd_ref,
                    o_hbm, xs_ref, xb_ref, acc_ref, sem_in, sem_out, *, TM, UT, NF):
    assert UT <= NF
    u = pl.program_id(0)
    f = pl.program_id(1)
    na = na_ref[0]
    nt = nt_ref[u]
    r0 = r0_ref[u]
    slot = u % 2
    un = jnp.minimum(u + 1, na - 1)
    prefetch = jnp.logical_and(u + 1 < na, f < nt_ref[un])

    def x_copy(base, i):
        rows = pl.ds(pl.multiple_of(base + i * TM, TM), TM)
        return pltpu.make_async_copy(x_hbm.at[rows], xs_ref.at[i % 2], sem_in.at[i % 2])

    def o_copy(base, i):
        rows = pl.ds(pl.multiple_of(base + i * TM, TM), TM)
        return pltpu.make_async_copy(acc_ref.at[i], o_hbm.at[rows], sem_out.at[i])

    def wait_outputs(unit):
        for i in range(UT):
            @pl.when(i < nt_ref[unit])
            def _():
                o_copy(r0_ref[unit], i).wait()

    @pl.when(u < na)
    def _():
        @pl.when(jnp.logical_and(f == 0, u == 0))
        def _():
            x_copy(r0, 0).start()
            for i in range(UT):
                @pl.when(i < nt)
                def _():
                    if i + 1 < UT:
                        @pl.when(i + 1 < nt)
                        def _():
                            x_copy(r0, i + 1).start()
                    x_copy(r0, i).wait()
                    xb_ref[0, i] = xs_ref[i % 2].astype(BF16)

        @pl.when(jnp.logical_and(f == 0, u > 0))
        def _():
            wait_outputs(u - 1)

        for j in range(UT):
            @pl.when(jnp.logical_and(prefetch, f == j))
            def _():
                x_copy(r0_ref[un], j).start()

        bg = bg_ref[...]
        bu = bu_ref[...]

        def tile_out(xt):
            g = _dot(xt, wg_ref[...].astype(BF16)) + bg
            up = _dot(xt, wu_ref[...].astype(BF16)) + bu
            gate = jnp.minimum(g, SWIGLU_LIMIT)
            lin = jnp.clip(up, -SWIGLU_LIMIT, SWIGLU_LIMIT)
            act = (gate * jax.nn.sigmoid(SWIGLU_ALPHA * gate) * (lin + 1.0)).astype(BF16)
            return _dot(act, wd_ref[...].astype(BF16))

        D = xb_ref.shape[3]

        def update(i, n, first):
            tiles = pl.ds(i, n)
            out = tile_out(xb_ref[slot, tiles].reshape(n * TM, D))
            if first:
                out = out + bd_ref[...]
            else:
                out = out + acc_ref[tiles].reshape(n * TM, D)
            acc_ref[tiles] = out.reshape(n, TM, D)

        def all_tiles(first):
            def body(j, c):
                update(pl.multiple_of(4 * j, 4), 4, first)
                return c
            lax.fori_loop(0, nt // 4, body, 0)
            done4 = (nt // 4) * 4

            @pl.when((nt & 2) != 0)
            def _():
                update(pl.multiple_of(done4, 2), 2, first)

            @pl.when((nt & 1) != 0)
            def _():
                update(nt - 1, 1, first)

        @pl.when(f == 0)
        def _():
            all_tiles(True)

        @pl.when(f > 0)
        def _():
            all_tiles(False)

        for j in range(UT):
            @pl.when(jnp.logical_and(prefetch, f == j))
            def _():
                x_copy(r0_ref[un], j).wait()
                xb_ref[1 - slot, j] = xs_ref[j % 2].astype(BF16)

        @pl.when(f == NF - 1)
        def _():
            for i in range(UT):
                @pl.when(i < nt)
                def _():
                    o_copy(r0, i).start()

            @pl.when(u == na - 1)
            def _():
                wait_outputs(u)


def moe_experts(x_buf, plan, w_gate_up, b_gate_up, w_down, b_down, name="moe_experts"):
    R, D = x_buf.shape
    E, _, F2 = w_gate_up.shape
    DF = F2 // 2
    TM, UT, TF = MOE_TM, MOE_UNIT_TILES, MOE_TF
    NF = DF // TF
    ue, r0, nt, na = plan
    U = ue.shape[0]

    def fsel(u, f, na_ref):
        return jnp.where(u < na_ref[0], f, NF - 1)

    kern = functools.partial(_experts_kernel, TM=TM, UT=UT, NF=NF)
    return pl.pallas_call(
        kern,
        grid_spec=pltpu.PrefetchScalarGridSpec(
            num_scalar_prefetch=4,
            grid=(U, NF),
            in_specs=[
                pl.BlockSpec(memory_space=pl.ANY),
                pl.BlockSpec((None, D, TF), lambda u, f, ue, r0, nt, na: (ue[u], 0, fsel(u, f, na))),
                pl.BlockSpec((None, D, TF), lambda u, f, ue, r0, nt, na: (ue[u], 0, NF + fsel(u, f, na))),
                pl.BlockSpec((None, 1, TF), lambda u, f, ue, r0, nt, na: (ue[u], 0, fsel(u, f, na))),
                pl.BlockSpec((None, 1, TF), lambda u, f, ue, r0, nt, na: (ue[u], 0, NF + fsel(u, f, na))),
                pl.BlockSpec((None, TF, D), lambda u, f, ue, r0, nt, na: (ue[u], fsel(u, f, na), 0)),
                pl.BlockSpec((None, 1, D), lambda u, f, ue, r0, nt, na: (ue[u], 0, 0)),
            ],
            out_specs=pl.BlockSpec(memory_space=pl.ANY),
            scratch_shapes=[
                pltpu.VMEM((2, TM, D), F32),
                pltpu.VMEM((2, UT, TM, D), BF16),
                pltpu.VMEM((UT, TM, D), F32),
                pltpu.SemaphoreType.DMA((2,)),
                pltpu.SemaphoreType.DMA((UT,)),
            ],
        ),
        out_shape=jax.ShapeDtypeStruct((R, D), F32),
        input_output_aliases={4: 0},
        compiler_params=_cparams(("arbitrary", "arbitrary")),
        name=name,
    )(ue, r0, nt, na, x_buf, w_gate_up, w_gate_up, b_gate_up.reshape(E, 1, F2), b_gate_up.reshape(E, 1, F2),
      w_down, b_down.reshape(E, 1, D))


def moe_plan(counts, n_tokens):
    TM, UT = MOE_TM, MOE_UNIT_TILES
    E = counts.shape[0]
    nt = (counts + TM - 1) // TM
    padded = nt * TM
    pad_start = jnp.cumsum(padded) - padded
    nu = (nt + UT - 1) // UT
    cu = jnp.cumsum(nu)
    n_units = cu[-1]
    U = E + (n_tokens * TOP_K) // (TM * UT)
    u = jnp.arange(U, dtype=jnp.int32)
    uc = jnp.minimum(u, n_units - 1)
    ue = jnp.clip(jnp.searchsorted(cu, uc, side='right'), 0, E - 1).astype(jnp.int32)
    j = uc - (cu[ue] - nu[ue])
    r0 = (pad_start[ue] + j * (UT * TM)).astype(jnp.int32)
    ntl = jnp.where(u < n_units, jnp.clip(nt[ue] - j * UT, 0, UT), 0).astype(jnp.int32)
    zstart = (pad_start + counts) // 8 * 8
    pad_end = pad_start + padded
    zrows = jnp.concatenate([zstart, pad_end - zstart, pad_end[-1:]]).astype(jnp.int32)
    return pad_start, zrows, (ue, r0, ntl, n_units.reshape(1).astype(jnp.int32))


def _combine_kernel(dest_ref, h_ref, route_ref, nw_ref, o_hbm, h2_ref, xn_ref, g_ref, sem, *, tm):
    i = pl.program_id(0)

    def start(rb, c):
        for j in range(ROW_UNROLL):
            r = rb * ROW_UNROLL + j
            for k in range(TOP_K):
                d = dest_ref[(i * tm + r) * TOP_K + k]
                pltpu.make_async_copy(o_hbm.at[pl.ds(d, 1)], g_ref.at[k, pl.ds(r, 1)], sem).start(priority=k % 2)
        return c

    lax.fori_loop(0, tm // ROW_UNROLL, start, 0)
    for k in range(TOP_K):
        pltpu.make_async_copy(o_hbm.at[pl.ds(0, tm)], g_ref.at[k], sem).wait()

    route = route_ref[...]
    y = g_ref[0] * route[:, 2 * TOP_K:2 * TOP_K + 1]
    for k in range(1, TOP_K):
        y = y + g_ref[k] * route[:, 2 * TOP_K + k:2 * TOP_K + k + 1]
    h2 = h_ref[...] + y
    h2_ref[...] = h2
    ms = jnp.mean(h2 * h2, axis=-1, keepdims=True)
    xn_ref[...] = (h2 * lax.rsqrt(ms + EPS) * nw_ref[...]).astype(xn_ref.dtype)


def moe_combine(h, route, dest_flat, out_buf, norm_w, tm=256, name="moe_combine"):
    T, D = h.shape
    kern = functools.partial(_combine_kernel, tm=tm)
    return pl.pallas_call(
        kern,
        grid_spec=pltpu.PrefetchScalarGridSpec(
            num_scalar_prefetch=1,
            grid=(T // tm,),
            in_specs=[pl.BlockSpec((tm, D), lambda i, d: (i, 0)),
                      pl.BlockSpec((tm, LANES), lambda i, d: (i, 0)),
                      pl.BlockSpec((1, D), lambda i, d: (0, 0)),
                      pl.BlockSpec(memory_space=pl.ANY)],
            out_specs=[pl.BlockSpec((tm, D), lambda i, d: (i, 0)),
                       pl.BlockSpec((tm, D), lambda i, d: (i, 0))],
            scratch_shapes=[pltpu.VMEM((TOP_K, tm, D), F32), pltpu.SemaphoreType.DMA(())],
        ),
        out_shape=[jax.ShapeDtypeStruct((T, D), F32), jax.ShapeDtypeStruct((T, D), BF16)],
        compiler_params=_cparams(("arbitrary",)),
        name=name,
    )(dest_flat, h, route, norm_w.reshape(1, D), out_buf)


def _ple_kernel(xn_ref, p1_ref, p2_ref, h_ref, wg_ref, wp_ref, o_ref, wgb_ref, wpb_ref, *, nfirst):
    i = pl.program_id(1)

    @pl.when(i == 0)
    def _():
        wgb_ref[...] = wg_ref[...].astype(BF16)
        wpb_ref[...] = wp_ref[...].astype(BF16)

    gate = jax.nn.sigmoid(_dot(xn_ref[...], wgb_ref[...]))
    p = _pick(i, nfirst, p1_ref, p2_ref).astype(BF16)
    o_ref[...] = h_ref[...] + _dot(p, wpb_ref[...]) * gate


def ple(xn, p1, p2, h, w_gate, w_ple, tm=1024, tn=512, name="ple"):
    T, D = h.shape
    P = p1.shape[1]
    na = p1.shape[0] // tm
    return pl.pallas_call(
        functools.partial(_ple_kernel, nfirst=na),
        grid=(D // tn, T // tm),
        in_specs=[
            pl.BlockSpec((tm, D), lambda j, i: (i, 0)),
            *_cat_specs((tm, P), na, row_axis=1),
            pl.BlockSpec((tm, tn), lambda j, i: (i, j)),
            pl.BlockSpec((D, tn), lambda j, i: (0, j)),
            pl.BlockSpec((P, tn), lambda j, i: (0, j)),
        ],
        out_specs=pl.BlockSpec((tm, tn), lambda j, i: (i, j)),
        out_shape=jax.ShapeDtypeStruct((T, D), F32),
        scratch_shapes=[pltpu.VMEM((D, tn), BF16), pltpu.VMEM((P, tn), BF16)],
        compiler_params=_cparams(("arbitrary", "arbitrary")),
        name=name,
    )(xn, p1, p2, h, w_gate, w_ple)


def kernel(x_prompt, x_sample, p_prompt, p_sample, state_hgrn, state_ssm_re, state_ssm_im, norm_mix, w_in, hg_lb, hg_gnorm, w_branch_a, ssm_a_re, ssm_a_im, ssm_log_dt, ssm_b_re, ssm_b_im, ssm_c_re, ssm_c_im, ssm_d, w_glu, w_branch_b, w_out, norm_moe, w_router, b_router, w_gate_up, b_gate_up, w_down, b_down, norm_ple, w_ple, w_ple_gate, norm_final):
    BP, LP, D = x_prompt.shape
    BS, LS, _ = x_sample.shape
    depth = w_in.shape[0]
    assert depth == 1
    TP, TS = BP * LP, BS * LS
    T = TP + TS
    G, P = ssm_a_re.shape[1:]
    KD = HG_HEADS * HG_DK
    W = SSM_GROUP * G

    xp = x_prompt.reshape(TP, D)
    xs = x_sample.reshape(TS, D)

    xn = rmsnorm_cat(xp, xs, norm_mix[0], BF16, name="norm_mix")
    tm = 1024
    w_in0 = w_in[0]
    z_hg = project(xn, w_in0, 0, T // tm, 0, 4, tm, 1024, "proj_hgrn")
    u_p = project(xn, w_in0, 0, TP // tm, 4, 1, tm, 1024, "proj_u_prompt")
    u_s = project(xn, w_in0, TP // tm, TS // tm, 4, 1, tm, 1024, "proj_u_sample")
    z_gate = project(xn, w_in0, 0, T // tm, 5, 4, tm, 1024, "proj_gates")

    lb = jax.nn.softmax(hg_lb.astype(F32), axis=0)[0]
    og_p, hg_p = hgrn2(z_hg, lb, hg_gnorm[0], None, row0=0, B=BP, L=LP, C=HG_CHUNK, nb=1, tok_step=256,
                       name="hgrn_prompt")
    og_s, hg_s = hgrn2(z_hg, lb, hg_gnorm[0], state_hgrn[0], row0=TP, B=BS, L=LS, C=LS, nb=4, tok_step=LS,
                       name="hgrn_sample")

    bd, cd, ab = s5_params(ssm_a_re[0], ssm_a_im[0], ssm_log_dt[0], ssm_b_re[0], ssm_b_im[0],
                           ssm_c_re[0], ssm_c_im[0])
    y_p, re_p, im_p = s5_prompt(u_p.reshape(BP, LP, W), bd, cd, ab, ssm_d[0], TC=256, name="s5_prompt")
    y_s, re_s, im_s = s5_sample(u_s, bd, cd, ab, ssm_d[0], state_ssm_re[0].reshape(BS, G * P),
                                state_ssm_im[0].reshape(BS, G * P), B=BS, L=LS, name="s5_sample")

    glu = gelu_glu(y_p.reshape(TP, W), y_s, w_glu[0])
    merged = gated_merge(og_p, og_s, glu, z_gate, w_branch_a[0], w_branch_b[0])
    h1 = resid_matmul(merged, xp, xs, w_out[0], name="out_proj")

    xn2, route, cnt = moe_router(h1, norm_moe[0], w_router[0], b_router[0])
    e_idx = route[:, 0:TOP_K].astype(jnp.int32)
    rank = route[:, TOP_K:2 * TOP_K].astype(jnp.int32)
    counts = cnt[0, :N_EXPERTS].astype(jnp.int32)
    pad_start, zrows, plan = moe_plan(counts, T)
    dest = (pad_start[e_idx] + rank).astype(jnp.int32).reshape(T * TOP_K)
    n_rows = T * TOP_K + N_EXPERTS * MOE_TM
    x_buf = moe_dispatch(xn2, dest, zrows, n_rows)
    out_buf = moe_experts(x_buf, plan, w_gate_up[0], b_gate_up[0], w_down[0], b_down[0])
    h2, xn3 = moe_combine(h1, route, dest, out_buf, norm_ple[0])

    h3 = ple(xn3, p_prompt[0].reshape(TP, -1), p_sample[0].reshape(TS, -1), h2, w_ple_gate[0], w_ple[0])
    y_p2, y_s2 = rmsnorm_split(h3, norm_final, TP, name="norm_final")

    y_prompt = y_p2.reshape(BP, LP, D)
    y_sample = y_s2.reshape(BS, LS, D)
    hs = (1, BP, HG_HEADS, HG_DK, HG_DV)
    return (y_prompt, y_sample,
            hg_p.reshape(hs), re_p.reshape(1, BP, G, P), im_p.reshape(1, BP, G, P),
            hg_s.reshape(1, BS, HG_HEADS, HG_DK, HG_DV), re_s.reshape(1, BS, G, P), im_s.reshape(1, BS, G, P))
```

```python
import functools

import jax
import jax.numpy as jnp
from jax import lax
from jax.experimental import pallas as pl
from jax.experimental.pallas import tpu as pltpu

F32 = jnp.float32
BF16 = jnp.bfloat16
EPS = 1e-6

HG_HEADS = 8
HG_DK = 128
HG_DV = 128
HG_CHUNK = 32
SSM_GROUP = 16
SSM_STATE = 64
N_EXPERTS = 32
TOP_K = 4
SWIGLU_LIMIT = 7.0
SWIGLU_ALPHA = 1.702

LANES = 128
VMEM_LIMIT_BYTES = 56 * 1024 * 1024
VMEM_LIMIT_EXPERTS_BYTES = 61 * 1024 * 1024

MOE_TM = 256
MOE_UNIT_TILES = 6
MOE_TF = 512
NEG_BIG = -1e30


def _cparams(sem, vmem_limit=VMEM_LIMIT_BYTES):
    return pltpu.CompilerParams(dimension_semantics=sem, vmem_limit_bytes=vmem_limit)


def _dot(a, b):
    return jnp.dot(a, b, preferred_element_type=F32)


def _split2(a):
    hi = a.astype(BF16)
    lo = (a - hi.astype(F32)).astype(BF16)
    return hi, lo


def _dot_x3(a, b_split):
    a1, a2 = _split2(a)
    b1, b2 = b_split
    return _dot(a1, b1) + (_dot(a1, b2) + _dot(a2, b1))


def _cat_specs(shape, nfirst, row_axis=0, col_axis=None):
    def col(ids):
        return 0 if col_axis is None else ids[col_axis]

    first = pl.BlockSpec(shape, lambda *ids: (jnp.minimum(ids[row_axis], nfirst - 1), col(ids)))
    second = pl.BlockSpec(shape, lambda *ids: (jnp.maximum(ids[row_axis] - nfirst, 0), col(ids)))
    return first, second


def _pick(i, nfirst, a_ref, b_ref):
    return jnp.where(i < nfirst, a_ref[...], b_ref[...])


def _rms(x, w):
    ms = jnp.mean(x * x, axis=-1, keepdims=True)
    return x * lax.rsqrt(ms + EPS) * w


def _rms_in2_kernel(a_ref, b_ref, w_ref, o_ref, *, nfirst):
    x = _pick(pl.program_id(0), nfirst, a_ref, b_ref)
    o_ref[...] = _rms(x, w_ref[...]).astype(o_ref.dtype)


def rmsnorm_cat(xa, xb, w, out_dtype, tm=512, name="rmsnorm"):
    (Ta, D), Tb = xa.shape, xb.shape[0]
    na = Ta // tm
    return pl.pallas_call(
        functools.partial(_rms_in2_kernel, nfirst=na),
        grid=((Ta + Tb) // tm,),
        in_specs=[*_cat_specs((tm, D), na), pl.BlockSpec((1, D), lambda i: (0, 0))],
        out_specs=pl.BlockSpec((tm, D), lambda i: (i, 0)),
        out_shape=jax.ShapeDtypeStruct((Ta + Tb, D), out_dtype),
        compiler_params=_cparams(("arbitrary",)),
        name=name,
    )(xa, xb, w.reshape(1, D))


def _rms_out2_kernel(x_ref, w_ref, oa_ref, ob_ref, *, nfirst):
    i = pl.program_id(0)
    y = _rms(x_ref[...], w_ref[...])

    @pl.when(i < nfirst)
    def _():
        oa_ref[...] = y

    @pl.when(i >= nfirst)
    def _():
        ob_ref[...] = y


def rmsnorm_split(x, w, Ta, tm=512, name="rmsnorm_split"):
    T, D = x.shape
    na = Ta // tm
    return pl.pallas_call(
        functools.partial(_rms_out2_kernel, nfirst=na),
        grid=(T // tm,),
        in_specs=[pl.BlockSpec((tm, D), lambda i: (i, 0)), pl.BlockSpec((1, D), lambda i: (0, 0))],
        out_specs=list(_cat_specs((tm, D), na)),
        out_shape=[jax.ShapeDtypeStruct((Ta, D), F32), jax.ShapeDtypeStruct((T - Ta, D), F32)],
        compiler_params=_cparams(("arbitrary",)),
        name=name,
    )(x, w.reshape(1, D))


def _proj_kernel(x_ref, w_ref, o_ref, wb_ref):
    @pl.when(pl.program_id(1) == 0)
    def _():
        wb_ref[...] = w_ref[...].astype(BF16)

    o_ref[...] = _dot(x_ref[...], wb_ref[...]).astype(o_ref.dtype)


def project(xn, w, row_blk0, n_row_blks, col_blk0, n_col_blks, tm, tn, name):
    K = xn.shape[1]
    return pl.pallas_call(
        _proj_kernel,
        grid=(n_col_blks, n_row_blks),
        in_specs=[
            pl.BlockSpec((tm, K), lambda j, i: (row_blk0 + i, 0)),
            pl.BlockSpec((K, tn), lambda j, i: (0, col_blk0 + j)),
        ],
        out_specs=pl.BlockSpec((tm, tn), lambda j, i: (i, j)),
        out_shape=jax.ShapeDtypeStruct((n_row_blks * tm, n_col_blks * tn), F32),
        scratch_shapes=[pltpu.VMEM((K, tn), BF16)],
        compiler_params=_cparams(("arbitrary", "arbitrary")),
        name=name,
    )(xn, w)


HGRN_CHUNKS_PER_TRIP = 4


def _hgrn_kernel(*refs, C, nchunk, nb, tok_step, has_s0, state_t):
    if has_s0:
        q_ref, f_ref, v_ref, g_ref, lb_ref, gn_ref, s0_ref, og_ref, so_ref, st_ref = refs
    else:
        q_ref, f_ref, v_ref, g_ref, lb_ref, gn_ref, og_ref, so_ref, st_ref = refs
        s0_ref = None
    H, DK, DV = HG_HEADS, HG_DK, HG_DV
    step = pl.program_id(1)

    @pl.when(step == 0)
    def _():
        for i in range(nb):
            for h in range(H):
                if has_s0:
                    st_ref[i, h] = s0_ref[i, h].T if state_t else s0_ref[i, h]
                else:
                    st_ref[i, h] = jnp.zeros(st_ref.shape[2:], F32)

    lbv = lb_ref[...]
    gnv = gn_ref[...]
    rr = lax.broadcasted_iota(jnp.int32, (C, C), 0)
    cc = lax.broadcasted_iota(jnp.int32, (C, C), 1)
    causal = rr >= cc
    tri = causal.astype(BF16)
    sls =[slice(h * DK, (h + 1) * DK) for h in range(H)]
    nt_dims = (((1,), (1,)), ((), ()))
    tn_dims = (((0,), (0,)), ((), ()))

    def tn(a, b):
        return lax.dot_general(a, b, tn_dims, preferred_element_type=F32)

    def prepare(r0):
        rows = pl.ds(r0, C)
        zq = q_ref[rows, :]
        zf = f_ref[rows, :]
        q = jax.nn.silu(zq)
        fe = lbv + (1.0 - lbv) * jax.nn.sigmoid(zf)
        k = 1.0 - fe
        gl = jnp.log(fe)
        g1 = gl.astype(BF16)
        r1 = gl - g1.astype(F32)
        g2 = r1.astype(BF16)
        g3 = (r1 - g2.astype(F32)).astype(BF16)
        b = _dot(tri, g1) + (_dot(tri, g2) + _dot(tri, g3))
        blast = b[C - 1:C, :]
        return dict(
            qt=(q * jnp.exp(b)).astype(BF16),
            kt=(k * jnp.exp(-b)).astype(BF16),
            kend=(k * jnp.exp(blast - b)).astype(BF16),
            eblast=jnp.exp(blast),
            vb=v_ref[rows, :].astype(BF16),
            gate=jax.nn.silu(g_ref[rows, :]),
        )

    def run(jobs):
        P = [prepare(r0) for _, r0 in jobs]
        atts = [[lax.dot_general(p["qt"][:, sl], p["kt"][:, sl], nt_dims, preferred_element_type=F32)
                 for sl in sls] for p in P]
        if state_t:
            upds = [[tn(p["vb"][:, sl], p["kend"][:, sl]) for sl in sls] for p in P]
            decs = [[p["eblast"][:, sl] for sl in sls] for p in P]
        else:
            upds = [[tn(p["kend"][:, sl], p["vb"][:, sl]) for sl in sls] for p in P]
            decs = [[jnp.broadcast_to(p["eblast"][:, sl], (8, DK)).T[:, 0:1] for sl in sls] for p in P]
        state = {}
        inters = []
        for j, (i, _) in enumerate(jobs):
            row = []
            for h, sl in enumerate(sls):
                s = state[(i, h)] if (i, h) in state else st_ref[i, h]
                qth = P[j]["qt"][:, sl]
                if state_t:
                    row.append(lax.dot_general(qth, s.astype(BF16), nt_dims, preferred_element_type=F32))
                else:
                    row.append(_dot(qth, s.astype(BF16)))
                state[(i, h)] = s * decs[j][h] + upds[j][h]
            inters.append(row)
        for (i, h), s in state.items():
            st_ref[i, h] = s
        outs = []
        for j, p in enumerate(P):
            cols = []
            for h, sl in enumerate(sls):
                att = jnp.where(causal, atts[j][h], 0.0).astype(BF16)
                o = _dot(att, p["vb"][:, sl]) + inters[j][h]
                ms = jnp.mean(o * o, axis=-1, keepdims=True)
                cols.append(o * lax.rsqrt(ms + EPS) * gnv * p["gate"][:, sl])
            outs.append(jnp.concatenate(cols, axis=1))
        return outs

    if nchunk == 1:
        outs = run([(i, i * tok_step) for i in range(nb)])
        og_ref[...] = jnp.concatenate(outs, axis=0).astype(og_ref.dtype)
    else:
        assert nb == 1 and nchunk % HGRN_CHUNKS_PER_TRIP == 0

        def body(t, carry):
            r0s = [pl.multiple_of((HGRN_CHUNKS_PER_TRIP * t + j) * C, C) for j in range(HGRN_CHUNKS_PER_TRIP)]
            outs = run([(0, r0) for r0 in r0s])
            for r0, o in zip(r0s, outs):
                og_ref[pl.ds(r0, C), :] = o.astype(og_ref.dtype)
            return carry

        lax.fori_loop(0, nchunk // HGRN_CHUNKS_PER_TRIP, body, 0)

    @pl.when(step == pl.num_programs(1) - 1)
    def _():
        for i in range(nb):
            for h in range(H):
                so_ref[i, h] = st_ref[i, h].T if state_t else st_ref[i, h]


def hgrn2(z_hg, lb, gn, s0, *, row0, B, L, C, nb, tok_step, name):
    H, DK, DV = HG_HEADS, HG_DK, HG_DV
    W = H * DK
    steps = L // tok_step
    rows = nb * tok_step
    nchunk = tok_step // C
    blk0 = row0 // rows
    has_s0 = s0 is not None

    def zspec(col):
        return pl.BlockSpec((rows, W), lambda bb, s: (blk0 + bb * steps + s, col))

    in_specs = [zspec(0), zspec(1), zspec(2), zspec(3),
                pl.BlockSpec((1, W), lambda bb, s: (0, 0)),
                pl.BlockSpec((1, DV), lambda bb, s: (0, 0))]
    args = [z_hg, z_hg, z_hg, z_hg, lb.reshape(1, W), gn.reshape(1, DV)]
    if has_s0:
        in_specs.append(pl.BlockSpec((nb, H, DK, DV), lambda bb, s: (bb, 0, 0, 0)))
        args.append(s0)
    state_t = L > C
    kern = functools.partial(_hgrn_kernel, C=C, nchunk=nchunk, nb=nb, tok_step=tok_step, has_s0=has_s0,
                             state_t=state_t)
    return pl.pallas_call(
        kern,
        grid=(B // nb, steps),
        in_specs=in_specs,
        out_specs=[pl.BlockSpec((rows, W), lambda bb, s: (bb * steps + s, 0)),
                   pl.BlockSpec((nb, H, DK, DV), lambda bb, s: (bb, 0, 0, 0))],
        out_shape=[jax.ShapeDtypeStruct((B * L, W), BF16),
                   jax.ShapeDtypeStruct((B, H, DK, DV), F32)],
        scratch_shapes=[pltpu.VMEM((nb, H, DV, DK), F32)],
        compiler_params=_cparams(("arbitrary", "arbitrary")),
        name=name,
    )(*args)


def _s5_step(ar, ai, xr, xi, br, bi):
    return ar * xr - ai * xi + br, ar * xi + ai * xr + bi


S5_STEPS_PER_TRIP = 4


def _s5_prompt_kernel(u_ref, bd_ref, cd_ref, a_ref, d_ref, y_ref, xr_ref, xi_ref, X_ref, st_ref, *, B, TC):
    NP = X_ref.shape[0] // 2
    assert 2 * B == 8
    tc = pl.program_id(1)

    def rows_of(b, s):
        return pl.ds(2 * b + s, TC, stride=8)

    @pl.when(tc == 0)
    def _():
        st_ref[...] = jnp.zeros_like(st_ref)

    bd = bd_ref[...].astype(BF16)
    for b in range(B):
        bu = _dot(u_ref[b].astype(BF16), bd)
        for c in range(4 * NP):
            X_ref[c // 2, rows_of(b, c % 2), :] = bu[:, c * LANES:(c + 1) * LANES]

    odd = lax.broadcasted_iota(jnp.int32, (8, LANES), 0) % 2 == 1

    def pair_const(row, p):
        lo = a_ref[row:row + 1, (2 * p) * LANES:(2 * p + 1) * LANES]
        hi = a_ref[row:row + 1, (2 * p + 1) * LANES:(2 * p + 2) * LANES]
        return jnp.where(odd, hi, lo)

    ars = [pair_const(0, p) for p in range(NP)]
    ais = [pair_const(1, p) for p in range(NP)]

    def body(i, carry):
        carry = list(carry)
        for j in range(S5_STEPS_PER_TRIP):
            rows = pl.ds(pl.multiple_of((i * S5_STEPS_PER_TRIP + j) * 8, 8), 8)
            for p in range(NP):
                xr, xi = _s5_step(ars[p], ais[p], carry[p][0], carry[p][1], X_ref[p, rows, :], X_ref[NP + p, rows, :])
                X_ref[p, rows, :] = xr
                X_ref[NP + p, rows, :] = xi
                carry[p] = (xr, xi)
        return tuple(carry)

    init = tuple((st_ref[p], st_ref[NP + p]) for p in range(NP))
    fin = lax.fori_loop(0, TC // S5_STEPS_PER_TRIP, body, init)
    for p in range(NP):
        st_ref[p] = fin[p][0]
        st_ref[NP + p] = fin[p][1]

    cd = cd_ref[...].astype(BF16)
    dv = d_ref[...]
    for b in range(B):
        xb = jnp.concatenate([X_ref[c // 2, rows_of(b, c % 2), :] for c in range(4 * NP)], axis=1).astype(BF16)
        y_ref[b] = _dot(xb, cd) + dv * u_ref[b]

    @pl.when(tc == pl.num_programs(1) - 1)
    def _():
        for c in range(2 * NP):
            cols = slice(c * LANES, (c + 1) * LANES)
            xr_ref[:, cols] = st_ref[c // 2, pl.ds(c % 2, B, stride=2), :]
            xi_ref[:, cols] = st_ref[NP + c // 2, pl.ds(c % 2, B, stride=2), :]


def s5_prompt(u, bd, cd, ab, dskip, *, TC, name):
    B, L, W = u.shape
    GB = W // LANES
    SW = bd.shape[2]
    half = SW // 2
    kern = functools.partial(_s5_prompt_kernel, B=B, TC=TC)
    return pl.pallas_call(
        kern,
        grid=(GB, L // TC),
        in_specs=[
            pl.BlockSpec((B, TC, LANES), lambda g, t: (0, t, g)),
            pl.BlockSpec((None, LANES, SW), lambda g, t: (g, 0, 0)),
            pl.BlockSpec((None, SW, LANES), lambda g, t: (g, 0, 0)),
            pl.BlockSpec((None, 2, half), lambda g, t: (g, 0, 0)),
            pl.BlockSpec((1, LANES), lambda g, t: (0, g)),
        ],
        out_specs=[
            pl.BlockSpec((B, TC, LANES), lambda g, t: (0, t, g)),
            pl.BlockSpec((B, half), lambda g, t: (0, g)),
            pl.BlockSpec((B, half), lambda g, t: (0, g)),
        ],
        out_shape=[jax.ShapeDtypeStruct((B, L, W), F32),
                   jax.ShapeDtypeStruct((B, GB * half), F32),
                   jax.ShapeDtypeStruct((B, GB * half), F32)],
        scratch_shapes=[pltpu.VMEM((SW // LANES // 2, 8 * TC, LANES), F32),
                        pltpu.VMEM((SW // LANES // 2, 8, LANES), F32)],
        compiler_params=_cparams(("arbitrary", "arbitrary")),
        name=name,
    )(u, bd, cd, ab, dskip.reshape(1, W))


def _s5_sample_kernel(u_ref, bd_ref, cd_ref, a_ref, d_ref, x0r_ref, x0i_ref, y_ref, xr_ref, xi_ref, X_ref, *, B, L):
    NC = X_ref.shape[0]
    NH = NC // 2
    u = u_ref[...]
    bu = _dot_x3(u, _split2(bd_ref[...]))
    for c in range(NC):
        X_ref[c] = bu[:, c * LANES:(c + 1) * LANES]
    for c in range(NH):
        cols = slice(c * LANES, (c + 1) * LANES)
        ar = a_ref[0:1, cols]
        ai = a_ref[1:2, cols]
        xr = x0r_ref[:, cols]
        xi = x0i_ref[:, cols]
        for t in range(L):
            rows = pl.ds(t, B, stride=L)
            xr, xi = _s5_step(ar, ai, xr, xi, X_ref[c, rows, :], X_ref[NH + c, rows, :])
            X_ref[c, rows, :] = xr
            X_ref[NH + c, rows, :] = xi
        xr_ref[:, cols] = xr
        xi_ref[:, cols] = xi
    xs = jnp.concatenate([X_ref[c] for c in range(NC)], axis=1).astype(BF16)
    y_ref[...] = _dot(xs, cd_ref[...].astype(BF16)) + d_ref[...] * u


def s5_sample(u, bd, cd, ab, dskip, x0r, x0i, *, B, L, name):
    T, W = u.shape
    GB = W // LANES
    SW = bd.shape[2]
    half = SW // 2
    kern = functools.partial(_s5_sample_kernel, B=B, L=L)
    return pl.pallas_call(
        kern,
        grid=(GB,),
        in_specs=[
            pl.BlockSpec((T, LANES), lambda g: (0, g)),
            pl.BlockSpec((None, LANES, SW), lambda g: (g, 0, 0)),
            pl.BlockSpec((None, SW, LANES), lambda g: (g, 0, 0)),
            pl.BlockSpec((None, 2, half), lambda g: (g, 0, 0)),
            pl.BlockSpec((1, LANES), lambda g: (0, g)),
            pl.BlockSpec((B, half), lambda g: (0, g)),
            pl.BlockSpec((B, half), lambda g: (0, g)),
        ],
        out_specs=[
            pl.BlockSpec((T, LANES), lambda g: (0, g)),
            pl.BlockSpec((B, half), lambda g: (0, g)),
            pl.BlockSpec((B, half), lambda g: (0, g)),
        ],
        out_shape=[jax.ShapeDtypeStruct((T, W), F32),
                   jax.ShapeDtypeStruct((B, GB * half), F32),
                   jax.ShapeDtypeStruct((B, GB * half), F32)],
        scratch_shapes=[pltpu.VMEM((SW // LANES, T, LANES), F32)],
        compiler_params=_cparams(("arbitrary",)),
        name=name,
    )(u, bd, cd, ab, dskip.reshape(1, W), x0r, x0i)


def s5_params(a_re, a_im, log_dt, b_re, b_im, c_re, c_im):
    G, P = a_re.shape
    gpb = LANES // SSM_GROUP
    GB = G // gpb
    A = lax.complex(a_re, a_im)
    dt = jnp.exp(log_dt)[:, None]
    A_bar = jnp.exp(A * dt)
    Bm = lax.complex(b_re, b_im)
    B_bar = ((A_bar - 1.0) / A)[..., None] * Bm
    eye = jnp.eye(gpb, dtype=F32)

    def in_map(m):
        m = m.reshape(GB, gpb, P, SSM_GROUP)
        return jnp.einsum('bgpc,gh->bgchp', m, eye).reshape(GB, gpb * SSM_GROUP, gpb * P)

    def out_map(m):
        m = m.reshape(GB, gpb, SSM_GROUP, P)
        return jnp.einsum('bgcp,gh->bgphc', m, eye).reshape(GB, gpb * P, gpb * SSM_GROUP)

    bd = jnp.concatenate([in_map(B_bar.real), in_map(B_bar.imag)], axis=2)
    cd = jnp.concatenate([out_map(c_re), out_map(-c_im)], axis=1)
    ab = jnp.stack([A_bar.real.reshape(GB, gpb * P), A_bar.imag.reshape(GB, gpb * P)], axis=1)
    return bd, cd, ab


def _glu_kernel(ya_ref, yb_ref, w_ref, o_ref, wb_ref, *, nfirst):
    i = pl.program_id(0)

    @pl.when(i == 0)
    def _():
        wb_ref[...] = w_ref[...].astype(BF16)

    zg = jax.nn.gelu(_pick(i, nfirst, ya_ref, yb_ref))
    o_ref[...] = (zg * jax.nn.sigmoid(_dot(zg.astype(BF16), wb_ref[...]))).astype(o_ref.dtype)


def gelu_glu(ya, yb, w, tm=512, name="gelu_glu"):
    (Ta, W), Tb = ya.shape, yb.shape[0]
    na = Ta // tm
    return pl.pallas_call(
        functools.partial(_glu_kernel, nfirst=na),
        grid=((Ta + Tb) // tm,),
        in_specs=[*_cat_specs((tm, W), na), pl.BlockSpec((W, W), lambda i: (0, 0))],
        out_specs=pl.BlockSpec((tm, W), lambda i: (i, 0)),
        out_shape=jax.ShapeDtypeStruct((Ta + Tb, W), BF16),
        scratch_shapes=[pltpu.VMEM((W, W), BF16)],
        compiler_params=_cparams(("arbitrary",)),
        name=name,
    )(ya, yb, w)


def _merge_kernel(oa1_ref, oa2_ref, ob_ref, ga_ref, gb_ref, wa_ref, wb_ref, o_ref, wab_ref, wbb_ref, *, nfirst):
    i = pl.program_id(1)

    @pl.when(i == 0)
    def _():
        wab_ref[...] = wa_ref[...].astype(BF16)
        wbb_ref[...] = wb_ref[...].astype(BF16)

    ya = _dot(_pick(i, nfirst, oa1_ref, oa2_ref), wab_ref[...])
    yb = _dot(ob_ref[...], wbb_ref[...])
    o_ref[...] = (jax.nn.sigmoid(ga_ref[...]) * ya + jax.nn.sigmoid(gb_ref[...]) * yb).astype(o_ref.dtype)


def gated_merge(oa1, oa2, ob, z_gate, wa, wb, tm=512, tn=1024, name="gated_merge"):
    T, K = ob.shape
    N = wa.shape[1]
    nj = N // tn
    na = oa1.shape[0] // tm
    return pl.pallas_call(
        functools.partial(_merge_kernel, nfirst=na),
        grid=(nj, T // tm),
        in_specs=[
            *_cat_specs((tm, K), na, row_axis=1),
            pl.BlockSpec((tm, K), lambda j, i: (i, 0)),
            pl.BlockSpec((tm, tn), lambda j, i: (i, j)),
            pl.BlockSpec((tm, tn), lambda j, i: (i, nj + j)),
            pl.BlockSpec((K, tn), lambda j, i: (0, j)),
            pl.BlockSpec((K, tn), lambda j, i: (0, j)),
        ],
        out_specs=pl.BlockSpec((tm, tn), lambda j, i: (i, j)),
        out_shape=jax.ShapeDtypeStruct((T, N), BF16),
        scratch_shapes=[pltpu.VMEM((K, tn), BF16), pltpu.VMEM((K, tn), BF16)],
        compiler_params=_cparams(("arbitrary", "arbitrary")),
        name=name,
    )(oa1, oa2, ob, z_gate, z_gate, wa, wb)


def _resid_kernel(x_ref, h1_ref, h2_ref, w_ref, o_ref, wb_ref, *, nfirst):
    i = pl.program_id(1)

    @pl.when(i == 0)
    def _():
        wb_ref[...] = w_ref[...].astype(BF16)

    o_ref[...] = _pick(i, nfirst, h1_ref, h2_ref) + _dot(x_ref[...], wb_ref[...])


def resid_matmul(x, h1, h2, w, tm=1024, tn=1024, name="resid_matmul"):
    T, K = x.shape
    N = w.shape[1]
    na = h1.shape[0] // tm
    return pl.pallas_call(
        functools.partial(_resid_kernel, nfirst=na),
        grid=(N // tn, T // tm),
        in_specs=[
            pl.BlockSpec((tm, K), lambda j, i: (i, 0)),
            *_cat_specs((tm, tn), na, row_axis=1, col_axis=0),
            pl.BlockSpec((K, tn), lambda j, i: (0, j)),
        ],
        out_specs=pl.BlockSpec((tm, tn), lambda j, i: (i, j)),
        out_shape=jax.ShapeDtypeStruct((T, N), F32),
        scratch_shapes=[pltpu.VMEM((K, tn), BF16)],
        compiler_params=_cparams(("arbitrary", "arbitrary")),
        name=name,
    )(x, h1, h2, w)


def _router_kernel(h_ref, nw_ref, wr_ref, br_ref, xn_ref, route_ref, cnt_ref, tri_ref, carry_ref, *, tm):
    i = pl.program_id(0)

    @pl.when(i == 0)
    def _():
        rr = lax.broadcasted_iota(jnp.int32, (tm, tm), 0)
        cc = lax.broadcasted_iota(jnp.int32, (tm, tm), 1)
        tri_ref[...] = (rr > cc).astype(BF16)
        carry_ref[...] = jnp.zeros_like(carry_ref)

    x = h_ref[...]
    ms = jnp.mean(x * x, axis=-1, keepdims=True)
    xn = x * lax.rsqrt(ms + EPS) * nw_ref[...]
    xn_ref[...] = xn
    logits = _dot_x3(xn, _split2(wr_ref[...])) + br_ref[...]
    lane = lax.broadcasted_iota(jnp.int32, (tm, LANES), 1)
    lanef = lane.astype(F32)
    cur = logits
    vals, hots, eids = [], [], []
    for _ in range(TOP_K):
        m = jnp.max(cur, axis=-1, keepdims=True)
        idx = jnp.min(jnp.where(cur == m, lanef, float(LANES)), axis=-1, keepdims=True)
        hot = lanef == idx
        vals.append(m)
        hots.append(hot)
        eids.append(idx)
        cur = jnp.where(hot, -jnp.inf, cur)
    es = [jnp.exp(v - vals[0]) for v in vals]
    den = es[0] + es[1] + es[2] + es[3]
    multi = jnp.zeros((tm, LANES), F32)
    for hot in hots:
        multi = multi + hot.astype(F32)
    base = carry_ref[0:1, :] + _dot(tri_ref[...], multi.astype(BF16))
    route = jnp.zeros((tm, LANES), F32)
    for k in range(TOP_K):
        e_k = eids[k]
        r_k = jnp.sum(jnp.where(hots[k], base, 0.0), axis=-1, keepdims=True)
        w_k = es[k] / den
        route = route + jnp.where(lane == k, e_k, 0.0) + jnp.where(lane == TOP_K + k, r_k, 0.0) \
            + jnp.where(lane == 2 * TOP_K + k, w_k, 0.0)
    route_ref[...] = route
    carry = carry_ref[0:1, :] + jnp.sum(multi, axis=0, keepdims=True)
    carry_ref[...] = jnp.broadcast_to(carry, carry_ref.shape)
    cnt_ref[...] = jnp.broadcast_to(carry, cnt_ref.shape)


def moe_router(h, norm_w, w_router, b_router, tm=512, name="moe_router"):
    T, D = h.shape
    E = w_router.shape[1]
    wr = jnp.pad(w_router, ((0, 0), (0, LANES - E)))
    br = jnp.pad(b_router.reshape(1, E), ((0, 0), (0, LANES - E)), constant_values=NEG_BIG)
    kern = functools.partial(_router_kernel, tm=tm)
    return pl.pallas_call(
        kern,
        grid=(T // tm,),
        in_specs=[
            pl.BlockSpec((tm, D), lambda i: (i, 0)),
            pl.BlockSpec((1, D), lambda i: (0, 0)),
            pl.BlockSpec((D, LANES), lambda i: (0, 0)),
            pl.BlockSpec((1, LANES), lambda i: (0, 0)),
        ],
        out_specs=[
            pl.BlockSpec((tm, D), lambda i: (i, 0)),
            pl.BlockSpec((tm, LANES), lambda i: (i, 0)),
            pl.BlockSpec((8, LANES), lambda i: (0, 0)),
        ],
        out_shape=[jax.ShapeDtypeStruct((T, D), F32),
                   jax.ShapeDtypeStruct((T, LANES), F32),
                   jax.ShapeDtypeStruct((8, LANES), F32)],
        scratch_shapes=[pltpu.VMEM((tm, tm), BF16), pltpu.VMEM((8, LANES), F32)],
        compiler_params=_cparams(("arbitrary",)),
        name=name,
    )(h, norm_w.reshape(1, D), wr, br)


ROW_UNROLL = 8


def _dispatch_kernel(dest_ref, zrow_ref, x_ref, buf_ref, zero_ref, sem, zsem, *, tm, n_tail):
    i = pl.program_id(0)
    E = (zrow_ref.shape[0] - 1) // 2
    ZR = zero_ref.shape[0]
    R = buf_ref.shape[0]

    @pl.when(i == 0)
    def _():
        zero_ref[...] = jnp.zeros_like(zero_ref)

        def zero_copy(row, n):
            return pltpu.make_async_copy(zero_ref.at[pl.ds(0, n)], buf_ref.at[pl.ds(pl.multiple_of(row, 8), n)], zsem)

        def zero_rows(act):
            for e in range(E):
                off = zrow_ref[e]
                n = ZR
                while n >= 8:
                    @pl.when((zrow_ref[E + e] & n) != 0)
                    def _():
                        act(zero_copy(off, n))
                    off = off + (zrow_ref[E + e] & n)
                    n //= 2
            for j in range(n_tail):
                row = zrow_ref[2 * E] + j * ZR

                @pl.when(row < R)
                def _():
                    act(zero_copy(row, ZR))

        zero_rows(lambda c: c.start())
        zero_rows(lambda c: c.wait())

    def start(rb, c):
        for j in range(ROW_UNROLL):
            r = rb * ROW_UNROLL + j
            for k in range(TOP_K):
                d = dest_ref[(i * tm + r) * TOP_K + k]
                pltpu.make_async_copy(x_ref.at[pl.ds(r, 1)], buf_ref.at[pl.ds(d, 1)], sem).start(priority=k % 2)
        return c

    lax.fori_loop(0, tm // ROW_UNROLL, start, 0)
    for k in range(TOP_K):
        pltpu.make_async_copy(x_ref, buf_ref.at[pl.ds(0, tm)], sem).wait()


def moe_dispatch(xn, dest_flat, zrows, n_rows, tm=256, name="moe_dispatch"):
    T, D = xn.shape
    n_tail = (n_rows - T * TOP_K) // MOE_TM
    kern = functools.partial(_dispatch_kernel, tm=tm, n_tail=n_tail)
    return pl.pallas_call(
        kern,
        grid_spec=pltpu.PrefetchScalarGridSpec(
            num_scalar_prefetch=2,
            grid=(T // tm,),
            in_specs=[pl.BlockSpec((tm, D), lambda i, d, z: (i, 0))],
            out_specs=pl.BlockSpec(memory_space=pl.ANY),
            scratch_shapes=[pltpu.VMEM((MOE_TM, D), F32), pltpu.SemaphoreType.DMA(()),
                            pltpu.SemaphoreType.DMA(())],
        ),
        out_shape=jax.ShapeDtypeStruct((n_rows, D), F32),
        compiler_params=_cparams(("arbitrary",)),
        name=name,
    )(dest_flat, zrows, xn)


def _experts_kernel(ue_ref, r0_ref, nt_ref, na_ref, x_hbm, wg_ref, wu_ref, bg_ref, bu_ref, wd_ref, bd_ref,
                    o_hbm, xs_ref, xb_ref, acc_ref, sem_in, sem_out, *, TM, UT, NF):
    TPS = -(-UT // NF)
    assert TPS <= 2
    u = pl.program_id(0)
    f = pl.program_id(1)
    na = na_ref[0]
    nt = nt_ref[u]
    r0 = r0_ref[u]
    slot = u % 2
    un = jnp.minimum(u + 1, na - 1)

    def prefetch(j):
        return jnp.logical_and(jnp.logical_and(u + 1 < na, j < nt_ref[un]), f == j // TPS)

    def x_copy(base, i):
        rows = pl.ds(pl.multiple_of(base + i * TM, TM), TM)
        return pltpu.make_async_copy(x_hbm.at[rows], xs_ref.at[i % 2], sem_in.at[i % 2])

    def o_copy(base, i):
        rows = pl.ds(pl.multiple_of(base + i * TM, TM), TM)
        return pltpu.make_async_copy(acc_ref.at[i], o_hbm.at[rows], sem_out.at[i])

    def wait_outputs(unit):
        for i in range(UT):
            @pl.when(i < nt_ref[unit])
            def _():
                o_copy(r0_ref[unit], i).wait()

    @pl.when(u < na)
    def _():
        @pl.when(jnp.logical_and(f == 0, u == 0))
        def _():
            x_copy(r0, 0).start()
            for i in range(UT):
                @pl.when(i < nt)
                def _():
                    if i + 1 < UT:
                        @pl.when(i + 1 < nt)
                        def _():
                            x_copy(r0, i + 1).start()
                    x_copy(r0, i).wait()
                    xb_ref[0, i] = xs_ref[i % 2].astype(BF16)

        @pl.when(jnp.logical_and(f == 0, u > 0))
        def _():
            wait_outputs(u - 1)

        for j in range(UT):
            @pl.when(prefetch(j))
            def _():
                x_copy(r0_ref[un], j).start()

        bg = bg_ref[...]
        bu = bu_ref[...]

        def tile_out(xt):
            g = _dot(xt, wg_ref[...].astype(BF16)) + bg
            up = _dot(xt, wu_ref[...].astype(BF16)) + bu
            gate = jnp.minimum(g, SWIGLU_LIMIT)
            lin = jnp.clip(up, -SWIGLU_LIMIT, SWIGLU_LIMIT)
            act = (gate * jax.nn.sigmoid(SWIGLU_ALPHA * gate) * (lin + 1.0)).astype(BF16)
            return _dot(act, wd_ref[...].astype(BF16))

        D = xb_ref.shape[3]

        def update(i, n, first):
            tiles = pl.ds(i, n)
            out = tile_out(xb_ref[slot, tiles].reshape(n * TM, D))
            if first:
                out = out + bd_ref[...]
            else:
                out = out + acc_ref[tiles].reshape(n * TM, D)
            acc_ref[tiles] = out.reshape(n, TM, D)

        def all_tiles(first):
            def body(j, c):
                update(pl.multiple_of(4 * j, 4), 4, first)
                return c
            lax.fori_loop(0, nt // 4, body, 0)
            done4 = (nt // 4) * 4

            @pl.when((nt & 2) != 0)
            def _():
                update(pl.multiple_of(done4, 2), 2, first)

            @pl.when((nt & 1) != 0)
            def _():
                update(nt - 1, 1, first)

        @pl.when(f == 0)
        def _():
            all_tiles(True)

        @pl.when(f > 0)
        def _():
            all_tiles(False)

        for j in range(UT):
            @pl.when(prefetch(j))
            def _():
                x_copy(r0_ref[un], j).wait()
                xb_ref[1 - slot, j] = xs_ref[j % 2].astype(BF16)

        @pl.when(f == NF - 1)
        def _():
            for i in range(UT):
                @pl.when(i < nt)
                def _():
                    o_copy(r0, i).start()

            @pl.when(u == na - 1)
            def _():
                wait_outputs(u)


def moe_experts(x_buf, plan, w_gate_up, b_gate_up, w_down, b_down, name="moe_experts"):
    R, D = x_buf.shape
    E, _, F2 = w_gate_up.shape
    DF = F2 // 2
    TM, UT, TF = MOE_TM, MOE_UNIT_TILES, MOE_TF
    NF = DF // TF
    ue, r0, nt, na = plan
    U = ue.shape[0]

    def fsel(u, f, na_ref):
        return jnp.where(u < na_ref[0], f, NF - 1)

    kern = functools.partial(_experts_kernel, TM=TM, UT=UT, NF=NF)
    return pl.pallas_call(
        kern,
        grid_spec=pltpu.PrefetchScalarGridSpec(
            num_scalar_prefetch=4,
            grid=(U, NF),
            in_specs=[
                pl.BlockSpec(memory_space=pl.ANY),
                pl.BlockSpec((None, D, TF), lambda u, f, ue, r0, nt, na: (ue[u], 0, fsel(u, f, na))),
                pl.BlockSpec((None, D, TF), lambda u, f, ue, r0, nt, na: (ue[u], 0, NF + fsel(u, f, na))),
                pl.BlockSpec((None, 1, TF), lambda u, f, ue, r0, nt, na: (ue[u], 0, fsel(u, f, na))),
                pl.BlockSpec((None, 1, TF), lambda u, f, ue, r0, nt, na: (ue[u], 0, NF + fsel(u, f, na))),
                pl.BlockSpec((None, TF, D), lambda u, f, ue, r0, nt, na: (ue[u], fsel(u, f, na), 0)),
                pl.BlockSpec((None, 1, D), lambda u, f, ue, r0, nt, na: (ue[u], 0, 0)),
            ],
            out_specs=pl.BlockSpec(memory_space=pl.ANY),
            scratch_shapes=[
                pltpu.VMEM((2, TM, D), F32),
                pltpu.VMEM((2, UT, TM, D), BF16),
                pltpu.VMEM((UT, TM, D), F32),
                pltpu.SemaphoreType.DMA((2,)),
                pltpu.SemaphoreType.DMA((UT,)),
            ],
        ),
        out_shape=jax.ShapeDtypeStruct((R, D), F32),
        input_output_aliases={4: 0},
        compiler_params=_cparams(("arbitrary", "arbitrary"), VMEM_LIMIT_EXPERTS_BYTES),
        name=name,
    )(ue, r0, nt, na, x_buf, w_gate_up, w_gate_up, b_gate_up.reshape(E, 1, F2), b_gate_up.reshape(E, 1, F2),
      w_down, b_down.reshape(E, 1, D))


def moe_plan(counts, n_tokens):
    TM, UT = MOE_TM, MOE_UNIT_TILES
    E = counts.shape[0]
    nt = (counts + TM - 1) // TM
    padded = nt * TM
    pad_start = jnp.cumsum(padded) - padded
    nu = (nt + UT - 1) // UT
    cu = jnp.cumsum(nu)
    n_units = cu[-1]
    U = E + (n_tokens * TOP_K) // (TM * UT)
    u = jnp.arange(U, dtype=jnp.int32)
    uc = jnp.minimum(u, n_units - 1)
    ue = jnp.clip(jnp.searchsorted(cu, uc, side='right'), 0, E - 1).astype(jnp.int32)
    j = uc - (cu[ue] - nu[ue])
    r0 = (pad_start[ue] + j * (UT * TM)).astype(jnp.int32)
    ntl = jnp.where(u < n_units, jnp.clip(nt[ue] - j * UT, 0, UT), 0).astype(jnp.int32)
    zstart = (pad_start + counts) // 8 * 8
    pad_end = pad_start + padded
    zrows = jnp.concatenate([zstart, pad_end - zstart, pad_end[-1:]]).astype(jnp.int32)
    return pad_start, zrows, (ue, r0, ntl, n_units.reshape(1).astype(jnp.int32))


def _combine_kernel(dest_ref, h_ref, route_ref, nw_ref, o_hbm, h2_ref, xn_ref, g_ref, sem, *, tm):
    i = pl.program_id(0)

    def start(rb, c):
        for j in range(ROW_UNROLL):
            r = rb * ROW_UNROLL + j
            for k in range(TOP_K):
                d = dest_ref[(i * tm + r) * TOP_K + k]
                pltpu.make_async_copy(o_hbm.at[pl.ds(d, 1)], g_ref.at[k, pl.ds(r, 1)], sem).start(priority=k % 2)
        return c

    lax.fori_loop(0, tm // ROW_UNROLL, start, 0)
    for k in range(TOP_K):
        pltpu.make_async_copy(o_hbm.at[pl.ds(0, tm)], g_ref.at[k], sem).wait()

    route = route_ref[...]
    y = g_ref[0] * route[:, 2 * TOP_K:2 * TOP_K + 1]
    for k in range(1, TOP_K):
        y = y + g_ref[k] * route[:, 2 * TOP_K + k:2 * TOP_K + k + 1]
    h2 = h_ref[...] + y
    h2_ref[...] = h2
    ms = jnp.mean(h2 * h2, axis=-1, keepdims=True)
    xn_ref[...] = (h2 * lax.rsqrt(ms + EPS) * nw_ref[...]).astype(xn_ref.dtype)


def moe_combine(h, route, dest_flat, out_buf, norm_w, tm=256, name="moe_combine"):
    T, D = h.shape
    kern = functools.partial(_combine_kernel, tm=tm)
    return pl.pallas_call(
        kern,
        grid_spec=pltpu.PrefetchScalarGridSpec(
            num_scalar_prefetch=1,
            grid=(T // tm,),
            in_specs=[pl.BlockSpec((tm, D), lambda i, d: (i, 0)),
                      pl.BlockSpec((tm, LANES), lambda i, d: (i, 0)),
                      pl.BlockSpec((1, D), lambda i, d: (0, 0)),
                      pl.BlockSpec(memory_space=pl.ANY)],
            out_specs=[pl.BlockSpec((tm, D), lambda i, d: (i, 0)),
                       pl.BlockSpec((tm, D), lambda i, d: (i, 0))],
            scratch_shapes=[pltpu.VMEM((TOP_K, tm, D), F32), pltpu.SemaphoreType.DMA(())],
        ),
        out_shape=[jax.ShapeDtypeStruct((T, D), F32), jax.ShapeDtypeStruct((T, D), BF16)],
        compiler_params=_cparams(("arbitrary",)),
        name=name,
    )(dest_flat, h, route, norm_w.reshape(1, D), out_buf)


def _ple_kernel(xn_ref, p1_ref, p2_ref, h_ref, wg_ref, wp_ref, o_ref, wgb_ref, wpb_ref, *, nfirst):
    i = pl.program_id(1)

    @pl.when(i == 0)
    def _():
        wgb_ref[...] = wg_ref[...].astype(BF16)
        wpb_ref[...] = wp_ref[...].astype(BF16)

    gate = jax.nn.sigmoid(_dot(xn_ref[...], wgb_ref[...]))
    p = _pick(i, nfirst, p1_ref, p2_ref).astype(BF16)
    o_ref[...] = h_ref[...] + _dot(p, wpb_ref[...]) * gate


def ple(xn, p1, p2, h, w_gate, w_ple, tm=1024, tn=512, name="ple"):
    T, D = h.shape
    P = p1.shape[1]
    na = p1.shape[0] // tm
    return pl.pallas_call(
        functools.partial(_ple_kernel, nfirst=na),
        grid=(D // tn, T // tm),
        in_specs=[
            pl.BlockSpec((tm, D), lambda j, i: (i, 0)),
            *_cat_specs((tm, P), na, row_axis=1),
            pl.BlockSpec((tm, tn), lambda j, i: (i, j)),
            pl.BlockSpec((D, tn), lambda j, i: (0, j)),
            pl.BlockSpec((P, tn), lambda j, i: (0, j)),
        ],
        out_specs=pl.BlockSpec((tm, tn), lambda j, i: (i, j)),
        out_shape=jax.ShapeDtypeStruct((T, D), F32),
        scratch_shapes=[pltpu.VMEM((D, tn), BF16), pltpu.VMEM((P, tn), BF16)],
        compiler_params=_cparams(("arbitrary", "arbitrary")),
        name=name,
    )(xn, p1, p2, h, w_gate, w_ple)


def kernel(x_prompt, x_sample, p_prompt, p_sample, state_hgrn, state_ssm_re, state_ssm_im, norm_mix, w_in, hg_lb, hg_gnorm, w_branch_a, ssm_a_re, ssm_a_im, ssm_log_dt, ssm_b_re, ssm_b_im, ssm_c_re, ssm_c_im, ssm_d, w_glu, w_branch_b, w_out, norm_moe, w_router, b_router, w_gate_up, b_gate_up, w_down, b_down, norm_ple, w_ple, w_ple_gate, norm_final):
    BP, LP, D = x_prompt.shape
    BS, LS, _ = x_sample.shape
    depth = w_in.shape[0]
    assert depth == 1
    TP, TS = BP * LP, BS * LS
    T = TP + TS
    G, P = ssm_a_re.shape[1:]
    KD = HG_HEADS * HG_DK
    W = SSM_GROUP * G

    xp = x_prompt.reshape(TP, D)
    xs = x_sample.reshape(TS, D)

    xn = rmsnorm_cat(xp, xs, norm_mix[0], BF16, name="norm_mix")
    tm = 1024
    w_in0 = w_in[0]
    z_hg = project(xn, w_in0, 0, T // tm, 0, 4, tm, 1024, "proj_hgrn")
    u_p = project(xn, w_in0, 0, TP // tm, 4, 1, tm, 1024, "proj_u_prompt")
    u_s = project(xn, w_in0, TP // tm, TS // tm, 4, 1, tm, 1024, "proj_u_sample")
    z_gate = project(xn, w_in0, 0, T // tm, 5, 4, tm, 1024, "proj_gates")

    lb = jax.nn.softmax(hg_lb.astype(F32), axis=0)[0]
    og_p, hg_p = hgrn2(z_hg, lb, hg_gnorm[0], None, row0=0, B=BP, L=LP, C=HG_CHUNK, nb=1, tok_step=256,
                       name="hgrn_prompt")
    og_s, hg_s = hgrn2(z_hg, lb, hg_gnorm[0], state_hgrn[0], row0=TP, B=BS, L=LS, C=LS, nb=4, tok_step=LS,
                       name="hgrn_sample")

    bd, cd, ab = s5_params(ssm_a_re[0], ssm_a_im[0], ssm_log_dt[0], ssm_b_re[0], ssm_b_im[0],
                           ssm_c_re[0], ssm_c_im[0])
    y_p, re_p, im_p = s5_prompt(u_p.reshape(BP, LP, W), bd, cd, ab, ssm_d[0], TC=256, name="s5_prompt")
    y_s, re_s, im_s = s5_sample(u_s, bd, cd, ab, ssm_d[0], state_ssm_re[0].reshape(BS, G * P),
                                state_ssm_im[0].reshape(BS, G * P), B=BS, L=LS, name="s5_sample")

    glu = gelu_glu(y_p.reshape(TP, W), y_s, w_glu[0])
    merged = gated_merge(og_p, og_s, glu, z_gate, w_branch_a[0], w_branch_b[0])
    h1 = resid_matmul(merged, xp, xs, w_out[0], name="out_proj")

    xn2, route, cnt = moe_router(h1, norm_moe[0], w_router[0], b_router[0])
    e_idx = route[:, 0:TOP_K].astype(jnp.int32)
    rank = route[:, TOP_K:2 * TOP_K].astype(jnp.int32)
    counts = cnt[0, :N_EXPERTS].astype(jnp.int32)
    pad_start, zrows, plan = moe_plan(counts, T)
    onehot = e_idx[..., None] == jnp.arange(N_EXPERTS, dtype=jnp.int32)
    dest = (jnp.sum(jnp.where(onehot, pad_start, 0), axis=-1) + rank).astype(jnp.int32).reshape(T * TOP_K)
    n_rows = T * TOP_K + N_EXPERTS * MOE_TM
    x_buf = moe_dispatch(xn2, dest, zrows, n_rows)
    out_buf = moe_experts(x_buf, plan, w_gate_up[0], b_gate_up[0], w_down[0], b_down[0])
    h2, xn3 = moe_combine(h1, route, dest, out_buf, norm_ple[0])

    h3 = ple(xn3, p_prompt[0].reshape(TP, -1), p_sample[0].reshape(TS, -1), h2, w_ple_gate[0], w_ple[0])
    y_p2, y_s2 = rmsnorm_split(h3, norm_final, TP, name="norm_final")

    y_prompt = y_p2.reshape(BP, LP, D)
    y_sample = y_s2.reshape(BS, LS, D)
    hs = (1, BP, HG_HEADS, HG_DK, HG_DV)
    return (y_prompt, y_sample,
            hg_p.reshape(hs), re_p.reshape(1, BP, G, P), im_p.reshape(1, BP, G, P),
            hg_s.reshape(1, BS, HG_HEADS, HG_DK, HG_DV), re_s.reshape(1, BS, G, P), im_s.reshape(1, BS, G, P))
```

```python
import functools

import jax
import jax.numpy as jnp
from jax import lax
from jax.experimental import pallas as pl
from jax.experimental.pallas import tpu as pltpu

F32 = jnp.float32
BF16 = jnp.bfloat16
EPS = 1e-6

HG_HEADS = 8
HG_DK = 128
HG_DV = 128
HG_CHUNK = 32
SSM_GROUP = 16
SSM_STATE = 64
N_EXPERTS = 32
TOP_K = 4
SWIGLU_LIMIT = 7.0
SWIGLU_ALPHA = 1.702

LANES = 128
VMEM_LIMIT_BYTES = 56 * 1024 * 1024
VMEM_LIMIT_EXPERTS_BYTES = 61 * 1024 * 1024

MOE_TM = 256
MOE_UNIT_TILES = 6
MOE_TF = 512
NEG_BIG = -1e30


def _cparams(sem, vmem_limit=VMEM_LIMIT_BYTES):
    return pltpu.CompilerParams(dimension_semantics=sem, vmem_limit_bytes=vmem_limit)


def _dot(a, b):
    return jnp.dot(a, b, preferred_element_type=F32)


def _split2(a):
    hi = a.astype(BF16)
    lo = (a - hi.astype(F32)).astype(BF16)
    return hi, lo


def _dot_x3(a, b_split):
    a1, a2 = _split2(a)
    b1, b2 = b_split
    return _dot(a1, b1) + (_dot(a1, b2) + _dot(a2, b1))


def _cat_specs(shape, nfirst, row_axis=0, col_axis=None):
    def col(ids):
        return 0 if col_axis is None else ids[col_axis]

    first = pl.BlockSpec(shape, lambda *ids: (jnp.minimum(ids[row_axis], nfirst - 1), col(ids)))
    second = pl.BlockSpec(shape, lambda *ids: (jnp.maximum(ids[row_axis] - nfirst, 0), col(ids)))
    return first, second


def _pick(i, nfirst, a_ref, b_ref):
    return jnp.where(i < nfirst, a_ref[...], b_ref[...])


def _rms(x, w):
    ms = jnp.mean(x * x, axis=-1, keepdims=True)
    return x * lax.rsqrt(ms + EPS) * w


def _rms_in2_kernel(a_ref, b_ref, w_ref, o_ref, *, nfirst):
    x = _pick(pl.program_id(0), nfirst, a_ref, b_ref)
    o_ref[...] = _rms(x, w_ref[...]).astype(o_ref.dtype)


def rmsnorm_cat(xa, xb, w, out_dtype, tm=512, name="rmsnorm"):
    (Ta, D), Tb = xa.shape, xb.shape[0]
    na = Ta // tm
    return pl.pallas_call(
        functools.partial(_rms_in2_kernel, nfirst=na),
        grid=((Ta + Tb) // tm,),
        in_specs=[*_cat_specs((tm, D), na), pl.BlockSpec((1, D), lambda i: (0, 0))],
        out_specs=pl.BlockSpec((tm, D), lambda i: (i, 0)),
        out_shape=jax.ShapeDtypeStruct((Ta + Tb, D), out_dtype),
        compiler_params=_cparams(("arbitrary",)),
        name=name,
    )(xa, xb, w.reshape(1, D))


def _proj_kernel(x_ref, w_ref, o_ref, wb_ref):
    @pl.when(pl.program_id(1) == 0)
    def _():
        wb_ref[...] = w_ref[...].astype(BF16)

    o_ref[...] = _dot(x_ref[...], wb_ref[...]).astype(o_ref.dtype)


def project(xn, w, row_blk0, n_row_blks, col_blk0, n_col_blks, tm, tn, name):
    K = xn.shape[1]
    return pl.pallas_call(
        _proj_kernel,
        grid=(n_col_blks, n_row_blks),
        in_specs=[
            pl.BlockSpec((tm, K), lambda j, i: (row_blk0 + i, 0)),
            pl.BlockSpec((K, tn), lambda j, i: (0, col_blk0 + j)),
        ],
        out_specs=pl.BlockSpec((tm, tn), lambda j, i: (i, j)),
        out_shape=jax.ShapeDtypeStruct((n_row_blks * tm, n_col_blks * tn), F32),
        scratch_shapes=[pltpu.VMEM((K, tn), BF16)],
        compiler_params=_cparams(("arbitrary", "arbitrary")),
        name=name,
    )(xn, w)


HGRN_CHUNKS_PER_TRIP = 4


def _hgrn_kernel(*refs, C, nchunk, nb, tok_step, has_s0, state_t):
    if has_s0:
        q_ref, f_ref, v_ref, g_ref, lb_ref, gn_ref, s0_ref, og_ref, so_ref, st_ref = refs
    else:
        q_ref, f_ref, v_ref, g_ref, lb_ref, gn_ref, og_ref, so_ref, st_ref = refs
        s0_ref = None
    H, DK, DV = HG_HEADS, HG_DK, HG_DV
    step = pl.program_id(1)

    @pl.when(step == 0)
    def _():
        for i in range(nb):
            for h in range(H):
                if has_s0:
                    st_ref[i, h] = s0_ref[i, h].T if state_t else s0_ref[i, h]
                else:
                    st_ref[i, h] = jnp.zeros(st_ref.shape[2:], F32)

    lbv = lb_ref[...]
    gnv = gn_ref[...]
    rr = lax.broadcasted_iota(jnp.int32, (C, C), 0)
    cc = lax.broadcasted_iota(jnp.int32, (C, C), 1)
    causal = rr >= cc
    tri = causal.astype(BF16)
    sls =[slice(h * DK, (h + 1) * DK) for h in range(H)]
    nt_dims = (((1,), (1,)), ((), ()))
    tn_dims = (((0,), (0,)), ((), ()))

    def tn(a, b):
        return lax.dot_general(a, b, tn_dims, preferred_element_type=F32)

    def prepare(r0):
        rows = pl.ds(r0, C)
        zq = q_ref[rows, :]
        zf = f_ref[rows, :]
        q = jax.nn.silu(zq)
        fe = lbv + (1.0 - lbv) * jax.nn.sigmoid(zf)
        k = 1.0 - fe
        gl = jnp.log(fe)
        g1 = gl.astype(BF16)
        r1 = gl - g1.astype(F32)
        g2 = r1.astype(BF16)
        g3 = (r1 - g2.astype(F32)).astype(BF16)
        b = _dot(tri, g1) + (_dot(tri, g2) + _dot(tri, g3))
        blast = b[C - 1:C, :]
        return dict(
            qt=(q * jnp.exp(b)).astype(BF16),
            kt=(k * jnp.exp(-b)).astype(BF16),
            kend=(k * jnp.exp(blast - b)).astype(BF16),
            eblast=jnp.exp(blast),
            vb=v_ref[rows, :].astype(BF16),
            gate=jax.nn.silu(g_ref[rows, :]),
        )

    def run(jobs):
        P = [prepare(r0) for _, r0 in jobs]
        atts = [[lax.dot_general(p["qt"][:, sl], p["kt"][:, sl], nt_dims, preferred_element_type=F32)
                 for sl in sls] for p in P]
        if state_t:
            upds = [[tn(p["vb"][:, sl], p["kend"][:, sl]) for sl in sls] for p in P]
            decs = [[p["eblast"][:, sl] for sl in sls] for p in P]
        else:
            upds = [[tn(p["kend"][:, sl], p["vb"][:, sl]) for sl in sls] for p in P]
            decs = [[jnp.broadcast_to(p["eblast"][:, sl], (8, DK)).T[:, 0:1] for sl in sls] for p in P]
        state = {}
        inters = []
        for j, (i, _) in enumerate(jobs):
            row = []
            for h, sl in enumerate(sls):
                s = state[(i, h)] if (i, h) in state else st_ref[i, h]
                qth = P[j]["qt"][:, sl]
                if state_t:
                    row.append(lax.dot_general(qth, s.astype(BF16), nt_dims, preferred_element_type=F32))
                else:
                    row.append(_dot(qth, s.astype(BF16)))
                state[(i, h)] = s * decs[j][h] + upds[j][h]
            inters.append(row)
        for (i, h), s in state.items():
            st_ref[i, h] = s
        outs = []
        for j, p in enumerate(P):
            cols = []
            for h, sl in enumerate(sls):
                att = jnp.where(causal, atts[j][h], 0.0).astype(BF16)
                o = _dot(att, p["vb"][:, sl]) + inters[j][h]
                ms = jnp.mean(o * o, axis=-1, keepdims=True)
                cols.append(o * lax.rsqrt(ms + EPS) * gnv * p["gate"][:, sl])
            outs.append(jnp.concatenate(cols, axis=1))
        return outs

    if nchunk == 1:
        outs = run([(i, i * tok_step) for i in range(nb)])
        og_ref[...] = jnp.concatenate(outs, axis=0).astype(og_ref.dtype)
    else:
        assert nb == 1 and nchunk % HGRN_CHUNKS_PER_TRIP == 0

        def body(t, carry):
            r0s = [pl.multiple_of((HGRN_CHUNKS_PER_TRIP * t + j) * C, C) for j in range(HGRN_CHUNKS_PER_TRIP)]
            outs = run([(0, r0) for r0 in r0s])
            for r0, o in zip(r0s, outs):
                og_ref[pl.ds(r0, C), :] = o.astype(og_ref.dtype)
            return carry

        lax.fori_loop(0, nchunk // HGRN_CHUNKS_PER_TRIP, body, 0)

    @pl.when(step == pl.num_programs(1) - 1)
    def _():
        for i in range(nb):
            for h in range(H):
                so_ref[i, h] = st_ref[i, h].T if state_t else st_ref[i, h]


def hgrn2(z_hg, lb, gn, s0, *, row0, B, L, C, nb, tok_step, name):
    H, DK, DV = HG_HEADS, HG_DK, HG_DV
    W = H * DK
    steps = L // tok_step
    rows = nb * tok_step
    nchunk = tok_step // C
    blk0 = row0 // rows
    has_s0 = s0 is not None

    def zspec(col):
        return pl.BlockSpec((rows, W), lambda bb, s: (blk0 + bb * steps + s, col))

    in_specs = [zspec(0), zspec(1), zspec(2), zspec(3),
                pl.BlockSpec((1, W), lambda bb, s: (0, 0)),
                pl.BlockSpec((1, DV), lambda bb, s: (0, 0))]
    args = [z_hg, z_hg, z_hg, z_hg, lb.reshape(1, W), gn.reshape(1, DV)]
    if has_s0:
        in_specs.append(pl.BlockSpec((nb, H, DK, DV), lambda bb, s: (bb, 0, 0, 0)))
        args.append(s0)
    state_t = L > C
    kern = functools.partial(_hgrn_kernel, C=C, nchunk=nchunk, nb=nb, tok_step=tok_step, has_s0=has_s0,
                             state_t=state_t)
    return pl.pallas_call(
        kern,
        grid=(B // nb, steps),
        in_specs=in_specs,
        out_specs=[pl.BlockSpec((rows, W), lambda bb, s: (bb * steps + s, 0)),
                   pl.BlockSpec((nb, H, DK, DV), lambda bb, s: (bb, 0, 0, 0))],
        out_shape=[jax.ShapeDtypeStruct((B * L, W), BF16),
                   jax.ShapeDtypeStruct((B, H, DK, DV), F32)],
        scratch_shapes=[pltpu.VMEM((nb, H, DV, DK), F32)],
        compiler_params=_cparams(("arbitrary", "arbitrary")),
        name=name,
    )(*args)


def _s5_step(ar, ai, xr, xi, br, bi):
    return ar * xr - ai * xi + br, ar * xi + ai * xr + bi


S5_STEPS_PER_TRIP = 4


def _s5_prompt_kernel(u_ref, bd_ref, cd_ref, a_ref, d_ref, y_ref, xr_ref, xi_ref, X_ref, st_ref, *, B, TC):
    NP = X_ref.shape[0] // 2
    assert 2 * B == 8
    tc = pl.program_id(1)

    def rows_of(b, s):
        return pl.ds(2 * b + s, TC, stride=8)

    @pl.when(tc == 0)
    def _():
        st_ref[...] = jnp.zeros_like(st_ref)

    bd = bd_ref[...].astype(BF16)
    for b in range(B):
        bu = _dot(u_ref[b].astype(BF16), bd)
        for c in range(4 * NP):
            X_ref[c // 2, rows_of(b, c % 2), :] = bu[:, c * LANES:(c + 1) * LANES]

    odd = lax.broadcasted_iota(jnp.int32, (8, LANES), 0) % 2 == 1

    def pair_const(row, p):
        lo = a_ref[row:row + 1, (2 * p) * LANES:(2 * p + 1) * LANES]
        hi = a_ref[row:row + 1, (2 * p + 1) * LANES:(2 * p + 2) * LANES]
        return jnp.where(odd, hi, lo)

    ars = [pair_const(0, p) for p in range(NP)]
    ais = [pair_const(1, p) for p in range(NP)]

    def body(i, carry):
        carry = list(carry)
        for j in range(S5_STEPS_PER_TRIP):
            rows = pl.ds(pl.multiple_of((i * S5_STEPS_PER_TRIP + j) * 8, 8), 8)
            for p in range(NP):
                xr, xi = _s5_step(ars[p], ais[p], carry[p][0], carry[p][1], X_ref[p, rows, :], X_ref[NP + p, rows, :])
                X_ref[p, rows, :] = xr
                X_ref[NP + p, rows, :] = xi
                carry[p] = (xr, xi)
        return tuple(carry)

    init = tuple((st_ref[p], st_ref[NP + p]) for p in range(NP))
    fin = lax.fori_loop(0, TC // S5_STEPS_PER_TRIP, body, init)
    for p in range(NP):
        st_ref[p] = fin[p][0]
        st_ref[NP + p] = fin[p][1]

    cd = cd_ref[...].astype(BF16)
    dv = d_ref[...]
    for b in range(B):
        xb = jnp.concatenate([X_ref[c // 2, rows_of(b, c % 2), :] for c in range(4 * NP)], axis=1).astype(BF16)
        y_ref[b] = _dot(xb, cd) + dv * u_ref[b]

    @pl.when(tc == pl.num_programs(1) - 1)
    def _():
        for c in range(2 * NP):
            cols = slice(c * LANES, (c + 1) * LANES)
            xr_ref[:, cols] = st_ref[c // 2, pl.ds(c % 2, B, stride=2), :]
            xi_ref[:, cols] = st_ref[NP + c // 2, pl.ds(c % 2, B, stride=2), :]


def s5_prompt(u, bd, cd, ab, dskip, *, TC, name):
    B, L, W = u.shape
    GB = W // LANES
    SW = bd.shape[2]
    half = SW // 2
    kern = functools.partial(_s5_prompt_kernel, B=B, TC=TC)
    return pl.pallas_call(
        kern,
        grid=(GB, L // TC),
        in_specs=[
            pl.BlockSpec((B, TC, LANES), lambda g, t: (0, t, g)),
            pl.BlockSpec((None, LANES, SW), lambda g, t: (g, 0, 0)),
            pl.BlockSpec((None, SW, LANES), lambda g, t: (g, 0, 0)),
            pl.BlockSpec((None, 2, half), lambda g, t: (g, 0, 0)),
            pl.BlockSpec((1, LANES), lambda g, t: (0, g)),
        ],
        out_specs=[
            pl.BlockSpec((B, TC, LANES), lambda g, t: (0, t, g)),
            pl.BlockSpec((B, half), lambda g, t: (0, g)),
            pl.BlockSpec((B, half), lambda g, t: (0, g)),
        ],
        out_shape=[jax.ShapeDtypeStruct((B, L, W), F32),
                   jax.ShapeDtypeStruct((B, GB * half), F32),
                   jax.ShapeDtypeStruct((B, GB * half), F32)],
        scratch_shapes=[pltpu.VMEM((SW // LANES // 2, 8 * TC, LANES), F32),
                        pltpu.VMEM((SW // LANES // 2, 8, LANES), F32)],
        compiler_params=_cparams(("arbitrary", "arbitrary")),
        name=name,
    )(u, bd, cd, ab, dskip.reshape(1, W))


def _s5_sample_kernel(u_ref, bd_ref, cd_ref, a_ref, d_ref, x0r_ref, x0i_ref, y_ref, xr_ref, xi_ref, X_ref, *, B, L):
    NC = X_ref.shape[0]
    NH = NC // 2
    u = u_ref[...]
    bu = _dot_x3(u, _split2(bd_ref[...]))
    for c in range(NC):
        X_ref[c] = bu[:, c * LANES:(c + 1) * LANES]
    for c in range(NH):
        cols = slice(c * LANES, (c + 1) * LANES)
        ar = a_ref[0:1, cols]
        ai = a_ref[1:2, cols]
        xr = x0r_ref[:, cols]
        xi = x0i_ref[:, cols]
        for t in range(L):
            rows = pl.ds(t, B, stride=L)
            xr, xi = _s5_step(ar, ai, xr, xi, X_ref[c, rows, :], X_ref[NH + c, rows, :])
            X_ref[c, rows, :] = xr
            X_ref[NH + c, rows, :] = xi
        xr_ref[:, cols] = xr
        xi_ref[:, cols] = xi
    xs = jnp.concatenate([X_ref[c] for c in range(NC)], axis=1).astype(BF16)
    y_ref[...] = _dot(xs, cd_ref[...].astype(BF16)) + d_ref[...] * u


def s5_sample(u, bd, cd, ab, dskip, x0r, x0i, *, B, L, name):
    T, W = u.shape
    GB = W // LANES
    SW = bd.shape[2]
    half = SW // 2
    kern = functools.partial(_s5_sample_kernel, B=B, L=L)
    return pl.pallas_call(
        kern,
        grid=(GB,),
        in_specs=[
            pl.BlockSpec((T, LANES), lambda g: (0, g)),
            pl.BlockSpec((None, LANES, SW), lambda g: (g, 0, 0)),
            pl.BlockSpec((None, SW, LANES), lambda g: (g, 0, 0)),
            pl.BlockSpec((None, 2, half), lambda g: (g, 0, 0)),
            pl.BlockSpec((1, LANES), lambda g: (0, g)),
            pl.BlockSpec((B, half), lambda g: (0, g)),
            pl.BlockSpec((B, half), lambda g: (0, g)),
        ],
        out_specs=[
            pl.BlockSpec((T, LANES), lambda g: (0, g)),
            pl.BlockSpec((B, half), lambda g: (0, g)),
            pl.BlockSpec((B, half), lambda g: (0, g)),
        ],
        out_shape=[jax.ShapeDtypeStruct((T, W), F32),
                   jax.ShapeDtypeStruct((B, GB * half), F32),
                   jax.ShapeDtypeStruct((B, GB * half), F32)],
        scratch_shapes=[pltpu.VMEM((SW // LANES, T, LANES), F32)],
        compiler_params=_cparams(("arbitrary",)),
        name=name,
    )(u, bd, cd, ab, dskip.reshape(1, W), x0r, x0i)


def s5_params(a_re, a_im, log_dt, b_re, b_im, c_re, c_im):
    G, P = a_re.shape
    gpb = LANES // SSM_GROUP
    GB = G // gpb
    A = lax.complex(a_re, a_im)
    dt = jnp.exp(log_dt)[:, None]
    A_bar = jnp.exp(A * dt)
    Bm = lax.complex(b_re, b_im)
    B_bar = ((A_bar - 1.0) / A)[..., None] * Bm
    eye = jnp.eye(gpb, dtype=F32)

    def in_map(m):
        m = m.reshape(GB, gpb, P, SSM_GROUP)
        return jnp.einsum('bgpc,gh->bgchp', m, eye).reshape(GB, gpb * SSM_GROUP, gpb * P)

    def out_map(m):
        m = m.reshape(GB, gpb, SSM_GROUP, P)
        return jnp.einsum('bgcp,gh->bgphc', m, eye).reshape(GB, gpb * P, gpb * SSM_GROUP)

    bd = jnp.concatenate([in_map(B_bar.real), in_map(B_bar.imag)], axis=2)
    cd = jnp.concatenate([out_map(c_re), out_map(-c_im)], axis=1)
    ab = jnp.stack([A_bar.real.reshape(GB, gpb * P), A_bar.imag.reshape(GB, gpb * P)], axis=1)
    return bd, cd, ab


def _glu_kernel(ya_ref, yb_ref, w_ref, o_ref, wb_ref, *, nfirst):
    i = pl.program_id(0)

    @pl.when(i == 0)
    def _():
        wb_ref[...] = w_ref[...].astype(BF16)

    zg = jax.nn.gelu(_pick(i, nfirst, ya_ref, yb_ref))
    o_ref[...] = (zg * jax.nn.sigmoid(_dot(zg.astype(BF16), wb_ref[...]))).astype(o_ref.dtype)


def gelu_glu(ya, yb, w, tm=512, name="gelu_glu"):
    (Ta, W), Tb = ya.shape, yb.shape[0]
    na = Ta // tm
    return pl.pallas_call(
        functools.partial(_glu_kernel, nfirst=na),
        grid=((Ta + Tb) // tm,),
        in_specs=[*_cat_specs((tm, W), na), pl.BlockSpec((W, W), lambda i: (0, 0))],
        out_specs=pl.BlockSpec((tm, W), lambda i: (i, 0)),
        out_shape=jax.ShapeDtypeStruct((Ta + Tb, W), BF16),
        scratch_shapes=[pltpu.VMEM((W, W), BF16)],
        compiler_params=_cparams(("arbitrary",)),
        name=name,
    )(ya, yb, w)


def _merge_kernel(oa1_ref, oa2_ref, ob_ref, ga_ref, gb_ref, wa_ref, wb_ref, o_ref, wab_ref, wbb_ref, *, nfirst):
    i = pl.program_id(1)

    @pl.when(i == 0)
    def _():
        wab_ref[...] = wa_ref[...].astype(BF16)
        wbb_ref[...] = wb_ref[...].astype(BF16)

    ya = _dot(_pick(i, nfirst, oa1_ref, oa2_ref), wab_ref[...])
    yb = _dot(ob_ref[...], wbb_ref[...])
    o_ref[...] = (jax.nn.sigmoid(ga_ref[...]) * ya + jax.nn.sigmoid(gb_ref[...]) * yb).astype(o_ref.dtype)


def gated_merge(oa1, oa2, ob, z_gate, wa, wb, tm=512, tn=1024, name="gated_merge"):
    T, K = ob.shape
    N = wa.shape[1]
    nj = N // tn
    na = oa1.shape[0] // tm
    return pl.pallas_call(
        functools.partial(_merge_kernel, nfirst=na),
        grid=(nj, T // tm),
        in_specs=[
            *_cat_specs((tm, K), na, row_axis=1),
            pl.BlockSpec((tm, K), lambda j, i: (i, 0)),
            pl.BlockSpec((tm, tn), lambda j, i: (i, j)),
            pl.BlockSpec((tm, tn), lambda j, i: (i, nj + j)),
            pl.BlockSpec((K, tn), lambda j, i: (0, j)),
            pl.BlockSpec((K, tn), lambda j, i: (0, j)),
        ],
        out_specs=pl.BlockSpec((tm, tn), lambda j, i: (i, j)),
        out_shape=jax.ShapeDtypeStruct((T, N), BF16),
        scratch_shapes=[pltpu.VMEM((K, tn), BF16), pltpu.VMEM((K, tn), BF16)],
        compiler_params=_cparams(("arbitrary", "arbitrary")),
        name=name,
    )(oa1, oa2, ob, z_gate, z_gate, wa, wb)


def _resid_kernel(x_ref, h1_ref, h2_ref, w_ref, o_ref, wb_ref, *, nfirst):
    i = pl.program_id(1)

    @pl.when(i == 0)
    def _():
        wb_ref[...] = w_ref[...].astype(BF16)

    o_ref[...] = _pick(i, nfirst, h1_ref, h2_ref) + _dot(x_ref[...], wb_ref[...])


def resid_matmul(x, h1, h2, w, tm=1024, tn=1024, name="resid_matmul"):
    T, K = x.shape
    N = w.shape[1]
    na = h1.shape[0] // tm
    return pl.pallas_call(
        functools.partial(_resid_kernel, nfirst=na),
        grid=(N // tn, T // tm),
        in_specs=[
            pl.BlockSpec((tm, K), lambda j, i: (i, 0)),
            *_cat_specs((tm, tn), na, row_axis=1, col_axis=0),
            pl.BlockSpec((K, tn), lambda j, i: (0, j)),
        ],
        out_specs=pl.BlockSpec((tm, tn), lambda j, i: (i, j)),
        out_shape=jax.ShapeDtypeStruct((T, N), F32),
        scratch_shapes=[pltpu.VMEM((K, tn), BF16)],
        compiler_params=_cparams(("arbitrary", "arbitrary")),
        name=name,
    )(x, h1, h2, w)


def _router_kernel(h_ref, nw_ref, wr_ref, br_ref, xn_ref, route_ref, cnt_ref, tri_ref, carry_ref, *, tm):
    i = pl.program_id(0)

    @pl.when(i == 0)
    def _():
        rr = lax.broadcasted_iota(jnp.int32, (tm, tm), 0)
        cc = lax.broadcasted_iota(jnp.int32, (tm, tm), 1)
        tri_ref[...] = (rr > cc).astype(BF16)
        carry_ref[...] = jnp.zeros_like(carry_ref)

    x = h_ref[...]
    ms = jnp.mean(x * x, axis=-1, keepdims=True)
    xn = x * lax.rsqrt(ms + EPS) * nw_ref[...]
    xn_ref[...] = xn
    logits = _dot_x3(xn, _split2(wr_ref[...])) + br_ref[...]
    lane = lax.broadcasted_iota(jnp.int32, (tm, LANES), 1)
    lanef = lane.astype(F32)
    cur = logits
    vals, hots, eids = [], [], []
    for _ in range(TOP_K):
        m = jnp.max(cur, axis=-1, keepdims=True)
        idx = jnp.min(jnp.where(cur == m, lanef, float(LANES)), axis=-1, keepdims=True)
        hot = lanef == idx
        vals.append(m)
        hots.append(hot)
        eids.append(idx)
        cur = jnp.where(hot, -jnp.inf, cur)
    es = [jnp.exp(v - vals[0]) for v in vals]
    den = es[0] + es[1] + es[2] + es[3]
    multi = jnp.zeros((tm, LANES), F32)
    for hot in hots:
        multi = multi + hot.astype(F32)
    base = carry_ref[0:1, :] + _dot(tri_ref[...], multi.astype(BF16))
    route = jnp.zeros((tm, LANES), F32)
    for k in range(TOP_K):
        e_k = eids[k]
        r_k = jnp.sum(jnp.where(hots[k], base, 0.0), axis=-1, keepdims=True)
        w_k = es[k] / den
        route = route + jnp.where(lane == k, e_k, 0.0) + jnp.where(lane == TOP_K + k, r_k, 0.0) \
            + jnp.where(lane == 2 * TOP_K + k, w_k, 0.0)
    route_ref[...] = route
    carry = carry_ref[0:1, :] + jnp.sum(multi, axis=0, keepdims=True)
    carry_ref[...] = jnp.broadcast_to(carry, carry_ref.shape)
    cnt_ref[...] = jnp.broadcast_to(carry, cnt_ref.shape)


def moe_router(h, norm_w, w_router, b_router, tm=512, name="moe_router"):
    T, D = h.shape
    E = w_router.shape[1]
    wr = jnp.pad(w_router, ((0, 0), (0, LANES - E)))
    br = jnp.pad(b_router.reshape(1, E), ((0, 0), (0, LANES - E)), constant_values=NEG_BIG)
    kern = functools.partial(_router_kernel, tm=tm)
    return pl.pallas_call(
        kern,
        grid=(T // tm,),
        in_specs=[
            pl.BlockSpec((tm, D), lambda i: (i, 0)),
            pl.BlockSpec((1, D), lambda i: (0, 0)),
            pl.BlockSpec((D, LANES), lambda i: (0, 0)),
            pl.BlockSpec((1, LANES), lambda i: (0, 0)),
        ],
        out_specs=[
            pl.BlockSpec((tm, D), lambda i: (i, 0)),
            pl.BlockSpec((tm, LANES), lambda i: (i, 0)),
            pl.BlockSpec((8, LANES), lambda i: (0, 0)),
        ],
        out_shape=[jax.ShapeDtypeStruct((T, D), F32),
                   jax.ShapeDtypeStruct((T, LANES), F32),
                   jax.ShapeDtypeStruct((8, LANES), F32)],
        scratch_shapes=[pltpu.VMEM((tm, tm), BF16), pltpu.VMEM((8, LANES), F32)],
        compiler_params=_cparams(("arbitrary",)),
        name=name,
    )(h, norm_w.reshape(1, D), wr, br)


ROW_UNROLL = 8


def _dispatch_kernel(dest_ref, zrow_ref, x_ref, buf_ref, zero_ref, xs_ref, sem, zsem, *, tm, n_tail):
    i = pl.program_id(0)
    E = (zrow_ref.shape[0] - 1) // 2
    ZR = zero_ref.shape[0]
    R = buf_ref.shape[0]

    @pl.when(i == 0)
    def _():
        zero_ref[...] = jnp.zeros_like(zero_ref)

        def zero_copy(row, n):
            return pltpu.make_async_copy(zero_ref.at[pl.ds(0, n)], buf_ref.at[pl.ds(pl.multiple_of(row, 8), n)], zsem)

        def zero_rows(act):
            for e in range(E):
                off = zrow_ref[e]
                n = ZR
                while n >= 8:
                    @pl.when((zrow_ref[E + e] & n) != 0)
                    def _():
                        act(zero_copy(off, n))
                    off = off + (zrow_ref[E + e] & n)
                    n //= 2
            for j in range(n_tail):
                row = zrow_ref[2 * E] + j * ZR

                @pl.when(row < R)
                def _():
                    act(zero_copy(row, ZR))

        zero_rows(lambda c: c.start())
        zero_rows(lambda c: c.wait())

    par = i % 2
    xs_ref[par] = x_ref[...]

    def start(rb, c):
        for j in range(ROW_UNROLL):
            r = rb * ROW_UNROLL + j
            for k in range(TOP_K):
                d = dest_ref[(i * tm + r) * TOP_K + k]
                pltpu.make_async_copy(xs_ref.at[par, pl.ds(r, 1)], buf_ref.at[pl.ds(d, 1)],
                                      sem.at[par]).start(priority=k % 2)
        return c

    lax.fori_loop(0, tm // ROW_UNROLL, start, 0)

    def wait_step(s):
        for k in range(TOP_K):
            pltpu.make_async_copy(xs_ref.at[s], buf_ref.at[pl.ds(0, tm)], sem.at[s]).wait()

    @pl.when(i > 0)
    def _():
        wait_step(1 - par)

    @pl.when(i == pl.num_programs(0) - 1)
    def _():
        wait_step(par)


def moe_dispatch(xn, dest_flat, zrows, n_rows, tm=256, name="moe_dispatch"):
    T, D = xn.shape
    n_tail = (n_rows - T * TOP_K) // MOE_TM
    kern = functools.partial(_dispatch_kernel, tm=tm, n_tail=n_tail)
    return pl.pallas_call(
        kern,
        grid_spec=pltpu.PrefetchScalarGridSpec(
            num_scalar_prefetch=2,
            grid=(T // tm,),
            in_specs=[pl.BlockSpec((tm, D), lambda i, d, z: (i, 0))],
            out_specs=pl.BlockSpec(memory_space=pl.ANY),
            scratch_shapes=[pltpu.VMEM((MOE_TM, D), F32), pltpu.VMEM((2, tm, D), F32),
                            pltpu.SemaphoreType.DMA((2,)), pltpu.SemaphoreType.DMA(())],
        ),
        out_shape=jax.ShapeDtypeStruct((n_rows, D), F32),
        compiler_params=_cparams(("arbitrary",)),
        name=name,
    )(dest_flat, zrows, xn)


def _experts_kernel(ue_ref, r0_ref, nt_ref, na_ref, x_hbm, wg_ref, wu_ref, bg_ref, bu_ref, wd_ref, bd_ref,
                    o_hbm, xs_ref, xb_ref, acc_ref, sem_in, sem_out, *, TM, UT, NF):
    TPS = -(-UT // NF)
    assert TPS <= 2
    u = pl.program_id(0)
    f = pl.program_id(1)
    na = na_ref[0]
    nt = nt_ref[u]
    r0 = r0_ref[u]
    slot = u % 2
    un = jnp.minimum(u + 1, na - 1)

    def prefetch(j):
        return jnp.logical_and(jnp.logical_and(u + 1 < na, j < nt_ref[un]), f == j // TPS)

    def x_copy(base, i):
        rows = pl.ds(pl.multiple_of(base + i * TM, TM), TM)
        return pltpu.make_async_copy(x_hbm.at[rows], xs_ref.at[i % 2], sem_in.at[i % 2])

    def o_copy(base, i):
        rows = pl.ds(pl.multiple_of(base + i * TM, TM), TM)
        return pltpu.make_async_copy(acc_ref.at[i], o_hbm.at[rows], sem_out.at[i])

    def wait_outputs(unit):
        for i in range(UT):
            @pl.when(i < nt_ref[unit])
            def _():
                o_copy(r0_ref[unit], i).wait()

    @pl.when(u < na)
    def _():
        @pl.when(jnp.logical_and(f == 0, u == 0))
        def _():
            x_copy(r0, 0).start()
            for i in range(UT):
                @pl.when(i < nt)
                def _():
                    if i + 1 < UT:
                        @pl.when(i + 1 < nt)
                        def _():
                            x_copy(r0, i + 1).start()
                    x_copy(r0, i).wait()
                    xb_ref[0, i] = xs_ref[i % 2].astype(BF16)

        @pl.when(jnp.logical_and(f == 0, u > 0))
        def _():
            wait_outputs(u - 1)

        for j in range(UT):
            @pl.when(prefetch(j))
            def _():
                x_copy(r0_ref[un], j).start()

        bg = bg_ref[...]
        bu = bu_ref[...]

        def tile_out(xt):
            g = _dot(xt, wg_ref[...].astype(BF16)) + bg
            up = _dot(xt, wu_ref[...].astype(BF16)) + bu
            gate = jnp.minimum(g, SWIGLU_LIMIT)
            lin = jnp.clip(up, -SWIGLU_LIMIT, SWIGLU_LIMIT)
            act = (gate * jax.nn.sigmoid(SWIGLU_ALPHA * gate) * (lin + 1.0)).astype(BF16)
            return _dot(act, wd_ref[...].astype(BF16))

        D = xb_ref.shape[3]

        def update(i, n, first):
            tiles = pl.ds(i, n)
            out = tile_out(xb_ref[slot, tiles].reshape(n * TM, D))
            if first:
                out = out + bd_ref[...]
            else:
                out = out + acc_ref[tiles].reshape(n * TM, D)
            acc_ref[tiles] = out.reshape(n, TM, D)

        def all_tiles(first):
            def body(j, c):
                update(pl.multiple_of(4 * j, 4), 4, first)
                return c
            lax.fori_loop(0, nt // 4, body, 0)
            done4 = (nt // 4) * 4

            @pl.when((nt & 2) != 0)
            def _():
                update(pl.multiple_of(done4, 2), 2, first)

            @pl.when((nt & 1) != 0)
            def _():
                update(nt - 1, 1, first)

        @pl.when(f == 0)
        def _():
            all_tiles(True)

        @pl.when(f > 0)
        def _():
            all_tiles(False)

        for j in range(UT):
            @pl.when(prefetch(j))
            def _():
                x_copy(r0_ref[un], j).wait()
                xb_ref[1 - slot, j] = xs_ref[j % 2].astype(BF16)

        @pl.when(f == NF - 1)
        def _():
            for i in range(UT):
                @pl.when(i < nt)
                def _():
                    o_copy(r0, i).start()

            @pl.when(u == na - 1)
            def _():
                wait_outputs(u)


def moe_experts(x_buf, plan, w_gate_up, b_gate_up, w_down, b_down, name="moe_experts"):
    R, D = x_buf.shape
    E, _, F2 = w_gate_up.shape
    DF = F2 // 2
    TM, UT, TF = MOE_TM, MOE_UNIT_TILES, MOE_TF
    NF = DF // TF
    ue, r0, nt, na = plan
    U = ue.shape[0]

    def fsel(u, f, na_ref):
        return jnp.where(u < na_ref[0], f, NF - 1)

    kern = functools.partial(_experts_kernel, TM=TM, UT=UT, NF=NF)
    return pl.pallas_call(
        kern,
        grid_spec=pltpu.PrefetchScalarGridSpec(
            num_scalar_prefetch=4,
            grid=(U, NF),
            in_specs=[
                pl.BlockSpec(memory_space=pl.ANY),
                pl.BlockSpec((None, D, TF), lambda u, f, ue, r0, nt, na: (ue[u], 0, fsel(u, f, na))),
                pl.BlockSpec((None, D, TF), lambda u, f, ue, r0, nt, na: (ue[u], 0, NF + fsel(u, f, na))),
                pl.BlockSpec((None, 1, TF), lambda u, f, ue, r0, nt, na: (ue[u], 0, fsel(u, f, na))),
                pl.BlockSpec((None, 1, TF), lambda u, f, ue, r0, nt, na: (ue[u], 0, NF + fsel(u, f, na))),
                pl.BlockSpec((None, TF, D), lambda u, f, ue, r0, nt, na: (ue[u], fsel(u, f, na), 0)),
                pl.BlockSpec((None, 1, D), lambda u, f, ue, r0, nt, na: (ue[u], 0, 0)),
            ],
            out_specs=pl.BlockSpec(memory_space=pl.ANY),
            scratch_shapes=[
                pltpu.VMEM((2, TM, D), F32),
                pltpu.VMEM((2, UT, TM, D), BF16),
                pltpu.VMEM((UT, TM, D), F32),
                pltpu.SemaphoreType.DMA((2,)),
                pltpu.SemaphoreType.DMA((UT,)),
            ],
        ),
        out_shape=jax.ShapeDtypeStruct((R, D), F32),
        input_output_aliases={4: 0},
        compiler_params=_cparams(("arbitrary", "arbitrary"), VMEM_LIMIT_EXPERTS_BYTES),
        name=name,
    )(ue, r0, nt, na, x_buf, w_gate_up, w_gate_up, b_gate_up.reshape(E, 1, F2), b_gate_up.reshape(E, 1, F2),
      w_down, b_down.reshape(E, 1, D))


def moe_plan(counts, n_tokens):
    TM, UT = MOE_TM, MOE_UNIT_TILES
    E = counts.shape[0]
    nt = (counts + TM - 1) // TM
    padded = nt * TM
    pad_start = jnp.cumsum(padded) - padded
    nu = (nt + UT - 1) // UT
    cu = jnp.cumsum(nu)
    n_units = cu[-1]
    U = E + (n_tokens * TOP_K) // (TM * UT)
    u = jnp.arange(U, dtype=jnp.int32)
    uc = jnp.minimum(u, n_units - 1)
    ue = jnp.clip(jnp.searchsorted(cu, uc, side='right'), 0, E - 1).astype(jnp.int32)
    j = uc - (cu[ue] - nu[ue])
    r0 = (pad_start[ue] + j * (UT * TM)).astype(jnp.int32)
    ntl = jnp.where(u < n_units, jnp.clip(nt[ue] - j * UT, 0, UT), 0).astype(jnp.int32)
    zstart = (pad_start + counts) // 8 * 8
    pad_end = pad_start + padded
    zrows = jnp.concatenate([zstart, pad_end - zstart, pad_end[-1:]]).astype(jnp.int32)
    return pad_start, zrows, (ue, r0, ntl, n_units.reshape(1).astype(jnp.int32))


def _combine_kernel(dest_ref, h_ref, route_ref, nw_ref, o_hbm, h2_ref, xn_ref, g_ref, sem, *, tm):
    i = pl.program_id(0)
    par = i % 2

    def gather_tile(t, slot):
        def start(rb, c):
            for j in range(ROW_UNROLL):
                r = rb * ROW_UNROLL + j
                for k in range(TOP_K):
                    d = dest_ref[(t * tm + r) * TOP_K + k]
                    pltpu.make_async_copy(o_hbm.at[pl.ds(d, 1)], g_ref.at[slot, k, pl.ds(r, 1)],
                                          sem.at[slot]).start(priority=k % 2)
            return c

        lax.fori_loop(0, tm // ROW_UNROLL, start, 0)

    @pl.when(i == 0)
    def _():
        gather_tile(i, par)

    @pl.when(i + 1 < pl.num_programs(0))
    def _():
        gather_tile(i + 1, 1 - par)

    for k in range(TOP_K):
        pltpu.make_async_copy(o_hbm.at[pl.ds(0, tm)], g_ref.at[par, k], sem.at[par]).wait()

    route = route_ref[...]
    y = g_ref[par, 0] * route[:, 2 * TOP_K:2 * TOP_K + 1]
    for k in range(1, TOP_K):
        y = y + g_ref[par, k] * route[:, 2 * TOP_K + k:2 * TOP_K + k + 1]
    h2 = h_ref[...] + y
    h2_ref[...] = h2
    ms = jnp.mean(h2 * h2, axis=-1, keepdims=True)
    xn_ref[...] = (h2 * lax.rsqrt(ms + EPS) * nw_ref[...]).astype(xn_ref.dtype)


def moe_combine(h, route, dest_flat, out_buf, norm_w, tm=256, name="moe_combine"):
    T, D = h.shape
    kern = functools.partial(_combine_kernel, tm=tm)
    return pl.pallas_call(
        kern,
        grid_spec=pltpu.PrefetchScalarGridSpec(
            num_scalar_prefetch=1,
            grid=(T // tm,),
            in_specs=[pl.BlockSpec((tm, D), lambda i, d: (i, 0)),
                      pl.BlockSpec((tm, LANES), lambda i, d: (i, 0)),
                      pl.BlockSpec((1, D), lambda i, d: (0, 0)),
                      pl.BlockSpec(memory_space=pl.ANY)],
            out_specs=[pl.BlockSpec((tm, D), lambda i, d: (i, 0)),
                       pl.BlockSpec((tm, D), lambda i, d: (i, 0))],
            scratch_shapes=[pltpu.VMEM((2, TOP_K, tm, D), F32), pltpu.SemaphoreType.DMA((2,))],
        ),
        out_shape=[jax.ShapeDtypeStruct((T, D), F32), jax.ShapeDtypeStruct((T, D), BF16)],
        compiler_params=_cparams(("arbitrary",)),
        name=name,
    )(dest_flat, h, route, norm_w.reshape(1, D), out_buf)


def _ple_kernel(xn_ref, p1_ref, p2_ref, h_ref, wg_ref, wp_ref, nf_ref, oa_ref, ob_ref, wgb_ref, wpb_ref, *, nfirst):
    i = pl.program_id(0)

    @pl.when(i == 0)
    def _():
        wgb_ref[...] = wg_ref[...].astype(BF16)
        wpb_ref[...] = wp_ref[...].astype(BF16)

    gate = jax.nn.sigmoid(_dot(xn_ref[...], wgb_ref[...]))
    p = _pick(i, nfirst, p1_ref, p2_ref).astype(BF16)
    y = _rms(h_ref[...] + _dot(p, wpb_ref[...]) * gate, nf_ref[...])

    @pl.when(i < nfirst)
    def _():
        oa_ref[...] = y

    @pl.when(i >= nfirst)
    def _():
        ob_ref[...] = y


def ple_final(xn, p1, p2, h, w_gate, w_ple, norm_w, tm=256, name="ple_final"):
    T, D = h.shape
    Ta, P = p1.shape
    na = Ta // tm
    once = pl.Buffered(1)
    return pl.pallas_call(
        functools.partial(_ple_kernel, nfirst=na),
        grid=(T // tm,),
        in_specs=[
            pl.BlockSpec((tm, D), lambda i: (i, 0)),
            *_cat_specs((tm, P), na),
            pl.BlockSpec((tm, D), lambda i: (i, 0)),
            pl.BlockSpec((D, D), lambda i: (0, 0), pipeline_mode=once),
            pl.BlockSpec((P, D), lambda i: (0, 0), pipeline_mode=once),
            pl.BlockSpec((1, D), lambda i: (0, 0)),
        ],
        out_specs=list(_cat_specs((tm, D), na)),
        out_shape=[jax.ShapeDtypeStruct((Ta, D), F32), jax.ShapeDtypeStruct((T - Ta, D), F32)],
        scratch_shapes=[pltpu.VMEM((D, D), BF16), pltpu.VMEM((P, D), BF16)],
        compiler_params=_cparams(("arbitrary",)),
        name=name,
    )(xn, p1, p2, h, w_gate, w_ple, norm_w.reshape(1, D))


def kernel(x_prompt, x_sample, p_prompt, p_sample, state_hgrn, state_ssm_re, state_ssm_im, norm_mix, w_in, hg_lb, hg_gnorm, w_branch_a, ssm_a_re, ssm_a_im, ssm_log_dt, ssm_b_re, ssm_b_im, ssm_c_re, ssm_c_im, ssm_d, w_glu, w_branch_b, w_out, norm_moe, w_router, b_router, w_gate_up, b_gate_up, w_down, b_down, norm_ple, w_ple, w_ple_gate, norm_final):
    BP, LP, D = x_prompt.shape
    BS, LS, _ = x_sample.shape
    depth = w_in.shape[0]
    assert depth == 1
    TP, TS = BP * LP, BS * LS
    T = TP + TS
    G, P = ssm_a_re.shape[1:]
    KD = HG_HEADS * HG_DK
    W = SSM_GROUP * G

    xp = x_prompt.reshape(TP, D)
    xs = x_sample.reshape(TS, D)

    xn = rmsnorm_cat(xp, xs, norm_mix[0], BF16, name="norm_mix")
    tm = 1024
    w_in0 = w_in[0]
    z_hg = project(xn, w_in0, 0, T // tm, 0, 4, tm, 1024, "proj_hgrn")
    u_p = project(xn, w_in0, 0, TP // tm, 4, 1, tm, 1024, "proj_u_prompt")
    u_s = project(xn, w_in0, TP // tm, TS // tm, 4, 1, tm, 1024, "proj_u_sample")
    z_gate = project(xn, w_in0, 0, T // tm, 5, 4, tm, 1024, "proj_gates")

    lb = jax.nn.softmax(hg_lb.astype(F32), axis=0)[0]
    og_p, hg_p = hgrn2(z_hg, lb, hg_gnorm[0], None, row0=0, B=BP, L=LP, C=HG_CHUNK, nb=1, tok_step=256,
                       name="hgrn_prompt")
    og_s, hg_s = hgrn2(z_hg, lb, hg_gnorm[0], state_hgrn[0], row0=TP, B=BS, L=LS, C=LS, nb=4, tok_step=LS,
                       name="hgrn_sample")

    bd, cd, ab = s5_params(ssm_a_re[0], ssm_a_im[0], ssm_log_dt[0], ssm_b_re[0], ssm_b_im[0],
                           ssm_c_re[0], ssm_c_im[0])
    y_p, re_p, im_p = s5_prompt(u_p.reshape(BP, LP, W), bd, cd, ab, ssm_d[0], TC=256, name="s5_prompt")
    y_s, re_s, im_s = s5_sample(u_s, bd, cd, ab, ssm_d[0], state_ssm_re[0].reshape(BS, G * P),
                                state_ssm_im[0].reshape(BS, G * P), B=BS, L=LS, name="s5_sample")

    glu = gelu_glu(y_p.reshape(TP, W), y_s, w_glu[0])
    merged = gated_merge(og_p, og_s, glu, z_gate, w_branch_a[0], w_branch_b[0])
    h1 = resid_matmul(merged, xp, xs, w_out[0], name="out_proj")

    xn2, route, cnt = moe_router(h1, norm_moe[0], w_router[0], b_router[0])
    e_idx = route[:, 0:TOP_K].astype(jnp.int32)
    rank = route[:, TOP_K:2 * TOP_K].astype(jnp.int32)
    counts = cnt[0, :N_EXPERTS].astype(jnp.int32)
    pad_start, zrows, plan = moe_plan(counts, T)
    onehot = e_idx[..., None] == jnp.arange(N_EXPERTS, dtype=jnp.int32)
    dest = (jnp.sum(jnp.where(onehot, pad_start, 0), axis=-1) + rank).astype(jnp.int32).reshape(T * TOP_K)
    n_rows = T * TOP_K + N_EXPERTS * MOE_TM
    x_buf = moe_dispatch(xn2, dest, zrows, n_rows)
    out_buf = moe_experts(x_buf, plan, w_gate_up[0], b_gate_up[0], w_down[0], b_down[0])
    h2, xn3 = moe_combine(h1, route, dest, out_buf, norm_ple[0])

    y_p2, y_s2 = ple_final(xn3, p_prompt[0].reshape(TP, -1), p_sample[0].reshape(TS, -1), h2, w_ple_gate[0],
                           w_ple[0], norm_final)

    y_prompt = y_p2.reshape(BP, LP, D)
    y_sample = y_s2.reshape(BS, LS, D)
    hs = (1, BP, HG_HEADS, HG_DK, HG_DV)
    return (y_prompt, y_sample,
            hg_p.reshape(hs), re_p.reshape(1, BP, G, P), im_p.reshape(1, BP, G, P),
            hg_s.reshape(1, BS, HG_HEADS, HG_DK, HG_DV), re_s.reshape(1, BS, G, P), im_s.reshape(1, BS, G, P))
```

```python
import functools

import jax
import jax.numpy as jnp
from jax import lax
from jax.experimental import pallas as pl
from jax.experimental.pallas import tpu as pltpu

F32 = jnp.float32
BF16 = jnp.bfloat16
EPS = 1e-6

HG_HEADS = 8
HG_DK = 128
HG_DV = 128
HG_CHUNK = 32
SSM_GROUP = 16
SSM_STATE = 64
N_EXPERTS = 32
TOP_K = 4
SWIGLU_LIMIT = 7.0
SWIGLU_ALPHA = 1.702

LANES = 128
VMEM_LIMIT_BYTES = 56 * 1024 * 1024
VMEM_LIMIT_EXPERTS_BYTES = 61 * 1024 * 1024

MOE_TM = 256
MOE_UNIT_TILES = 6
MOE_TF = 512
NEG_BIG = -1e30


def _cparams(sem, vmem_limit=VMEM_LIMIT_BYTES):
    return pltpu.CompilerParams(dimension_semantics=sem, vmem_limit_bytes=vmem_limit)


def _dot(a, b):
    return jnp.dot(a, b, preferred_element_type=F32)


def _split2(a):
    hi = a.astype(BF16)
    lo = (a - hi.astype(F32)).astype(BF16)
    return hi, lo


def _dot_x3(a, b_split):
    a1, a2 = _split2(a)
    b1, b2 = b_split
    return _dot(a1, b1) + (_dot(a1, b2) + _dot(a2, b1))


def _cat_specs(shape, nfirst, row_axis=0, col_axis=None):
    def col(ids):
        return 0 if col_axis is None else ids[col_axis]

    first = pl.BlockSpec(shape, lambda *ids: (jnp.minimum(ids[row_axis], nfirst - 1), col(ids)))
    second = pl.BlockSpec(shape, lambda *ids: (jnp.maximum(ids[row_axis] - nfirst, 0), col(ids)))
    return first, second


def _pick(i, nfirst, a_ref, b_ref):
    return jnp.where(i < nfirst, a_ref[...], b_ref[...])


def _rms(x, w):
    ms = jnp.mean(x * x, axis=-1, keepdims=True)
    return x * lax.rsqrt(ms + EPS) * w


def _rms_in2_kernel(a_ref, b_ref, w_ref, o_ref, *, nfirst):
    x = _pick(pl.program_id(0), nfirst, a_ref, b_ref)
    o_ref[...] = _rms(x, w_ref[...]).astype(o_ref.dtype)


def rmsnorm_cat(xa, xb, w, out_dtype, tm=512, name="rmsnorm"):
    (Ta, D), Tb = xa.shape, xb.shape[0]
    na = Ta // tm
    return pl.pallas_call(
        functools.partial(_rms_in2_kernel, nfirst=na),
        grid=((Ta + Tb) // tm,),
        in_specs=[*_cat_specs((tm, D), na), pl.BlockSpec((1, D), lambda i: (0, 0))],
        out_specs=pl.BlockSpec((tm, D), lambda i: (i, 0)),
        out_shape=jax.ShapeDtypeStruct((Ta + Tb, D), out_dtype),
        compiler_params=_cparams(("arbitrary",)),
        name=name,
    )(xa, xb, w.reshape(1, D))


def _proj_kernel(x_ref, w_ref, o_ref, wb_ref):
    @pl.when(pl.program_id(1) == 0)
    def _():
        wb_ref[...] = w_ref[...].astype(BF16)

    o_ref[...] = _dot(x_ref[...], wb_ref[...]).astype(o_ref.dtype)


def project(xn, w, row_blk0, n_row_blks, col_blk0, n_col_blks, tm, tn, name):
    K = xn.shape[1]
    return pl.pallas_call(
        _proj_kernel,
        grid=(n_col_blks, n_row_blks),
        in_specs=[
            pl.BlockSpec((tm, K), lambda j, i: (row_blk0 + i, 0)),
            pl.BlockSpec((K, tn), lambda j, i: (0, col_blk0 + j)),
        ],
        out_specs=pl.BlockSpec((tm, tn), lambda j, i: (i, j)),
        out_shape=jax.ShapeDtypeStruct((n_row_blks * tm, n_col_blks * tn), F32),
        scratch_shapes=[pltpu.VMEM((K, tn), BF16)],
        compiler_params=_cparams(("arbitrary", "arbitrary")),
        name=name,
    )(xn, w)


HGRN_CHUNKS_PER_TRIP = 4


def _hgrn_kernel(*refs, C, nchunk, nb, tok_step, has_s0, state_t):
    if has_s0:
        q_ref, f_ref, v_ref, g_ref, lb_ref, gn_ref, s0_ref, og_ref, so_ref, st_ref = refs
    else:
        q_ref, f_ref, v_ref, g_ref, lb_ref, gn_ref, og_ref, so_ref, st_ref = refs
        s0_ref = None
    H, DK, DV = HG_HEADS, HG_DK, HG_DV
    step = pl.program_id(1)

    @pl.when(step == 0)
    def _():
        for i in range(nb):
            for h in range(H):
                if has_s0:
                    st_ref[i, h] = s0_ref[i, h].T if state_t else s0_ref[i, h]
                else:
                    st_ref[i, h] = jnp.zeros(st_ref.shape[2:], F32)

    lbv = lb_ref[...]
    gnv = gn_ref[...]
    rr = lax.broadcasted_iota(jnp.int32, (C, C), 0)
    cc = lax.broadcasted_iota(jnp.int32, (C, C), 1)
    causal = rr >= cc
    tri = causal.astype(BF16)
    sls =[slice(h * DK, (h + 1) * DK) for h in range(H)]
    nt_dims = (((1,), (1,)), ((), ()))
    tn_dims = (((0,), (0,)), ((), ()))

    def tn(a, b):
        return lax.dot_general(a, b, tn_dims, preferred_element_type=F32)

    def prepare(r0):
        rows = pl.ds(r0, C)
        zq = q_ref[rows, :]
        zf = f_ref[rows, :]
        q = jax.nn.silu(zq)
        fe = lbv + (1.0 - lbv) * jax.nn.sigmoid(zf)
        k = 1.0 - fe
        gl = jnp.log(fe)
        g1 = gl.astype(BF16)
        r1 = gl - g1.astype(F32)
        g2 = r1.astype(BF16)
        g3 = (r1 - g2.astype(F32)).astype(BF16)
        b = _dot(tri, g1) + (_dot(tri, g2) + _dot(tri, g3))
        blast = b[C - 1:C, :]
        return dict(
            qt=(q * jnp.exp(b)).astype(BF16),
            kt=(k * jnp.exp(-b)).astype(BF16),
            kend=(k * jnp.exp(blast - b)).astype(BF16),
            eblast=jnp.exp(blast),
            vb=v_ref[rows, :].astype(BF16),
            gate=jax.nn.silu(g_ref[rows, :]),
        )

    def run(jobs):
        P = [prepare(r0) for _, r0 in jobs]
        atts = [[lax.dot_general(p["qt"][:, sl], p["kt"][:, sl], nt_dims, preferred_element_type=F32)
                 for sl in sls] for p in P]
        if state_t:
            upds = [[tn(p["vb"][:, sl], p["kend"][:, sl]) for sl in sls] for p in P]
            decs = [[p["eblast"][:, sl] for sl in sls] for p in P]
        else:
            upds = [[tn(p["kend"][:, sl], p["vb"][:, sl]) for sl in sls] for p in P]
            decs = [[jnp.broadcast_to(p["eblast"][:, sl], (8, DK)).T[:, 0:1] for sl in sls] for p in P]
        state = {}
        inters = []
        for j, (i, _) in enumerate(jobs):
            row = []
            for h, sl in enumerate(sls):
                s = state[(i, h)] if (i, h) in state else st_ref[i, h]
                qth = P[j]["qt"][:, sl]
                if state_t:
                    row.append(lax.dot_general(qth, s.astype(BF16), nt_dims, preferred_element_type=F32))
                else:
                    row.append(_dot(qth, s.astype(BF16)))
                state[(i, h)] = s * decs[j][h] + upds[j][h]
            inters.append(row)
        for (i, h), s in state.items():
            st_ref[i, h] = s
        outs = []
        for j, p in enumerate(P):
            cols = []
            for h, sl in enumerate(sls):
                att = jnp.where(causal, atts[j][h], 0.0).astype(BF16)
                o = _dot(att, p["vb"][:, sl]) + inters[j][h]
                ms = jnp.mean(o * o, axis=-1, keepdims=True)
                cols.append(o * lax.rsqrt(ms + EPS) * gnv * p["gate"][:, sl])
            outs.append(jnp.concatenate(cols, axis=1))
        return outs

    if nchunk == 1:
        outs = run([(i, i * tok_step) for i in range(nb)])
        og_ref[...] = jnp.concatenate(outs, axis=0).astype(og_ref.dtype)
    else:
        assert nb == 1 and nchunk % HGRN_CHUNKS_PER_TRIP == 0

        def body(t, carry):
            r0s = [pl.multiple_of((HGRN_CHUNKS_PER_TRIP * t + j) * C, C) for j in range(HGRN_CHUNKS_PER_TRIP)]
            outs = run([(0, r0) for r0 in r0s])
            for r0, o in zip(r0s, outs):
                og_ref[pl.ds(r0, C), :] = o.astype(og_ref.dtype)
            return carry

        lax.fori_loop(0, nchunk // HGRN_CHUNKS_PER_TRIP, body, 0)

    @pl.when(step == pl.num_programs(1) - 1)
    def _():
        for i in range(nb):
            for h in range(H):
                so_ref[i, h] = st_ref[i, h].T if state_t else st_ref[i, h]


def hgrn2(z_hg, lb, gn, s0, *, row0, B, L, C, nb, tok_step, name):
    H, DK, DV = HG_HEADS, HG_DK, HG_DV
    W = H * DK
    steps = L // tok_step
    rows = nb * tok_step
    nchunk = tok_step // C
    blk0 = row0 // rows
    has_s0 = s0 is not None

    def zspec(col):
        return pl.BlockSpec((rows, W), lambda bb, s: (blk0 + bb * steps + s, col))

    in_specs = [zspec(0), zspec(1), zspec(2), zspec(3),
                pl.BlockSpec((1, W), lambda bb, s: (0, 0)),
                pl.BlockSpec((1, DV), lambda bb, s: (0, 0))]
    args = [z_hg, z_hg, z_hg, z_hg, lb.reshape(1, W), gn.reshape(1, DV)]
    if has_s0:
        in_specs.append(pl.BlockSpec((nb, H, DK, DV), lambda bb, s: (bb, 0, 0, 0)))
        args.append(s0)
    state_t = L > C
    kern = functools.partial(_hgrn_kernel, C=C, nchunk=nchunk, nb=nb, tok_step=tok_step, has_s0=has_s0,
                             state_t=state_t)
    return pl.pallas_call(
        kern,
        grid=(B // nb, steps),
        in_specs=in_specs,
        out_specs=[pl.BlockSpec((rows, W), lambda bb, s: (bb * steps + s, 0)),
                   pl.BlockSpec((nb, H, DK, DV), lambda bb, s: (bb, 0, 0, 0))],
        out_shape=[jax.ShapeDtypeStruct((B * L, W), BF16),
                   jax.ShapeDtypeStruct((B, H, DK, DV), F32)],
        scratch_shapes=[pltpu.VMEM((nb, H, DV, DK), F32)],
        compiler_params=_cparams(("arbitrary", "arbitrary")),
        name=name,
    )(*args)


def _s5_step(ar, ai, xr, xi, br, bi):
    return ar * xr - ai * xi + br, ar * xi + ai * xr + bi


S5_STEPS_PER_TRIP = 4
S5_GROUPS_PER_BLOCK = 16


def _s5_prompt_kernel(u_ref, bd_ref, cd_ref, a_ref, d_ref, y_ref, xr_ref, xi_ref, X_ref, st_ref, *, B, TC):
    NP = X_ref.shape[0] // 2
    assert 2 * B == 8
    tc = pl.program_id(1)

    def rows_of(b, s):
        return pl.ds(2 * b + s, TC, stride=8)

    @pl.when(tc == 0)
    def _():
        st_ref[...] = jnp.zeros_like(st_ref)

    bd = bd_ref[...].astype(BF16)
    for b in range(B):
        bu = _dot(u_ref[b].astype(BF16), bd)
        for c in range(4 * NP):
            X_ref[c // 2, rows_of(b, c % 2), :] = bu[:, c * LANES:(c + 1) * LANES]

    odd = lax.broadcasted_iota(jnp.int32, (8, LANES), 0) % 2 == 1

    def pair_const(row, p):
        lo = a_ref[row:row + 1, (2 * p) * LANES:(2 * p + 1) * LANES]
        hi = a_ref[row:row + 1, (2 * p + 1) * LANES:(2 * p + 2) * LANES]
        return jnp.where(odd, hi, lo)

    ars = [pair_const(0, p) for p in range(NP)]
    ais = [pair_const(1, p) for p in range(NP)]

    def body(i, carry):
        carry = list(carry)
        for j in range(S5_STEPS_PER_TRIP):
            rows = pl.ds(pl.multiple_of((i * S5_STEPS_PER_TRIP + j) * 8, 8), 8)
            for p in range(NP):
                xr, xi = _s5_step(ars[p], ais[p], carry[p][0], carry[p][1], X_ref[p, rows, :], X_ref[NP + p, rows, :])
                X_ref[p, rows, :] = xr
                X_ref[NP + p, rows, :] = xi
                carry[p] = (xr, xi)
        return tuple(carry)

    init = tuple((st_ref[p], st_ref[NP + p]) for p in range(NP))
    fin = lax.fori_loop(0, TC // S5_STEPS_PER_TRIP, body, init)
    for p in range(NP):
        st_ref[p] = fin[p][0]
        st_ref[NP + p] = fin[p][1]

    cd = cd_ref[...].astype(BF16)
    dv = d_ref[...]
    for b in range(B):
        xb = jnp.concatenate([X_ref[c // 2, rows_of(b, c % 2), :] for c in range(4 * NP)], axis=1).astype(BF16)
        y_ref[b] = _dot(xb, cd) + dv * u_ref[b]

    @pl.when(tc == pl.num_programs(1) - 1)
    def _():
        for c in range(2 * NP):
            cols = slice(c * LANES, (c + 1) * LANES)
            xr_ref[:, cols] = st_ref[c // 2, pl.ds(c % 2, B, stride=2), :]
            xi_ref[:, cols] = st_ref[NP + c // 2, pl.ds(c % 2, B, stride=2), :]


def s5_prompt(u, bd, cd, ab, dskip, *, TC, name):
    B, L, W = u.shape
    GB, UB, SW = bd.shape
    half = SW // 2
    kern = functools.partial(_s5_prompt_kernel, B=B, TC=TC)
    return pl.pallas_call(
        kern,
        grid=(GB, L // TC),
        in_specs=[
            pl.BlockSpec((B, TC, UB), lambda g, t: (0, t, g)),
            pl.BlockSpec((None, UB, SW), lambda g, t: (g, 0, 0)),
            pl.BlockSpec((None, SW, UB), lambda g, t: (g, 0, 0)),
            pl.BlockSpec((None, 2, half), lambda g, t: (g, 0, 0)),
            pl.BlockSpec((1, UB), lambda g, t: (0, g)),
        ],
        out_specs=[
            pl.BlockSpec((B, TC, UB), lambda g, t: (0, t, g)),
            pl.BlockSpec((B, half), lambda g, t: (0, g)),
            pl.BlockSpec((B, half), lambda g, t: (0, g)),
        ],
        out_shape=[jax.ShapeDtypeStruct((B, L, W), F32),
                   jax.ShapeDtypeStruct((B, GB * half), F32),
                   jax.ShapeDtypeStruct((B, GB * half), F32)],
        scratch_shapes=[pltpu.VMEM((SW // LANES // 2, 8 * TC, LANES), F32),
                        pltpu.VMEM((SW // LANES // 2, 8, LANES), F32)],
        compiler_params=_cparams(("arbitrary", "arbitrary")),
        name=name,
    )(u, bd, cd, ab, dskip.reshape(1, W))


def _s5_sample_kernel(u_ref, bd_ref, cd_ref, a_ref, d_ref, x0r_ref, x0i_ref, y_ref, xr_ref, xi_ref, X_ref, *, B, L):
    NC = X_ref.shape[0]
    NH = NC // 2
    u = u_ref[...]
    bu = _dot_x3(u, _split2(bd_ref[...]))
    for c in range(NC):
        X_ref[c] = bu[:, c * LANES:(c + 1) * LANES]
    for c in range(NH):
        cols = slice(c * LANES, (c + 1) * LANES)
        ar = a_ref[0:1, cols]
        ai = a_ref[1:2, cols]
        xr = x0r_ref[:, cols]
        xi = x0i_ref[:, cols]
        for t in range(L):
            rows = pl.ds(t, B, stride=L)
            xr, xi = _s5_step(ar, ai, xr, xi, X_ref[c, rows, :], X_ref[NH + c, rows, :])
            X_ref[c, rows, :] = xr
            X_ref[NH + c, rows, :] = xi
        xr_ref[:, cols] = xr
        xi_ref[:, cols] = xi
    xs = jnp.concatenate([X_ref[c] for c in range(NC)], axis=1).astype(BF16)
    y_ref[...] = _dot(xs, cd_ref[...].astype(BF16)) + d_ref[...] * u


def s5_sample(u, bd, cd, ab, dskip, x0r, x0i, *, B, L, name):
    T, W = u.shape
    GB, UB, SW = bd.shape
    half = SW // 2
    kern = functools.partial(_s5_sample_kernel, B=B, L=L)
    return pl.pallas_call(
        kern,
        grid=(GB,),
        in_specs=[
            pl.BlockSpec((T, UB), lambda g: (0, g)),
            pl.BlockSpec((None, UB, SW), lambda g: (g, 0, 0)),
            pl.BlockSpec((None, SW, UB), lambda g: (g, 0, 0)),
            pl.BlockSpec((None, 2, half), lambda g: (g, 0, 0)),
            pl.BlockSpec((1, UB), lambda g: (0, g)),
            pl.BlockSpec((B, half), lambda g: (0, g)),
            pl.BlockSpec((B, half), lambda g: (0, g)),
        ],
        out_specs=[
            pl.BlockSpec((T, UB), lambda g: (0, g)),
            pl.BlockSpec((B, half), lambda g: (0, g)),
            pl.BlockSpec((B, half), lambda g: (0, g)),
        ],
        out_shape=[jax.ShapeDtypeStruct((T, W), F32),
                   jax.ShapeDtypeStruct((B, GB * half), F32),
                   jax.ShapeDtypeStruct((B, GB * half), F32)],
        scratch_shapes=[pltpu.VMEM((SW // LANES, T, LANES), F32)],
        compiler_params=_cparams(("arbitrary",)),
        name=name,
    )(u, bd, cd, ab, dskip.reshape(1, W), x0r, x0i)


def s5_params(a_re, a_im, log_dt, b_re, b_im, c_re, c_im):
    G, P = a_re.shape
    gpb = S5_GROUPS_PER_BLOCK
    GB = G // gpb
    A = lax.complex(a_re, a_im)
    dt = jnp.exp(log_dt)[:, None]
    A_bar = jnp.exp(A * dt)
    Bm = lax.complex(b_re, b_im)
    B_bar = ((A_bar - 1.0) / A)[..., None] * Bm
    eye = jnp.eye(gpb, dtype=F32)

    def in_map(m):
        m = m.reshape(GB, gpb, P, SSM_GROUP)
        return jnp.einsum('bgpc,gh->bgchp', m, eye).reshape(GB, gpb * SSM_GROUP, gpb * P)

    def out_map(m):
        m = m.reshape(GB, gpb, SSM_GROUP, P)
        return jnp.einsum('bgcp,gh->bgphc', m, eye).reshape(GB, gpb * P, gpb * SSM_GROUP)

    bd = jnp.concatenate([in_map(B_bar.real), in_map(B_bar.imag)], axis=2)
    cd = jnp.concatenate([out_map(c_re), out_map(-c_im)], axis=1)
    ab = jnp.stack([A_bar.real.reshape(GB, gpb * P), A_bar.imag.reshape(GB, gpb * P)], axis=1)
    return bd, cd, ab


def _glu_kernel(ya_ref, yb_ref, w_ref, o_ref, wb_ref, *, nfirst):
    i = pl.program_id(0)

    @pl.when(i == 0)
    def _():
        wb_ref[...] = w_ref[...].astype(BF16)

    zg = jax.nn.gelu(_pick(i, nfirst, ya_ref, yb_ref))
    o_ref[...] = (zg * jax.nn.sigmoid(_dot(zg.astype(BF16), wb_ref[...]))).astype(o_ref.dtype)


def gelu_glu(ya, yb, w, tm=512, name="gelu_glu"):
    (Ta, W), Tb = ya.shape, yb.shape[0]
    na = Ta // tm
    return pl.pallas_call(
        functools.partial(_glu_kernel, nfirst=na),
        grid=((Ta + Tb) // tm,),
        in_specs=[*_cat_specs((tm, W), na), pl.BlockSpec((W, W), lambda i: (0, 0))],
        out_specs=pl.BlockSpec((tm, W), lambda i: (i, 0)),
        out_shape=jax.ShapeDtypeStruct((Ta + Tb, W), BF16),
        scratch_shapes=[pltpu.VMEM((W, W), BF16)],
        compiler_params=_cparams(("arbitrary",)),
        name=name,
    )(ya, yb, w)


def _merge_kernel(oa1_ref, oa2_ref, ob_ref, ga_ref, gb_ref, wa_ref, wb_ref, o_ref, wab_ref, wbb_ref, *, nfirst):
    i = pl.program_id(1)

    @pl.when(i == 0)
    def _():
        wab_ref[...] = wa_ref[...].astype(BF16)
        wbb_ref[...] = wb_ref[...].astype(BF16)

    ya = _dot(_pick(i, nfirst, oa1_ref, oa2_ref), wab_ref[...])
    yb = _dot(ob_ref[...], wbb_ref[...])
    o_ref[...] = (jax.nn.sigmoid(ga_ref[...]) * ya + jax.nn.sigmoid(gb_ref[...]) * yb).astype(o_ref.dtype)


def gated_merge(oa1, oa2, ob, z_gate, wa, wb, tm=512, tn=1024, name="gated_merge"):
    T, K = ob.shape
    N = wa.shape[1]
    nj = N // tn
    na = oa1.shape[0] // tm
    return pl.pallas_call(
        functools.partial(_merge_kernel, nfirst=na),
        grid=(nj, T // tm),
        in_specs=[
            *_cat_specs((tm, K), na, row_axis=1),
            pl.BlockSpec((tm, K), lambda j, i: (i, 0)),
            pl.BlockSpec((tm, tn), lambda j, i: (i, j)),
            pl.BlockSpec((tm, tn), lambda j, i: (i, nj + j)),
            pl.BlockSpec((K, tn), lambda j, i: (0, j)),
            pl.BlockSpec((K, tn), lambda j, i: (0, j)),
        ],
        out_specs=pl.BlockSpec((tm, tn), lambda j, i: (i, j)),
        out_shape=jax.ShapeDtypeStruct((T, N), BF16),
        scratch_shapes=[pltpu.VMEM((K, tn), BF16), pltpu.VMEM((K, tn), BF16)],
        compiler_params=_cparams(("arbitrary", "arbitrary")),
        name=name,
    )(oa1, oa2, ob, z_gate, z_gate, wa, wb)


def _resid_kernel(x_ref, h1_ref, h2_ref, w_ref, o_ref, wb_ref, *, nfirst):
    i = pl.program_id(1)

    @pl.when(i == 0)
    def _():
        wb_ref[...] = w_ref[...].astype(BF16)

    o_ref[...] = _pick(i, nfirst, h1_ref, h2_ref) + _dot(x_ref[...], wb_ref[...])


def resid_matmul(x, h1, h2, w, tm=1024, tn=1024, name="resid_matmul"):
    T, K = x.shape
    N = w.shape[1]
    na = h1.shape[0] // tm
    return pl.pallas_call(
        functools.partial(_resid_kernel, nfirst=na),
        grid=(N // tn, T // tm),
        in_specs=[
            pl.BlockSpec((tm, K), lambda j, i: (i, 0)),
            *_cat_specs((tm, tn), na, row_axis=1, col_axis=0),
            pl.BlockSpec((K, tn), lambda j, i: (0, j)),
        ],
        out_specs=pl.BlockSpec((tm, tn), lambda j, i: (i, j)),
        out_shape=jax.ShapeDtypeStruct((T, N), F32),
        scratch_shapes=[pltpu.VMEM((K, tn), BF16)],
        compiler_params=_cparams(("arbitrary", "arbitrary")),
        name=name,
    )(x, h1, h2, w)


def _router_kernel(h_ref, nw_ref, wr_ref, br_ref, xn_ref, route_ref, cnt_ref, tri_ref, carry_ref, *, tm):
    i = pl.program_id(0)

    @pl.when(i == 0)
    def _():
        rr = lax.broadcasted_iota(jnp.int32, (tm, tm), 0)
        cc = lax.broadcasted_iota(jnp.int32, (tm, tm), 1)
        tri_ref[...] = (rr > cc).astype(BF16)
        carry_ref[...] = jnp.zeros_like(carry_ref)

    x = h_ref[...]
    ms = jnp.mean(x * x, axis=-1, keepdims=True)
    xn = x * lax.rsqrt(ms + EPS) * nw_ref[...]
    xn_ref[...] = xn
    logits = _dot_x3(xn, _split2(wr_ref[...])) + br_ref[...]
    lane = lax.broadcasted_iota(jnp.int32, (tm, LANES), 1)
    lanef = lane.astype(F32)
    cur = logits
    vals, hots, eids = [], [], []
    for _ in range(TOP_K):
        m = jnp.max(cur, axis=-1, keepdims=True)
        idx = jnp.min(jnp.where(cur == m, lanef, float(LANES)), axis=-1, keepdims=True)
        hot = lanef == idx
        vals.append(m)
        hots.append(hot)
        eids.append(idx)
        cur = jnp.where(hot, -jnp.inf, cur)
    es = [jnp.exp(v - vals[0]) for v in vals]
    den = es[0] + es[1] + es[2] + es[3]
    multi = jnp.zeros((tm, LANES), F32)
    for hot in hots:
        multi = multi + hot.astype(F32)
    base = carry_ref[0:1, :] + _dot(tri_ref[...], multi.astype(BF16))
    route = jnp.zeros((tm, LANES), F32)
    for k in range(TOP_K):
        e_k = eids[k]
        r_k = jnp.sum(jnp.where(hots[k], base, 0.0), axis=-1, keepdims=True)
        w_k = es[k] / den
        route = route + jnp.where(lane == k, e_k, 0.0) + jnp.where(lane == TOP_K + k, r_k, 0.0) \
            + jnp.where(lane == 2 * TOP_K + k, w_k, 0.0)
    route_ref[...] = route
    carry = carry_ref[0:1, :] + jnp.sum(multi, axis=0, keepdims=True)
    carry_ref[...] = jnp.broadcast_to(carry, carry_ref.shape)
    cnt_ref[...] = jnp.broadcast_to(carry, cnt_ref.shape)


def moe_router(h, norm_w, w_router, b_router, tm=512, name="moe_router"):
    T, D = h.shape
    E = w_router.shape[1]
    wr = jnp.pad(w_router, ((0, 0), (0, LANES - E)))
    br = jnp.pad(b_router.reshape(1, E), ((0, 0), (0, LANES - E)), constant_values=NEG_BIG)
    kern = functools.partial(_router_kernel, tm=tm)
    return pl.pallas_call(
        kern,
        grid=(T // tm,),
        in_specs=[
            pl.BlockSpec((tm, D), lambda i: (i, 0)),
            pl.BlockSpec((1, D), lambda i: (0, 0)),
            pl.BlockSpec((D, LANES), lambda i: (0, 0)),
            pl.BlockSpec((1, LANES), lambda i: (0, 0)),
        ],
        out_specs=[
            pl.BlockSpec((tm, D), lambda i: (i, 0)),
            pl.BlockSpec((tm, LANES), lambda i: (i, 0)),
            pl.BlockSpec((8, LANES), lambda i: (0, 0)),
        ],
        out_shape=[jax.ShapeDtypeStruct((T, D), F32),
                   jax.ShapeDtypeStruct((T, LANES), F32),
                   jax.ShapeDtypeStruct((8, LANES), F32)],
        scratch_shapes=[pltpu.VMEM((tm, tm), BF16), pltpu.VMEM((8, LANES), F32)],
        compiler_params=_cparams(("arbitrary",)),
        name=name,
    )(h, norm_w.reshape(1, D), wr, br)


ROW_UNROLL = 8


def _dispatch_kernel(dest_ref, zrow_ref, x_ref, buf_ref, zero_ref, xs_ref, sem, zsem, *, tm, n_tail):
    i = pl.program_id(0)
    E = (zrow_ref.shape[0] - 1) // 2
    ZR = zero_ref.shape[0]
    R = buf_ref.shape[0]

    @pl.when(i == 0)
    def _():
        zero_ref[...] = jnp.zeros_like(zero_ref)

        def zero_copy(row, n):
            return pltpu.make_async_copy(zero_ref.at[pl.ds(0, n)], buf_ref.at[pl.ds(pl.multiple_of(row, 8), n)], zsem)

        def zero_rows(act):
            for e in range(E):
                off = zrow_ref[e]
                n = ZR
                while n >= 8:
                    @pl.when((zrow_ref[E + e] & n) != 0)
                    def _():
                        act(zero_copy(off, n))
                    off = off + (zrow_ref[E + e] & n)
                    n //= 2
            for j in range(n_tail):
                row = zrow_ref[2 * E] + j * ZR

                @pl.when(row < R)
                def _():
                    act(zero_copy(row, ZR))

        zero_rows(lambda c: c.start())
        zero_rows(lambda c: c.wait())

    par = i % 2
    xs_ref[par] = x_ref[...]

    def start(rb, c):
        for j in range(ROW_UNROLL):
            r = rb * ROW_UNROLL + j
            for k in range(TOP_K):
                d = dest_ref[(i * tm + r) * TOP_K + k]
                pltpu.make_async_copy(xs_ref.at[par, pl.ds(r, 1)], buf_ref.at[pl.ds(d, 1)],
                                      sem.at[par]).start(priority=k % 2)
        return c

    lax.fori_loop(0, tm // ROW_UNROLL, start, 0)

    def wait_step(s):
        for k in range(TOP_K):
            pltpu.make_async_copy(xs_ref.at[s], buf_ref.at[pl.ds(0, tm)], sem.at[s]).wait()

    @pl.when(i > 0)
    def _():
        wait_step(1 - par)

    @pl.when(i == pl.num_programs(0) - 1)
    def _():
        wait_step(par)


def moe_dispatch(xn, dest_flat, zrows, n_rows, tm=256, name="moe_dispatch"):
    T, D = xn.shape
    n_tail = (n_rows - T * TOP_K) // MOE_TM
    kern = functools.partial(_dispatch_kernel, tm=tm, n_tail=n_tail)
    return pl.pallas_call(
        kern,
        grid_spec=pltpu.PrefetchScalarGridSpec(
            num_scalar_prefetch=2,
            grid=(T // tm,),
            in_specs=[pl.BlockSpec((tm, D), lambda i, d, z: (i, 0))],
            out_specs=pl.BlockSpec(memory_space=pl.ANY),
            scratch_shapes=[pltpu.VMEM((MOE_TM, D), F32), pltpu.VMEM((2, tm, D), F32),
                            pltpu.SemaphoreType.DMA((2,)), pltpu.SemaphoreType.DMA(())],
        ),
        out_shape=jax.ShapeDtypeStruct((n_rows, D), F32),
        compiler_params=_cparams(("arbitrary",)),
        name=name,
    )(dest_flat, zrows, xn)


def _experts_kernel(ue_ref, r0_ref, nt_ref, na_ref, x_hbm, wg_ref, wu_ref, bg_ref, bu_ref, wd_ref, bd_ref,
                    o_hbm, xs_ref, xb_ref, acc_ref, sem_in, sem_out, *, TM, UT, NF):
    TPS = -(-UT // NF)
    assert TPS <= 2
    u = pl.program_id(0)
    f = pl.program_id(1)
    na = na_ref[0]
    nt = nt_ref[u]
    r0 = r0_ref[u]
    slot = u % 2
    un = jnp.minimum(u + 1, na - 1)

    def prefetch(j):
        return jnp.logical_and(jnp.logical_and(u + 1 < na, j < nt_ref[un]), f == j // TPS)

    def x_copy(base, i):
        rows = pl.ds(pl.multiple_of(base + i * TM, TM), TM)
        return pltpu.make_async_copy(x_hbm.at[rows], xs_ref.at[i % 2], sem_in.at[i % 2])

    def o_copy(base, i):
        rows = pl.ds(pl.multiple_of(base + i * TM, TM), TM)
        return pltpu.make_async_copy(acc_ref.at[i], o_hbm.at[rows], sem_out.at[i])

    def wait_outputs(unit):
        for i in range(UT):
            @pl.when(i < nt_ref[unit])
            def _():
                o_copy(r0_ref[unit], i).wait()

    @pl.when(u < na)
    def _():
        @pl.when(jnp.logical_and(f == 0, u == 0))
        def _():
            x_copy(r0, 0).start()
            for i in range(UT):
                @pl.when(i < nt)
                def _():
                    if i + 1 < UT:
                        @pl.when(i + 1 < nt)
                        def _():
                            x_copy(r0, i + 1).start()
                    x_copy(r0, i).wait()
                    xb_ref[0, i] = xs_ref[i % 2].astype(BF16)

        @pl.when(jnp.logical_and(f == 0, u > 0))
        def _():
            wait_outputs(u - 1)

        for j in range(UT):
            @pl.when(prefetch(j))
            def _():
                x_copy(r0_ref[un], j).start()

        bg = bg_ref[...]
        bu = bu_ref[...]

        def tile_out(xt):
            g = _dot(xt, wg_ref[...].astype(BF16)) + bg
            up = _dot(xt, wu_ref[...].astype(BF16)) + bu
            gate = jnp.minimum(g, SWIGLU_LIMIT)
            lin = jnp.clip(up, -SWIGLU_LIMIT, SWIGLU_LIMIT)
            act = (gate * jax.nn.sigmoid(SWIGLU_ALPHA * gate) * (lin + 1.0)).astype(BF16)
            return _dot(act, wd_ref[...].astype(BF16))

        D = xb_ref.shape[3]

        def update(i, n, first):
            tiles = pl.ds(i, n)
            out = tile_out(xb_ref[slot, tiles].reshape(n * TM, D))
            if first:
                out = out + bd_ref[...]
            else:
                out = out + acc_ref[tiles].reshape(n * TM, D)
            acc_ref[tiles] = out.reshape(n, TM, D)

        def all_tiles(first):
            def body(j, c):
                update(pl.multiple_of(4 * j, 4), 4, first)
                return c
            lax.fori_loop(0, nt // 4, body, 0)
            done4 = (nt // 4) * 4

            @pl.when((nt & 2) != 0)
            def _():
                update(pl.multiple_of(done4, 2), 2, first)

            @pl.when((nt & 1) != 0)
            def _():
                update(nt - 1, 1, first)

        @pl.when(f == 0)
        def _():
            all_tiles(True)

        @pl.when(f > 0)
        def _():
            all_tiles(False)

        for j in range(UT):
            @pl.when(prefetch(j))
            def _():
                x_copy(r0_ref[un], j).wait()
                xb_ref[1 - slot, j] = xs_ref[j % 2].astype(BF16)

        @pl.when(f == NF - 1)
        def _():
            for i in range(UT):
                @pl.when(i < nt)
                def _():
                    o_copy(r0, i).start()

            @pl.when(u == na - 1)
            def _():
                wait_outputs(u)


def moe_experts(x_buf, plan, w_gate_up, b_gate_up, w_down, b_down, name="moe_experts"):
    R, D = x_buf.shape
    E, _, F2 = w_gate_up.shape
    DF = F2 // 2
    TM, UT, TF = MOE_TM, MOE_UNIT_TILES, MOE_TF
    NF = DF // TF
    ue, r0, nt, na = plan
    U = ue.shape[0]

    def fsel(u, f, na_ref):
        return jnp.where(u < na_ref[0], f, NF - 1)

    kern = functools.partial(_experts_kernel, TM=TM, UT=UT, NF=NF)
    return pl.pallas_call(
        kern,
        grid_spec=pltpu.PrefetchScalarGridSpec(
            num_scalar_prefetch=4,
            grid=(U, NF),
            in_specs=[
                pl.BlockSpec(memory_space=pl.ANY),
                pl.BlockSpec((None, D, TF), lambda u, f, ue, r0, nt, na: (ue[u], 0, fsel(u, f, na))),
                pl.BlockSpec((None, D, TF), lambda u, f, ue, r0, nt, na: (ue[u], 0, NF + fsel(u, f, na))),
                pl.BlockSpec((None, 1, TF), lambda u, f, ue, r0, nt, na: (ue[u], 0, fsel(u, f, na))),
                pl.BlockSpec((None, 1, TF), lambda u, f, ue, r0, nt, na: (ue[u], 0, NF + fsel(u, f, na))),
                pl.BlockSpec((None, TF, D), lambda u, f, ue, r0, nt, na: (ue[u], fsel(u, f, na), 0)),
                pl.BlockSpec((None, 1, D), lambda u, f, ue, r0, nt, na: (ue[u], 0, 0)),
            ],
            out_specs=pl.BlockSpec(memory_space=pl.ANY),
            scratch_shapes=[
                pltpu.VMEM((2, TM, D), F32),
                pltpu.VMEM((2, UT, TM, D), BF16),
                pltpu.VMEM((UT, TM, D), F32),
                pltpu.SemaphoreType.DMA((2,)),
                pltpu.SemaphoreType.DMA((UT,)),
            ],
        ),
        out_shape=jax.ShapeDtypeStruct((R, D), F32),
        input_output_aliases={4: 0},
        compiler_params=_cparams(("arbitrary", "arbitrary"), VMEM_LIMIT_EXPERTS_BYTES),
        name=name,
    )(ue, r0, nt, na, x_buf, w_gate_up, w_gate_up, b_gate_up.reshape(E, 1, F2), b_gate_up.reshape(E, 1, F2),
      w_down, b_down.reshape(E, 1, D))


def moe_plan(counts, n_tokens):
    TM, UT = MOE_TM, MOE_UNIT_TILES
    E = counts.shape[0]
    nt = (counts + TM - 1) // TM
    padded = nt * TM
    pad_start = jnp.cumsum(padded) - padded
    nu = (nt + UT - 1) // UT
    cu = jnp.cumsum(nu)
    n_units = cu[-1]
    U = E + (n_tokens * TOP_K) // (TM * UT)
    u = jnp.arange(U, dtype=jnp.int32)
    uc = jnp.minimum(u, n_units - 1)
    ue = jnp.clip(jnp.searchsorted(cu, uc, side='right'), 0, E - 1).astype(jnp.int32)
    j = uc - (cu[ue] - nu[ue])
    r0 = (pad_start[ue] + j * (UT * TM)).astype(jnp.int32)
    ntl = jnp.where(u < n_units, jnp.clip(nt[ue] - j * UT, 0, UT), 0).astype(jnp.int32)
    zstart = (pad_start + counts) // 8 * 8
    pad_end = pad_start + padded
    zrows = jnp.concatenate([zstart, pad_end - zstart, pad_end[-1:]]).astype(jnp.int32)
    return pad_start, zrows, (ue, r0, ntl, n_units.reshape(1).astype(jnp.int32))


def _combine_kernel(dest_ref, h_ref, route_ref, nw_ref, o_hbm, h2_ref, xn_ref, g_ref, sem, *, tm):
    i = pl.program_id(0)
    par = i % 2

    def gather_tile(t, slot):
        def start(rb, c):
            for j in range(ROW_UNROLL):
                r = rb * ROW_UNROLL + j
                for k in range(TOP_K):
                    d = dest_ref[(t * tm + r) * TOP_K + k]
                    pltpu.make_async_copy(o_hbm.at[pl.ds(d, 1)], g_ref.at[slot, k, pl.ds(r, 1)],
                                          sem.at[slot]).start(priority=k % 2)
            return c

        lax.fori_loop(0, tm // ROW_UNROLL, start, 0)

    @pl.when(i == 0)
    def _():
        gather_tile(i, par)

    @pl.when(i + 1 < pl.num_programs(0))
    def _():
        gather_tile(i + 1, 1 - par)

    for k in range(TOP_K):
        pltpu.make_async_copy(o_hbm.at[pl.ds(0, tm)], g_ref.at[par, k], sem.at[par]).wait()

    route = route_ref[...]
    y = g_ref[par, 0] * route[:, 2 * TOP_K:2 * TOP_K + 1]
    for k in range(1, TOP_K):
        y = y + g_ref[par, k] * route[:, 2 * TOP_K + k:2 * TOP_K + k + 1]
    h2 = h_ref[...] + y
    h2_ref[...] = h2
    ms = jnp.mean(h2 * h2, axis=-1, keepdims=True)
    xn_ref[...] = (h2 * lax.rsqrt(ms + EPS) * nw_ref[...]).astype(xn_ref.dtype)


def moe_combine(h, route, dest_flat, out_buf, norm_w, tm=256, name="moe_combine"):
    T, D = h.shape
    kern = functools.partial(_combine_kernel, tm=tm)
    return pl.pallas_call(
        kern,
        grid_spec=pltpu.PrefetchScalarGridSpec(
            num_scalar_prefetch=1,
            grid=(T // tm,),
            in_specs=[pl.BlockSpec((tm, D), lambda i, d: (i, 0)),
                      pl.BlockSpec((tm, LANES), lambda i, d: (i, 0)),
                      pl.BlockSpec((1, D), lambda i, d: (0, 0)),
                      pl.BlockSpec(memory_space=pl.ANY)],
            out_specs=[pl.BlockSpec((tm, D), lambda i, d: (i, 0)),
                       pl.BlockSpec((tm, D), lambda i, d: (i, 0))],
            scratch_shapes=[pltpu.VMEM((2, TOP_K, tm, D), F32), pltpu.SemaphoreType.DMA((2,))],
        ),
        out_shape=[jax.ShapeDtypeStruct((T, D), F32), jax.ShapeDtypeStruct((T, D), BF16)],
        compiler_params=_cparams(("arbitrary",)),
        name=name,
    )(dest_flat, h, route, norm_w.reshape(1, D), out_buf)


def _ple_kernel(xn_ref, p1_ref, p2_ref, h_ref, wg_ref, wp_ref, nf_ref, oa_ref, ob_ref, wgb_ref, wpb_ref, *, nfirst):
    i = pl.program_id(0)

    @pl.when(i == 0)
    def _():
        wgb_ref[...] = wg_ref[...].astype(BF16)
        wpb_ref[...] = wp_ref[...].astype(BF16)

    gate = jax.nn.sigmoid(_dot(xn_ref[...], wgb_ref[...]))
    p = _pick(i, nfirst, p1_ref, p2_ref).astype(BF16)
    y = _rms(h_ref[...] + _dot(p, wpb_ref[...]) * gate, nf_ref[...])

    @pl.when(i < nfirst)
    def _():
        oa_ref[...] = y

    @pl.when(i >= nfirst)
    def _():
        ob_ref[...] = y


def ple_final(xn, p1, p2, h, w_gate, w_ple, norm_w, tm=256, name="ple_final"):
    T, D = h.shape
    Ta, P = p1.shape
    na = Ta // tm
    once = pl.Buffered(1)
    return pl.pallas_call(
        functools.partial(_ple_kernel, nfirst=na),
        grid=(T // tm,),
        in_specs=[
            pl.BlockSpec((tm, D), lambda i: (i, 0)),
            *_cat_specs((tm, P), na),
            pl.BlockSpec((tm, D), lambda i: (i, 0)),
            pl.BlockSpec((D, D), lambda i: (0, 0), pipeline_mode=once),
            pl.BlockSpec((P, D), lambda i: (0, 0), pipeline_mode=once),
            pl.BlockSpec((1, D), lambda i: (0, 0)),
        ],
        out_specs=list(_cat_specs((tm, D), na)),
        out_shape=[jax.ShapeDtypeStruct((Ta, D), F32), jax.ShapeDtypeStruct((T - Ta, D), F32)],
        scratch_shapes=[pltpu.VMEM((D, D), BF16), pltpu.VMEM((P, D), BF16)],
        compiler_params=_cparams(("arbitrary",)),
        name=name,
    )(xn, p1, p2, h, w_gate, w_ple, norm_w.reshape(1, D))


def kernel(x_prompt, x_sample, p_prompt, p_sample, state_hgrn, state_ssm_re, state_ssm_im, norm_mix, w_in, hg_lb, hg_gnorm, w_branch_a, ssm_a_re, ssm_a_im, ssm_log_dt, ssm_b_re, ssm_b_im, ssm_c_re, ssm_c_im, ssm_d, w_glu, w_branch_b, w_out, norm_moe, w_router, b_router, w_gate_up, b_gate_up, w_down, b_down, norm_ple, w_ple, w_ple_gate, norm_final):
    BP, LP, D = x_prompt.shape
    BS, LS, _ = x_sample.shape
    depth = w_in.shape[0]
    assert depth == 1
    TP, TS = BP * LP, BS * LS
    T = TP + TS
    G, P = ssm_a_re.shape[1:]
    KD = HG_HEADS * HG_DK
    W = SSM_GROUP * G

    xp = x_prompt.reshape(TP, D)
    xs = x_sample.reshape(TS, D)

    xn = rmsnorm_cat(xp, xs, norm_mix[0], BF16, name="norm_mix")
    tm = 1024
    w_in0 = w_in[0]
    z_hg = project(xn, w_in0, 0, T // tm, 0, 4, tm, 1024, "proj_hgrn")
    u_p = project(xn, w_in0, 0, TP // tm, 4, 1, tm, 1024, "proj_u_prompt")
    u_s = project(xn, w_in0, TP // tm, TS // tm, 4, 1, tm, 1024, "proj_u_sample")
    z_gate = project(xn, w_in0, 0, T // tm, 5, 4, tm, 1024, "proj_gates")

    lb = jax.nn.softmax(hg_lb.astype(F32), axis=0)[0]
    og_p, hg_p = hgrn2(z_hg, lb, hg_gnorm[0], None, row0=0, B=BP, L=LP, C=HG_CHUNK, nb=1, tok_step=256,
                       name="hgrn_prompt")
    og_s, hg_s = hgrn2(z_hg, lb, hg_gnorm[0], state_hgrn[0], row0=TP, B=BS, L=LS, C=LS, nb=8, tok_step=LS,
                       name="hgrn_sample")

    bd, cd, ab = s5_params(ssm_a_re[0], ssm_a_im[0], ssm_log_dt[0], ssm_b_re[0], ssm_b_im[0],
                           ssm_c_re[0], ssm_c_im[0])
    y_p, re_p, im_p = s5_prompt(u_p.reshape(BP, LP, W), bd, cd, ab, ssm_d[0], TC=256, name="s5_prompt")
    y_s, re_s, im_s = s5_sample(u_s, bd, cd, ab, ssm_d[0], state_ssm_re[0].reshape(BS, G * P),
                                state_ssm_im[0].reshape(BS, G * P), B=BS, L=LS, name="s5_sample")

    glu = gelu_glu(y_p.reshape(TP, W), y_s, w_glu[0])
    merged = gated_merge(og_p, og_s, glu, z_gate, w_branch_a[0], w_branch_b[0])
    h1 = resid_matmul(merged, xp, xs, w_out[0], name="out_proj")

    xn2, route, cnt = moe_router(h1, norm_moe[0], w_router[0], b_router[0])
    e_idx = route[:, 0:TOP_K].astype(jnp.int32)
    rank = route[:, TOP_K:2 * TOP_K].astype(jnp.int32)
    counts = cnt[0, :N_EXPERTS].astype(jnp.int32)
    pad_start, zrows, plan = moe_plan(counts, T)
    onehot = e_idx[..., None] == jnp.arange(N_EXPERTS, dtype=jnp.int32)
    dest = (jnp.sum(jnp.where(onehot, pad_start, 0), axis=-1) + rank).astype(jnp.int32).reshape(T * TOP_K)
    n_rows = T * TOP_K + N_EXPERTS * MOE_TM
    x_buf = moe_dispatch(xn2, dest, zrows, n_rows)
    out_buf = moe_experts(x_buf, plan, w_gate_up[0], b_gate_up[0], w_down[0], b_down[0])
    h2, xn3 = moe_combine(h1, route, dest, out_buf, norm_ple[0])

    y_p2, y_s2 = ple_final(xn3, p_prompt[0].reshape(TP, -1), p_sample[0].reshape(TS, -1), h2, w_ple_gate[0],
                           w_ple[0], norm_final)

    y_prompt = y_p2.reshape(BP, LP, D)
    y_sample = y_s2.reshape(BS, LS, D)
    hs = (1, BP, HG_HEADS, HG_DK, HG_DV)
    return (y_prompt, y_sample,
            hg_p.reshape(hs), re_p.reshape(1, BP, G, P), im_p.reshape(1, BP, G, P),
            hg_s.reshape(1, BS, HG_HEADS, HG_DK, HG_DV), re_s.reshape(1, BS, G, P), im_s.reshape(1, BS, G, P))
```

```python
import functools

import jax
import jax.numpy as jnp
from jax import lax
from jax.experimental import pallas as pl
from jax.experimental.pallas import tpu as pltpu

F32 = jnp.float32
BF16 = jnp.bfloat16
EPS = 1e-6

HG_HEADS = 8
HG_DK = 128
HG_DV = 128
HG_CHUNK = 32
SSM_GROUP = 16
SSM_STATE = 64
N_EXPERTS = 32
TOP_K = 4
SWIGLU_LIMIT = 7.0
SWIGLU_ALPHA = 1.702

LANES = 128
VMEM_LIMIT_BYTES = 56 * 1024 * 1024
VMEM_LIMIT_EXPERTS_BYTES = 61 * 1024 * 1024

MOE_TM = 256
MOE_UNIT_TILES = 6
MOE_TF = 512
NEG_BIG = -1e30


def _cparams(sem, vmem_limit=VMEM_LIMIT_BYTES):
    return pltpu.CompilerParams(dimension_semantics=sem, vmem_limit_bytes=vmem_limit)


def _dot(a, b):
    return jnp.dot(a, b, preferred_element_type=F32)


def _split2(a):
    hi = a.astype(BF16)
    lo = (a - hi.astype(F32)).astype(BF16)
    return hi, lo


def _dot_x3(a, b_split):
    a1, a2 = _split2(a)
    b1, b2 = b_split
    return _dot(a1, b1) + (_dot(a1, b2) + _dot(a2, b1))


def _cat_specs(shape, nfirst, row_axis=0, col_axis=None):
    def col(ids):
        return 0 if col_axis is None else ids[col_axis]

    first = pl.BlockSpec(shape, lambda *ids: (jnp.minimum(ids[row_axis], nfirst - 1), col(ids)))
    second = pl.BlockSpec(shape, lambda *ids: (jnp.maximum(ids[row_axis] - nfirst, 0), col(ids)))
    return first, second


def _pick(i, nfirst, a_ref, b_ref):
    return jnp.where(i < nfirst, a_ref[...], b_ref[...])


def _rms(x, w):
    ms = jnp.mean(x * x, axis=-1, keepdims=True)
    return x * lax.rsqrt(ms + EPS) * w


def _rms_in2_kernel(a_ref, b_ref, w_ref, o_ref, *, nfirst):
    x = _pick(pl.program_id(0), nfirst, a_ref, b_ref)
    o_ref[...] = _rms(x, w_ref[...]).astype(o_ref.dtype)


def rmsnorm_cat(xa, xb, w, out_dtype, tm=512, name="rmsnorm"):
    (Ta, D), Tb = xa.shape, xb.shape[0]
    na = Ta // tm
    return pl.pallas_call(
        functools.partial(_rms_in2_kernel, nfirst=na),
        grid=((Ta + Tb) // tm,),
        in_specs=[*_cat_specs((tm, D), na), pl.BlockSpec((1, D), lambda i: (0, 0))],
        out_specs=pl.BlockSpec((tm, D), lambda i: (i, 0)),
        out_shape=jax.ShapeDtypeStruct((Ta + Tb, D), out_dtype),
        compiler_params=_cparams(("arbitrary",)),
        name=name,
    )(xa, xb, w.reshape(1, D))


def _proj_kernel(x_ref, w_ref, o_ref, wb_ref):
    @pl.when(pl.program_id(1) == 0)
    def _():
        wb_ref[...] = w_ref[...].astype(BF16)

    o_ref[...] = _dot(x_ref[...], wb_ref[...]).astype(o_ref.dtype)


def project(xn, w, row_blk0, n_row_blks, col_blk0, n_col_blks, tm, tn, name, out_dtype=F32):
    K = xn.shape[1]
    return pl.pallas_call(
        _proj_kernel,
        grid=(n_col_blks, n_row_blks),
        in_specs=[
            pl.BlockSpec((tm, K), lambda j, i: (row_blk0 + i, 0)),
            pl.BlockSpec((K, tn), lambda j, i: (0, col_blk0 + j)),
        ],
        out_specs=pl.BlockSpec((tm, tn), lambda j, i: (i, j)),
        out_shape=jax.ShapeDtypeStruct((n_row_blks * tm, n_col_blks * tn), out_dtype),
        scratch_shapes=[pltpu.VMEM((K, tn), BF16)],
        compiler_params=_cparams(("arbitrary", "arbitrary")),
        name=name,
    )(xn, w)


HGRN_CHUNKS_PER_TRIP = 4


def _hgrn_kernel(*refs, C, nchunk, nb, tok_step, has_s0, state_t):
    if has_s0:
        q_ref, f_ref, v_ref, g_ref, lb_ref, gn_ref, s0_ref, og_ref, so_ref, st_ref = refs
    else:
        q_ref, f_ref, v_ref, g_ref, lb_ref, gn_ref, og_ref, so_ref, st_ref = refs
        s0_ref = None
    H, DK, DV = HG_HEADS, HG_DK, HG_DV
    step = pl.program_id(1)

    @pl.when(step == 0)
    def _():
        for i in range(nb):
            for h in range(H):
                if has_s0:
                    st_ref[i, h] = s0_ref[i, h].T if state_t else s0_ref[i, h]
                else:
                    st_ref[i, h] = jnp.zeros(st_ref.shape[2:], F32)

    lbv = lb_ref[...]
    gnv = gn_ref[...]
    rr = lax.broadcasted_iota(jnp.int32, (C, C), 0)
    cc = lax.broadcasted_iota(jnp.int32, (C, C), 1)
    causal = rr >= cc
    tri = causal.astype(BF16)
    sls =[slice(h * DK, (h + 1) * DK) for h in range(H)]
    nt_dims = (((1,), (1,)), ((), ()))
    tn_dims = (((0,), (0,)), ((), ()))

    def tn(a, b):
        return lax.dot_general(a, b, tn_dims, preferred_element_type=F32)

    def prepare(r0):
        rows = pl.ds(r0, C)
        zq = q_ref[rows, :]
        zf = f_ref[rows, :]
        q = jax.nn.silu(zq)
        fe = lbv + (1.0 - lbv) * jax.nn.sigmoid(zf)
        k = 1.0 - fe
        gl = jnp.log(fe)
        g1 = gl.astype(BF16)
        r1 = gl - g1.astype(F32)
        g2 = r1.astype(BF16)
        g3 = (r1 - g2.astype(F32)).astype(BF16)
        b = _dot(tri, g1) + (_dot(tri, g2) + _dot(tri, g3))
        blast = b[C - 1:C, :]
        return dict(
            qt=(q * jnp.exp(b)).astype(BF16),
            kt=(k * jnp.exp(-b)).astype(BF16),
            kend=(k * jnp.exp(blast - b)).astype(BF16),
            eblast=jnp.exp(blast),
            vb=v_ref[rows, :].astype(BF16),
            gate=jax.nn.silu(g_ref[rows, :]),
        )

    def run(jobs):
        P = [prepare(r0) for _, r0 in jobs]
        atts = [[lax.dot_general(p["qt"][:, sl], p["kt"][:, sl], nt_dims, preferred_element_type=F32)
                 for sl in sls] for p in P]
        if state_t:
            upds = [[tn(p["vb"][:, sl], p["kend"][:, sl]) for sl in sls] for p in P]
            decs = [[p["eblast"][:, sl] for sl in sls] for p in P]
        else:
            upds = [[tn(p["kend"][:, sl], p["vb"][:, sl]) for sl in sls] for p in P]
            decs = [[jnp.broadcast_to(p["eblast"][:, sl], (8, DK)).T[:, 0:1] for sl in sls] for p in P]
        state = {}
        inters = []
        for j, (i, _) in enumerate(jobs):
            row = []
            for h, sl in enumerate(sls):
                s = state[(i, h)] if (i, h) in state else st_ref[i, h]
                qth = P[j]["qt"][:, sl]
                if state_t:
                    row.append(lax.dot_general(qth, s.astype(BF16), nt_dims, preferred_element_type=F32))
                else:
                    row.append(_dot(qth, s.astype(BF16)))
                state[(i, h)] = s * decs[j][h] + upds[j][h]
            inters.append(row)
        for (i, h), s in state.items():
            st_ref[i, h] = s
        outs = []
        for j, p in enumerate(P):
            cols = []
            for h, sl in enumerate(sls):
                att = jnp.where(causal, atts[j][h], 0.0).astype(BF16)
                o = _dot(att, p["vb"][:, sl]) + inters[j][h]
                ms = jnp.mean(o * o, axis=-1, keepdims=True)
                cols.append(o * lax.rsqrt(ms + EPS) * gnv * p["gate"][:, sl])
            outs.append(jnp.concatenate(cols, axis=1))
        return outs

    if nchunk == 1:
        outs = run([(i, i * tok_step) for i in range(nb)])
        og_ref[...] = jnp.concatenate(outs, axis=0).astype(og_ref.dtype)
    else:
        assert nb == 1 and nchunk % HGRN_CHUNKS_PER_TRIP == 0

        def body(t, carry):
            r0s = [pl.multiple_of((HGRN_CHUNKS_PER_TRIP * t + j) * C, C) for j in range(HGRN_CHUNKS_PER_TRIP)]
            outs = run([(0, r0) for r0 in r0s])
            for r0, o in zip(r0s, outs):
                og_ref[pl.ds(r0, C), :] = o.astype(og_ref.dtype)
            return carry

        lax.fori_loop(0, nchunk // HGRN_CHUNKS_PER_TRIP, body, 0)

    @pl.when(step == pl.num_programs(1) - 1)
    def _():
        for i in range(nb):
            for h in range(H):
                so_ref[i, h] = st_ref[i, h].T if state_t else st_ref[i, h]


def hgrn2(z_hg, lb, gn, s0, *, row0, B, L, C, nb, tok_step, name):
    H, DK, DV = HG_HEADS, HG_DK, HG_DV
    W = H * DK
    steps = L // tok_step
    rows = nb * tok_step
    nchunk = tok_step // C
    blk0 = row0 // rows
    has_s0 = s0 is not None

    def zspec(col):
        return pl.BlockSpec((rows, W), lambda bb, s: (blk0 + bb * steps + s, col))

    in_specs = [zspec(0), zspec(1), zspec(2), zspec(3),
                pl.BlockSpec((1, W), lambda bb, s: (0, 0)),
                pl.BlockSpec((1, DV), lambda bb, s: (0, 0))]
    args = [z_hg, z_hg, z_hg, z_hg, lb.reshape(1, W), gn.reshape(1, DV)]
    if has_s0:
        in_specs.append(pl.BlockSpec((nb, H, DK, DV), lambda bb, s: (bb, 0, 0, 0)))
        args.append(s0)
    state_t = L > C
    kern = functools.partial(_hgrn_kernel, C=C, nchunk=nchunk, nb=nb, tok_step=tok_step, has_s0=has_s0,
                             state_t=state_t)
    return pl.pallas_call(
        kern,
        grid=(B // nb, steps),
        in_specs=in_specs,
        out_specs=[pl.BlockSpec((rows, W), lambda bb, s: (bb * steps + s, 0)),
                   pl.BlockSpec((nb, H, DK, DV), lambda bb, s: (bb, 0, 0, 0))],
        out_shape=[jax.ShapeDtypeStruct((B * L, W), BF16),
                   jax.ShapeDtypeStruct((B, H, DK, DV), F32)],
        scratch_shapes=[pltpu.VMEM((nb, H, DV, DK), F32)],
        compiler_params=_cparams(("arbitrary", "arbitrary")),
        name=name,
    )(*args)


def _s5_step(ar, ai, xr, xi, br, bi):
    return ar * xr - ai * xi + br, ar * xi + ai * xr + bi


S5_STEPS_PER_TRIP = 4
S5_GROUPS_PER_BLOCK = 16


def _s5_prompt_kernel(u_ref, bd_ref, cd_ref, a_ref, d_ref, y_ref, xr_ref, xi_ref, X_ref, st_ref, *, B, TC):
    NP = X_ref.shape[0] // 2
    assert 2 * B == 8
    tc = pl.program_id(1)

    def rows_of(b, s):
        return pl.ds(2 * b + s, TC, stride=8)

    @pl.when(tc == 0)
    def _():
        st_ref[...] = jnp.zeros_like(st_ref)

    bd = bd_ref[...].astype(BF16)
    for b in range(B):
        bu = _dot(u_ref[b].astype(BF16), bd)
        for c in range(4 * NP):
            X_ref[c // 2, rows_of(b, c % 2), :] = bu[:, c * LANES:(c + 1) * LANES]

    odd = lax.broadcasted_iota(jnp.int32, (8, LANES), 0) % 2 == 1

    def pair_const(row, p):
        lo = a_ref[row:row + 1, (2 * p) * LANES:(2 * p + 1) * LANES]
        hi = a_ref[row:row + 1, (2 * p + 1) * LANES:(2 * p + 2) * LANES]
        return jnp.where(odd, hi, lo)

    ars = [pair_const(0, p) for p in range(NP)]
    ais = [pair_const(1, p) for p in range(NP)]

    def body(i, carry):
        carry = list(carry)
        for j in range(S5_STEPS_PER_TRIP):
            rows = pl.ds(pl.multiple_of((i * S5_STEPS_PER_TRIP + j) * 8, 8), 8)
            for p in range(NP):
                xr, xi = _s5_step(ars[p], ais[p], carry[p][0], carry[p][1], X_ref[p, rows, :], X_ref[NP + p, rows, :])
                X_ref[p, rows, :] = xr
                X_ref[NP + p, rows, :] = xi
                carry[p] = (xr, xi)
        return tuple(carry)

    init = tuple((st_ref[p], st_ref[NP + p]) for p in range(NP))
    fin = lax.fori_loop(0, TC // S5_STEPS_PER_TRIP, body, init)
    for p in range(NP):
        st_ref[p] = fin[p][0]
        st_ref[NP + p] = fin[p][1]

    cd = cd_ref[...].astype(BF16)
    dv = d_ref[...]
    for b in range(B):
        xb = jnp.concatenate([X_ref[c // 2, rows_of(b, c % 2), :] for c in range(4 * NP)], axis=1).astype(BF16)
        y_ref[b] = _dot(xb, cd) + dv * u_ref[b]

    @pl.when(tc == pl.num_programs(1) - 1)
    def _():
        for c in range(2 * NP):
            cols = slice(c * LANES, (c + 1) * LANES)
            xr_ref[:, cols] = st_ref[c // 2, pl.ds(c % 2, B, stride=2), :]
            xi_ref[:, cols] = st_ref[NP + c // 2, pl.ds(c % 2, B, stride=2), :]


def s5_prompt(u, bd, cd, ab, dskip, *, TC, name):
    B, L, W = u.shape
    GB, UB, SW = bd.shape
    half = SW // 2
    kern = functools.partial(_s5_prompt_kernel, B=B, TC=TC)
    return pl.pallas_call(
        kern,
        grid=(GB, L // TC),
        in_specs=[
            pl.BlockSpec((B, TC, UB), lambda g, t: (0, t, g)),
            pl.BlockSpec((None, UB, SW), lambda g, t: (g, 0, 0)),
            pl.BlockSpec((None, SW, UB), lambda g, t: (g, 0, 0)),
            pl.BlockSpec((None, 2, half), lambda g, t: (g, 0, 0)),
            pl.BlockSpec((1, UB), lambda g, t: (0, g)),
        ],
        out_specs=[
            pl.BlockSpec((B, TC, UB), lambda g, t: (0, t, g)),
            pl.BlockSpec((B, half), lambda g, t: (0, g)),
            pl.BlockSpec((B, half), lambda g, t: (0, g)),
        ],
        out_shape=[jax.ShapeDtypeStruct((B, L, W), F32),
                   jax.ShapeDtypeStruct((B, GB * half), F32),
                   jax.ShapeDtypeStruct((B, GB * half), F32)],
        scratch_shapes=[pltpu.VMEM((SW // LANES // 2, 8 * TC, LANES), F32),
                        pltpu.VMEM((SW // LANES // 2, 8, LANES), F32)],
        compiler_params=_cparams(("arbitrary", "arbitrary")),
        name=name,
    )(u, bd, cd, ab, dskip.reshape(1, W))


def _s5_sample_kernel(u_ref, bd_ref, cd_ref, a_ref, d_ref, x0r_ref, x0i_ref, y_ref, xr_ref, xi_ref, X_ref, *, B, L):
    NC = X_ref.shape[0]
    NH = NC // 2
    u = u_ref[...]
    bu = _dot_x3(u, _split2(bd_ref[...]))
    for c in range(NC):
        X_ref[c] = bu[:, c * LANES:(c + 1) * LANES]
    for c in range(NH):
        cols = slice(c * LANES, (c + 1) * LANES)
        ar = a_ref[0:1, cols]
        ai = a_ref[1:2, cols]
        xr = x0r_ref[:, cols]
        xi = x0i_ref[:, cols]
        for t in range(L):
            rows = pl.ds(t, B, stride=L)
            xr, xi = _s5_step(ar, ai, xr, xi, X_ref[c, rows, :], X_ref[NH + c, rows, :])
            X_ref[c, rows, :] = xr
            X_ref[NH + c, rows, :] = xi
        xr_ref[:, cols] = xr
        xi_ref[:, cols] = xi
    xs = jnp.concatenate([X_ref[c] for c in range(NC)], axis=1).astype(BF16)
    y_ref[...] = _dot(xs, cd_ref[...].astype(BF16)) + d_ref[...] * u


def s5_sample(u, bd, cd, ab, dskip, x0r, x0i, *, B, L, name):
    T, W = u.shape
    GB, UB, SW = bd.shape
    half = SW // 2
    kern = functools.partial(_s5_sample_kernel, B=B, L=L)
    return pl.pallas_call(
        kern,
        grid=(GB,),
        in_specs=[
            pl.BlockSpec((T, UB), lambda g: (0, g)),
            pl.BlockSpec((None, UB, SW), lambda g: (g, 0, 0)),
            pl.BlockSpec((None, SW, UB), lambda g: (g, 0, 0)),
            pl.BlockSpec((None, 2, half), lambda g: (g, 0, 0)),
            pl.BlockSpec((1, UB), lambda g: (0, g)),
            pl.BlockSpec((B, half), lambda g: (0, g)),
            pl.BlockSpec((B, half), lambda g: (0, g)),
        ],
        out_specs=[
            pl.BlockSpec((T, UB), lambda g: (0, g)),
            pl.BlockSpec((B, half), lambda g: (0, g)),
            pl.BlockSpec((B, half), lambda g: (0, g)),
        ],
        out_shape=[jax.ShapeDtypeStruct((T, W), F32),
                   jax.ShapeDtypeStruct((B, GB * half), F32),
                   jax.ShapeDtypeStruct((B, GB * half), F32)],
        scratch_shapes=[pltpu.VMEM((SW // LANES, T, LANES), F32)],
        compiler_params=_cparams(("arbitrary",)),
        name=name,
    )(u, bd, cd, ab, dskip.reshape(1, W), x0r, x0i)


def s5_params(a_re, a_im, log_dt, b_re, b_im, c_re, c_im):
    G, P = a_re.shape
    gpb = S5_GROUPS_PER_BLOCK
    GB = G // gpb
    A = lax.complex(a_re, a_im)
    dt = jnp.exp(log_dt)[:, None]
    A_bar = jnp.exp(A * dt)
    Bm = lax.complex(b_re, b_im)
    B_bar = ((A_bar - 1.0) / A)[..., None] * Bm
    eye = jnp.eye(gpb, dtype=F32)

    def in_map(m):
        m = m.reshape(GB, gpb, P, SSM_GROUP)
        return jnp.einsum('bgpc,gh->bgchp', m, eye).reshape(GB, gpb * SSM_GROUP, gpb * P)

    def out_map(m):
        m = m.reshape(GB, gpb, SSM_GROUP, P)
        return jnp.einsum('bgcp,gh->bgphc', m, eye).reshape(GB, gpb * P, gpb * SSM_GROUP)

    bd = jnp.concatenate([in_map(B_bar.real), in_map(B_bar.imag)], axis=2)
    cd = jnp.concatenate([out_map(c_re), out_map(-c_im)], axis=1)
    ab = jnp.stack([A_bar.real.reshape(GB, gpb * P), A_bar.imag.reshape(GB, gpb * P)], axis=1)
    return bd, cd, ab


def _glu_kernel(ya_ref, yb_ref, w_ref, o_ref, wb_ref, *, nfirst):
    i = pl.program_id(0)

    @pl.when(i == 0)
    def _():
        wb_ref[...] = w_ref[...].astype(BF16)

    zg = jax.nn.gelu(_pick(i, nfirst, ya_ref, yb_ref))
    o_ref[...] = (zg * jax.nn.sigmoid(_dot(zg.astype(BF16), wb_ref[...]))).astype(o_ref.dtype)


def gelu_glu(ya, yb, w, tm=512, name="gelu_glu"):
    (Ta, W), Tb = ya.shape, yb.shape[0]
    na = Ta // tm
    return pl.pallas_call(
        functools.partial(_glu_kernel, nfirst=na),
        grid=((Ta + Tb) // tm,),
        in_specs=[*_cat_specs((tm, W), na), pl.BlockSpec((W, W), lambda i: (0, 0))],
        out_specs=pl.BlockSpec((tm, W), lambda i: (i, 0)),
        out_shape=jax.ShapeDtypeStruct((Ta + Tb, W), BF16),
        scratch_shapes=[pltpu.VMEM((W, W), BF16)],
        compiler_params=_cparams(("arbitrary",)),
        name=name,
    )(ya, yb, w)


def _merge_kernel(oa1_ref, oa2_ref, ob_ref, ga_ref, gb_ref, wa_ref, wb_ref, o_ref, wab_ref, wbb_ref, *, nfirst):
    i = pl.program_id(1)

    @pl.when(i == 0)
    def _():
        wab_ref[...] = wa_ref[...].astype(BF16)
        wbb_ref[...] = wb_ref[...].astype(BF16)

    ya = _dot(_pick(i, nfirst, oa1_ref, oa2_ref), wab_ref[...])
    yb = _dot(ob_ref[...], wbb_ref[...])
    ga = jax.nn.sigmoid(ga_ref[...].astype(F32))
    gb = jax.nn.sigmoid(gb_ref[...].astype(F32))
    o_ref[...] = (ga * ya + gb * yb).astype(o_ref.dtype)


def gated_merge(oa1, oa2, ob, z_gate, wa, wb, tm=1024, tn=1024, name="gated_merge"):
    T, K = ob.shape
    N = wa.shape[1]
    nj = N // tn
    na = oa1.shape[0] // tm
    return pl.pallas_call(
        functools.partial(_merge_kernel, nfirst=na),
        grid=(nj, T // tm),
        in_specs=[
            *_cat_specs((tm, K), na, row_axis=1),
            pl.BlockSpec((tm, K), lambda j, i: (i, 0)),
            pl.BlockSpec((tm, tn), lambda j, i: (i, j)),
            pl.BlockSpec((tm, tn), lambda j, i: (i, nj + j)),
            pl.BlockSpec((K, tn), lambda j, i: (0, j)),
            pl.BlockSpec((K, tn), lambda j, i: (0, j)),
        ],
        out_specs=pl.BlockSpec((tm, tn), lambda j, i: (i, j)),
        out_shape=jax.ShapeDtypeStruct((T, N), BF16),
        scratch_shapes=[pltpu.VMEM((K, tn), BF16), pltpu.VMEM((K, tn), BF16)],
        compiler_params=_cparams(("arbitrary", "arbitrary")),
        name=name,
    )(oa1, oa2, ob, z_gate, z_gate, wa, wb)


def _resid_kernel(x_ref, h1_ref, h2_ref, w_ref, o_ref, wb_ref, *, nfirst):
    i = pl.program_id(1)

    @pl.when(i == 0)
    def _():
        wb_ref[...] = w_ref[...].astype(BF16)

    o_ref[...] = _pick(i, nfirst, h1_ref, h2_ref) + _dot(x_ref[...], wb_ref[...])


def resid_matmul(x, h1, h2, w, tm=1024, tn=1024, name="resid_matmul"):
    T, K = x.shape
    N = w.shape[1]
    na = h1.shape[0] // tm
    return pl.pallas_call(
        functools.partial(_resid_kernel, nfirst=na),
        grid=(N // tn, T // tm),
        in_specs=[
            pl.BlockSpec((tm, K), lambda j, i: (i, 0)),
            *_cat_specs((tm, tn), na, row_axis=1, col_axis=0),
            pl.BlockSpec((K, tn), lambda j, i: (0, j)),
        ],
        out_specs=pl.BlockSpec((tm, tn), lambda j, i: (i, j)),
        out_shape=jax.ShapeDtypeStruct((T, N), F32),
        scratch_shapes=[pltpu.VMEM((K, tn), BF16)],
        compiler_params=_cparams(("arbitrary", "arbitrary")),
        name=name,
    )(x, h1, h2, w)


def _router_kernel(h_ref, nw_ref, wr_ref, br_ref, xn_ref, route_ref, cnt_ref, tri_ref, carry_ref, *, tm):
    i = pl.program_id(0)

    @pl.when(i == 0)
    def _():
        rr = lax.broadcasted_iota(jnp.int32, (tm, tm), 0)
        cc = lax.broadcasted_iota(jnp.int32, (tm, tm), 1)
        tri_ref[...] = (rr > cc).astype(BF16)
        carry_ref[...] = jnp.zeros_like(carry_ref)

    x = h_ref[...]
    ms = jnp.mean(x * x, axis=-1, keepdims=True)
    xn = x * lax.rsqrt(ms + EPS) * nw_ref[...]
    xn_ref[...] = xn
    logits = _dot(xn.astype(BF16), wr_ref[...].astype(BF16)) + br_ref[...]
    lane = lax.broadcasted_iota(jnp.int32, (tm, LANES), 1)
    lanef = lane.astype(F32)
    cur = logits
    vals, hots, eids = [], [], []
    for _ in range(TOP_K):
        m = jnp.max(cur, axis=-1, keepdims=True)
        idx = jnp.min(jnp.where(cur == m, lanef, float(LANES)), axis=-1, keepdims=True)
        hot = lanef == idx
        vals.append(m)
        hots.append(hot)
        eids.append(idx)
        cur = jnp.where(hot, -jnp.inf, cur)
    es = [jnp.exp(v - vals[0]) for v in vals]
    den = es[0] + es[1] + es[2] + es[3]
    multi = jnp.zeros((tm, LANES), F32)
    for hot in hots:
        multi = multi + hot.astype(F32)
    base = carry_ref[0:1, :] + _dot(tri_ref[...], multi.astype(BF16))
    route = jnp.zeros((tm, LANES), F32)
    for k in range(TOP_K):
        e_k = eids[k]
        r_k = jnp.sum(jnp.where(hots[k], base, 0.0), axis=-1, keepdims=True)
        w_k = es[k] / den
        route = route + jnp.where(lane == k, e_k, 0.0) + jnp.where(lane == TOP_K + k, r_k, 0.0) \
            + jnp.where(lane == 2 * TOP_K + k, w_k, 0.0)
    route_ref[...] = route
    carry = carry_ref[0:1, :] + jnp.sum(multi, axis=0, keepdims=True)
    carry_ref[...] = jnp.broadcast_to(carry, carry_ref.shape)
    cnt_ref[...] = jnp.broadcast_to(carry, cnt_ref.shape)


def moe_router(h, norm_w, w_router, b_router, tm=512, name="moe_router"):
    T, D = h.shape
    E = w_router.shape[1]
    wr = jnp.pad(w_router, ((0, 0), (0, LANES - E)))
    br = jnp.pad(b_router.reshape(1, E), ((0, 0), (0, LANES - E)), constant_values=NEG_BIG)
    kern = functools.partial(_router_kernel, tm=tm)
    return pl.pallas_call(
        kern,
        grid=(T // tm,),
        in_specs=[
            pl.BlockSpec((tm, D), lambda i: (i, 0)),
            pl.BlockSpec((1, D), lambda i: (0, 0)),
            pl.BlockSpec((D, LANES), lambda i: (0, 0)),
            pl.BlockSpec((1, LANES), lambda i: (0, 0)),
        ],
        out_specs=[
            pl.BlockSpec((tm, D), lambda i: (i, 0)),
            pl.BlockSpec((tm, LANES), lambda i: (i, 0)),
            pl.BlockSpec((8, LANES), lambda i: (0, 0)),
        ],
        out_shape=[jax.ShapeDtypeStruct((T, D), F32),
                   jax.ShapeDtypeStruct((T, LANES), F32),
                   jax.ShapeDtypeStruct((8, LANES), F32)],
        scratch_shapes=[pltpu.VMEM((tm, tm), BF16), pltpu.VMEM((8, LANES), F32)],
        compiler_params=_cparams(("arbitrary",)),
        name=name,
    )(h, norm_w.reshape(1, D), wr, br)


ROW_UNROLL = 8


def _dispatch_kernel(dest_ref, zrow_ref, x_ref, buf_ref, zero_ref, xs_ref, sem, zsem, *, tm, n_tail):
    i = pl.program_id(0)
    E = (zrow_ref.shape[0] - 1) // 2
    ZR = zero_ref.shape[0]
    R = buf_ref.shape[0]

    @pl.when(i == 0)
    def _():
        zero_ref[...] = jnp.zeros_like(zero_ref)

        def zero_copy(row, n):
            return pltpu.make_async_copy(zero_ref.at[pl.ds(0, n)], buf_ref.at[pl.ds(pl.multiple_of(row, 8), n)], zsem)

        def zero_rows(act):
            for e in range(E):
                off = zrow_ref[e]
                n = ZR
                while n >= 8:
                    @pl.when((zrow_ref[E + e] & n) != 0)
                    def _():
                        act(zero_copy(off, n))
                    off = off + (zrow_ref[E + e] & n)
                    n //= 2
            for j in range(n_tail):
                row = zrow_ref[2 * E] + j * ZR

                @pl.when(row < R)
                def _():
                    act(zero_copy(row, ZR))

        zero_rows(lambda c: c.start())
        zero_rows(lambda c: c.wait())

    par = i % 2
    xs_ref[par] = x_ref[...]

    def start(rb, c):
        for j in range(ROW_UNROLL):
            r = rb * ROW_UNROLL + j
            for k in range(TOP_K):
                d = dest_ref[(i * tm + r) * TOP_K + k]
                pltpu.make_async_copy(xs_ref.at[par, pl.ds(r, 1)], buf_ref.at[pl.ds(d, 1)],
                                      sem.at[par]).start(priority=k % 2)
        return c

    lax.fori_loop(0, tm // ROW_UNROLL, start, 0)

    def wait_step(s):
        for k in range(TOP_K):
            pltpu.make_async_copy(xs_ref.at[s], buf_ref.at[pl.ds(0, tm)], sem.at[s]).wait()

    @pl.when(i > 0)
    def _():
        wait_step(1 - par)

    @pl.when(i == pl.num_programs(0) - 1)
    def _():
        wait_step(par)


def moe_dispatch(xn, dest_flat, zrows, n_rows, tm=256, name="moe_dispatch"):
    T, D = xn.shape
    n_tail = (n_rows - T * TOP_K) // MOE_TM
    kern = functools.partial(_dispatch_kernel, tm=tm, n_tail=n_tail)
    return pl.pallas_call(
        kern,
        grid_spec=pltpu.PrefetchScalarGridSpec(
            num_scalar_prefetch=2,
            grid=(T // tm,),
            in_specs=[pl.BlockSpec((tm, D), lambda i, d, z: (i, 0))],
            out_specs=pl.BlockSpec(memory_space=pl.ANY),
            scratch_shapes=[pltpu.VMEM((MOE_TM, D), F32), pltpu.VMEM((2, tm, D), F32),
                            pltpu.SemaphoreType.DMA((2,)), pltpu.SemaphoreType.DMA(())],
        ),
        out_shape=jax.ShapeDtypeStruct((n_rows, D), F32),
        compiler_params=_cparams(("arbitrary",)),
        name=name,
    )(dest_flat, zrows, xn)


def _experts_kernel(ue_ref, r0_ref, nt_ref, na_ref, x_hbm, wg_ref, wu_ref, bg_ref, bu_ref, wd_ref, bd_ref,
                    o_hbm, xs_ref, xb_ref, acc_ref, sem_in, sem_out, *, TM, UT, NF):
    TPS = -(-UT // NF)
    assert TPS <= 2
    u = pl.program_id(0)
    f = pl.program_id(1)
    na = na_ref[0]
    nt = nt_ref[u]
    r0 = r0_ref[u]
    slot = u % 2
    un = jnp.minimum(u + 1, na - 1)

    def prefetch(j):
        return jnp.logical_and(jnp.logical_and(u + 1 < na, j < nt_ref[un]), f == j // TPS)

    def x_copy(base, i):
        rows = pl.ds(pl.multiple_of(base + i * TM, TM), TM)
        return pltpu.make_async_copy(x_hbm.at[rows], xs_ref.at[i % 2], sem_in.at[i % 2])

    def o_copy(base, i):
        rows = pl.ds(pl.multiple_of(base + i * TM, TM), TM)
        return pltpu.make_async_copy(acc_ref.at[i], o_hbm.at[rows], sem_out.at[i])

    def wait_outputs(unit):
        for i in range(UT):
            @pl.when(i < nt_ref[unit])
            def _():
                o_copy(r0_ref[unit], i).wait()

    @pl.when(u < na)
    def _():
        @pl.when(jnp.logical_and(f == 0, u == 0))
        def _():
            x_copy(r0, 0).start()
            for i in range(UT):
                @pl.when(i < nt)
                def _():
                    if i + 1 < UT:
                        @pl.when(i + 1 < nt)
                        def _():
                            x_copy(r0, i + 1).start()
                    x_copy(r0, i).wait()
                    xb_ref[0, i] = xs_ref[i % 2].astype(BF16)

        for j in range(UT):
            @pl.when(prefetch(j))
            def _():
                x_copy(r0_ref[un], j).start()

        @pl.when(jnp.logical_and(f == 0, u > 0))
        def _():
            wait_outputs(u - 1)

        bg = bg_ref[...]
        bu = bu_ref[...]

        def tile_out(xt):
            g = _dot(xt, wg_ref[...].astype(BF16)) + bg
            up = _dot(xt, wu_ref[...].astype(BF16)) + bu
            gate = jnp.minimum(g, SWIGLU_LIMIT)
            lin = jnp.clip(up, -SWIGLU_LIMIT, SWIGLU_LIMIT)
            act = (gate * jax.nn.sigmoid(SWIGLU_ALPHA * gate) * (lin + 1.0)).astype(BF16)
            return _dot(act, wd_ref[...].astype(BF16))

        D = xb_ref.shape[3]

        def update(i, n, first):
            tiles = pl.ds(i, n)
            out = tile_out(xb_ref[slot, tiles].reshape(n * TM, D))
            if first:
                out = out + bd_ref[...]
            else:
                out = out + acc_ref[tiles].reshape(n * TM, D)
            acc_ref[tiles] = out.reshape(n, TM, D)

            @pl.when(f == NF - 1)
            def _():
                for t in range(n):
                    o_copy(r0, i + t).start()

        def all_tiles(first):
            def body(j, c):
                update(pl.multiple_of(4 * j, 4), 4, first)
                return c
            lax.fori_loop(0, nt // 4, body, 0)
            done4 = (nt // 4) * 4

            @pl.when((nt & 2) != 0)
            def _():
                update(pl.multiple_of(done4, 2), 2, first)

            @pl.when((nt & 1) != 0)
            def _():
                update(nt - 1, 1, first)

        @pl.when(f == 0)
        def _():
            all_tiles(True)

        @pl.when(f > 0)
        def _():
            all_tiles(False)

        for j in range(UT):
            @pl.when(prefetch(j))
            def _():
                x_copy(r0_ref[un], j).wait()
                xb_ref[1 - slot, j] = xs_ref[j % 2].astype(BF16)

        @pl.when(jnp.logical_and(f == NF - 1, u == na - 1))
        def _():
            wait_outputs(u)


def moe_experts(x_buf, plan, w_gate_up, b_gate_up, w_down, b_down, name="moe_experts"):
    R, D = x_buf.shape
    E, _, F2 = w_gate_up.shape
    DF = F2 // 2
    TM, UT, TF = MOE_TM, MOE_UNIT_TILES, MOE_TF
    NF = DF // TF
    ue, r0, nt, na = plan
    U = ue.shape[0]

    def fsel(u, f, na_ref):
        return jnp.where(u < na_ref[0], f, NF - 1)

    kern = functools.partial(_experts_kernel, TM=TM, UT=UT, NF=NF)
    return pl.pallas_call(
        kern,
        grid_spec=pltpu.PrefetchScalarGridSpec(
            num_scalar_prefetch=4,
            grid=(U, NF),
            in_specs=[
                pl.BlockSpec(memory_space=pl.ANY),
                pl.BlockSpec((None, D, TF), lambda u, f, ue, r0, nt, na: (ue[u], 0, fsel(u, f, na))),
                pl.BlockSpec((None, D, TF), lambda u, f, ue, r0, nt, na: (ue[u], 0, NF + fsel(u, f, na))),
                pl.BlockSpec((None, 1, TF), lambda u, f, ue, r0, nt, na: (ue[u], 0, fsel(u, f, na))),
                pl.BlockSpec((None, 1, TF), lambda u, f, ue, r0, nt, na: (ue[u], 0, NF + fsel(u, f, na))),
                pl.BlockSpec((None, TF, D), lambda u, f, ue, r0, nt, na: (ue[u], fsel(u, f, na), 0)),
                pl.BlockSpec((None, 1, D), lambda u, f, ue, r0, nt, na: (ue[u], 0, 0)),
            ],
            out_specs=pl.BlockSpec(memory_space=pl.ANY),
            scratch_shapes=[
                pltpu.VMEM((2, TM, D), F32),
                pltpu.VMEM((2, UT, TM, D), BF16),
                pltpu.VMEM((UT, TM, D), F32),
                pltpu.SemaphoreType.DMA((2,)),
                pltpu.SemaphoreType.DMA((UT,)),
            ],
        ),
        out_shape=jax.ShapeDtypeStruct((R, D), F32),
        input_output_aliases={4: 0},
        compiler_params=_cparams(("arbitrary", "arbitrary"), VMEM_LIMIT_EXPERTS_BYTES),
        name=name,
    )(ue, r0, nt, na, x_buf, w_gate_up, w_gate_up, b_gate_up.reshape(E, 1, F2), b_gate_up.reshape(E, 1, F2),
      w_down, b_down.reshape(E, 1, D))


def moe_plan(counts, n_tokens):
    TM, UT = MOE_TM, MOE_UNIT_TILES
    E = counts.shape[0]
    nt = (counts + TM - 1) // TM
    padded = nt * TM
    pad_start = jnp.cumsum(padded) - padded
    nu = (nt + UT - 1) // UT
    cu = jnp.cumsum(nu)
    n_units = cu[-1]
    U = E + (n_tokens * TOP_K) // (TM * UT)
    u = jnp.arange(U, dtype=jnp.int32)
    uc = jnp.minimum(u, n_units - 1)
    ue = jnp.clip(jnp.searchsorted(cu, uc, side='right'), 0, E - 1).astype(jnp.int32)
    j = uc - (cu[ue] - nu[ue])
    r0 = (pad_start[ue] + j * (UT * TM)).astype(jnp.int32)
    ntl = jnp.where(u < n_units, jnp.clip(nt[ue] - j * UT, 0, UT), 0).astype(jnp.int32)
    zstart = (pad_start + counts) // 8 * 8
    pad_end = pad_start + padded
    zrows = jnp.concatenate([zstart, pad_end - zstart, pad_end[-1:]]).astype(jnp.int32)
    return pad_start, zrows, (ue, r0, ntl, n_units.reshape(1).astype(jnp.int32))


def _combine_kernel(dest_ref, h_ref, route_ref, nw_ref, o_hbm, h2_ref, xn_ref, g_ref, sem, *, tm):
    i = pl.program_id(0)
    par = i % 2

    def gather_tile(t, slot):
        def start(rb, c):
            for j in range(ROW_UNROLL):
                r = rb * ROW_UNROLL + j
                for k in range(TOP_K):
                    d = dest_ref[(t * tm + r) * TOP_K + k]
                    pltpu.make_async_copy(o_hbm.at[pl.ds(d, 1)], g_ref.at[slot, k, pl.ds(r, 1)],
                                          sem.at[slot]).start(priority=k % 2)
            return c

        lax.fori_loop(0, tm // ROW_UNROLL, start, 0)

    @pl.when(i == 0)
    def _():
        gather_tile(i, par)

    @pl.when(i + 1 < pl.num_programs(0))
    def _():
        gather_tile(i + 1, 1 - par)

    for k in range(TOP_K):
        pltpu.make_async_copy(o_hbm.at[pl.ds(0, tm)], g_ref.at[par, k], sem.at[par]).wait()

    route = route_ref[...]
    y = g_ref[par, 0] * route[:, 2 * TOP_K:2 * TOP_K + 1]
    for k in range(1, TOP_K):
        y = y + g_ref[par, k] * route[:, 2 * TOP_K + k:2 * TOP_K + k + 1]
    h2 = h_ref[...] + y
    h2_ref[...] = h2
    ms = jnp.mean(h2 * h2, axis=-1, keepdims=True)
    xn_ref[...] = (h2 * lax.rsqrt(ms + EPS) * nw_ref[...]).astype(xn_ref.dtype)


def moe_combine(h, route, dest_flat, out_buf, norm_w, tm=256, name="moe_combine"):
    T, D = h.shape
    kern = functools.partial(_combine_kernel, tm=tm)
    return pl.pallas_call(
        kern,
        grid_spec=pltpu.PrefetchScalarGridSpec(
            num_scalar_prefetch=1,
            grid=(T // tm,),
            in_specs=[pl.BlockSpec((tm, D), lambda i, d: (i, 0)),
                      pl.BlockSpec((tm, LANES), lambda i, d: (i, 0)),
                      pl.BlockSpec((1, D), lambda i, d: (0, 0)),
                      pl.BlockSpec(memory_space=pl.ANY)],
            out_specs=[pl.BlockSpec((tm, D), lambda i, d: (i, 0)),
                       pl.BlockSpec((tm, D), lambda i, d: (i, 0))],
            scratch_shapes=[pltpu.VMEM((2, TOP_K, tm, D), F32), pltpu.SemaphoreType.DMA((2,))],
        ),
        out_shape=[jax.ShapeDtypeStruct((T, D), F32), jax.ShapeDtypeStruct((T, D), BF16)],
        compiler_params=_cparams(("arbitrary",)),
        name=name,
    )(dest_flat, h, route, norm_w.reshape(1, D), out_buf)


def _ple_kernel(xn_ref, p1_ref, p2_ref, h_ref, wg_ref, wp_ref, nf_ref, oa_ref, ob_ref, wgb_ref, wpb_ref, *, nfirst):
    i = pl.program_id(0)

    @pl.when(i == 0)
    def _():
        wgb_ref[...] = wg_ref[...].astype(BF16)
        wpb_ref[...] = wp_ref[...].astype(BF16)

    gate = jax.nn.sigmoid(_dot(xn_ref[...], wgb_ref[...]))
    p = _pick(i, nfirst, p1_ref, p2_ref).astype(BF16)
    y = _rms(h_ref[...] + _dot(p, wpb_ref[...]) * gate, nf_ref[...])

    @pl.when(i < nfirst)
    def _():
        oa_ref[...] = y

    @pl.when(i >= nfirst)
    def _():
        ob_ref[...] = y


def ple_final(xn, p1, p2, h, w_gate, w_ple, norm_w, tm=256, name="ple_final"):
    T, D = h.shape
    Ta, P = p1.shape
    na = Ta // tm
    once = pl.Buffered(1)
    return pl.pallas_call(
        functools.partial(_ple_kernel, nfirst=na),
        grid=(T // tm,),
        in_specs=[
            pl.BlockSpec((tm, D), lambda i: (i, 0)),
            *_cat_specs((tm, P), na),
            pl.BlockSpec((tm, D), lambda i: (i, 0)),
            pl.BlockSpec((D, D), lambda i: (0, 0), pipeline_mode=once),
            pl.BlockSpec((P, D), lambda i: (0, 0), pipeline_mode=once),
            pl.BlockSpec((1, D), lambda i: (0, 0)),
        ],
        out_specs=list(_cat_specs((tm, D), na)),
        out_shape=[jax.ShapeDtypeStruct((Ta, D), F32), jax.ShapeDtypeStruct((T - Ta, D), F32)],
        scratch_shapes=[pltpu.VMEM((D, D), BF16), pltpu.VMEM((P, D), BF16)],
        compiler_params=_cparams(("arbitrary",)),
        name=name,
    )(xn, p1, p2, h, w_gate, w_ple, norm_w.reshape(1, D))


def kernel(x_prompt, x_sample, p_prompt, p_sample, state_hgrn, state_ssm_re, state_ssm_im, norm_mix, w_in, hg_lb, hg_gnorm, w_branch_a, ssm_a_re, ssm_a_im, ssm_log_dt, ssm_b_re, ssm_b_im, ssm_c_re, ssm_c_im, ssm_d, w_glu, w_branch_b, w_out, norm_moe, w_router, b_router, w_gate_up, b_gate_up, w_down, b_down, norm_ple, w_ple, w_ple_gate, norm_final):
    BP, LP, D = x_prompt.shape
    BS, LS, _ = x_sample.shape
    depth = w_in.shape[0]
    assert depth == 1
    TP, TS = BP * LP, BS * LS
    T = TP + TS
    G, P = ssm_a_re.shape[1:]
    KD = HG_HEADS * HG_DK
    W = SSM_GROUP * G

    xp = x_prompt.reshape(TP, D)
    xs = x_sample.reshape(TS, D)

    xn = rmsnorm_cat(xp, xs, norm_mix[0], BF16, name="norm_mix")
    tm = 1024
    w_in0 = w_in[0]
    z_hg = project(xn, w_in0, 0, T // tm, 0, 4, tm, 1024, "proj_hgrn")
    u_p = project(xn, w_in0, 0, TP // tm, 4, 1, tm, 1024, "proj_u_prompt")
    u_s = project(xn, w_in0, TP // tm, TS // tm, 4, 1, tm, 1024, "proj_u_sample")
    z_gate = project(xn, w_in0, 0, T // tm, 5, 4, tm, 1024, "proj_gates", out_dtype=BF16)

    lb = jax.nn.softmax(hg_lb.astype(F32), axis=0)[0]
    og_p, hg_p = hgrn2(z_hg, lb, hg_gnorm[0], None, row0=0, B=BP, L=LP, C=HG_CHUNK, nb=1, tok_step=256,
                       name="hgrn_prompt")
    og_s, hg_s = hgrn2(z_hg, lb, hg_gnorm[0], state_hgrn[0], row0=TP, B=BS, L=LS, C=LS, nb=8, tok_step=LS,
                       name="hgrn_sample")

    bd, cd, ab = s5_params(ssm_a_re[0], ssm_a_im[0], ssm_log_dt[0], ssm_b_re[0], ssm_b_im[0],
                           ssm_c_re[0], ssm_c_im[0])
    y_p, re_p, im_p = s5_prompt(u_p.reshape(BP, LP, W), bd, cd, ab, ssm_d[0], TC=256, name="s5_prompt")
    y_s, re_s, im_s = s5_sample(u_s, bd, cd, ab, ssm_d[0], state_ssm_re[0].reshape(BS, G * P),
                                state_ssm_im[0].reshape(BS, G * P), B=BS, L=LS, name="s5_sample")

    glu = gelu_glu(y_p.reshape(TP, W), y_s, w_glu[0])
    merged = gated_merge(og_p, og_s, glu, z_gate, w_branch_a[0], w_branch_b[0])
    h1 = resid_matmul(merged, xp, xs, w_out[0], name="out_proj")

    xn2, route, cnt = moe_router(h1, norm_moe[0], w_router[0], b_router[0])
    e_idx = route[:, 0:TOP_K].astype(jnp.int32)
    rank = route[:, TOP_K:2 * TOP_K].astype(jnp.int32)
    counts = cnt[0, :N_EXPERTS].astype(jnp.int32)
    pad_start, zrows, plan = moe_plan(counts, T)
    onehot = e_idx[..., None] == jnp.arange(N_EXPERTS, dtype=jnp.int32)
    dest = (jnp.sum(jnp.where(onehot, pad_start, 0), axis=-1) + rank).astype(jnp.int32).reshape(T * TOP_K)
    n_rows = T * TOP_K + N_EXPERTS * MOE_TM
    x_buf = moe_dispatch(xn2, dest, zrows, n_rows)
    out_buf = moe_experts(x_buf, plan, w_gate_up[0], b_gate_up[0], w_down[0], b_down[0])
    h2, xn3 = moe_combine(h1, route, dest, out_buf, norm_ple[0])

    y_p2, y_s2 = ple_final(xn3, p_prompt[0].reshape(TP, -1), p_sample[0].reshape(TS, -1), h2, w_ple_gate[0],
                           w_ple[0], norm_final)

    y_prompt = y_p2.reshape(BP, LP, D)
    y_sample = y_s2.reshape(BS, LS, D)
    hs = (1, BP, HG_HEADS, HG_DK, HG_DV)
    return (y_prompt, y_sample,
            hg_p.reshape(hs), re_p.reshape(1, BP, G, P), im_p.reshape(1, BP, G, P),
            hg_s.reshape(1, BS, HG_HEADS, HG_DK, HG_DV), re_s.reshape(1, BS, G, P), im_s.reshape(1, BS, G, P))
```

```python
import functools

import jax
import jax.numpy as jnp
from jax import lax
from jax.experimental import pallas as pl
from jax.experimental.pallas import tpu as pltpu

F32 = jnp.float32
BF16 = jnp.bfloat16
EPS = 1e-6

HG_HEADS = 8
HG_DK = 128
HG_DV = 128
HG_CHUNK = 32
SSM_GROUP = 16
SSM_STATE = 64
N_EXPERTS = 32
TOP_K = 4
SWIGLU_LIMIT = 7.0
SWIGLU_ALPHA = 1.702

LANES = 128
VMEM_LIMIT_BYTES = 56 * 1024 * 1024
VMEM_LIMIT_EXPERTS_BYTES = 61 * 1024 * 1024

MOE_TM = 256
MOE_UNIT_TILES = 6
MOE_TF = 512
NEG_BIG = -1e30


def _cparams(sem, vmem_limit=VMEM_LIMIT_BYTES):
    return pltpu.CompilerParams(dimension_semantics=sem, vmem_limit_bytes=vmem_limit)


def _dot(a, b):
    return jnp.dot(a, b, preferred_element_type=F32)


def _split2(a):
    hi = a.astype(BF16)
    lo = (a - hi.astype(F32)).astype(BF16)
    return hi, lo


def _dot_x3(a, b_split):
    a1, a2 = _split2(a)
    b1, b2 = b_split
    return _dot(a1, b1) + (_dot(a1, b2) + _dot(a2, b1))


def _cat_specs(shape, nfirst, row_axis=0, col_axis=None):
    def col(ids):
        return 0 if col_axis is None else ids[col_axis]

    first = pl.BlockSpec(shape, lambda *ids: (jnp.minimum(ids[row_axis], nfirst - 1), col(ids)))
    second = pl.BlockSpec(shape, lambda *ids: (jnp.maximum(ids[row_axis] - nfirst, 0), col(ids)))
    return first, second


def _pick(i, nfirst, a_ref, b_ref):
    return jnp.where(i < nfirst, a_ref[...], b_ref[...])


def _rms(x, w):
    ms = jnp.mean(x * x, axis=-1, keepdims=True)
    return x * lax.rsqrt(ms + EPS) * w


def _rms_in2_kernel(a_ref, b_ref, w_ref, o_ref, *, nfirst):
    x = _pick(pl.program_id(0), nfirst, a_ref, b_ref)
    o_ref[...] = _rms(x, w_ref[...]).astype(o_ref.dtype)


def rmsnorm_cat(xa, xb, w, out_dtype, tm=512, name="rmsnorm"):
    (Ta, D), Tb = xa.shape, xb.shape[0]
    na = Ta // tm
    return pl.pallas_call(
        functools.partial(_rms_in2_kernel, nfirst=na),
        grid=((Ta + Tb) // tm,),
        in_specs=[*_cat_specs((tm, D), na), pl.BlockSpec((1, D), lambda i: (0, 0))],
        out_specs=pl.BlockSpec((tm, D), lambda i: (i, 0)),
        out_shape=jax.ShapeDtypeStruct((Ta + Tb, D), out_dtype),
        compiler_params=_cparams(("arbitrary",)),
        name=name,
    )(xa, xb, w.reshape(1, D))


def _proj_kernel(x_ref, w_ref, o_ref, wb_ref):
    @pl.when(pl.program_id(1) == 0)
    def _():
        wb_ref[...] = w_ref[...].astype(BF16)

    o_ref[...] = _dot(x_ref[...], wb_ref[...]).astype(o_ref.dtype)


def project(xn, w, row_blk0, n_row_blks, col_blk0, n_col_blks, tm, tn, name, out_dtype=F32):
    K = xn.shape[1]
    return pl.pallas_call(
        _proj_kernel,
        grid=(n_col_blks, n_row_blks),
        in_specs=[
            pl.BlockSpec((tm, K), lambda j, i: (row_blk0 + i, 0)),
            pl.BlockSpec((K, tn), lambda j, i: (0, col_blk0 + j)),
        ],
        out_specs=pl.BlockSpec((tm, tn), lambda j, i: (i, j)),
        out_shape=jax.ShapeDtypeStruct((n_row_blks * tm, n_col_blks * tn), out_dtype),
        scratch_shapes=[pltpu.VMEM((K, tn), BF16)],
        compiler_params=_cparams(("arbitrary", "arbitrary")),
        name=name,
    )(xn, w)


HGRN_CHUNKS_PER_TRIP = 4


def _hgrn_kernel(*refs, C, nchunk, nb, tok_step, has_s0, state_t):
    if has_s0:
        q_ref, f_ref, v_ref, g_ref, lb_ref, gn_ref, s0_ref, og_ref, so_ref, st_ref = refs
    else:
        q_ref, f_ref, v_ref, g_ref, lb_ref, gn_ref, og_ref, so_ref, st_ref = refs
        s0_ref = None
    H, DK, DV = HG_HEADS, HG_DK, HG_DV
    step = pl.program_id(1)

    @pl.when(step == 0)
    def _():
        for i in range(nb):
            for h in range(H):
                if has_s0:
                    st_ref[i, h] = s0_ref[i, h].T if state_t else s0_ref[i, h]
                else:
                    st_ref[i, h] = jnp.zeros(st_ref.shape[2:], F32)

    lbv = lb_ref[...]
    gnv = gn_ref[...]
    rr = lax.broadcasted_iota(jnp.int32, (C, C), 0)
    cc = lax.broadcasted_iota(jnp.int32, (C, C), 1)
    causal = rr >= cc
    tri = causal.astype(BF16)
    sls =[slice(h * DK, (h + 1) * DK) for h in range(H)]
    nt_dims = (((1,), (1,)), ((), ()))
    tn_dims = (((0,), (0,)), ((), ()))

    def tn(a, b):
        return lax.dot_general(a, b, tn_dims, preferred_element_type=F32)

    def prepare(r0):
        rows = pl.ds(r0, C)
        zq = q_ref[rows, :]
        zf = f_ref[rows, :]
        q = jax.nn.silu(zq)
        fe = lbv + (1.0 - lbv) * jax.nn.sigmoid(zf)
        k = 1.0 - fe
        gl = jnp.log(fe)
        g1 = gl.astype(BF16)
        r1 = gl - g1.astype(F32)
        g2 = r1.astype(BF16)
        g3 = (r1 - g2.astype(F32)).astype(BF16)
        b = _dot(tri, g1) + (_dot(tri, g2) + _dot(tri, g3))
        blast = b[C - 1:C, :]
        return dict(
            qt=(q * jnp.exp(b)).astype(BF16),
            kt=(k * jnp.exp(-b)).astype(BF16),
            kend=(k * jnp.exp(blast - b)).astype(BF16),
            eblast=jnp.exp(blast),
            vb=v_ref[rows, :].astype(BF16),
            gate=jax.nn.silu(g_ref[rows, :]),
        )

    def run(jobs):
        P = [prepare(r0) for _, r0 in jobs]
        atts = [[lax.dot_general(p["qt"][:, sl], p["kt"][:, sl], nt_dims, preferred_element_type=F32)
                 for sl in sls] for p in P]
        if state_t:
            upds = [[tn(p["vb"][:, sl], p["kend"][:, sl]) for sl in sls] for p in P]
            decs = [[p["eblast"][:, sl] for sl in sls] for p in P]
        else:
            upds = [[tn(p["kend"][:, sl], p["vb"][:, sl]) for sl in sls] for p in P]
            decs = [[jnp.broadcast_to(p["eblast"][:, sl], (8, DK)).T[:, 0:1] for sl in sls] for p in P]
        state = {}
        inters = []
        for j, (i, _) in enumerate(jobs):
            row = []
            for h, sl in enumerate(sls):
                s = state[(i, h)] if (i, h) in state else st_ref[i, h]
                qth = P[j]["qt"][:, sl]
                if state_t:
                    row.append(lax.dot_general(qth, s.astype(BF16), nt_dims, preferred_element_type=F32))
                else:
                    row.append(_dot(qth, s.astype(BF16)))
                state[(i, h)] = s * decs[j][h] + upds[j][h]
            inters.append(row)
        for (i, h), s in state.items():
            st_ref[i, h] = s
        outs = []
        for j, p in enumerate(P):
            cols = []
            for h, sl in enumerate(sls):
                att = jnp.where(causal, atts[j][h], 0.0).astype(BF16)
                o = _dot(att, p["vb"][:, sl]) + inters[j][h]
                ms = jnp.mean(o * o, axis=-1, keepdims=True)
                cols.append(o * lax.rsqrt(ms + EPS) * gnv * p["gate"][:, sl])
            outs.append(jnp.concatenate(cols, axis=1))
        return outs

    if nchunk == 1:
        outs = run([(i, i * tok_step) for i in range(nb)])
        og_ref[...] = jnp.concatenate(outs, axis=0).astype(og_ref.dtype)
    else:
        assert nb == 1 and nchunk % HGRN_CHUNKS_PER_TRIP == 0

        def body(t, carry):
            r0s = [pl.multiple_of((HGRN_CHUNKS_PER_TRIP * t + j) * C, C) for j in range(HGRN_CHUNKS_PER_TRIP)]
            outs = run([(0, r0) for r0 in r0s])
            for r0, o in zip(r0s, outs):
                og_ref[pl.ds(r0, C), :] = o.astype(og_ref.dtype)
            return carry

        lax.fori_loop(0, nchunk // HGRN_CHUNKS_PER_TRIP, body, 0)

    @pl.when(step == pl.num_programs(1) - 1)
    def _():
        for i in range(nb):
            for h in range(H):
                so_ref[i, h] = st_ref[i, h].T if state_t else st_ref[i, h]


def hgrn2(z_hg, lb, gn, s0, *, row0, B, L, C, nb, tok_step, name):
    H, DK, DV = HG_HEADS, HG_DK, HG_DV
    W = H * DK
    steps = L // tok_step
    rows = nb * tok_step
    nchunk = tok_step // C
    blk0 = row0 // rows
    has_s0 = s0 is not None

    def zspec(col):
        return pl.BlockSpec((rows, W), lambda bb, s: (blk0 + bb * steps + s, col))

    in_specs = [zspec(0), zspec(1), zspec(2), zspec(3),
                pl.BlockSpec((1, W), lambda bb, s: (0, 0)),
                pl.BlockSpec((1, DV), lambda bb, s: (0, 0))]
    args = [z_hg, z_hg, z_hg, z_hg, lb.reshape(1, W), gn.reshape(1, DV)]
    if has_s0:
        in_specs.append(pl.BlockSpec((nb, H, DK, DV), lambda bb, s: (bb, 0, 0, 0)))
        args.append(s0)
    state_t = L > C
    kern = functools.partial(_hgrn_kernel, C=C, nchunk=nchunk, nb=nb, tok_step=tok_step, has_s0=has_s0,
                             state_t=state_t)
    return pl.pallas_call(
        kern,
        grid=(B // nb, steps),
        in_specs=in_specs,
        out_specs=[pl.BlockSpec((rows, W), lambda bb, s: (bb * steps + s, 0)),
                   pl.BlockSpec((nb, H, DK, DV), lambda bb, s: (bb, 0, 0, 0))],
        out_shape=[jax.ShapeDtypeStruct((B * L, W), BF16),
                   jax.ShapeDtypeStruct((B, H, DK, DV), F32)],
        scratch_shapes=[pltpu.VMEM((nb, H, DV, DK), F32)],
        compiler_params=_cparams(("arbitrary", "arbitrary")),
        name=name,
    )(*args)


def _s5_step(ar, ai, xr, xi, br, bi):
    return ar * xr - ai * xi + br, ar * xi + ai * xr + bi


S5_STEPS_PER_TRIP = 4
S5_GROUPS_PER_BLOCK = 16


def _s5_prompt_kernel(u_ref, bd_ref, cd_ref, a_ref, d_ref, y_ref, xr_ref, xi_ref, X_ref, st_ref, *, B, TC):
    NP = X_ref.shape[0] // 2
    assert 2 * B == 8
    tc = pl.program_id(1)

    def rows_of(b, s):
        return pl.ds(2 * b + s, TC, stride=8)

    @pl.when(tc == 0)
    def _():
        st_ref[...] = jnp.zeros_like(st_ref)

    bd = bd_ref[...].astype(BF16)
    for b in range(B):
        bu = _dot(u_ref[b].astype(BF16), bd)
        for c in range(4 * NP):
            X_ref[c // 2, rows_of(b, c % 2), :] = bu[:, c * LANES:(c + 1) * LANES]

    odd = lax.broadcasted_iota(jnp.int32, (8, LANES), 0) % 2 == 1

    def pair_const(row, p):
        lo = a_ref[row:row + 1, (2 * p) * LANES:(2 * p + 1) * LANES]
        hi = a_ref[row:row + 1, (2 * p + 1) * LANES:(2 * p + 2) * LANES]
        return jnp.where(odd, hi, lo)

    ars = [pair_const(0, p) for p in range(NP)]
    ais = [pair_const(1, p) for p in range(NP)]

    def body(i, carry):
        carry = list(carry)
        for j in range(S5_STEPS_PER_TRIP):
            rows = pl.ds(pl.multiple_of((i * S5_STEPS_PER_TRIP + j) * 8, 8), 8)
            for p in range(NP):
                xr, xi = _s5_step(ars[p], ais[p], carry[p][0], carry[p][1], X_ref[p, rows, :], X_ref[NP + p, rows, :])
                X_ref[p, rows, :] = xr
                X_ref[NP + p, rows, :] = xi
                carry[p] = (xr, xi)
        return tuple(carry)

    init = tuple((st_ref[p], st_ref[NP + p]) for p in range(NP))
    fin = lax.fori_loop(0, TC // S5_STEPS_PER_TRIP, body, init)
    for p in range(NP):
        st_ref[p] = fin[p][0]
        st_ref[NP + p] = fin[p][1]

    cd = cd_ref[...].astype(BF16)
    dv = d_ref[...]
    for b in range(B):
        xb = jnp.concatenate([X_ref[c // 2, rows_of(b, c % 2), :] for c in range(4 * NP)], axis=1).astype(BF16)
        y_ref[b] = _dot(xb, cd) + dv * u_ref[b]

    @pl.when(tc == pl.num_programs(1) - 1)
    def _():
        for c in range(2 * NP):
            cols = slice(c * LANES, (c + 1) * LANES)
            xr_ref[:, cols] = st_ref[c // 2, pl.ds(c % 2, B, stride=2), :]
            xi_ref[:, cols] = st_ref[NP + c // 2, pl.ds(c % 2, B, stride=2), :]


def s5_prompt(u, bd, cd, ab, dskip, *, TC, name):
    B, L, W = u.shape
    GB, UB, SW = bd.shape
    half = SW // 2
    kern = functools.partial(_s5_prompt_kernel, B=B, TC=TC)
    return pl.pallas_call(
        kern,
        grid=(GB, L // TC),
        in_specs=[
            pl.BlockSpec((B, TC, UB), lambda g, t: (0, t, g)),
            pl.BlockSpec((None, UB, SW), lambda g, t: (g, 0, 0)),
            pl.BlockSpec((None, SW, UB), lambda g, t: (g, 0, 0)),
            pl.BlockSpec((None, 2, half), lambda g, t: (g, 0, 0)),
            pl.BlockSpec((1, UB), lambda g, t: (0, g)),
        ],
        out_specs=[
            pl.BlockSpec((B, TC, UB), lambda g, t: (0, t, g)),
            pl.BlockSpec((B, half), lambda g, t: (0, g)),
            pl.BlockSpec((B, half), lambda g, t: (0, g)),
        ],
        out_shape=[jax.ShapeDtypeStruct((B, L, W), F32),
                   jax.ShapeDtypeStruct((B, GB * half), F32),
                   jax.ShapeDtypeStruct((B, GB * half), F32)],
        scratch_shapes=[pltpu.VMEM((SW // LANES // 2, 8 * TC, LANES), F32),
                        pltpu.VMEM((SW // LANES // 2, 8, LANES), F32)],
        compiler_params=_cparams(("arbitrary", "arbitrary")),
        name=name,
    )(u, bd, cd, ab, dskip.reshape(1, W))


def _s5_sample_kernel(u_ref, bd_ref, cd_ref, a_ref, d_ref, x0r_ref, x0i_ref, y_ref, xr_ref, xi_ref, X_ref, *, B, L):
    NC = X_ref.shape[0]
    NH = NC // 2
    u = u_ref[...]
    bu = _dot_x3(u, _split2(bd_ref[...]))
    for c in range(NC):
        X_ref[c] = bu[:, c * LANES:(c + 1) * LANES]
    for c in range(NH):
        cols = slice(c * LANES, (c + 1) * LANES)
        ar = a_ref[0:1, cols]
        ai = a_ref[1:2, cols]
        xr = x0r_ref[:, cols]
        xi = x0i_ref[:, cols]
        for t in range(L):
            rows = pl.ds(t, B, stride=L)
            xr, xi = _s5_step(ar, ai, xr, xi, X_ref[c, rows, :], X_ref[NH + c, rows, :])
            X_ref[c, rows, :] = xr
            X_ref[NH + c, rows, :] = xi
        xr_ref[:, cols] = xr
        xi_ref[:, cols] = xi
    xs = jnp.concatenate([X_ref[c] for c in range(NC)], axis=1).astype(BF16)
    y_ref[...] = _dot(xs, cd_ref[...].astype(BF16)) + d_ref[...] * u


def s5_sample(u, bd, cd, ab, dskip, x0r, x0i, *, B, L, name):
    T, W = u.shape
    GB, UB, SW = bd.shape
    half = SW // 2
    kern = functools.partial(_s5_sample_kernel, B=B, L=L)
    return pl.pallas_call(
        kern,
        grid=(GB,),
        in_specs=[
            pl.BlockSpec((T, UB), lambda g: (0, g)),
            pl.BlockSpec((None, UB, SW), lambda g: (g, 0, 0)),
            pl.BlockSpec((None, SW, UB), lambda g: (g, 0, 0)),
            pl.BlockSpec((None, 2, half), lambda g: (g, 0, 0)),
            pl.BlockSpec((1, UB), lambda g: (0, g)),
            pl.BlockSpec((B, half), lambda g: (0, g)),
            pl.BlockSpec((B, half), lambda g: (0, g)),
        ],
        out_specs=[
            pl.BlockSpec((T, UB), lambda g: (0, g)),
            pl.BlockSpec((B, half), lambda g: (0, g)),
            pl.BlockSpec((B, half), lambda g: (0, g)),
        ],
        out_shape=[jax.ShapeDtypeStruct((T, W), F32),
                   jax.ShapeDtypeStruct((B, GB * half), F32),
                   jax.ShapeDtypeStruct((B, GB * half), F32)],
        scratch_shapes=[pltpu.VMEM((SW // LANES, T, LANES), F32)],
        compiler_params=_cparams(("arbitrary",)),
        name=name,
    )(u, bd, cd, ab, dskip.reshape(1, W), x0r, x0i)


def s5_params(a_re, a_im, log_dt, b_re, b_im, c_re, c_im):
    G, P = a_re.shape
    gpb = S5_GROUPS_PER_BLOCK
    GB = G // gpb
    A = lax.complex(a_re, a_im)
    dt = jnp.exp(log_dt)[:, None]
    A_bar = jnp.exp(A * dt)
    Bm = lax.complex(b_re, b_im)
    B_bar = ((A_bar - 1.0) / A)[..., None] * Bm
    eye = jnp.eye(gpb, dtype=F32)

    def in_map(m):
        m = m.reshape(GB, gpb, P, SSM_GROUP)
        return jnp.einsum('bgpc,gh->bgchp', m, eye).reshape(GB, gpb * SSM_GROUP, gpb * P)

    def out_map(m):
        m = m.reshape(GB, gpb, SSM_GROUP, P)
        return jnp.einsum('bgcp,gh->bgphc', m, eye).reshape(GB, gpb * P, gpb * SSM_GROUP)

    bd = jnp.concatenate([in_map(B_bar.real), in_map(B_bar.imag)], axis=2)
    cd = jnp.concatenate([out_map(c_re), out_map(-c_im)], axis=1)
    ab = jnp.stack([A_bar.real.reshape(GB, gpb * P), A_bar.imag.reshape(GB, gpb * P)], axis=1)
    return bd, cd, ab


def _glu_kernel(ya_ref, yb_ref, w_ref, o_ref, wb_ref, *, nfirst):
    i = pl.program_id(0)

    @pl.when(i == 0)
    def _():
        wb_ref[...] = w_ref[...].astype(BF16)

    zg = jax.nn.gelu(_pick(i, nfirst, ya_ref, yb_ref))
    o_ref[...] = (zg * jax.nn.sigmoid(_dot(zg.astype(BF16), wb_ref[...]))).astype(o_ref.dtype)


def gelu_glu(ya, yb, w, tm=512, name="gelu_glu"):
    (Ta, W), Tb = ya.shape, yb.shape[0]
    na = Ta // tm
    return pl.pallas_call(
        functools.partial(_glu_kernel, nfirst=na),
        grid=((Ta + Tb) // tm,),
        in_specs=[*_cat_specs((tm, W), na), pl.BlockSpec((W, W), lambda i: (0, 0))],
        out_specs=pl.BlockSpec((tm, W), lambda i: (i, 0)),
        out_shape=jax.ShapeDtypeStruct((Ta + Tb, W), BF16),
        scratch_shapes=[pltpu.VMEM((W, W), BF16)],
        compiler_params=_cparams(("arbitrary",)),
        name=name,
    )(ya, yb, w)


def _merge_kernel(oa1_ref, oa2_ref, ob_ref, ga_ref, gb_ref, wa_ref, wb_ref, o_ref, wab_ref, wbb_ref, *, nfirst):
    i = pl.program_id(1)

    @pl.when(i == 0)
    def _():
        wab_ref[...] = wa_ref[...].astype(BF16)
        wbb_ref[...] = wb_ref[...].astype(BF16)

    ya = _dot(_pick(i, nfirst, oa1_ref, oa2_ref), wab_ref[...])
    yb = _dot(ob_ref[...], wbb_ref[...])
    ga = jax.nn.sigmoid(ga_ref[...].astype(F32))
    gb = jax.nn.sigmoid(gb_ref[...].astype(F32))
    o_ref[...] = (ga * ya + gb * yb).astype(o_ref.dtype)


def gated_merge(oa1, oa2, ob, z_gate, wa, wb, tm=1024, tn=1024, name="gated_merge"):
    T, K = ob.shape
    N = wa.shape[1]
    nj = N // tn
    na = oa1.shape[0] // tm
    return pl.pallas_call(
        functools.partial(_merge_kernel, nfirst=na),
        grid=(nj, T // tm),
        in_specs=[
            *_cat_specs((tm, K), na, row_axis=1),
            pl.BlockSpec((tm, K), lambda j, i: (i, 0)),
            pl.BlockSpec((tm, tn), lambda j, i: (i, j)),
            pl.BlockSpec((tm, tn), lambda j, i: (i, nj + j)),
            pl.BlockSpec((K, tn), lambda j, i: (0, j)),
            pl.BlockSpec((K, tn), lambda j, i: (0, j)),
        ],
        out_specs=pl.BlockSpec((tm, tn), lambda j, i: (i, j)),
        out_shape=jax.ShapeDtypeStruct((T, N), BF16),
        scratch_shapes=[pltpu.VMEM((K, tn), BF16), pltpu.VMEM((K, tn), BF16)],
        compiler_params=_cparams(("arbitrary", "arbitrary")),
        name=name,
    )(oa1, oa2, ob, z_gate, z_gate, wa, wb)


def _resid_kernel(x_ref, h1_ref, h2_ref, w_ref, o_ref, wb_ref, *, nfirst):
    i = pl.program_id(1)

    @pl.when(i == 0)
    def _():
        wb_ref[...] = w_ref[...].astype(BF16)

    o_ref[...] = _pick(i, nfirst, h1_ref, h2_ref) + _dot(x_ref[...], wb_ref[...])


def resid_matmul(x, h1, h2, w, tm=1024, tn=1024, name="resid_matmul"):
    T, K = x.shape
    N = w.shape[1]
    na = h1.shape[0] // tm
    return pl.pallas_call(
        functools.partial(_resid_kernel, nfirst=na),
        grid=(N // tn, T // tm),
        in_specs=[
            pl.BlockSpec((tm, K), lambda j, i: (i, 0)),
            *_cat_specs((tm, tn), na, row_axis=1, col_axis=0),
            pl.BlockSpec((K, tn), lambda j, i: (0, j)),
        ],
        out_specs=pl.BlockSpec((tm, tn), lambda j, i: (i, j)),
        out_shape=jax.ShapeDtypeStruct((T, N), F32),
        scratch_shapes=[pltpu.VMEM((K, tn), BF16)],
        compiler_params=_cparams(("arbitrary", "arbitrary")),
        name=name,
    )(x, h1, h2, w)


def _router_kernel(h_ref, nw_ref, wr_ref, br_ref, xn_ref, route_ref, cnt_ref, tri_ref, carry_ref, *, tm):
    i = pl.program_id(0)

    @pl.when(i == 0)
    def _():
        rr = lax.broadcasted_iota(jnp.int32, (tm, tm), 0)
        cc = lax.broadcasted_iota(jnp.int32, (tm, tm), 1)
        tri_ref[...] = (rr > cc).astype(BF16)
        carry_ref[...] = jnp.zeros_like(carry_ref)

    x = h_ref[...]
    ms = jnp.mean(x * x, axis=-1, keepdims=True)
    xn = x * lax.rsqrt(ms + EPS) * nw_ref[...]
    xn_ref[...] = xn
    logits = _dot(xn.astype(BF16), wr_ref[...].astype(BF16)) + br_ref[...]
    lane = lax.broadcasted_iota(jnp.int32, (tm, LANES), 1)
    lanef = lane.astype(F32)
    cur = logits
    vals, hots, eids = [], [], []
    for _ in range(TOP_K):
        m = jnp.max(cur, axis=-1, keepdims=True)
        idx = jnp.min(jnp.where(cur == m, lanef, float(LANES)), axis=-1, keepdims=True)
        hot = lanef == idx
        vals.append(m)
        hots.append(hot)
        eids.append(idx)
        cur = jnp.where(hot, -jnp.inf, cur)
    es = [jnp.exp(v - vals[0]) for v in vals]
    den = es[0] + es[1] + es[2] + es[3]
    multi = jnp.zeros((tm, LANES), F32)
    for hot in hots:
        multi = multi + hot.astype(F32)
    base = carry_ref[0:1, :] + _dot(tri_ref[...], multi.astype(BF16))
    route = jnp.zeros((tm, LANES), F32)
    for k in range(TOP_K):
        e_k = eids[k]
        r_k = jnp.sum(jnp.where(hots[k], base, 0.0), axis=-1, keepdims=True)
        w_k = es[k] / den
        route = route + jnp.where(lane == k, e_k, 0.0) + jnp.where(lane == TOP_K + k, r_k, 0.0) \
            + jnp.where(lane == 2 * TOP_K + k, w_k, 0.0)
    route_ref[...] = route
    carry = carry_ref[0:1, :] + jnp.sum(multi, axis=0, keepdims=True)
    carry_ref[...] = jnp.broadcast_to(carry, carry_ref.shape)
    cnt_ref[...] = jnp.broadcast_to(carry, cnt_ref.shape)


def moe_router(h, norm_w, w_router, b_router, tm=512, name="moe_router"):
    T, D = h.shape
    E = w_router.shape[1]
    wr = jnp.pad(w_router, ((0, 0), (0, LANES - E)))
    br = jnp.pad(b_router.reshape(1, E), ((0, 0), (0, LANES - E)), constant_values=NEG_BIG)
    kern = functools.partial(_router_kernel, tm=tm)
    return pl.pallas_call(
        kern,
        grid=(T // tm,),
        in_specs=[
            pl.BlockSpec((tm, D), lambda i: (i, 0)),
            pl.BlockSpec((1, D), lambda i: (0, 0)),
            pl.BlockSpec((D, LANES), lambda i: (0, 0)),
            pl.BlockSpec((1, LANES), lambda i: (0, 0)),
        ],
        out_specs=[
            pl.BlockSpec((tm, D), lambda i: (i, 0)),
            pl.BlockSpec((tm, LANES), lambda i: (i, 0)),
            pl.BlockSpec((8, LANES), lambda i: (0, 0)),
        ],
        out_shape=[jax.ShapeDtypeStruct((T, D), F32),
                   jax.ShapeDtypeStruct((T, LANES), F32),
                   jax.ShapeDtypeStruct((8, LANES), F32)],
        scratch_shapes=[pltpu.VMEM((tm, tm), BF16), pltpu.VMEM((8, LANES), F32)],
        compiler_params=_cparams(("arbitrary",)),
        name=name,
    )(h, norm_w.reshape(1, D), wr, br)


ROW_UNROLL = 8


def _dispatch_kernel(dest_ref, zrow_ref, x_ref, buf_ref, zero_ref, xs_ref, sem, zsem, *, tm, n_tail):
    i = pl.program_id(0)
    E = (zrow_ref.shape[0] - 1) // 2
    ZR = zero_ref.shape[0]
    R = buf_ref.shape[0]

    @pl.when(i == 0)
    def _():
        zero_ref[...] = jnp.zeros_like(zero_ref)

        def zero_copy(row, n):
            return pltpu.make_async_copy(zero_ref.at[pl.ds(0, n)], buf_ref.at[pl.ds(pl.multiple_of(row, 8), n)], zsem)

        def zero_rows(act):
            for e in range(E):
                off = zrow_ref[e]
                n = ZR
                while n >= 8:
                    @pl.when((zrow_ref[E + e] & n) != 0)
                    def _():
                        act(zero_copy(off, n))
                    off = off + (zrow_ref[E + e] & n)
                    n //= 2
            for j in range(n_tail):
                row = zrow_ref[2 * E] + j * ZR

                @pl.when(row < R)
                def _():
                    act(zero_copy(row, ZR))

        zero_rows(lambda c: c.start())
        zero_rows(lambda c: c.wait())

    par = i % 2
    xs_ref[par] = x_ref[...]

    def start(rb, c):
        for j in range(ROW_UNROLL):
            r = rb * ROW_UNROLL + j
            for k in range(TOP_K):
                d = dest_ref[(i * tm + r) * TOP_K + k]
                pltpu.make_async_copy(xs_ref.at[par, pl.ds(r, 1)], buf_ref.at[pl.ds(d, 1)],
                                      sem.at[par]).start(priority=k % 2)
        return c

    lax.fori_loop(0, tm // ROW_UNROLL, start, 0)

    def wait_step(s):
        for k in range(TOP_K):
            pltpu.make_async_copy(xs_ref.at[s], buf_ref.at[pl.ds(0, tm)], sem.at[s]).wait()

    @pl.when(i > 0)
    def _():
        wait_step(1 - par)

    @pl.when(i == pl.num_programs(0) - 1)
    def _():
        wait_step(par)


def moe_dispatch(xn, dest_flat, zrows, n_rows, tm=256, name="moe_dispatch"):
    T, D = xn.shape
    n_tail = (n_rows - T * TOP_K) // MOE_TM
    kern = functools.partial(_dispatch_kernel, tm=tm, n_tail=n_tail)
    return pl.pallas_call(
        kern,
        grid_spec=pltpu.PrefetchScalarGridSpec(
            num_scalar_prefetch=2,
            grid=(T // tm,),
            in_specs=[pl.BlockSpec((tm, D), lambda i, d, z: (i, 0))],
            out_specs=pl.BlockSpec(memory_space=pl.ANY),
            scratch_shapes=[pltpu.VMEM((MOE_TM, D), F32), pltpu.VMEM((2, tm, D), F32),
                            pltpu.SemaphoreType.DMA((2,)), pltpu.SemaphoreType.DMA(())],
        ),
        out_shape=jax.ShapeDtypeStruct((n_rows, D), F32),
        compiler_params=_cparams(("arbitrary",)),
        name=name,
    )(dest_flat, zrows, xn)


def _experts_kernel(ue_ref, r0_ref, nt_ref, na_ref, x_hbm, wg_ref, wu_ref, bg_ref, bu_ref, wd_ref, bd_ref,
                    o_hbm, xs_ref, xb_ref, acc_ref, sem_in, sem_out, *, TM, UT, NF):
    TPS = -(-UT // NF)
    assert TPS <= 2
    u = pl.program_id(0)
    f = pl.program_id(1)
    na = na_ref[0]
    nt = nt_ref[u]
    r0 = r0_ref[u]
    slot = u % 2
    un = jnp.minimum(u + 1, na - 1)

    def prefetch(j):
        return jnp.logical_and(jnp.logical_and(u + 1 < na, j < nt_ref[un]), f == j // TPS)

    def x_copy(base, i):
        rows = pl.ds(pl.multiple_of(base + i * TM, TM), TM)
        return pltpu.make_async_copy(x_hbm.at[rows], xs_ref.at[i % 2], sem_in.at[i % 2])

    def o_copy(base, i):
        rows = pl.ds(pl.multiple_of(base + i * TM, TM), TM)
        return pltpu.make_async_copy(acc_ref.at[i], o_hbm.at[rows], sem_out.at[i])

    def wait_outputs(unit):
        for i in range(UT):
            @pl.when(i < nt_ref[unit])
            def _():
                o_copy(r0_ref[unit], i).wait()

    @pl.when(u < na)
    def _():
        @pl.when(jnp.logical_and(f == 0, u == 0))
        def _():
            x_copy(r0, 0).start()
            for i in range(UT):
                @pl.when(i < nt)
                def _():
                    if i + 1 < UT:
                        @pl.when(i + 1 < nt)
                        def _():
                            x_copy(r0, i + 1).start()
                    x_copy(r0, i).wait()
                    xb_ref[0, i] = xs_ref[i % 2].astype(BF16)

        for j in range(UT):
            @pl.when(prefetch(j))
            def _():
                x_copy(r0_ref[un], j).start()

        up = jnp.maximum(u - 1, 0)

        def wait_prev_output(j):
            @pl.when(jnp.logical_and(u > 0, j < nt_ref[up]))
            def _():
                o_copy(r0_ref[up], j).wait()

        bg = bg_ref[...]
        bu = bu_ref[...]

        def tile_out(xt):
            g = _dot(xt, wg_ref[...].astype(BF16)) + bg
            up = _dot(xt, wu_ref[...].astype(BF16)) + bu
            gate = jnp.minimum(g, SWIGLU_LIMIT)
            lin = jnp.clip(up, -SWIGLU_LIMIT, SWIGLU_LIMIT)
            act = (gate * jax.nn.sigmoid(SWIGLU_ALPHA * gate) * (lin + 1.0)).astype(BF16)
            return _dot(act, wd_ref[...].astype(BF16))

        D = xb_ref.shape[3]

        def update(i, n, first):
            tiles = pl.ds(i, n)
            if first:
                for t in range(n):
                    wait_prev_output(i + t)
            out = tile_out(xb_ref[slot, tiles].reshape(n * TM, D))
            if first:
                out = out + bd_ref[...]
            else:
                out = out + acc_ref[tiles].reshape(n * TM, D)
            acc_ref[tiles] = out.reshape(n, TM, D)

            @pl.when(f == NF - 1)
            def _():
                for t in range(n):
                    o_copy(r0, i + t).start()

        def all_tiles(first):
            def body(j, c):
                update(pl.multiple_of(4 * j, 4), 4, first)
                return c
            lax.fori_loop(0, nt // 4, body, 0)
            done4 = (nt // 4) * 4

            @pl.when((nt & 2) != 0)
            def _():
                update(pl.multiple_of(done4, 2), 2, first)

            @pl.when((nt & 1) != 0)
            def _():
                update(nt - 1, 1, first)

        @pl.when(f == 0)
        def _():
            all_tiles(True)
            for j in range(UT):
                @pl.when(j >= nt)
                def _():
                    wait_prev_output(j)

        @pl.when(f > 0)
        def _():
            all_tiles(False)

        for j in range(UT):
            @pl.when(prefetch(j))
            def _():
                x_copy(r0_ref[un], j).wait()
                xb_ref[1 - slot, j] = xs_ref[j % 2].astype(BF16)

        @pl.when(jnp.logical_and(f == NF - 1, u == na - 1))
        def _():
            wait_outputs(u)


def moe_experts(x_buf, plan, w_gate_up, b_gate_up, w_down, b_down, name="moe_experts"):
    R, D = x_buf.shape
    E, _, F2 = w_gate_up.shape
    DF = F2 // 2
    TM, UT, TF = MOE_TM, MOE_UNIT_TILES, MOE_TF
    NF = DF // TF
    ue, r0, nt, na = plan
    U = ue.shape[0]

    def fsel(u, f, na_ref):
        return jnp.where(u < na_ref[0], f, NF - 1)

    kern = functools.partial(_experts_kernel, TM=TM, UT=UT, NF=NF)
    return pl.pallas_call(
        kern,
        grid_spec=pltpu.PrefetchScalarGridSpec(
            num_scalar_prefetch=4,
            grid=(U, NF),
            in_specs=[
                pl.BlockSpec(memory_space=pl.ANY),
                pl.BlockSpec((None, D, TF), lambda u, f, ue, r0, nt, na: (ue[u], 0, fsel(u, f, na))),
                pl.BlockSpec((None, D, TF), lambda u, f, ue, r0, nt, na: (ue[u], 0, NF + fsel(u, f, na))),
                pl.BlockSpec((None, 1, TF), lambda u, f, ue, r0, nt, na: (ue[u], 0, fsel(u, f, na))),
                pl.BlockSpec((None, 1, TF), lambda u, f, ue, r0, nt, na: (ue[u], 0, NF + fsel(u, f, na))),
                pl.BlockSpec((None, TF, D), lambda u, f, ue, r0, nt, na: (ue[u], fsel(u, f, na), 0)),
                pl.BlockSpec((None, 1, D), lambda u, f, ue, r0, nt, na: (ue[u], 0, 0)),
            ],
            out_specs=pl.BlockSpec(memory_space=pl.ANY),
            scratch_shapes=[
                pltpu.VMEM((2, TM, D), F32),
                pltpu.VMEM((2, UT, TM, D), BF16),
                pltpu.VMEM((UT, TM, D), F32),
                pltpu.SemaphoreType.DMA((2,)),
                pltpu.SemaphoreType.DMA((UT,)),
            ],
        ),
        out_shape=jax.ShapeDtypeStruct((R, D), F32),
        input_output_aliases={4: 0},
        compiler_params=_cparams(("arbitrary", "arbitrary"), VMEM_LIMIT_EXPERTS_BYTES),
        name=name,
    )(ue, r0, nt, na, x_buf, w_gate_up, w_gate_up, b_gate_up.reshape(E, 1, F2), b_gate_up.reshape(E, 1, F2),
      w_down, b_down.reshape(E, 1, D))


def moe_plan(counts, n_tokens):
    TM, UT = MOE_TM, MOE_UNIT_TILES
    E = counts.shape[0]
    nt = (counts + TM - 1) // TM
    padded = nt * TM
    pad_start = jnp.cumsum(padded) - padded
    nu = (nt + UT - 1) // UT
    cu = jnp.cumsum(nu)
    n_units = cu[-1]
    U = E + (n_tokens * TOP_K) // (TM * UT)
    u = jnp.arange(U, dtype=jnp.int32)
    uc = jnp.minimum(u, n_units - 1)
    ue = jnp.clip(jnp.searchsorted(cu, uc, side='right'), 0, E - 1).astype(jnp.int32)
    j = uc - (cu[ue] - nu[ue])
    r0 = (pad_start[ue] + j * (UT * TM)).astype(jnp.int32)
    ntl = jnp.where(u < n_units, jnp.clip(nt[ue] - j * UT, 0, UT), 0).astype(jnp.int32)
    zstart = (pad_start + counts) // 8 * 8
    pad_end = pad_start + padded
    zrows = jnp.concatenate([zstart, pad_end - zstart, pad_end[-1:]]).astype(jnp.int32)
    return pad_start, zrows, (ue, r0, ntl, n_units.reshape(1).astype(jnp.int32))


def _combine_kernel(dest_ref, h_ref, route_ref, nw_ref, o_hbm, h2_ref, xn_ref, g_ref, sem, *, tm):
    i = pl.program_id(0)
    par = i % 2

    def gather_tile(t, slot):
        def start(rb, c):
            for j in range(ROW_UNROLL):
                r = rb * ROW_UNROLL + j
                for k in range(TOP_K):
                    d = dest_ref[(t * tm + r) * TOP_K + k]
                    pltpu.make_async_copy(o_hbm.at[pl.ds(d, 1)], g_ref.at[slot, k, pl.ds(r, 1)],
                                          sem.at[slot]).start(priority=k % 2)
            return c

        lax.fori_loop(0, tm // ROW_UNROLL, start, 0)

    @pl.when(i == 0)
    def _():
        gather_tile(i, par)

    @pl.when(i + 1 < pl.num_programs(0))
    def _():
        gather_tile(i + 1, 1 - par)

    for k in range(TOP_K):
        pltpu.make_async_copy(o_hbm.at[pl.ds(0, tm)], g_ref.at[par, k], sem.at[par]).wait()

    route = route_ref[...]
    y = g_ref[par, 0] * route[:, 2 * TOP_K:2 * TOP_K + 1]
    for k in range(1, TOP_K):
        y = y + g_ref[par, k] * route[:, 2 * TOP_K + k:2 * TOP_K + k + 1]
    h2 = h_ref[...] + y
    h2_ref[...] = h2
    ms = jnp.mean(h2 * h2, axis=-1, keepdims=True)
    xn_ref[...] = (h2 * lax.rsqrt(ms + EPS) * nw_ref[...]).astype(xn_ref.dtype)


def moe_combine(h, route, dest_flat, out_buf, norm_w, tm=256, name="moe_combine"):
    T, D = h.shape
    kern = functools.partial(_combine_kernel, tm=tm)
    return pl.pallas_call(
        kern,
        grid_spec=pltpu.PrefetchScalarGridSpec(
            num_scalar_prefetch=1,
            grid=(T // tm,),
            in_specs=[pl.BlockSpec((tm, D), lambda i, d: (i, 0)),
                      pl.BlockSpec((tm, LANES), lambda i, d: (i, 0)),
                      pl.BlockSpec((1, D), lambda i, d: (0, 0)),
                      pl.BlockSpec(memory_space=pl.ANY)],
            out_specs=[pl.BlockSpec((tm, D), lambda i, d: (i, 0)),
                       pl.BlockSpec((tm, D), lambda i, d: (i, 0))],
            scratch_shapes=[pltpu.VMEM((2, TOP_K, tm, D), F32), pltpu.SemaphoreType.DMA((2,))],
        ),
        out_shape=[jax.ShapeDtypeStruct((T, D), F32), jax.ShapeDtypeStruct((T, D), BF16)],
        compiler_params=_cparams(("arbitrary",)),
        name=name,
    )(dest_flat, h, route, norm_w.reshape(1, D), out_buf)


def _ple_kernel(xn_ref, p1_ref, p2_ref, h_ref, wg_ref, wp_ref, nf_ref, oa_ref, ob_ref, wgb_ref, wpb_ref, *, nfirst):
    i = pl.program_id(0)

    @pl.when(i == 0)
    def _():
        wgb_ref[...] = wg_ref[...].astype(BF16)
        wpb_ref[...] = wp_ref[...].astype(BF16)

    gate = jax.nn.sigmoid(_dot(xn_ref[...], wgb_ref[...]))
    p = _pick(i, nfirst, p1_ref, p2_ref).astype(BF16)
    y = _rms(h_ref[...] + _dot(p, wpb_ref[...]) * gate, nf_ref[...])

    @pl.when(i < nfirst)
    def _():
        oa_ref[...] = y

    @pl.when(i >= nfirst)
    def _():
        ob_ref[...] = y


def ple_final(xn, p1, p2, h, w_gate, w_ple, norm_w, tm=256, name="ple_final"):
    T, D = h.shape
    Ta, P = p1.shape
    na = Ta // tm
    once = pl.Buffered(1)
    return pl.pallas_call(
        functools.partial(_ple_kernel, nfirst=na),
        grid=(T // tm,),
        in_specs=[
            pl.BlockSpec((tm, D), lambda i: (i, 0)),
            *_cat_specs((tm, P), na),
            pl.BlockSpec((tm, D), lambda i: (i, 0)),
            pl.BlockSpec((D, D), lambda i: (0, 0), pipeline_mode=once),
            pl.BlockSpec((P, D), lambda i: (0, 0), pipeline_mode=once),
            pl.BlockSpec((1, D), lambda i: (0, 0)),
        ],
        out_specs=list(_cat_specs((tm, D), na)),
        out_shape=[jax.ShapeDtypeStruct((Ta, D), F32), jax.ShapeDtypeStruct((T - Ta, D), F32)],
        scratch_shapes=[pltpu.VMEM((D, D), BF16), pltpu.VMEM((P, D), BF16)],
        compiler_params=_cparams(("arbitrary",)),
        name=name,
    )(xn, p1, p2, h, w_gate, w_ple, norm_w.reshape(1, D))


def kernel(x_prompt, x_sample, p_prompt, p_sample, state_hgrn, state_ssm_re, state_ssm_im, norm_mix, w_in, hg_lb, hg_gnorm, w_branch_a, ssm_a_re, ssm_a_im, ssm_log_dt, ssm_b_re, ssm_b_im, ssm_c_re, ssm_c_im, ssm_d, w_glu, w_branch_b, w_out, norm_moe, w_router, b_router, w_gate_up, b_gate_up, w_down, b_down, norm_ple, w_ple, w_ple_gate, norm_final):
    BP, LP, D = x_prompt.shape
    BS, LS, _ = x_sample.shape
    depth = w_in.shape[0]
    assert depth == 1
    TP, TS = BP * LP, BS * LS
    T = TP + TS
    G, P = ssm_a_re.shape[1:]
    KD = HG_HEADS * HG_DK
    W = SSM_GROUP * G

    xp = x_prompt.reshape(TP, D)
    xs = x_sample.reshape(TS, D)

    xn = rmsnorm_cat(xp, xs, norm_mix[0], BF16, name="norm_mix")
    tm = 1024
    w_in0 = w_in[0]
    z_hg = project(xn, w_in0, 0, T // tm, 0, 4, tm, 1024, "proj_hgrn")
    u_p = project(xn, w_in0, 0, TP // tm, 4, 1, tm, 1024, "proj_u_prompt")
    u_s = project(xn, w_in0, TP // tm, TS // tm, 4, 1, tm, 1024, "proj_u_sample")
    z_gate = project(xn, w_in0, 0, T // tm, 5, 4, tm, 1024, "proj_gates", out_dtype=BF16)

    lb = jax.nn.softmax(hg_lb.astype(F32), axis=0)[0]
    og_p, hg_p = hgrn2(z_hg, lb, hg_gnorm[0], None, row0=0, B=BP, L=LP, C=HG_CHUNK, nb=1, tok_step=256,
                       name="hgrn_prompt")
    og_s, hg_s = hgrn2(z_hg, lb, hg_gnorm[0], state_hgrn[0], row0=TP, B=BS, L=LS, C=LS, nb=8, tok_step=LS,
                       name="hgrn_sample")

    bd, cd, ab = s5_params(ssm_a_re[0], ssm_a_im[0], ssm_log_dt[0], ssm_b_re[0], ssm_b_im[0],
                           ssm_c_re[0], ssm_c_im[0])
    y_p, re_p, im_p = s5_prompt(u_p.reshape(BP, LP, W), bd, cd, ab, ssm_d[0], TC=256, name="s5_prompt")
    y_s, re_s, im_s = s5_sample(u_s, bd, cd, ab, ssm_d[0], state_ssm_re[0].reshape(BS, G * P),
                                state_ssm_im[0].reshape(BS, G * P), B=BS, L=LS, name="s5_sample")

    glu = gelu_glu(y_p.reshape(TP, W), y_s, w_glu[0])
    merged = gated_merge(og_p, og_s, glu, z_gate, w_branch_a[0], w_branch_b[0])
    h1 = resid_matmul(merged, xp, xs, w_out[0], name="out_proj")

    xn2, route, cnt = moe_router(h1, norm_moe[0], w_router[0], b_router[0])
    e_idx = route[:, 0:TOP_K].astype(jnp.int32)
    rank = route[:, TOP_K:2 * TOP_K].astype(jnp.int32)
    counts = cnt[0, :N_EXPERTS].astype(jnp.int32)
    pad_start, zrows, plan = moe_plan(counts, T)
    onehot = e_idx[..., None] == jnp.arange(N_EXPERTS, dtype=jnp.int32)
    dest = (jnp.sum(jnp.where(onehot, pad_start, 0), axis=-1) + rank).astype(jnp.int32).reshape(T * TOP_K)
    n_rows = T * TOP_K + N_EXPERTS * MOE_TM
    x_buf = moe_dispatch(xn2, dest, zrows, n_rows)
    out_buf = moe_experts(x_buf, plan, w_gate_up[0], b_gate_up[0], w_down[0], b_down[0])
    h2, xn3 = moe_combine(h1, route, dest, out_buf, norm_ple[0])

    y_p2, y_s2 = ple_final(xn3, p_prompt[0].reshape(TP, -1), p_sample[0].reshape(TS, -1), h2, w_ple_gate[0],
                           w_ple[0], norm_final)

    y_prompt = y_p2.reshape(BP, LP, D)
    y_sample = y_s2.reshape(BS, LS, D)
    hs = (1, BP, HG_HEADS, HG_DK, HG_DV)
    return (y_prompt, y_sample,
            hg_p.reshape(hs), re_p.reshape(1, BP, G, P), im_p.reshape(1, BP, G, P),
            hg_s.reshape(1, BS, HG_HEADS, HG_DK, HG_DV), re_s.reshape(1, BS, G, P), im_s.reshape(1, BS, G, P))
```

```python
import functools

import jax
import jax.numpy as jnp
from jax import lax
from jax.experimental import pallas as pl
from jax.experimental.pallas import tpu as pltpu

F32 = jnp.float32
BF16 = jnp.bfloat16
EPS = 1e-6

HG_HEADS = 8
HG_DK = 128
HG_DV = 128
HG_CHUNK = 32
SSM_GROUP = 16
SSM_STATE = 64
N_EXPERTS = 32
TOP_K = 4
SWIGLU_LIMIT = 7.0
SWIGLU_ALPHA = 1.702

LANES = 128
VMEM_LIMIT_BYTES = 56 * 1024 * 1024
VMEM_LIMIT_EXPERTS_BYTES = 61 * 1024 * 1024

MOE_TM = 256
MOE_UNIT_TILES = 6
MOE_TF = 512
NEG_BIG = -1e30


def _cparams(sem, vmem_limit=VMEM_LIMIT_BYTES):
    return pltpu.CompilerParams(dimension_semantics=sem, vmem_limit_bytes=vmem_limit)


def _dot(a, b):
    return jnp.dot(a, b, preferred_element_type=F32)


def _split2(a):
    hi = a.astype(BF16)
    lo = (a - hi.astype(F32)).astype(BF16)
    return hi, lo


def _dot_x3(a, b_split):
    a1, a2 = _split2(a)
    b1, b2 = b_split
    return _dot(a1, b1) + (_dot(a1, b2) + _dot(a2, b1))


def _cat_specs(shape, nfirst, row_axis=0, col_axis=None):
    def col(ids):
        return 0 if col_axis is None else ids[col_axis]

    first = pl.BlockSpec(shape, lambda *ids: (jnp.minimum(ids[row_axis], nfirst - 1), col(ids)))
    second = pl.BlockSpec(shape, lambda *ids: (jnp.maximum(ids[row_axis] - nfirst, 0), col(ids)))
    return first, second


def _pick(i, nfirst, a_ref, b_ref):
    return jnp.where(i < nfirst, a_ref[...], b_ref[...])


def _rms(x, w):
    ms = jnp.mean(x * x, axis=-1, keepdims=True)
    return x * lax.rsqrt(ms + EPS) * w


def _rms_in2_kernel(a_ref, b_ref, w_ref, o_ref, *, nfirst):
    x = _pick(pl.program_id(0), nfirst, a_ref, b_ref)
    o_ref[...] = _rms(x, w_ref[...]).astype(o_ref.dtype)


def rmsnorm_cat(xa, xb, w, out_dtype, tm=512, name="rmsnorm"):
    (Ta, D), Tb = xa.shape, xb.shape[0]
    na = Ta // tm
    return pl.pallas_call(
        functools.partial(_rms_in2_kernel, nfirst=na),
        grid=((Ta + Tb) // tm,),
        in_specs=[*_cat_specs((tm, D), na), pl.BlockSpec((1, D), lambda i: (0, 0))],
        out_specs=pl.BlockSpec((tm, D), lambda i: (i, 0)),
        out_shape=jax.ShapeDtypeStruct((Ta + Tb, D), out_dtype),
        compiler_params=_cparams(("arbitrary",)),
        name=name,
    )(xa, xb, w.reshape(1, D))


def _proj_kernel(x_ref, w_ref, o_ref, wb_ref):
    @pl.when(pl.program_id(1) == 0)
    def _():
        wb_ref[...] = w_ref[...].astype(BF16)

    o_ref[...] = _dot(x_ref[...], wb_ref[...]).astype(o_ref.dtype)


def project(xn, w, row_blk0, n_row_blks, col_blk0, n_col_blks, tm, tn, name, out_dtype=F32):
    K = xn.shape[1]
    return pl.pallas_call(
        _proj_kernel,
        grid=(n_col_blks, n_row_blks),
        in_specs=[
            pl.BlockSpec((tm, K), lambda j, i: (row_blk0 + i, 0)),
            pl.BlockSpec((K, tn), lambda j, i: (0, col_blk0 + j)),
        ],
        out_specs=pl.BlockSpec((tm, tn), lambda j, i: (i, j)),
        out_shape=jax.ShapeDtypeStruct((n_row_blks * tm, n_col_blks * tn), out_dtype),
        scratch_shapes=[pltpu.VMEM((K, tn), BF16)],
        compiler_params=_cparams(("arbitrary", "arbitrary")),
        name=name,
    )(xn, w)


HGRN_CHUNKS_PER_TRIP = 4


def _hgrn_kernel(*refs, C, nchunk, nb, tok_step, has_s0, state_t):
    if has_s0:
        q_ref, f_ref, v_ref, g_ref, lb_ref, gn_ref, s0_ref, og_ref, so_ref, st_ref = refs
    else:
        q_ref, f_ref, v_ref, g_ref, lb_ref, gn_ref, og_ref, so_ref, st_ref = refs
        s0_ref = None
    H, DK, DV = HG_HEADS, HG_DK, HG_DV
    step = pl.program_id(1)

    @pl.when(step == 0)
    def _():
        for i in range(nb):
            for h in range(H):
                if has_s0:
                    st_ref[i, h] = s0_ref[i, h].T if state_t else s0_ref[i, h]
                else:
                    st_ref[i, h] = jnp.zeros(st_ref.shape[2:], F32)

    lbv = lb_ref[...]
    gnv = gn_ref[...]
    rr = lax.broadcasted_iota(jnp.int32, (C, C), 0)
    cc = lax.broadcasted_iota(jnp.int32, (C, C), 1)
    causal = rr >= cc
    tri = causal.astype(BF16)
    sls =[slice(h * DK, (h + 1) * DK) for h in range(H)]
    nt_dims = (((1,), (1,)), ((), ()))
    tn_dims = (((0,), (0,)), ((), ()))

    def tn(a, b):
        return lax.dot_general(a, b, tn_dims, preferred_element_type=F32)

    def prepare(r0):
        rows = pl.ds(r0, C)
        zq = q_ref[rows, :]
        zf = f_ref[rows, :]
        q = jax.nn.silu(zq)
        fe = lbv + (1.0 - lbv) * jax.nn.sigmoid(zf)
        k = 1.0 - fe
        gl = jnp.log(fe)
        g1 = gl.astype(BF16)
        r1 = gl - g1.astype(F32)
        g2 = r1.astype(BF16)
        g3 = (r1 - g2.astype(F32)).astype(BF16)
        b = _dot(tri, g1) + (_dot(tri, g2) + _dot(tri, g3))
        blast = b[C - 1:C, :]
        return dict(
            qt=(q * jnp.exp(b)).astype(BF16),
            kt=(k * jnp.exp(-b)).astype(BF16),
            kend=(k * jnp.exp(blast - b)).astype(BF16),
            eblast=jnp.exp(blast),
            vb=v_ref[rows, :].astype(BF16),
            gate=jax.nn.silu(g_ref[rows, :]),
        )

    def run(jobs):
        P = [prepare(r0) for _, r0 in jobs]
        atts = [[lax.dot_general(p["qt"][:, sl], p["kt"][:, sl], nt_dims, preferred_element_type=F32)
                 for sl in sls] for p in P]
        if state_t:
            upds = [[tn(p["vb"][:, sl], p["kend"][:, sl]) for sl in sls] for p in P]
            decs = [[p["eblast"][:, sl] for sl in sls] for p in P]
        else:
            upds = [[tn(p["kend"][:, sl], p["vb"][:, sl]) for sl in sls] for p in P]
            decs = [[jnp.broadcast_to(p["eblast"][:, sl], (8, DK)).T[:, 0:1] for sl in sls] for p in P]
        state = {}
        inters = []
        for j, (i, _) in enumerate(jobs):
            row = []
            for h, sl in enumerate(sls):
                s = state[(i, h)] if (i, h) in state else st_ref[i, h]
                qth = P[j]["qt"][:, sl]
                if state_t:
                    row.append(lax.dot_general(qth, s.astype(BF16), nt_dims, preferred_element_type=F32))
                else:
                    row.append(_dot(qth, s.astype(BF16)))
                state[(i, h)] = s * decs[j][h] + upds[j][h]
            inters.append(row)
        for (i, h), s in state.items():
            st_ref[i, h] = s
        outs = []
        for j, p in enumerate(P):
            cols = []
            for h, sl in enumerate(sls):
                att = jnp.where(causal, atts[j][h], 0.0).astype(BF16)
                o = _dot(att, p["vb"][:, sl]) + inters[j][h]
                ms = jnp.mean(o * o, axis=-1, keepdims=True)
                cols.append(o * lax.rsqrt(ms + EPS) * gnv * p["gate"][:, sl])
            outs.append(jnp.concatenate(cols, axis=1))
        return outs

    if nchunk == 1:
        outs = run([(i, i * tok_step) for i in range(nb)])
        og_ref[...] = jnp.concatenate(outs, axis=0).astype(og_ref.dtype)
    else:
        assert nb == 1 and nchunk % HGRN_CHUNKS_PER_TRIP == 0

        def body(t, carry):
            r0s = [pl.multiple_of((HGRN_CHUNKS_PER_TRIP * t + j) * C, C) for j in range(HGRN_CHUNKS_PER_TRIP)]
            outs = run([(0, r0) for r0 in r0s])
            for r0, o in zip(r0s, outs):
                og_ref[pl.ds(r0, C), :] = o.astype(og_ref.dtype)
            return carry

        lax.fori_loop(0, nchunk // HGRN_CHUNKS_PER_TRIP, body, 0)

    @pl.when(step == pl.num_programs(1) - 1)
    def _():
        for i in range(nb):
            for h in range(H):
                so_ref[i, h] = st_ref[i, h].T if state_t else st_ref[i, h]


def hgrn2(z_hg, lb, gn, s0, *, row0, B, L, C, nb, tok_step, name):
    H, DK, DV = HG_HEADS, HG_DK, HG_DV
    W = H * DK
    steps = L // tok_step
    rows = nb * tok_step
    nchunk = tok_step // C
    blk0 = row0 // rows
    has_s0 = s0 is not None

    def zspec(col):
        return pl.BlockSpec((rows, W), lambda bb, s: (blk0 + bb * steps + s, col))

    in_specs = [zspec(0), zspec(1), zspec(2), zspec(3),
                pl.BlockSpec((1, W), lambda bb, s: (0, 0)),
                pl.BlockSpec((1, DV), lambda bb, s: (0, 0))]
    args = [z_hg, z_hg, z_hg, z_hg, lb.reshape(1, W), gn.reshape(1, DV)]
    if has_s0:
        in_specs.append(pl.BlockSpec((nb, H, DK, DV), lambda bb, s: (bb, 0, 0, 0)))
        args.append(s0)
    state_t = L > C
    kern = functools.partial(_hgrn_kernel, C=C, nchunk=nchunk, nb=nb, tok_step=tok_step, has_s0=has_s0,
                             state_t=state_t)
    return pl.pallas_call(
        kern,
        grid=(B // nb, steps),
        in_specs=in_specs,
        out_specs=[pl.BlockSpec((rows, W), lambda bb, s: (bb * steps + s, 0)),
                   pl.BlockSpec((nb, H, DK, DV), lambda bb, s: (bb, 0, 0, 0))],
        out_shape=[jax.ShapeDtypeStruct((B * L, W), BF16),
                   jax.ShapeDtypeStruct((B, H, DK, DV), F32)],
        scratch_shapes=[pltpu.VMEM((nb, H, DV, DK), F32)],
        compiler_params=_cparams(("arbitrary", "arbitrary")),
        name=name,
    )(*args)


def _s5_step(ar, ai, xr, xi, br, bi):
    return ar * xr - ai * xi + br, ar * xi + ai * xr + bi


S5_STEPS_PER_TRIP = 4
S5_GROUPS_PER_BLOCK = 16


def _s5_prompt_kernel(u_ref, bd_ref, cd_ref, a_ref, d_ref, y_ref, xr_ref, xi_ref, X_ref, st_ref, *, B, TC):
    NP = X_ref.shape[0] // 2
    assert 2 * B == 8
    tc = pl.program_id(1)

    def rows_of(b, s):
        return pl.ds(2 * b + s, TC, stride=8)

    @pl.when(tc == 0)
    def _():
        st_ref[...] = jnp.zeros_like(st_ref)

    bd = bd_ref[...].astype(BF16)
    for b in range(B):
        bu = _dot(u_ref[b].astype(BF16), bd)
        for c in range(4 * NP):
            X_ref[c // 2, rows_of(b, c % 2), :] = bu[:, c * LANES:(c + 1) * LANES]

    odd = lax.broadcasted_iota(jnp.int32, (8, LANES), 0) % 2 == 1

    def pair_const(row, p):
        lo = a_ref[row:row + 1, (2 * p) * LANES:(2 * p + 1) * LANES]
        hi = a_ref[row:row + 1, (2 * p + 1) * LANES:(2 * p + 2) * LANES]
        return jnp.where(odd, hi, lo)

    ars = [pair_const(0, p) for p in range(NP)]
    ais = [pair_const(1, p) for p in range(NP)]

    def body(i, carry):
        carry = list(carry)
        for j in range(S5_STEPS_PER_TRIP):
            rows = pl.ds(pl.multiple_of((i * S5_STEPS_PER_TRIP + j) * 8, 8), 8)
            for p in range(NP):
                xr, xi = _s5_step(ars[p], ais[p], carry[p][0], carry[p][1], X_ref[p, rows, :], X_ref[NP + p, rows, :])
                X_ref[p, rows, :] = xr
                X_ref[NP + p, rows, :] = xi
                carry[p] = (xr, xi)
        return tuple(carry)

    init = tuple((st_ref[p], st_ref[NP + p]) for p in range(NP))
    fin = lax.fori_loop(0, TC // S5_STEPS_PER_TRIP, body, init)
    for p in range(NP):
        st_ref[p] = fin[p][0]
        st_ref[NP + p] = fin[p][1]

    cd = cd_ref[...].astype(BF16)
    dv = d_ref[...]
    for b in range(B):
        xb = jnp.concatenate([X_ref[c // 2, rows_of(b, c % 2), :] for c in range(4 * NP)], axis=1).astype(BF16)
        y_ref[b] = _dot(xb, cd) + dv * u_ref[b]

    @pl.when(tc == pl.num_programs(1) - 1)
    def _():
        for c in range(2 * NP):
            cols = slice(c * LANES, (c + 1) * LANES)
            xr_ref[:, cols] = st_ref[c // 2, pl.ds(c % 2, B, stride=2), :]
            xi_ref[:, cols] = st_ref[NP + c // 2, pl.ds(c % 2, B, stride=2), :]


def s5_prompt(u, bd, cd, ab, dskip, *, TC, name):
    B, L, W = u.shape
    GB, UB, SW = bd.shape
    half = SW // 2
    kern = functools.partial(_s5_prompt_kernel, B=B, TC=TC)
    return pl.pallas_call(
        kern,
        grid=(GB, L // TC),
        in_specs=[
            pl.BlockSpec((B, TC, UB), lambda g, t: (0, t, g)),
            pl.BlockSpec((None, UB, SW), lambda g, t: (g, 0, 0)),
            pl.BlockSpec((None, SW, UB), lambda g, t: (g, 0, 0)),
            pl.BlockSpec((None, 2, half), lambda g, t: (g, 0, 0)),
            pl.BlockSpec((1, UB), lambda g, t: (0, g)),
        ],
        out_specs=[
            pl.BlockSpec((B, TC, UB), lambda g, t: (0, t, g)),
            pl.BlockSpec((B, half), lambda g, t: (0, g)),
            pl.BlockSpec((B, half), lambda g, t: (0, g)),
        ],
        out_shape=[jax.ShapeDtypeStruct((B, L, W), F32),
                   jax.ShapeDtypeStruct((B, GB * half), F32),
                   jax.ShapeDtypeStruct((B, GB * half), F32)],
        scratch_shapes=[pltpu.VMEM((SW // LANES // 2, 8 * TC, LANES), F32),
                        pltpu.VMEM((SW // LANES // 2, 8, LANES), F32)],
        compiler_params=_cparams(("arbitrary", "arbitrary")),
        name=name,
    )(u, bd, cd, ab, dskip.reshape(1, W))


def _s5_sample_kernel(u_ref, bd_ref, cd_ref, a_ref, d_ref, x0r_ref, x0i_ref, y_ref, xr_ref, xi_ref, X_ref, *, B, L):
    NC = X_ref.shape[0]
    NH = NC // 2
    u = u_ref[...]
    bu = _dot_x3(u, _split2(bd_ref[...]))
    for c in range(NC):
        X_ref[c] = bu[:, c * LANES:(c + 1) * LANES]
    for c in range(NH):
        cols = slice(c * LANES, (c + 1) * LANES)
        ar = a_ref[0:1, cols]
        ai = a_ref[1:2, cols]
        xr = x0r_ref[:, cols]
        xi = x0i_ref[:, cols]
        for t in range(L):
            rows = pl.ds(t, B, stride=L)
            xr, xi = _s5_step(ar, ai, xr, xi, X_ref[c, rows, :], X_ref[NH + c, rows, :])
            X_ref[c, rows, :] = xr
            X_ref[NH + c, rows, :] = xi
        xr_ref[:, cols] = xr
        xi_ref[:, cols] = xi
    xs = jnp.concatenate([X_ref[c] for c in range(NC)], axis=1).astype(BF16)
    y_ref[...] = _dot(xs, cd_ref[...].astype(BF16)) + d_ref[...] * u


def s5_sample(u, bd, cd, ab, dskip, x0r, x0i, *, B, L, name):
    T, W = u.shape
    GB, UB, SW = bd.shape
    half = SW // 2
    kern = functools.partial(_s5_sample_kernel, B=B, L=L)
    return pl.pallas_call(
        kern,
        grid=(GB,),
        in_specs=[
            pl.BlockSpec((T, UB), lambda g: (0, g)),
            pl.BlockSpec((None, UB, SW), lambda g: (g, 0, 0)),
            pl.BlockSpec((None, SW, UB), lambda g: (g, 0, 0)),
            pl.BlockSpec((None, 2, half), lambda g: (g, 0, 0)),
            pl.BlockSpec((1, UB), lambda g: (0, g)),
            pl.BlockSpec((B, half), lambda g: (0, g)),
            pl.BlockSpec((B, half), lambda g: (0, g)),
        ],
        out_specs=[
            pl.BlockSpec((T, UB), lambda g: (0, g)),
            pl.BlockSpec((B, half), lambda g: (0, g)),
            pl.BlockSpec((B, half), lambda g: (0, g)),
        ],
        out_shape=[jax.ShapeDtypeStruct((T, W), F32),
                   jax.ShapeDtypeStruct((B, GB * half), F32),
                   jax.ShapeDtypeStruct((B, GB * half), F32)],
        scratch_shapes=[pltpu.VMEM((SW // LANES, T, LANES), F32)],
        compiler_params=_cparams(("arbitrary",)),
        name=name,
    )(u, bd, cd, ab, dskip.reshape(1, W), x0r, x0i)


def s5_params(a_re, a_im, log_dt, b_re, b_im, c_re, c_im):
    G, P = a_re.shape
    gpb = S5_GROUPS_PER_BLOCK
    GB = G // gpb
    A = lax.complex(a_re, a_im)
    dt = jnp.exp(log_dt)[:, None]
    A_bar = jnp.exp(A * dt)
    Bm = lax.complex(b_re, b_im)
    B_bar = ((A_bar - 1.0) / A)[..., None] * Bm
    eye = jnp.eye(gpb, dtype=F32)

    def in_map(m):
        m = m.reshape(GB, gpb, P, SSM_GROUP)
        return jnp.einsum('bgpc,gh->bgchp', m, eye).reshape(GB, gpb * SSM_GROUP, gpb * P)

    def out_map(m):
        m = m.reshape(GB, gpb, SSM_GROUP, P)
        return jnp.einsum('bgcp,gh->bgphc', m, eye).reshape(GB, gpb * P, gpb * SSM_GROUP)

    bd = jnp.concatenate([in_map(B_bar.real), in_map(B_bar.imag)], axis=2)
    cd = jnp.concatenate([out_map(c_re), out_map(-c_im)], axis=1)
    ab = jnp.stack([A_bar.real.reshape(GB, gpb * P), A_bar.imag.reshape(GB, gpb * P)], axis=1)
    return bd, cd, ab


def _glu_kernel(ya_ref, yb_ref, w_ref, o_ref, wb_ref, *, nfirst):
    i = pl.program_id(0)

    @pl.when(i == 0)
    def _():
        wb_ref[...] = w_ref[...].astype(BF16)

    zg = jax.nn.gelu(_pick(i, nfirst, ya_ref, yb_ref))
    o_ref[...] = (zg * jax.nn.sigmoid(_dot(zg.astype(BF16), wb_ref[...]))).astype(o_ref.dtype)


def gelu_glu(ya, yb, w, tm=512, name="gelu_glu"):
    (Ta, W), Tb = ya.shape, yb.shape[0]
    na = Ta // tm
    return pl.pallas_call(
        functools.partial(_glu_kernel, nfirst=na),
        grid=((Ta + Tb) // tm,),
        in_specs=[*_cat_specs((tm, W), na), pl.BlockSpec((W, W), lambda i: (0, 0))],
        out_specs=pl.BlockSpec((tm, W), lambda i: (i, 0)),
        out_shape=jax.ShapeDtypeStruct((Ta + Tb, W), BF16),
        scratch_shapes=[pltpu.VMEM((W, W), BF16)],
        compiler_params=_cparams(("arbitrary",)),
        name=name,
    )(ya, yb, w)


def _merge_kernel(oa1_ref, oa2_ref, ob_ref, ga_ref, gb_ref, wa_ref, wb_ref, o_ref, wab_ref, wbb_ref, *, nfirst):
    i = pl.program_id(1)

    @pl.when(i == 0)
    def _():
        wab_ref[...] = wa_ref[...].astype(BF16)
        wbb_ref[...] = wb_ref[...].astype(BF16)

    ya = _dot(_pick(i, nfirst, oa1_ref, oa2_ref), wab_ref[...])
    yb = _dot(ob_ref[...], wbb_ref[...])
    ga = jax.nn.sigmoid(ga_ref[...].astype(F32))
    gb = jax.nn.sigmoid(gb_ref[...].astype(F32))
    o_ref[...] = (ga * ya + gb * yb).astype(o_ref.dtype)


def gated_merge(oa1, oa2, ob, z_gate, wa, wb, tm=1024, tn=1024, name="gated_merge"):
    T, K = ob.shape
    N = wa.shape[1]
    nj = N // tn
    na = oa1.shape[0] // tm
    return pl.pallas_call(
        functools.partial(_merge_kernel, nfirst=na),
        grid=(nj, T // tm),
        in_specs=[
            *_cat_specs((tm, K), na, row_axis=1),
            pl.BlockSpec((tm, K), lambda j, i: (i, 0)),
            pl.BlockSpec((tm, tn), lambda j, i: (i, j)),
            pl.BlockSpec((tm, tn), lambda j, i: (i, nj + j)),
            pl.BlockSpec((K, tn), lambda j, i: (0, j)),
            pl.BlockSpec((K, tn), lambda j, i: (0, j)),
        ],
        out_specs=pl.BlockSpec((tm, tn), lambda j, i: (i, j)),
        out_shape=jax.ShapeDtypeStruct((T, N), BF16),
        scratch_shapes=[pltpu.VMEM((K, tn), BF16), pltpu.VMEM((K, tn), BF16)],
        compiler_params=_cparams(("arbitrary", "arbitrary")),
        name=name,
    )(oa1, oa2, ob, z_gate, z_gate, wa, wb)


def _resid_kernel(x_ref, h1_ref, h2_ref, w_ref, o_ref, wb_ref, *, nfirst):
    i = pl.program_id(1)

    @pl.when(i == 0)
    def _():
        wb_ref[...] = w_ref[...].astype(BF16)

    o_ref[...] = _pick(i, nfirst, h1_ref, h2_ref) + _dot(x_ref[...], wb_ref[...])


def resid_matmul(x, h1, h2, w, tm=1024, tn=1024, name="resid_matmul"):
    T, K = x.shape
    N = w.shape[1]
    na = h1.shape[0] // tm
    return pl.pallas_call(
        functools.partial(_resid_kernel, nfirst=na),
        grid=(N // tn, T // tm),
        in_specs=[
            pl.BlockSpec((tm, K), lambda j, i: (i, 0)),
            *_cat_specs((tm, tn), na, row_axis=1, col_axis=0),
            pl.BlockSpec((K, tn), lambda j, i: (0, j)),
        ],
        out_specs=pl.BlockSpec((tm, tn), lambda j, i: (i, j)),
        out_shape=jax.ShapeDtypeStruct((T, N), F32),
        scratch_shapes=[pltpu.VMEM((K, tn), BF16)],
        compiler_params=_cparams(("arbitrary", "arbitrary")),
        name=name,
    )(x, h1, h2, w)


def _router_kernel(h_ref, nw_ref, wr_ref, br_ref, xn_ref, route_ref, cnt_ref, tri_ref, carry_ref, *, tm):
    i = pl.program_id(0)

    @pl.when(i == 0)
    def _():
        rr = lax.broadcasted_iota(jnp.int32, (tm, tm), 0)
        cc = lax.broadcasted_iota(jnp.int32, (tm, tm), 1)
        tri_ref[...] = (rr > cc).astype(BF16)
        carry_ref[...] = jnp.zeros_like(carry_ref)

    x = h_ref[...]
    ms = jnp.mean(x * x, axis=-1, keepdims=True)
    xn = x * lax.rsqrt(ms + EPS) * nw_ref[...]
    xn_ref[...] = _pack_bf16_pairs(xn)
    logits = _dot(xn.astype(BF16), wr_ref[...].astype(BF16)) + br_ref[...]
    lane = lax.broadcasted_iota(jnp.int32, (tm, LANES), 1)
    lanef = lane.astype(F32)
    cur = logits
    vals, hots, eids = [], [], []
    for _ in range(TOP_K):
        m = jnp.max(cur, axis=-1, keepdims=True)
        idx = jnp.min(jnp.where(cur == m, lanef, float(LANES)), axis=-1, keepdims=True)
        hot = lanef == idx
        vals.append(m)
        hots.append(hot)
        eids.append(idx)
        cur = jnp.where(hot, -jnp.inf, cur)
    es = [jnp.exp(v - vals[0]) for v in vals]
    den = es[0] + es[1] + es[2] + es[3]
    multi = jnp.zeros((tm, LANES), F32)
    for hot in hots:
        multi = multi + hot.astype(F32)
    base = carry_ref[0:1, :] + _dot(tri_ref[...], multi.astype(BF16))
    route = jnp.zeros((tm, LANES), F32)
    for k in range(TOP_K):
        e_k = eids[k]
        r_k = jnp.sum(jnp.where(hots[k], base, 0.0), axis=-1, keepdims=True)
        w_k = es[k] / den
        route = route + jnp.where(lane == k, e_k, 0.0) + jnp.where(lane == TOP_K + k, r_k, 0.0) \
            + jnp.where(lane == 2 * TOP_K + k, w_k, 0.0)
    route_ref[...] = route
    carry = carry_ref[0:1, :] + jnp.sum(multi, axis=0, keepdims=True)
    carry_ref[...] = jnp.broadcast_to(carry, carry_ref.shape)
    cnt_ref[...] = jnp.broadcast_to(carry, cnt_ref.shape)


def moe_router(h, norm_w, w_router, b_router, tm=512, name="moe_router"):
    T, D = h.shape
    E = w_router.shape[1]
    wr = jnp.pad(w_router, ((0, 0), (0, LANES - E)))
    br = jnp.pad(b_router.reshape(1, E), ((0, 0), (0, LANES - E)), constant_values=NEG_BIG)
    kern = functools.partial(_router_kernel, tm=tm)
    return pl.pallas_call(
        kern,
        grid=(T // tm,),
        in_specs=[
            pl.BlockSpec((tm, D), lambda i: (i, 0)),
            pl.BlockSpec((1, D), lambda i: (0, 0)),
            pl.BlockSpec((D, LANES), lambda i: (0, 0)),
            pl.BlockSpec((1, LANES), lambda i: (0, 0)),
        ],
        out_specs=[
            pl.BlockSpec((tm, D // 2), lambda i: (i, 0)),
            pl.BlockSpec((tm, LANES), lambda i: (i, 0)),
            pl.BlockSpec((8, LANES), lambda i: (0, 0)),
        ],
        out_shape=[jax.ShapeDtypeStruct((T, D // 2), jnp.uint32),
                   jax.ShapeDtypeStruct((T, LANES), F32),
                   jax.ShapeDtypeStruct((8, LANES), F32)],
        scratch_shapes=[pltpu.VMEM((tm, tm), BF16), pltpu.VMEM((8, LANES), F32)],
        compiler_params=_cparams(("arbitrary",)),
        name=name,
    )(h, norm_w.reshape(1, D), wr, br)


ROW_UNROLL = 8


def _pack_bf16_pairs(x):
    h = x.shape[1] // 2
    lo = lax.bitcast_convert_type(x[:, :h].astype(BF16).astype(F32), jnp.uint32)
    hi = lax.bitcast_convert_type(x[:, h:].astype(BF16).astype(F32), jnp.uint32)
    return (hi & jnp.uint32(0xFFFF0000)) | (lo >> 16)


def _unpack_bf16_pairs(w):
    lo = lax.bitcast_convert_type(w << 16, F32).astype(BF16)
    hi = lax.bitcast_convert_type(w & jnp.uint32(0xFFFF0000), F32).astype(BF16)
    return jnp.concatenate([lo, hi], axis=1)


def _dispatch_kernel(dest_ref, zrow_ref, x_ref, buf_ref, zero_ref, xs_ref, sem, zsem, *, tm, n_tail):
    i = pl.program_id(0)
    E = (zrow_ref.shape[0] - 1) // 2
    ZR = zero_ref.shape[0]
    R = buf_ref.shape[0]

    @pl.when(i == 0)
    def _():
        zero_ref[...] = jnp.zeros_like(zero_ref)

        def zero_copy(row, n):
            return pltpu.make_async_copy(zero_ref.at[pl.ds(0, n)], buf_ref.at[pl.ds(pl.multiple_of(row, 8), n)], zsem)

        def zero_rows(act):
            for e in range(E):
                off = zrow_ref[e]
                n = ZR
                while n >= 8:
                    @pl.when((zrow_ref[E + e] & n) != 0)
                    def _():
                        act(zero_copy(off, n))
                    off = off + (zrow_ref[E + e] & n)
                    n //= 2
            for j in range(n_tail):
                row = zrow_ref[2 * E] + j * ZR

                @pl.when(row < R)
                def _():
                    act(zero_copy(row, ZR))

        zero_rows(lambda c: c.start())
        zero_rows(lambda c: c.wait())

    par = i % 2
    xs_ref[par] = x_ref[...]

    def start(rb, c):
        for j in range(ROW_UNROLL):
            r = rb * ROW_UNROLL + j
            for k in range(TOP_K):
                d = dest_ref[(i * tm + r) * TOP_K + k]
                pltpu.make_async_copy(xs_ref.at[par, pl.ds(r, 1)], buf_ref.at[pl.ds(d, 1)],
                                      sem.at[par]).start(priority=k % 2)
        return c

    lax.fori_loop(0, tm // ROW_UNROLL, start, 0)

    def wait_step(s):
        for k in range(TOP_K):
            pltpu.make_async_copy(xs_ref.at[s], buf_ref.at[pl.ds(0, tm)], sem.at[s]).wait()

    @pl.when(i > 0)
    def _():
        wait_step(1 - par)

    @pl.when(i == pl.num_programs(0) - 1)
    def _():
        wait_step(par)


def moe_dispatch(xn, dest_flat, zrows, n_rows, tm=256, name="moe_dispatch"):
    T, H = xn.shape
    n_tail = (n_rows - T * TOP_K) // MOE_TM
    kern = functools.partial(_dispatch_kernel, tm=tm, n_tail=n_tail)
    return pl.pallas_call(
        kern,
        grid_spec=pltpu.PrefetchScalarGridSpec(
            num_scalar_prefetch=2,
            grid=(T // tm,),
            in_specs=[pl.BlockSpec((tm, H), lambda i, d, z: (i, 0))],
            out_specs=pl.BlockSpec(memory_space=pl.ANY),
            scratch_shapes=[pltpu.VMEM((MOE_TM, H), jnp.uint32), pltpu.VMEM((2, tm, H), jnp.uint32),
                            pltpu.SemaphoreType.DMA((2,)), pltpu.SemaphoreType.DMA(())],
        ),
        out_shape=jax.ShapeDtypeStruct((n_rows, H), jnp.uint32),
        compiler_params=_cparams(("arbitrary",)),
        name=name,
    )(dest_flat, zrows, xn)


def _experts_kernel(ue_ref, r0_ref, nt_ref, na_ref, x_hbm, wg_ref, wu_ref, bg_ref, bu_ref, wd_ref, bd_ref,
                    o_hbm, xb_ref, acc_ref, zero_ref, sem_in, sem_out, zsem, *, TM, UT, NF, n_tail):
    TPS = -(-UT // NF)
    u = pl.program_id(0)
    f = pl.program_id(1)
    na = na_ref[0]
    nt = nt_ref[u]
    r0 = r0_ref[u]
    slot = u % 2
    un = jnp.minimum(u + 1, na - 1)

    def prefetch(j):
        return jnp.logical_and(jnp.logical_and(u + 1 < na, j < nt_ref[un]), f == j // TPS)

    def x_copy(base, i, s):
        rows = pl.ds(pl.multiple_of(base + i * TM, TM), TM)
        return pltpu.make_async_copy(x_hbm.at[rows], xb_ref.at[s, i], sem_in.at[i])

    def o_copy(base, i):
        rows = pl.ds(pl.multiple_of(base + i * TM, TM), TM)
        return pltpu.make_async_copy(acc_ref.at[i], o_hbm.at[rows], sem_out.at[i])

    def wait_outputs(unit):
        for i in range(UT):
            @pl.when(i < nt_ref[unit])
            def _():
                o_copy(r0_ref[unit], i).wait()

    @pl.when(u < na)
    def _():
        @pl.when(jnp.logical_and(f == 0, u == 0))
        def _():
            for act in (lambda c: c.start(), lambda c: c.wait()):
                for i in range(UT):
                    @pl.when(i < nt)
                    def _():
                        act(x_copy(r0, i, 0))

        for j in range(UT):
            @pl.when(prefetch(j))
            def _():
                x_copy(r0_ref[un], j, 1 - slot).start()

        up = jnp.maximum(u - 1, 0)

        def wait_prev_output(j):
            @pl.when(jnp.logical_and(u > 0, j < nt_ref[up]))
            def _():
                o_copy(r0_ref[up], j).wait()

        bg = bg_ref[...]
        bu = bu_ref[...]

        def tile_out(xt):
            g = _dot(xt, wg_ref[...].astype(BF16)) + bg
            up = _dot(xt, wu_ref[...].astype(BF16)) + bu
            gate = jnp.minimum(g, SWIGLU_LIMIT)
            lin = jnp.clip(up, -SWIGLU_LIMIT, SWIGLU_LIMIT)
            act = (gate * jax.nn.sigmoid(SWIGLU_ALPHA * gate) * (lin + 1.0)).astype(BF16)
            return _dot(act, wd_ref[...].astype(BF16))

        D = acc_ref.shape[2]

        def update(i, n, first):
            tiles = pl.ds(i, n)
            if first:
                for t in range(n):
                    wait_prev_output(i + t)
            out = tile_out(_unpack_bf16_pairs(xb_ref[slot, tiles].reshape(n * TM, D // 2)))
            if first:
                out = out + bd_ref[...]
            else:
                out = out + acc_ref[tiles].reshape(n * TM, D)
            acc_ref[tiles] = out.reshape(n, TM, D)

            @pl.when(f == NF - 1)
            def _():
                for t in range(n):
                    o_copy(r0, i + t).start()

        def all_tiles(first):
            def body(j, c):
                update(pl.multiple_of(4 * j, 4), 4, first)
                return c
            lax.fori_loop(0, nt // 4, body, 0)
            done4 = (nt // 4) * 4

            @pl.when((nt & 2) != 0)
            def _():
                update(pl.multiple_of(done4, 2), 2, first)

            @pl.when((nt & 1) != 0)
            def _():
                update(nt - 1, 1, first)

        @pl.when(f == 0)
        def _():
            all_tiles(True)
            for j in range(UT):
                @pl.when(j >= nt)
                def _():
                    wait_prev_output(j)

        @pl.when(f > 0)
        def _():
            all_tiles(False)

        for j in range(UT):
            @pl.when(prefetch(j))
            def _():
                x_copy(r0_ref[un], j, 1 - slot).wait()

        @pl.when(jnp.logical_and(f == NF - 1, u == na - 1))
        def _():
            zero_ref[...] = jnp.zeros_like(zero_ref)
            for act in (lambda c: c.start(), lambda c: c.wait()):
                for j in range(n_tail):
                    row = na_ref[1] + j * TM

                    @pl.when(row < o_hbm.shape[0])
                    def _():
                        act(pltpu.make_async_copy(zero_ref, o_hbm.at[pl.ds(pl.multiple_of(row, TM), TM)], zsem))
            wait_outputs(u)


def moe_experts(x_buf, plan, n_assign, w_gate_up, b_gate_up, w_down, b_down, name="moe_experts"):
    R, H = x_buf.shape
    D = 2 * H
    n_tail = (R - n_assign) // MOE_TM
    E, _, F2 = w_gate_up.shape
    DF = F2 // 2
    TM, UT, TF = MOE_TM, MOE_UNIT_TILES, MOE_TF
    NF = DF // TF
    ue, r0, nt, na = plan
    U = ue.shape[0]

    def fsel(u, f, na_ref):
        return jnp.where(u < na_ref[0], f, NF - 1)

    kern = functools.partial(_experts_kernel, TM=TM, UT=UT, NF=NF, n_tail=n_tail)
    return pl.pallas_call(
        kern,
        grid_spec=pltpu.PrefetchScalarGridSpec(
            num_scalar_prefetch=4,
            grid=(U, NF),
            in_specs=[
                pl.BlockSpec(memory_space=pl.ANY),
                pl.BlockSpec((None, D, TF), lambda u, f, ue, r0, nt, na: (ue[u], 0, fsel(u, f, na))),
                pl.BlockSpec((None, D, TF), lambda u, f, ue, r0, nt, na: (ue[u], 0, NF + fsel(u, f, na))),
                pl.BlockSpec((None, 1, TF), lambda u, f, ue, r0, nt, na: (ue[u], 0, fsel(u, f, na))),
                pl.BlockSpec((None, 1, TF), lambda u, f, ue, r0, nt, na: (ue[u], 0, NF + fsel(u, f, na))),
                pl.BlockSpec((None, TF, D), lambda u, f, ue, r0, nt, na: (ue[u], fsel(u, f, na), 0)),
                pl.BlockSpec((None, 1, D), lambda u, f, ue, r0, nt, na: (ue[u], 0, 0)),
            ],
            out_specs=pl.BlockSpec(memory_space=pl.ANY),
            scratch_shapes=[
                pltpu.VMEM((2, UT, TM, H), jnp.uint32),
                pltpu.VMEM((UT, TM, D), F32),
                pltpu.VMEM((TM, D), F32),
                pltpu.SemaphoreType.DMA((UT,)),
                pltpu.SemaphoreType.DMA((UT,)),
                pltpu.SemaphoreType.DMA(()),
            ],
        ),
        out_shape=jax.ShapeDtypeStruct((R, D), F32),
        compiler_params=_cparams(("arbitrary", "arbitrary"), VMEM_LIMIT_EXPERTS_BYTES),
        name=name,
    )(ue, r0, nt, na, x_buf, w_gate_up, w_gate_up, b_gate_up.reshape(E, 1, F2), b_gate_up.reshape(E, 1, F2),
      w_down, b_down.reshape(E, 1, D))


def moe_plan(counts, n_tokens):
    TM, UT = MOE_TM, MOE_UNIT_TILES
    E = counts.shape[0]
    nt = (counts + TM - 1) // TM
    padded = nt * TM
    pad_start = jnp.cumsum(padded) - padded
    nu = (nt + UT - 1) // UT
    cu = jnp.cumsum(nu)
    n_units = cu[-1]
    U = E + (n_tokens * TOP_K) // (TM * UT)
    u = jnp.arange(U, dtype=jnp.int32)
    uc = jnp.minimum(u, n_units - 1)
    ue = jnp.clip(jnp.searchsorted(cu, uc, side='right'), 0, E - 1).astype(jnp.int32)
    j = uc - (cu[ue] - nu[ue])
    r0 = (pad_start[ue] + j * (UT * TM)).astype(jnp.int32)
    ntl = jnp.where(u < n_units, jnp.clip(nt[ue] - j * UT, 0, UT), 0).astype(jnp.int32)
    zstart = (pad_start + counts) // 8 * 8
    pad_end = pad_start + padded
    zrows = jnp.concatenate([zstart, pad_end - zstart, pad_end[-1:]]).astype(jnp.int32)
    meta = jnp.stack([n_units, pad_end[-1]]).astype(jnp.int32)
    return pad_start, zrows, (ue, r0, ntl, meta)


def _combine_kernel(dest_ref, h_ref, route_ref, nw_ref, o_hbm, h2_ref, xn_ref, g_ref, sem, *, tm):
    i = pl.program_id(0)
    par = i % 2

    def gather_tile(t, slot):
        def start(rb, c):
            for j in range(ROW_UNROLL):
                r = rb * ROW_UNROLL + j
                for k in range(TOP_K):
                    d = dest_ref[(t * tm + r) * TOP_K + k]
                    pltpu.make_async_copy(o_hbm.at[pl.ds(d, 1)], g_ref.at[slot, k, pl.ds(r, 1)],
                                          sem.at[slot]).start(priority=k % 2)
            return c

        lax.fori_loop(0, tm // ROW_UNROLL, start, 0)

    @pl.when(i == 0)
    def _():
        gather_tile(i, par)

    @pl.when(i + 1 < pl.num_programs(0))
    def _():
        gather_tile(i + 1, 1 - par)

    for k in range(TOP_K):
        pltpu.make_async_copy(o_hbm.at[pl.ds(0, tm)], g_ref.at[par, k], sem.at[par]).wait()

    route = route_ref[...]
    y = g_ref[par, 0] * route[:, 2 * TOP_K:2 * TOP_K + 1]
    for k in range(1, TOP_K):
        y = y + g_ref[par, k] * route[:, 2 * TOP_K + k:2 * TOP_K + k + 1]
    h2 = h_ref[...] + y
    h2_ref[...] = h2
    ms = jnp.mean(h2 * h2, axis=-1, keepdims=True)
    xn_ref[...] = (h2 * lax.rsqrt(ms + EPS) * nw_ref[...]).astype(xn_ref.dtype)


def moe_combine(h, route, dest_flat, out_buf, norm_w, tm=256, name="moe_combine"):
    T, D = h.shape
    kern = functools.partial(_combine_kernel, tm=tm)
    return pl.pallas_call(
        kern,
        grid_spec=pltpu.PrefetchScalarGridSpec(
            num_scalar_prefetch=1,
            grid=(T // tm,),
            in_specs=[pl.BlockSpec((tm, D), lambda i, d: (i, 0)),
                      pl.BlockSpec((tm, LANES), lambda i, d: (i, 0)),
                      pl.BlockSpec((1, D), lambda i, d: (0, 0)),
                      pl.BlockSpec(memory_space=pl.ANY)],
            out_specs=[pl.BlockSpec((tm, D), lambda i, d: (i, 0)),
                       pl.BlockSpec((tm, D), lambda i, d: (i, 0))],
            scratch_shapes=[pltpu.VMEM((2, TOP_K, tm, D), F32), pltpu.SemaphoreType.DMA((2,))],
        ),
        out_shape=[jax.ShapeDtypeStruct((T, D), F32), jax.ShapeDtypeStruct((T, D), BF16)],
        compiler_params=_cparams(("arbitrary",)),
        name=name,
    )(dest_flat, h, route, norm_w.reshape(1, D), out_buf)


def _ple_kernel(xn_ref, p1_ref, p2_ref, h_ref, wg_ref, wp_ref, nf_ref, oa_ref, ob_ref, wgb_ref, wpb_ref, *, nfirst):
    i = pl.program_id(0)

    @pl.when(i == 0)
    def _():
        wgb_ref[...] = wg_ref[...].astype(BF16)
        wpb_ref[...] = wp_ref[...].astype(BF16)

    gate = jax.nn.sigmoid(_dot(xn_ref[...], wgb_ref[...]))
    p = _pick(i, nfirst, p1_ref, p2_ref).astype(BF16)
    y = _rms(h_ref[...] + _dot(p, wpb_ref[...]) * gate, nf_ref[...])

    @pl.when(i < nfirst)
    def _():
        oa_ref[...] = y

    @pl.when(i >= nfirst)
    def _():
        ob_ref[...] = y


def ple_final(xn, p1, p2, h, w_gate, w_ple, norm_w, tm=256, name="ple_final"):
    T, D = h.shape
    Ta, P = p1.shape
    na = Ta // tm
    once = pl.Buffered(1)
    return pl.pallas_call(
        functools.partial(_ple_kernel, nfirst=na),
        grid=(T // tm,),
        in_specs=[
            pl.BlockSpec((tm, D), lambda i: (i, 0)),
            *_cat_specs((tm, P), na),
            pl.BlockSpec((tm, D), lambda i: (i, 0)),
            pl.BlockSpec((D, D), lambda i: (0, 0), pipeline_mode=once),
            pl.BlockSpec((P, D), lambda i: (0, 0), pipeline_mode=once),
            pl.BlockSpec((1, D), lambda i: (0, 0)),
        ],
        out_specs=list(_cat_specs((tm, D), na)),
        out_shape=[jax.ShapeDtypeStruct((Ta, D), F32), jax.ShapeDtypeStruct((T - Ta, D), F32)],
        scratch_shapes=[pltpu.VMEM((D, D), BF16), pltpu.VMEM((P, D), BF16)],
        compiler_params=_cparams(("arbitrary",)),
        name=name,
    )(xn, p1, p2, h, w_gate, w_ple, norm_w.reshape(1, D))


def kernel(x_prompt, x_sample, p_prompt, p_sample, state_hgrn, state_ssm_re, state_ssm_im, norm_mix, w_in, hg_lb, hg_gnorm, w_branch_a, ssm_a_re, ssm_a_im, ssm_log_dt, ssm_b_re, ssm_b_im, ssm_c_re, ssm_c_im, ssm_d, w_glu, w_branch_b, w_out, norm_moe, w_router, b_router, w_gate_up, b_gate_up, w_down, b_down, norm_ple, w_ple, w_ple_gate, norm_final):
    BP, LP, D = x_prompt.shape
    BS, LS, _ = x_sample.shape
    depth = w_in.shape[0]
    assert depth == 1
    TP, TS = BP * LP, BS * LS
    T = TP + TS
    G, P = ssm_a_re.shape[1:]
    KD = HG_HEADS * HG_DK
    W = SSM_GROUP * G

    xp = x_prompt.reshape(TP, D)
    xs = x_sample.reshape(TS, D)

    xn = rmsnorm_cat(xp, xs, norm_mix[0], BF16, name="norm_mix")
    tm = 1024
    w_in0 = w_in[0]
    z_hg = project(xn, w_in0, 0, T // tm, 0, 4, tm, 1024, "proj_hgrn")
    u_p = project(xn, w_in0, 0, TP // tm, 4, 1, tm, 1024, "proj_u_prompt")
    u_s = project(xn, w_in0, TP // tm, TS // tm, 4, 1, tm, 1024, "proj_u_sample")
    z_gate = project(xn, w_in0, 0, T // tm, 5, 4, tm, 1024, "proj_gates", out_dtype=BF16)

    lb = jax.nn.softmax(hg_lb.astype(F32), axis=0)[0]
    og_p, hg_p = hgrn2(z_hg, lb, hg_gnorm[0], None, row0=0, B=BP, L=LP, C=HG_CHUNK, nb=1, tok_step=256,
                       name="hgrn_prompt")
    og_s, hg_s = hgrn2(z_hg, lb, hg_gnorm[0], state_hgrn[0], row0=TP, B=BS, L=LS, C=LS, nb=8, tok_step=LS,
                       name="hgrn_sample")

    bd, cd, ab = s5_params(ssm_a_re[0], ssm_a_im[0], ssm_log_dt[0], ssm_b_re[0], ssm_b_im[0],
                           ssm_c_re[0], ssm_c_im[0])
    y_p, re_p, im_p = s5_prompt(u_p.reshape(BP, LP, W), bd, cd, ab, ssm_d[0], TC=256, name="s5_prompt")
    y_s, re_s, im_s = s5_sample(u_s, bd, cd, ab, ssm_d[0], state_ssm_re[0].reshape(BS, G * P),
                                state_ssm_im[0].reshape(BS, G * P), B=BS, L=LS, name="s5_sample")

    glu = gelu_glu(y_p.reshape(TP, W), y_s, w_glu[0])
    merged = gated_merge(og_p, og_s, glu, z_gate, w_branch_a[0], w_branch_b[0])
    h1 = resid_matmul(merged, xp, xs, w_out[0], name="out_proj")

    xn2, route, cnt = moe_router(h1, norm_moe[0], w_router[0], b_router[0])
    e_idx = route[:, 0:TOP_K].astype(jnp.int32)
    rank = route[:, TOP_K:2 * TOP_K].astype(jnp.int32)
    counts = cnt[0, :N_EXPERTS].astype(jnp.int32)
    pad_start, zrows, plan = moe_plan(counts, T)
    onehot = e_idx[..., None] == jnp.arange(N_EXPERTS, dtype=jnp.int32)
    dest = (jnp.sum(jnp.where(onehot, pad_start, 0), axis=-1) + rank).astype(jnp.int32).reshape(T * TOP_K)
    n_rows = T * TOP_K + N_EXPERTS * MOE_TM
    x_buf = moe_dispatch(xn2, dest, zrows, n_rows)
    out_buf = moe_experts(x_buf, plan, T * TOP_K, w_gate_up[0], b_gate_up[0], w_down[0], b_down[0])
    h2, xn3 = moe_combine(h1, route, dest, out_buf, norm_ple[0])

    y_p2, y_s2 = ple_final(xn3, p_prompt[0].reshape(TP, -1), p_sample[0].reshape(TS, -1), h2, w_ple_gate[0],
                           w_ple[0], norm_final)

    y_prompt = y_p2.reshape(BP, LP, D)
    y_sample = y_s2.reshape(BS, LS, D)
    hs = (1, BP, HG_HEADS, HG_DK, HG_DV)
    return (y_prompt, y_sample,
            hg_p.reshape(hs), re_p.reshape(1, BP, G, P), im_p.reshape(1, BP, G, P),
            hg_s.reshape(1, BS, HG_HEADS, HG_DK, HG_DV), re_s.reshape(1, BS, G, P), im_s.reshape(1, BS, G, P))
```

```python
import functools

import jax
import jax.numpy as jnp
from jax import lax
from jax.experimental import pallas as pl
from jax.experimental.pallas import tpu as pltpu

F32 = jnp.float32
BF16 = jnp.bfloat16
EPS = 1e-6

HG_HEADS = 8
HG_DK = 128
HG_DV = 128
HG_CHUNK = 32
SSM_GROUP = 16
N_EXPERTS = 32
TOP_K = 4
SWIGLU_LIMIT = 7.0
SWIGLU_ALPHA = 1.702

LANES = 128
VMEM_LIMIT_BYTES = 56 * 1024 * 1024
VMEM_LIMIT_EXPERTS_BYTES = 61 * 1024 * 1024

MOE_TM = 128
MOE_UNIT_TILES = 12
MOE_GROUP = 8
MOE_TF = 512
NEG_BIG = -1e30


def _cparams(sem, vmem_limit=VMEM_LIMIT_BYTES):
    return pltpu.CompilerParams(dimension_semantics=sem, vmem_limit_bytes=vmem_limit)


def _dot(a, b):
    return jnp.dot(a, b, preferred_element_type=F32)


def _split2(a):
    hi = a.astype(BF16)
    lo = (a - hi.astype(F32)).astype(BF16)
    return hi, lo


def _dot_x3(a, b_split):
    a1, a2 = _split2(a)
    b1, b2 = b_split
    return _dot(a1, b1) + (_dot(a1, b2) + _dot(a2, b1))


WEIGHT_STAGE_ROWS = 256


def _load_weight_bf16(w_hbm, wb_ref, stage_ref, sem):
    K = w_hbm.shape[0]
    rows = min(stage_ref.shape[1], K)
    n = K // rows

    def copy(c):
        return pltpu.make_async_copy(w_hbm.at[pl.ds(c * rows, rows)], stage_ref.at[c % 2, pl.ds(0, rows)],
                                     sem.at[c % 2])

    copy(0).start()
    for c in range(n):
        if c + 1 < n:
            copy(c + 1).start()
        copy(c).wait()
        wb_ref[pl.ds(c * rows, rows), :] = stage_ref[c % 2, pl.ds(0, rows), :].astype(BF16)


def _cat_specs(shape, nfirst, row_axis=0, col_axis=None):
    def col(ids):
        return 0 if col_axis is None else ids[col_axis]

    first = pl.BlockSpec(shape, lambda *ids: (jnp.minimum(ids[row_axis], nfirst - 1), col(ids)))
    second = pl.BlockSpec(shape, lambda *ids: (jnp.maximum(ids[row_axis] - nfirst, 0), col(ids)))
    return first, second


def _pick(i, nfirst, a_ref, b_ref):
    return jnp.where(i < nfirst, a_ref[...], b_ref[...])


def _rms(x, w):
    ms = jnp.mean(x * x, axis=-1, keepdims=True)
    return x * lax.rsqrt(ms + EPS) * w


def _rms_in2_kernel(a_ref, b_ref, w_ref, o_ref, *, nfirst):
    x = _pick(pl.program_id(0), nfirst, a_ref, b_ref)
    o_ref[...] = _rms(x, w_ref[...]).astype(o_ref.dtype)


def rmsnorm_cat(xa, xb, w, out_dtype, tm=512, name="rmsnorm"):
    (Ta, D), Tb = xa.shape, xb.shape[0]
    na = Ta // tm
    return pl.pallas_call(
        functools.partial(_rms_in2_kernel, nfirst=na),
        grid=((Ta + Tb) // tm,),
        in_specs=[*_cat_specs((tm, D), na), pl.BlockSpec((1, D), lambda i: (0, 0))],
        out_specs=pl.BlockSpec((tm, D), lambda i: (i, 0)),
        out_shape=jax.ShapeDtypeStruct((Ta + Tb, D), out_dtype),
        compiler_params=_cparams(("arbitrary",)),
        name=name,
    )(xa, xb, w.reshape(1, D))


def _proj_kernel(x_ref, w_ref, o_ref, wb_ref):
    @pl.when(pl.program_id(1) == 0)
    def _():
        wb_ref[...] = w_ref[...].astype(BF16)

    o_ref[...] = _dot(x_ref[...], wb_ref[...]).astype(o_ref.dtype)


def project(xn, w, row_blk0, n_row_blks, col_blk0, n_col_blks, tm, tn, name, out_dtype=F32):
    K = xn.shape[1]
    return pl.pallas_call(
        _proj_kernel,
        grid=(n_col_blks, n_row_blks),
        in_specs=[
            pl.BlockSpec((tm, K), lambda j, i: (row_blk0 + i, 0)),
            pl.BlockSpec((K, tn), lambda j, i: (0, col_blk0 + j)),
        ],
        out_specs=pl.BlockSpec((tm, tn), lambda j, i: (i, j)),
        out_shape=jax.ShapeDtypeStruct((n_row_blks * tm, n_col_blks * tn), out_dtype),
        scratch_shapes=[pltpu.VMEM((K, tn), BF16)],
        compiler_params=_cparams(("arbitrary", "arbitrary")),
        name=name,
    )(xn, w)


HGRN_CHUNKS_PER_TRIP = 4


def _hgrn_kernel(*refs, C, nchunk, nb, tok_step, has_s0, state_t):
    if has_s0:
        q_ref, f_ref, v_ref, g_ref, lb_ref, gn_ref, s0_ref, og_ref, so_ref, st_ref = refs
    else:
        q_ref, f_ref, v_ref, g_ref, lb_ref, gn_ref, og_ref, so_ref, st_ref = refs
        s0_ref = None
    H, DK, DV = HG_HEADS, HG_DK, HG_DV
    step = pl.program_id(1)

    @pl.when(step == 0)
    def _():
        for i in range(nb):
            for h in range(H):
                if has_s0:
                    st_ref[i, h] = s0_ref[i, h].T if state_t else s0_ref[i, h]
                else:
                    st_ref[i, h] = jnp.zeros(st_ref.shape[2:], F32)

    lbv = lb_ref[...]
    gnv = gn_ref[...]
    rr = lax.broadcasted_iota(jnp.int32, (C, C), 0)
    cc = lax.broadcasted_iota(jnp.int32, (C, C), 1)
    causal = rr >= cc
    tri = causal.astype(BF16)
    sls =[slice(h * DK, (h + 1) * DK) for h in range(H)]
    nt_dims = (((1,), (1,)), ((), ()))
    tn_dims = (((0,), (0,)), ((), ()))

    def tn(a, b):
        return lax.dot_general(a, b, tn_dims, preferred_element_type=F32)

    def prepare(r0):
        rows = pl.ds(r0, C)
        zq = q_ref[rows, :]
        zf = f_ref[rows, :]
        q = jax.nn.silu(zq)
        fe = lbv + (1.0 - lbv) * jax.nn.sigmoid(zf)
        k = 1.0 - fe
        gl = jnp.log(fe)
        g1 = gl.astype(BF16)
        r1 = gl - g1.astype(F32)
        g2 = r1.astype(BF16)
        g3 = (r1 - g2.astype(F32)).astype(BF16)
        b = _dot(tri, g1) + (_dot(tri, g2) + _dot(tri, g3))
        blast = b[C - 1:C, :]
        return dict(
            qt=(q * jnp.exp(b)).astype(BF16),
            kt=(k * jnp.exp(-b)).astype(BF16),
            kend=(k * jnp.exp(blast - b)).astype(BF16),
            eblast=jnp.exp(blast),
            vb=v_ref[rows, :].astype(BF16),
            gate=jax.nn.silu(g_ref[rows, :]),
        )

    def run(jobs):
        P = [prepare(r0) for _, r0 in jobs]
        atts = [[lax.dot_general(p["qt"][:, sl], p["kt"][:, sl], nt_dims, preferred_element_type=F32)
                 for sl in sls] for p in P]
        if state_t:
            upds = [[tn(p["vb"][:, sl], p["kend"][:, sl]) for sl in sls] for p in P]
            decs = [[p["eblast"][:, sl] for sl in sls] for p in P]
        else:
            upds = [[tn(p["kend"][:, sl], p["vb"][:, sl]) for sl in sls] for p in P]
            decs = [[jnp.broadcast_to(p["eblast"][:, sl], (8, DK)).T[:, 0:1] for sl in sls] for p in P]
        state = {}
        inters = []
        for j, (i, _) in enumerate(jobs):
            row = []
            for h, sl in enumerate(sls):
                s = state[(i, h)] if (i, h) in state else st_ref[i, h]
                qth = P[j]["qt"][:, sl]
                if state_t:
                    row.append(lax.dot_general(qth, s.astype(BF16), nt_dims, preferred_element_type=F32))
                else:
                    row.append(_dot(qth, s.astype(BF16)))
                state[(i, h)] = s * decs[j][h] + upds[j][h]
            inters.append(row)
        for (i, h), s in state.items():
            st_ref[i, h] = s
        outs = []
        for j, p in enumerate(P):
            cols = []
            for h, sl in enumerate(sls):
                att = jnp.where(causal, atts[j][h], 0.0).astype(BF16)
                o = _dot(att, p["vb"][:, sl]) + inters[j][h]
                ms = jnp.mean(o * o, axis=-1, keepdims=True)
                cols.append(o * lax.rsqrt(ms + EPS) * gnv * p["gate"][:, sl])
            outs.append(jnp.concatenate(cols, axis=1))
        return outs

    if nchunk == 1:
        outs = run([(i, i * tok_step) for i in range(nb)])
        og_ref[...] = jnp.concatenate(outs, axis=0).astype(og_ref.dtype)
    else:
        assert nb == 1 and nchunk % HGRN_CHUNKS_PER_TRIP == 0

        def body(t, carry):
            r0s = [pl.multiple_of((HGRN_CHUNKS_PER_TRIP * t + j) * C, C) for j in range(HGRN_CHUNKS_PER_TRIP)]
            outs = run([(0, r0) for r0 in r0s])
            for r0, o in zip(r0s, outs):
                og_ref[pl.ds(r0, C), :] = o.astype(og_ref.dtype)
            return carry

        lax.fori_loop(0, nchunk // HGRN_CHUNKS_PER_TRIP, body, 0)

    @pl.when(step == pl.num_programs(1) - 1)
    def _():
        for i in range(nb):
            for h in range(H):
                so_ref[i, h] = st_ref[i, h].T if state_t else st_ref[i, h]


def hgrn2(z_hg, lb, gn, s0, *, row0, B, L, C, nb, tok_step, name):
    H, DK, DV = HG_HEADS, HG_DK, HG_DV
    W = H * DK
    steps = L // tok_step
    rows = nb * tok_step
    nchunk = tok_step // C
    blk0 = row0 // rows
    has_s0 = s0 is not None

    def zspec(col):
        return pl.BlockSpec((rows, W), lambda bb, s: (blk0 + bb * steps + s, col))

    in_specs = [zspec(0), zspec(1), zspec(2), zspec(3),
                pl.BlockSpec((1, W), lambda bb, s: (0, 0)),
                pl.BlockSpec((1, DV), lambda bb, s: (0, 0))]
    args = [z_hg, z_hg, z_hg, z_hg, lb.reshape(1, W), gn.reshape(1, DV)]
    if has_s0:
        in_specs.append(pl.BlockSpec((nb, H, DK, DV), lambda bb, s: (bb, 0, 0, 0)))
        args.append(s0)
    state_t = L > C
    kern = functools.partial(_hgrn_kernel, C=C, nchunk=nchunk, nb=nb, tok_step=tok_step, has_s0=has_s0,
                             state_t=state_t)
    return pl.pallas_call(
        kern,
        grid=(B // nb, steps),
        in_specs=in_specs,
        out_specs=[pl.BlockSpec((rows, W), lambda bb, s: (bb * steps + s, 0)),
                   pl.BlockSpec((nb, H, DK, DV), lambda bb, s: (bb, 0, 0, 0))],
        out_shape=[jax.ShapeDtypeStruct((B * L, W), BF16),
                   jax.ShapeDtypeStruct((B, H, DK, DV), F32)],
        scratch_shapes=[pltpu.VMEM((nb, H, DV, DK), F32)],
        compiler_params=_cparams(("arbitrary", "arbitrary")),
        name=name,
    )(*args)


def _s5_step(ar, ai, xr, xi, br, bi):
    return ar * xr - ai * xi + br, ar * xi + ai * xr + bi


S5_STEPS_PER_TRIP = 4
S5_GROUPS_PER_BLOCK = 16


def _s5_prompt_kernel(u_ref, bd_ref, cd_ref, a_ref, d_ref, y_ref, xr_ref, xi_ref, X_ref, st_ref, *, B, TC):
    NP = X_ref.shape[0] // 2
    assert 2 * B == 8
    tc = pl.program_id(1)

    def rows_of(b, s):
        return pl.ds(2 * b + s, TC, stride=8)

    @pl.when(tc == 0)
    def _():
        st_ref[...] = jnp.zeros_like(st_ref)

    bd = bd_ref[...].astype(BF16)
    for b in range(B):
        bu = _dot(u_ref[b].astype(BF16), bd)
        for c in range(4 * NP):
            X_ref[c // 2, rows_of(b, c % 2), :] = bu[:, c * LANES:(c + 1) * LANES]

    odd = lax.broadcasted_iota(jnp.int32, (8, LANES), 0) % 2 == 1

    def pair_const(row, p):
        lo = a_ref[row:row + 1, (2 * p) * LANES:(2 * p + 1) * LANES]
        hi = a_ref[row:row + 1, (2 * p + 1) * LANES:(2 * p + 2) * LANES]
        return jnp.where(odd, hi, lo)

    ars = [pair_const(0, p) for p in range(NP)]
    ais = [pair_const(1, p) for p in range(NP)]

    def body(i, carry):
        carry = list(carry)
        for j in range(S5_STEPS_PER_TRIP):
            rows = pl.ds(pl.multiple_of((i * S5_STEPS_PER_TRIP + j) * 8, 8), 8)
            for p in range(NP):
                xr, xi = _s5_step(ars[p], ais[p], carry[p][0], carry[p][1], X_ref[p, rows, :], X_ref[NP + p, rows, :])
                X_ref[p, rows, :] = xr
                X_ref[NP + p, rows, :] = xi
                carry[p] = (xr, xi)
        return tuple(carry)

    init = tuple((st_ref[p], st_ref[NP + p]) for p in range(NP))
    fin = lax.fori_loop(0, TC // S5_STEPS_PER_TRIP, body, init)
    for p in range(NP):
        st_ref[p] = fin[p][0]
        st_ref[NP + p] = fin[p][1]

    cd = cd_ref[...].astype(BF16)
    dv = d_ref[...]
    for b in range(B):
        xb = jnp.concatenate([X_ref[c // 2, rows_of(b, c % 2), :] for c in range(4 * NP)], axis=1).astype(BF16)
        y_ref[b] = _dot(xb, cd) + dv * u_ref[b]

    @pl.when(tc == pl.num_programs(1) - 1)
    def _():
        for c in range(2 * NP):
            cols = slice(c * LANES, (c + 1) * LANES)
            xr_ref[:, cols] = st_ref[c // 2, pl.ds(c % 2, B, stride=2), :]
            xi_ref[:, cols] = st_ref[NP + c // 2, pl.ds(c % 2, B, stride=2), :]


def s5_prompt(u, bd, cd, ab, dskip, *, TC, name):
    B, L, W = u.shape
    GB, UB, SW = bd.shape
    half = SW // 2
    kern = functools.partial(_s5_prompt_kernel, B=B, TC=TC)
    return pl.pallas_call(
        kern,
        grid=(GB, L // TC),
        in_specs=[
            pl.BlockSpec((B, TC, UB), lambda g, t: (0, t, g)),
            pl.BlockSpec((None, UB, SW), lambda g, t: (g, 0, 0)),
            pl.BlockSpec((None, SW, UB), lambda g, t: (g, 0, 0)),
            pl.BlockSpec((None, 2, half), lambda g, t: (g, 0, 0)),
            pl.BlockSpec((1, UB), lambda g, t: (0, g)),
        ],
        out_specs=[
            pl.BlockSpec((B, TC, UB), lambda g, t: (0, t, g)),
            pl.BlockSpec((B, half), lambda g, t: (0, g)),
            pl.BlockSpec((B, half), lambda g, t: (0, g)),
        ],
        out_shape=[jax.ShapeDtypeStruct((B, L, W), F32),
                   jax.ShapeDtypeStruct((B, GB * half), F32),
                   jax.ShapeDtypeStruct((B, GB * half), F32)],
        scratch_shapes=[pltpu.VMEM((SW // LANES // 2, 8 * TC, LANES), F32),
                        pltpu.VMEM((SW // LANES // 2, 8, LANES), F32)],
        compiler_params=_cparams(("arbitrary", "arbitrary")),
        name=name,
    )(u, bd, cd, ab, dskip.reshape(1, W))


def _s5_sample_kernel(u_ref, bd_ref, cd_ref, a_ref, d_ref, x0r_ref, x0i_ref, y_ref, xr_ref, xi_ref, X_ref, *, B, L):
    NC = X_ref.shape[0]
    NH = NC // 2
    u = u_ref[...]
    bu = _dot_x3(u, _split2(bd_ref[...]))
    for c in range(NC):
        X_ref[c] = bu[:, c * LANES:(c + 1) * LANES]
    for c in range(NH):
        cols = slice(c * LANES, (c + 1) * LANES)
        ar = a_ref[0:1, cols]
        ai = a_ref[1:2, cols]
        xr = x0r_ref[:, cols]
        xi = x0i_ref[:, cols]
        for t in range(L):
            rows = pl.ds(t, B, stride=L)
            xr, xi = _s5_step(ar, ai, xr, xi, X_ref[c, rows, :], X_ref[NH + c, rows, :])
            X_ref[c, rows, :] = xr
            X_ref[NH + c, rows, :] = xi
        xr_ref[:, cols] = xr
        xi_ref[:, cols] = xi
    xs = jnp.concatenate([X_ref[c] for c in range(NC)], axis=1).astype(BF16)
    y_ref[...] = _dot(xs, cd_ref[...].astype(BF16)) + d_ref[...] * u


def s5_sample(u, bd, cd, ab, dskip, x0r, x0i, *, B, L, name):
    T, W = u.shape
    GB, UB, SW = bd.shape
    half = SW // 2
    kern = functools.partial(_s5_sample_kernel, B=B, L=L)
    return pl.pallas_call(
        kern,
        grid=(GB,),
        in_specs=[
            pl.BlockSpec((T, UB), lambda g: (0, g)),
            pl.BlockSpec((None, UB, SW), lambda g: (g, 0, 0)),
            pl.BlockSpec((None, SW, UB), lambda g: (g, 0, 0)),
            pl.BlockSpec((None, 2, half), lambda g: (g, 0, 0)),
            pl.BlockSpec((1, UB), lambda g: (0, g)),
            pl.BlockSpec((B, half), lambda g: (0, g)),
            pl.BlockSpec((B, half), lambda g: (0, g)),
        ],
        out_specs=[
            pl.BlockSpec((T, UB), lambda g: (0, g)),
            pl.BlockSpec((B, half), lambda g: (0, g)),
            pl.BlockSpec((B, half), lambda g: (0, g)),
        ],
        out_shape=[jax.ShapeDtypeStruct((T, W), F32),
                   jax.ShapeDtypeStruct((B, GB * half), F32),
                   jax.ShapeDtypeStruct((B, GB * half), F32)],
        scratch_shapes=[pltpu.VMEM((SW // LANES, T, LANES), F32)],
        compiler_params=_cparams(("arbitrary",)),
        name=name,
    )(u, bd, cd, ab, dskip.reshape(1, W), x0r, x0i)


def s5_params(a_re, a_im, log_dt, b_re, b_im, c_re, c_im):
    G, P = a_re.shape
    gpb = S5_GROUPS_PER_BLOCK
    GB = G // gpb
    A = lax.complex(a_re, a_im)
    dt = jnp.exp(log_dt)[:, None]
    A_bar = jnp.exp(A * dt)
    Bm = lax.complex(b_re, b_im)
    B_bar = ((A_bar - 1.0) / A)[..., None] * Bm
    eye = jnp.eye(gpb, dtype=F32)

    def in_map(m):
        m = m.reshape(GB, gpb, P, SSM_GROUP)
        return jnp.einsum('bgpc,gh->bgchp', m, eye).reshape(GB, gpb * SSM_GROUP, gpb * P)

    def out_map(m):
        m = m.reshape(GB, gpb, SSM_GROUP, P)
        return jnp.einsum('bgcp,gh->bgphc', m, eye).reshape(GB, gpb * P, gpb * SSM_GROUP)

    bd = jnp.concatenate([in_map(B_bar.real), in_map(B_bar.imag)], axis=2)
    cd = jnp.concatenate([out_map(c_re), out_map(-c_im)], axis=1)
    ab = jnp.stack([A_bar.real.reshape(GB, gpb * P), A_bar.imag.reshape(GB, gpb * P)], axis=1)
    return bd, cd, ab


def _glu_kernel(ya_ref, yb_ref, w_ref, o_ref, wb_ref, *, nfirst):
    i = pl.program_id(0)

    @pl.when(i == 0)
    def _():
        wb_ref[...] = w_ref[...].astype(BF16)

    zg = jax.nn.gelu(_pick(i, nfirst, ya_ref, yb_ref))
    o_ref[...] = (zg * jax.nn.sigmoid(_dot(zg.astype(BF16), wb_ref[...]))).astype(o_ref.dtype)


def gelu_glu(ya, yb, w, tm=512, name="gelu_glu"):
    (Ta, W), Tb = ya.shape, yb.shape[0]
    na = Ta // tm
    return pl.pallas_call(
        functools.partial(_glu_kernel, nfirst=na),
        grid=((Ta + Tb) // tm,),
        in_specs=[*_cat_specs((tm, W), na), pl.BlockSpec((W, W), lambda i: (0, 0))],
        out_specs=pl.BlockSpec((tm, W), lambda i: (i, 0)),
        out_shape=jax.ShapeDtypeStruct((Ta + Tb, W), BF16),
        scratch_shapes=[pltpu.VMEM((W, W), BF16)],
        compiler_params=_cparams(("arbitrary",)),
        name=name,
    )(ya, yb, w)


def _merge_kernel(oa1_ref, oa2_ref, ob_ref, ga_ref, gb_ref, wa_ref, wb_ref, o_ref, wab_ref, wbb_ref, *, nfirst):
    i = pl.program_id(1)

    @pl.when(i == 0)
    def _():
        wab_ref[...] = wa_ref[...].astype(BF16)
        wbb_ref[...] = wb_ref[...].astype(BF16)

    ya = _dot(_pick(i, nfirst, oa1_ref, oa2_ref), wab_ref[...])
    yb = _dot(ob_ref[...], wbb_ref[...])
    ga = jax.nn.sigmoid(ga_ref[...].astype(F32))
    gb = jax.nn.sigmoid(gb_ref[...].astype(F32))
    o_ref[...] = (ga * ya + gb * yb).astype(o_ref.dtype)


def gated_merge(oa1, oa2, ob, z_gate, wa, wb, tm=1024, tn=1024, name="gated_merge"):
    T, K = ob.shape
    N = wa.shape[1]
    nj = N // tn
    na = oa1.shape[0] // tm
    return pl.pallas_call(
        functools.partial(_merge_kernel, nfirst=na),
        grid=(nj, T // tm),
        in_specs=[
            *_cat_specs((tm, K), na, row_axis=1),
            pl.BlockSpec((tm, K), lambda j, i: (i, 0)),
            pl.BlockSpec((tm, tn), lambda j, i: (i, j)),
            pl.BlockSpec((tm, tn), lambda j, i: (i, nj + j)),
            pl.BlockSpec((K, tn), lambda j, i: (0, j)),
            pl.BlockSpec((K, tn), lambda j, i: (0, j)),
        ],
        out_specs=pl.BlockSpec((tm, tn), lambda j, i: (i, j)),
        out_shape=jax.ShapeDtypeStruct((T, N), BF16),
        scratch_shapes=[pltpu.VMEM((K, tn), BF16), pltpu.VMEM((K, tn), BF16)],
        compiler_params=_cparams(("arbitrary", "arbitrary")),
        name=name,
    )(oa1, oa2, ob, z_gate, z_gate, wa, wb)


def _outproj_router_kernel(m_ref, h1_ref, h2_ref, w_hbm, nw_ref, wr_ref, br_ref, ho_ref, xn_ref, route_ref, cnt_ref,
                           wb_ref, tri_ref, carry_ref, stage_ref, sem, *, tm, nfirst):
    i = pl.program_id(0)

    @pl.when(i == 0)
    def _():
        _load_weight_bf16(w_hbm, wb_ref, stage_ref, sem)
        rr = lax.broadcasted_iota(jnp.int32, (tm, tm), 0)
        cc = lax.broadcasted_iota(jnp.int32, (tm, tm), 1)
        tri_ref[...] = (rr > cc).astype(BF16)
        carry_ref[...] = jnp.zeros_like(carry_ref)

    x = _pick(i, nfirst, h1_ref, h2_ref) + _dot(m_ref[...], wb_ref[...])
    ho_ref[...] = x
    ms = jnp.mean(x * x, axis=-1, keepdims=True)
    xn = x * lax.rsqrt(ms + EPS) * nw_ref[...]
    xn_ref[...] = _pack_bf16_pairs(xn)
    logits = _dot(xn.astype(BF16), wr_ref[...].astype(BF16)) + br_ref[...]
    lane = lax.broadcasted_iota(jnp.int32, (tm, LANES), 1)
    lanef = lane.astype(F32)
    cur = logits
    vals, hots, eids = [], [], []
    for _ in range(TOP_K):
        m = jnp.max(cur, axis=-1, keepdims=True)
        idx = jnp.min(jnp.where(cur == m, lanef, float(LANES)), axis=-1, keepdims=True)
        hot = lanef == idx
        vals.append(m)
        hots.append(hot)
        eids.append(idx)
        cur = jnp.where(hot, -jnp.inf, cur)
    es = [jnp.exp(v - vals[0]) for v in vals]
    den = es[0] + es[1] + es[2] + es[3]
    multi = jnp.zeros((tm, LANES), F32)
    for hot in hots:
        multi = multi + hot.astype(F32)
    base = carry_ref[0:1, :] + _dot(tri_ref[...], multi.astype(BF16))
    route = jnp.zeros((tm, LANES), F32)
    for k in range(TOP_K):
        e_k = eids[k]
        r_k = jnp.sum(jnp.where(hots[k], base, 0.0), axis=-1, keepdims=True)
        w_k = es[k] / den
        route = route + jnp.where(lane == k, e_k, 0.0) + jnp.where(lane == TOP_K + k, r_k, 0.0) \
            + jnp.where(lane == 2 * TOP_K + k, w_k, 0.0)
    route_ref[...] = route
    carry = carry_ref[0:1, :] + jnp.sum(multi, axis=0, keepdims=True)
    carry_ref[...] = jnp.broadcast_to(carry, carry_ref.shape)
    cnt_ref[...] = jnp.broadcast_to(carry, cnt_ref.shape)


def out_proj_router(merged, h1, h2, w_out, norm_w, w_router, b_router, tm=512, name="out_proj_router"):
    T, K = merged.shape
    D = w_out.shape[1]
    E = w_router.shape[1]
    na = h1.shape[0] // tm
    wr = jnp.pad(w_router, ((0, 0), (0, LANES - E)))
    br = jnp.pad(b_router.reshape(1, E), ((0, 0), (0, LANES - E)), constant_values=NEG_BIG)
    kern = functools.partial(_outproj_router_kernel, tm=tm, nfirst=na)
    return pl.pallas_call(
        kern,
        grid=(T // tm,),
        in_specs=[
            pl.BlockSpec((tm, K), lambda i: (i, 0)),
            *_cat_specs((tm, D), na),
            pl.BlockSpec(memory_space=pl.ANY),
            pl.BlockSpec((1, D), lambda i: (0, 0)),
            pl.BlockSpec((D, LANES), lambda i: (0, 0)),
            pl.BlockSpec((1, LANES), lambda i: (0, 0)),
        ],
        out_specs=[
            pl.BlockSpec((tm, D), lambda i: (i, 0)),
            pl.BlockSpec((tm, D // 2), lambda i: (i, 0)),
            pl.BlockSpec((tm, LANES), lambda i: (i, 0)),
            pl.BlockSpec((8, LANES), lambda i: (0, 0)),
        ],
        out_shape=[jax.ShapeDtypeStruct((T, D), F32),
                   jax.ShapeDtypeStruct((T, D // 2), jnp.uint32),
                   jax.ShapeDtypeStruct((T, LANES), F32),
                   jax.ShapeDtypeStruct((8, LANES), F32)],
        scratch_shapes=[pltpu.VMEM((K, D), BF16), pltpu.VMEM((tm, tm), BF16), pltpu.VMEM((8, LANES), F32),
                        pltpu.VMEM((2, WEIGHT_STAGE_ROWS, D), F32), pltpu.SemaphoreType.DMA((2,))],
        compiler_params=_cparams(("arbitrary",)),
        name=name,
    )(merged, h1, h2, w_out, norm_w.reshape(1, D), wr, br)


ROW_UNROLL = 8


def _pack_bf16_pairs(x):
    h = x.shape[1] // 2
    lo = lax.bitcast_convert_type(x[:, :h].astype(BF16).astype(F32), jnp.uint32)
    hi = lax.bitcast_convert_type(x[:, h:].astype(BF16).astype(F32), jnp.uint32)
    return (hi & jnp.uint32(0xFFFF0000)) | (lo >> 16)


def _unpack_bf16_pairs(w):
    lo = lax.bitcast_convert_type(w << 16, F32).astype(BF16)
    hi = lax.bitcast_convert_type(w & jnp.uint32(0xFFFF0000), F32).astype(BF16)
    return jnp.concatenate([lo, hi], axis=1)


def _dispatch_kernel(dest_ref, zrow_ref, x_ref, buf_ref, zero_ref, xs_ref, sem, zsem, *, tm, n_tail):
    i = pl.program_id(0)
    E = (zrow_ref.shape[0] - 1) // 2
    ZR = zero_ref.shape[0]
    R = buf_ref.shape[0]

    @pl.when(i == 0)
    def _():
        zero_ref[...] = jnp.zeros_like(zero_ref)

        def zero_copy(row, n):
            return pltpu.make_async_copy(zero_ref.at[pl.ds(0, n)], buf_ref.at[pl.ds(pl.multiple_of(row, 8), n)], zsem)

        def zero_rows(act):
            for e in range(E):
                off = zrow_ref[e]
                n = ZR
                while n >= 8:
                    @pl.when((zrow_ref[E + e] & n) != 0)
                    def _():
                        act(zero_copy(off, n))
                    off = off + (zrow_ref[E + e] & n)
                    n //= 2
            for j in range(n_tail):
                row = zrow_ref[2 * E] + j * ZR

                @pl.when(row < R)
                def _():
                    act(zero_copy(row, ZR))

        zero_rows(lambda c: c.start())
        zero_rows(lambda c: c.wait())

    par = i % 2
    xs_ref[par] = x_ref[...]

    def start(rb, c):
        for j in range(ROW_UNROLL):
            r = rb * ROW_UNROLL + j
            for k in range(TOP_K):
                d = dest_ref[(i * tm + r) * TOP_K + k]
                pltpu.make_async_copy(xs_ref.at[par, pl.ds(r, 1)], buf_ref.at[pl.ds(d, 1)],
                                      sem.at[par]).start(priority=k % 2)
        return c

    lax.fori_loop(0, tm // ROW_UNROLL, start, 0)

    def wait_step(s):
        for k in range(TOP_K):
            pltpu.make_async_copy(xs_ref.at[s], buf_ref.at[pl.ds(0, tm)], sem.at[s]).wait()

    @pl.when(i > 0)
    def _():
        wait_step(1 - par)

    @pl.when(i == pl.num_programs(0) - 1)
    def _():
        wait_step(par)


def moe_dispatch(xn, dest_flat, zrows, n_rows, tm=256, name="moe_dispatch"):
    T, H = xn.shape
    n_tail = (n_rows - T * TOP_K) // MOE_TM
    kern = functools.partial(_dispatch_kernel, tm=tm, n_tail=n_tail)
    return pl.pallas_call(
        kern,
        grid_spec=pltpu.PrefetchScalarGridSpec(
            num_scalar_prefetch=2,
            grid=(T // tm,),
            in_specs=[pl.BlockSpec((tm, H), lambda i, d, z: (i, 0))],
            out_specs=pl.BlockSpec(memory_space=pl.ANY),
            scratch_shapes=[pltpu.VMEM((MOE_TM, H), jnp.uint32), pltpu.VMEM((2, tm, H), jnp.uint32),
                            pltpu.SemaphoreType.DMA((2,)), pltpu.SemaphoreType.DMA(())],
        ),
        out_shape=jax.ShapeDtypeStruct((n_rows, H), jnp.uint32),
        compiler_params=_cparams(("arbitrary",)),
        name=name,
    )(dest_flat, zrows, xn)


def _experts_kernel(ue_ref, r0_ref, nt_ref, na_ref, x_hbm, wg_ref, wu_ref, bg_ref, bu_ref, wd_ref, bd_ref,
                    o_hbm, xb_ref, acc_ref, zero_ref, sem_in, sem_out, zsem, *, TM, UT, NF, n_tail):
    TPS = -(-UT // NF)
    u = pl.program_id(0)
    f = pl.program_id(1)
    na = na_ref[0]
    nt = nt_ref[u]
    r0 = r0_ref[u]
    slot = u % 2
    un = jnp.minimum(u + 1, na - 1)

    def prefetch(j):
        return jnp.logical_and(jnp.logical_and(u + 1 < na, j < nt_ref[un]), f == j // TPS)

    def x_copy(base, i, s):
        rows = pl.ds(pl.multiple_of(base + i * TM, TM), TM)
        return pltpu.make_async_copy(x_hbm.at[rows], xb_ref.at[s, i], sem_in.at[i])

    def o_copy(base, i):
        rows = pl.ds(pl.multiple_of(base + i * TM, TM), TM)
        return pltpu.make_async_copy(acc_ref.at[i], o_hbm.at[rows], sem_out.at[i])

    def wait_outputs(unit):
        for i in range(UT):
            @pl.when(i < nt_ref[unit])
            def _():
                o_copy(r0_ref[unit], i).wait()

    @pl.when(u < na)
    def _():
        @pl.when(jnp.logical_and(f == 0, u == 0))
        def _():
            for act in (lambda c: c.start(), lambda c: c.wait()):
                for i in range(UT):
                    @pl.when(i < nt)
                    def _():
                        act(x_copy(r0, i, 0))

        for j in range(UT):
            @pl.when(prefetch(j))
            def _():
                x_copy(r0_ref[un], j, 1 - slot).start()

        up = jnp.maximum(u - 1, 0)

        def wait_prev_output(j):
            @pl.when(jnp.logical_and(u > 0, j < nt_ref[up]))
            def _():
                o_copy(r0_ref[up], j).wait()

        bg = bg_ref[...]
        bu = bu_ref[...]

        def tile_out(xt):
            g = _dot(xt, wg_ref[...].astype(BF16)) + bg
            up = _dot(xt, wu_ref[...].astype(BF16)) + bu
            gate = jnp.minimum(g, SWIGLU_LIMIT)
            lin = jnp.clip(up, -SWIGLU_LIMIT, SWIGLU_LIMIT)
            act = (gate * jax.nn.sigmoid(SWIGLU_ALPHA * gate) * (lin + 1.0)).astype(BF16)
            return _dot(act, wd_ref[...].astype(BF16))

        D = acc_ref.shape[2]

        def update(i, n, first):
            tiles = pl.ds(i, n)
            if first:
                for t in range(n):
                    wait_prev_output(i + t)
            out = tile_out(_unpack_bf16_pairs(xb_ref[slot, tiles].reshape(n * TM, D // 2)))
            if first:
                out = out + bd_ref[...]
            else:
                out = out + acc_ref[tiles].reshape(n * TM, D)
            acc_ref[tiles] = out.reshape(n, TM, D)

            @pl.when(f == NF - 1)
            def _():
                for t in range(n):
                    o_copy(r0, i + t).start()

        def all_tiles(first):
            G = MOE_GROUP

            def body(j, c):
                update(pl.multiple_of(G * j, G), G, first)
                return c
            lax.fori_loop(0, nt // G, body, 0)
            done = (nt // G) * G
            n = G // 2
            while n >= 1:
                @pl.when((nt & n) != 0)
                def _():
                    update(pl.multiple_of(done, n), n, first)
                done = done + (nt & n)
                n //= 2

        @pl.when(f == 0)
        def _():
            all_tiles(True)
            for j in range(UT):
                @pl.when(j >= nt)
                def _():
                    wait_prev_output(j)

        @pl.when(f > 0)
        def _():
            all_tiles(False)

        for j in range(UT):
            @pl.when(prefetch(j))
            def _():
                x_copy(r0_ref[un], j, 1 - slot).wait()

        @pl.when(jnp.logical_and(f == NF - 1, u == na - 1))
        def _():
            zero_ref[...] = jnp.zeros_like(zero_ref)
            for act in (lambda c: c.start(), lambda c: c.wait()):
                for j in range(n_tail):
                    row = na_ref[1] + j * TM

                    @pl.when(row < o_hbm.shape[0])
                    def _():
                        act(pltpu.make_async_copy(zero_ref, o_hbm.at[pl.ds(pl.multiple_of(row, TM), TM)], zsem))
            wait_outputs(u)


def moe_experts(x_buf, plan, n_assign, w_gate_up, b_gate_up, w_down, b_down, name="moe_experts"):
    R, H = x_buf.shape
    D = 2 * H
    n_tail = (R - n_assign) // MOE_TM
    E, _, F2 = w_gate_up.shape
    DF = F2 // 2
    TM, UT, TF = MOE_TM, MOE_UNIT_TILES, MOE_TF
    NF = DF // TF
    ue, r0, nt, na = plan
    U = ue.shape[0]

    def fsel(u, f, na_ref):
        return jnp.where(u < na_ref[0], f, NF - 1)

    kern = functools.partial(_experts_kernel, TM=TM, UT=UT, NF=NF, n_tail=n_tail)
    return pl.pallas_call(
        kern,
        grid_spec=pltpu.PrefetchScalarGridSpec(
            num_scalar_prefetch=4,
            grid=(U, NF),
            in_specs=[
                pl.BlockSpec(memory_space=pl.ANY),
                pl.BlockSpec((None, D, TF), lambda u, f, ue, r0, nt, na: (ue[u], 0, fsel(u, f, na))),
                pl.BlockSpec((None, D, TF), lambda u, f, ue, r0, nt, na: (ue[u], 0, NF + fsel(u, f, na))),
                pl.BlockSpec((None, 1, TF), lambda u, f, ue, r0, nt, na: (ue[u], 0, fsel(u, f, na))),
                pl.BlockSpec((None, 1, TF), lambda u, f, ue, r0, nt, na: (ue[u], 0, NF + fsel(u, f, na))),
                pl.BlockSpec((None, TF, D), lambda u, f, ue, r0, nt, na: (ue[u], fsel(u, f, na), 0)),
                pl.BlockSpec((None, 1, D), lambda u, f, ue, r0, nt, na: (ue[u], 0, 0)),
            ],
            out_specs=pl.BlockSpec(memory_space=pl.ANY),
            scratch_shapes=[
                pltpu.VMEM((2, UT, TM, H), jnp.uint32),
                pltpu.VMEM((UT, TM, D), F32),
                pltpu.VMEM((TM, D), F32),
                pltpu.SemaphoreType.DMA((UT,)),
                pltpu.SemaphoreType.DMA((UT,)),
                pltpu.SemaphoreType.DMA(()),
            ],
        ),
        out_shape=jax.ShapeDtypeStruct((R, D), F32),
        compiler_params=_cparams(("arbitrary", "arbitrary"), VMEM_LIMIT_EXPERTS_BYTES),
        name=name,
    )(ue, r0, nt, na, x_buf, w_gate_up, w_gate_up, b_gate_up.reshape(E, 1, F2), b_gate_up.reshape(E, 1, F2),
      w_down, b_down.reshape(E, 1, D))


def moe_plan(counts, n_tokens):
    TM, UT = MOE_TM, MOE_UNIT_TILES
    E = counts.shape[0]
    nt = (counts + TM - 1) // TM
    padded = nt * TM
    pad_start = jnp.cumsum(padded) - padded
    nu = (nt + UT - 1) // UT
    cu = jnp.cumsum(nu)
    n_units = cu[-1]
    U = E + (n_tokens * TOP_K) // (TM * UT)
    u = jnp.arange(U, dtype=jnp.int32)
    uc = jnp.minimum(u, n_units - 1)
    ue = jnp.clip(jnp.searchsorted(cu, uc, side='right'), 0, E - 1).astype(jnp.int32)
    j = uc - (cu[ue] - nu[ue])
    r0 = (pad_start[ue] + j * (UT * TM)).astype(jnp.int32)
    ntl = jnp.where(u < n_units, jnp.clip(nt[ue] - j * UT, 0, UT), 0).astype(jnp.int32)
    zstart = (pad_start + counts) // 8 * 8
    pad_end = pad_start + padded
    zrows = jnp.concatenate([zstart, pad_end - zstart, pad_end[-1:]]).astype(jnp.int32)
    meta = jnp.stack([n_units, pad_end[-1]]).astype(jnp.int32)
    return pad_start, zrows, (ue, r0, ntl, meta)


def _combine_kernel(dest_ref, h_ref, route_ref, nw_ref, o_hbm, h2_ref, xn_ref, g_ref, sem, *, tm):
    i = pl.program_id(0)
    par = i % 2

    def gather_tile(t, slot):
        def start(rb, c):
            for j in range(ROW_UNROLL):
                r = rb * ROW_UNROLL + j
                for k in range(TOP_K):
                    d = dest_ref[(t * tm + r) * TOP_K + k]
                    pltpu.make_async_copy(o_hbm.at[pl.ds(d, 1)], g_ref.at[slot, k, pl.ds(r, 1)],
                                          sem.at[slot]).start(priority=k % 2)
            return c

        lax.fori_loop(0, tm // ROW_UNROLL, start, 0)

    @pl.when(i == 0)
    def _():
        gather_tile(i, par)

    @pl.when(i + 1 < pl.num_programs(0))
    def _():
        gather_tile(i + 1, 1 - par)

    for k in range(TOP_K):
        pltpu.make_async_copy(o_hbm.at[pl.ds(0, tm)], g_ref.at[par, k], sem.at[par]).wait()

    route = route_ref[...]
    y = g_ref[par, 0] * route[:, 2 * TOP_K:2 * TOP_K + 1]
    for k in range(1, TOP_K):
        y = y + g_ref[par, k] * route[:, 2 * TOP_K + k:2 * TOP_K + k + 1]
    h2 = h_ref[...] + y
    h2_ref[...] = h2
    ms = jnp.mean(h2 * h2, axis=-1, keepdims=True)
    xn_ref[...] = (h2 * lax.rsqrt(ms + EPS) * nw_ref[...]).astype(xn_ref.dtype)


def moe_combine(h, route, dest_flat, out_buf, norm_w, tm=256, name="moe_combine"):
    T, D = h.shape
    kern = functools.partial(_combine_kernel, tm=tm)
    return pl.pallas_call(
        kern,
        grid_spec=pltpu.PrefetchScalarGridSpec(
            num_scalar_prefetch=1,
            grid=(T // tm,),
            in_specs=[pl.BlockSpec((tm, D), lambda i, d: (i, 0)),
                      pl.BlockSpec((tm, LANES), lambda i, d: (i, 0)),
                      pl.BlockSpec((1, D), lambda i, d: (0, 0)),
                      pl.BlockSpec(memory_space=pl.ANY)],
            out_specs=[pl.BlockSpec((tm, D), lambda i, d: (i, 0)),
                       pl.BlockSpec((tm, D), lambda i, d: (i, 0))],
            scratch_shapes=[pltpu.VMEM((2, TOP_K, tm, D), F32), pltpu.SemaphoreType.DMA((2,))],
        ),
        out_shape=[jax.ShapeDtypeStruct((T, D), F32), jax.ShapeDtypeStruct((T, D), BF16)],
        compiler_params=_cparams(("arbitrary",)),
        name=name,
    )(dest_flat, h, route, norm_w.reshape(1, D), out_buf)


def _ple_kernel(xn_ref, p1_ref, p2_ref, h_ref, wg_hbm, wp_hbm, nf_ref, oa_ref, ob_ref, wgb_ref, wpb_ref, stage_ref, sem,
                *, nfirst):
    i = pl.program_id(0)

    @pl.when(i == 0)
    def _():
        _load_weight_bf16(wg_hbm, wgb_ref, stage_ref, sem)
        _load_weight_bf16(wp_hbm, wpb_ref, stage_ref, sem)

    gate = jax.nn.sigmoid(_dot(xn_ref[...], wgb_ref[...]))
    p = _pick(i, nfirst, p1_ref, p2_ref).astype(BF16)
    y = _rms(h_ref[...] + _dot(p, wpb_ref[...]) * gate, nf_ref[...])

    @pl.when(i < nfirst)
    def _():
        oa_ref[...] = y

    @pl.when(i >= nfirst)
    def _():
        ob_ref[...] = y


def ple_final(xn, p1, p2, h, w_gate, w_ple, norm_w, tm=512, name="ple_final"):
    T, D = h.shape
    Ta, P = p1.shape
    na = Ta // tm
    return pl.pallas_call(
        functools.partial(_ple_kernel, nfirst=na),
        grid=(T // tm,),
        in_specs=[
            pl.BlockSpec((tm, D), lambda i: (i, 0)),
            *_cat_specs((tm, P), na),
            pl.BlockSpec((tm, D), lambda i: (i, 0)),
            pl.BlockSpec(memory_space=pl.ANY),
            pl.BlockSpec(memory_space=pl.ANY),
            pl.BlockSpec((1, D), lambda i: (0, 0)),
        ],
        out_specs=list(_cat_specs((tm, D), na)),
        out_shape=[jax.ShapeDtypeStruct((Ta, D), F32), jax.ShapeDtypeStruct((T - Ta, D), F32)],
        scratch_shapes=[pltpu.VMEM((D, D), BF16), pltpu.VMEM((P, D), BF16),
                        pltpu.VMEM((2, WEIGHT_STAGE_ROWS, D), F32), pltpu.SemaphoreType.DMA((2,))],
        compiler_params=_cparams(("arbitrary",)),
        name=name,
    )(xn, p1, p2, h, w_gate, w_ple, norm_w.reshape(1, D))


def kernel(x_prompt, x_sample, p_prompt, p_sample, state_hgrn, state_ssm_re, state_ssm_im, norm_mix, w_in, hg_lb, hg_gnorm, w_branch_a, ssm_a_re, ssm_a_im, ssm_log_dt, ssm_b_re, ssm_b_im, ssm_c_re, ssm_c_im, ssm_d, w_glu, w_branch_b, w_out, norm_moe, w_router, b_router, w_gate_up, b_gate_up, w_down, b_down, norm_ple, w_ple, w_ple_gate, norm_final):
    BP, LP, D = x_prompt.shape
    BS, LS, _ = x_sample.shape
    depth = w_in.shape[0]
    assert depth == 1
    TP, TS = BP * LP, BS * LS
    T = TP + TS
    G, P = ssm_a_re.shape[1:]
    W = SSM_GROUP * G

    xp = x_prompt.reshape(TP, D)
    xs = x_sample.reshape(TS, D)

    xn = rmsnorm_cat(xp, xs, norm_mix[0], BF16, name="norm_mix")
    tm = 1024
    w_in0 = w_in[0]
    z_hg = project(xn, w_in0, 0, T // tm, 0, 4, tm, 1024, "proj_hgrn")
    u_p = project(xn, w_in0, 0, TP // tm, 4, 1, tm, 1024, "proj_u_prompt")
    u_s = project(xn, w_in0, TP // tm, TS // tm, 4, 1, tm, 1024, "proj_u_sample")
    z_gate = project(xn, w_in0, 0, T // tm, 5, 4, tm, 1024, "proj_gates", out_dtype=BF16)

    lb = jax.nn.softmax(hg_lb.astype(F32), axis=0)[0]
    og_p, hg_p = hgrn2(z_hg, lb, hg_gnorm[0], None, row0=0, B=BP, L=LP, C=HG_CHUNK, nb=1, tok_step=256,
                       name="hgrn_prompt")
    og_s, hg_s = hgrn2(z_hg, lb, hg_gnorm[0], state_hgrn[0], row0=TP, B=BS, L=LS, C=LS, nb=8, tok_step=LS,
                       name="hgrn_sample")

    bd, cd, ab = s5_params(ssm_a_re[0], ssm_a_im[0], ssm_log_dt[0], ssm_b_re[0], ssm_b_im[0],
                           ssm_c_re[0], ssm_c_im[0])
    y_p, re_p, im_p = s5_prompt(u_p.reshape(BP, LP, W), bd, cd, ab, ssm_d[0], TC=256, name="s5_prompt")
    y_s, re_s, im_s = s5_sample(u_s, bd, cd, ab, ssm_d[0], state_ssm_re[0].reshape(BS, G * P),
                                state_ssm_im[0].reshape(BS, G * P), B=BS, L=LS, name="s5_sample")

    glu = gelu_glu(y_p.reshape(TP, W), y_s, w_glu[0])
    merged = gated_merge(og_p, og_s, glu, z_gate, w_branch_a[0], w_branch_b[0])

    h1, xn2, route, cnt = out_proj_router(merged, xp, xs, w_out[0], norm_moe[0], w_router[0], b_router[0])
    e_idx = route[:, 0:TOP_K].astype(jnp.int32)
    rank = route[:, TOP_K:2 * TOP_K].astype(jnp.int32)
    counts = cnt[0, :N_EXPERTS].astype(jnp.int32)
    pad_start, zrows, plan = moe_plan(counts, T)
    onehot = e_idx[..., None] == jnp.arange(N_EXPERTS, dtype=jnp.int32)
    dest = (jnp.sum(jnp.where(onehot, pad_start, 0), axis=-1) + rank).astype(jnp.int32).reshape(T * TOP_K)
    n_rows = T * TOP_K + N_EXPERTS * MOE_TM
    x_buf = moe_dispatch(xn2, dest, zrows, n_rows)
    out_buf = moe_experts(x_buf, plan, T * TOP_K, w_gate_up[0], b_gate_up[0], w_down[0], b_down[0])
    h2, xn3 = moe_combine(h1, route, dest, out_buf, norm_ple[0])

    y_p2, y_s2 = ple_final(xn3, p_prompt[0].reshape(TP, -1), p_sample[0].reshape(TS, -1), h2, w_ple_gate[0],
                           w_ple[0], norm_final)

    y_prompt = y_p2.reshape(BP, LP, D)
    y_sample = y_s2.reshape(BS, LS, D)
    hs = (1, BP, HG_HEADS, HG_DK, HG_DV)
    return (y_prompt, y_sample,
            hg_p.reshape(hs), re_p.reshape(1, BP, G, P), im_p.reshape(1, BP, G, P),
            hg_s.reshape(1, BS, HG_HEADS, HG_DK, HG_DV), re_s.reshape(1, BS, G, P), im_s.reshape(1, BS, G, P))
```

```python
import functools

import jax
import jax.numpy as jnp
from jax import lax
from jax.experimental import pallas as pl
from jax.experimental.pallas import tpu as pltpu

F32 = jnp.float32
BF16 = jnp.bfloat16
EPS = 1e-6

HG_HEADS = 8
HG_DK = 128
HG_DV = 128
HG_CHUNK = 32
SSM_GROUP = 16
N_EXPERTS = 32
TOP_K = 4
SWIGLU_LIMIT = 7.0
SWIGLU_ALPHA = 1.702

LANES = 128
VMEM_LIMIT_BYTES = 56 * 1024 * 1024
VMEM_LIMIT_EXPERTS_BYTES = 61 * 1024 * 1024

MOE_TM = 128
MOE_UNIT_TILES = 12
MOE_GROUP = 8
MOE_TF = 512
NEG_BIG = -1e30


def _cparams(sem, vmem_limit=VMEM_LIMIT_BYTES):
    return pltpu.CompilerParams(dimension_semantics=sem, vmem_limit_bytes=vmem_limit)


def _dot(a, b):
    return jnp.dot(a, b, preferred_element_type=F32)


def _split2(a):
    hi = a.astype(BF16)
    lo = (a - hi.astype(F32)).astype(BF16)
    return hi, lo


def _dot_x3(a, b_split):
    a1, a2 = _split2(a)
    b1, b2 = b_split
    return _dot(a1, b1) + (_dot(a1, b2) + _dot(a2, b1))


WEIGHT_STAGE_ROWS = 256


def _load_weight_bf16(w_hbm, wb_ref, stage_ref, sem):
    K = w_hbm.shape[0]
    rows = min(stage_ref.shape[1], K)
    n = K // rows

    def copy(c):
        return pltpu.make_async_copy(w_hbm.at[pl.ds(c * rows, rows)], stage_ref.at[c % 2, pl.ds(0, rows)],
                                     sem.at[c % 2])

    copy(0).start()
    for c in range(n):
        if c + 1 < n:
            copy(c + 1).start()
        copy(c).wait()
        wb_ref[pl.ds(c * rows, rows), :] = stage_ref[c % 2, pl.ds(0, rows), :].astype(BF16)


def _cat_specs(shape, nfirst, row_axis=0, col_axis=None):
    def col(ids):
        return 0 if col_axis is None else ids[col_axis]

    first = pl.BlockSpec(shape, lambda *ids: (jnp.minimum(ids[row_axis], nfirst - 1), col(ids)))
    second = pl.BlockSpec(shape, lambda *ids: (jnp.maximum(ids[row_axis] - nfirst, 0), col(ids)))
    return first, second


def _pick(i, nfirst, a_ref, b_ref):
    return jnp.where(i < nfirst, a_ref[...], b_ref[...])


def _rms(x, w):
    ms = jnp.mean(x * x, axis=-1, keepdims=True)
    return x * lax.rsqrt(ms + EPS) * w


def _rms_in2_kernel(a_ref, b_ref, w_ref, o_ref, *, nfirst):
    x = _pick(pl.program_id(0), nfirst, a_ref, b_ref)
    o_ref[...] = _rms(x, w_ref[...]).astype(o_ref.dtype)


def rmsnorm_cat(xa, xb, w, out_dtype, tm=512, name="rmsnorm"):
    (Ta, D), Tb = xa.shape, xb.shape[0]
    na = Ta // tm
    return pl.pallas_call(
        functools.partial(_rms_in2_kernel, nfirst=na),
        grid=((Ta + Tb) // tm,),
        in_specs=[*_cat_specs((tm, D), na), pl.BlockSpec((1, D), lambda i: (0, 0))],
        out_specs=pl.BlockSpec((tm, D), lambda i: (i, 0)),
        out_shape=jax.ShapeDtypeStruct((Ta + Tb, D), out_dtype),
        compiler_params=_cparams(("arbitrary",)),
        name=name,
    )(xa, xb, w.reshape(1, D))


def _proj_kernel(x_ref, w_ref, o_ref, wb_ref):
    @pl.when(pl.program_id(1) == 0)
    def _():
        wb_ref[...] = w_ref[...].astype(BF16)

    o_ref[...] = _dot(x_ref[...], wb_ref[...]).astype(o_ref.dtype)


def project(xn, w, row_blk0, n_row_blks, col_blk0, n_col_blks, tm, tn, name, out_dtype=F32):
    K = xn.shape[1]
    return pl.pallas_call(
        _proj_kernel,
        grid=(n_col_blks, n_row_blks),
        in_specs=[
            pl.BlockSpec((tm, K), lambda j, i: (row_blk0 + i, 0)),
            pl.BlockSpec((K, tn), lambda j, i: (0, col_blk0 + j)),
        ],
        out_specs=pl.BlockSpec((tm, tn), lambda j, i: (i, j)),
        out_shape=jax.ShapeDtypeStruct((n_row_blks * tm, n_col_blks * tn), out_dtype),
        scratch_shapes=[pltpu.VMEM((K, tn), BF16)],
        compiler_params=_cparams(("arbitrary", "arbitrary")),
        name=name,
    )(xn, w)


HGRN_CHUNKS_PER_TRIP = 4


def _hgrn_kernel(*refs, C, nchunk, nb, tok_step, has_s0, state_t):
    if has_s0:
        q_ref, f_ref, v_ref, g_ref, lb_ref, gn_ref, s0_ref, og_ref, so_ref, st_ref = refs
    else:
        q_ref, f_ref, v_ref, g_ref, lb_ref, gn_ref, og_ref, so_ref, st_ref = refs
        s0_ref = None
    H, DK, DV = HG_HEADS, HG_DK, HG_DV
    step = pl.program_id(1)

    @pl.when(step == 0)
    def _():
        for i in range(nb):
            for h in range(H):
                if has_s0:
                    st_ref[i, h] = s0_ref[i, h].T if state_t else s0_ref[i, h]
                else:
                    st_ref[i, h] = jnp.zeros(st_ref.shape[2:], F32)

    lbv = lb_ref[...]
    gnv = gn_ref[...]
    rr = lax.broadcasted_iota(jnp.int32, (C, C), 0)
    cc = lax.broadcasted_iota(jnp.int32, (C, C), 1)
    causal = rr >= cc
    tri = causal.astype(BF16)
    sls =[slice(h * DK, (h + 1) * DK) for h in range(H)]
    nt_dims = (((1,), (1,)), ((), ()))
    tn_dims = (((0,), (0,)), ((), ()))

    def tn(a, b):
        return lax.dot_general(a, b, tn_dims, preferred_element_type=F32)

    def prepare(r0):
        rows = pl.ds(r0, C)
        zq = q_ref[rows, :]
        zf = f_ref[rows, :]
        q = jax.nn.silu(zq)
        fe = lbv + (1.0 - lbv) * jax.nn.sigmoid(zf)
        k = 1.0 - fe
        gl = jnp.log(fe)
        g1 = gl.astype(BF16)
        r1 = gl - g1.astype(F32)
        g2 = r1.astype(BF16)
        g3 = (r1 - g2.astype(F32)).astype(BF16)
        b = _dot(tri, g1) + (_dot(tri, g2) + _dot(tri, g3))
        blast = b[C - 1:C, :]
        return dict(
            qt=(q * jnp.exp(b)).astype(BF16),
            kt=(k * jnp.exp(-b)).astype(BF16),
            kend=(k * jnp.exp(blast - b)).astype(BF16),
            eblast=jnp.exp(blast),
            vb=v_ref[rows, :].astype(BF16),
            gate=jax.nn.silu(g_ref[rows, :]),
        )

    def run(jobs):
        P = [prepare(r0) for _, r0 in jobs]
        atts = [[lax.dot_general(p["qt"][:, sl], p["kt"][:, sl], nt_dims, preferred_element_type=F32)
                 for sl in sls] for p in P]
        if state_t:
            upds = [[tn(p["vb"][:, sl], p["kend"][:, sl]) for sl in sls] for p in P]
            decs = [[p["eblast"][:, sl] for sl in sls] for p in P]
        else:
            upds = [[tn(p["kend"][:, sl], p["vb"][:, sl]) for sl in sls] for p in P]
            decs = [[jnp.broadcast_to(p["eblast"][:, sl], (8, DK)).T[:, 0:1] for sl in sls] for p in P]
        state = {}
        inters = []
        for j, (i, _) in enumerate(jobs):
            row = []
            for h, sl in enumerate(sls):
                s = state[(i, h)] if (i, h) in state else st_ref[i, h]
                qth = P[j]["qt"][:, sl]
                if state_t:
                    row.append(lax.dot_general(qth, s.astype(BF16), nt_dims, preferred_element_type=F32))
                else:
                    row.append(_dot(qth, s.astype(BF16)))
                state[(i, h)] = s * decs[j][h] + upds[j][h]
            inters.append(row)
        for (i, h), s in state.items():
            st_ref[i, h] = s
        outs = []
        for j, p in enumerate(P):
            cols = []
            for h, sl in enumerate(sls):
                att = jnp.where(causal, atts[j][h], 0.0).astype(BF16)
                o = _dot(att, p["vb"][:, sl]) + inters[j][h]
                ms = jnp.mean(o * o, axis=-1, keepdims=True)
                cols.append(o * lax.rsqrt(ms + EPS) * gnv * p["gate"][:, sl])
            outs.append(jnp.concatenate(cols, axis=1))
        return outs

    if nchunk == 1:
        outs = run([(i, i * tok_step) for i in range(nb)])
        og_ref[...] = jnp.concatenate(outs, axis=0).astype(og_ref.dtype)
    else:
        assert nb == 1 and nchunk % HGRN_CHUNKS_PER_TRIP == 0

        def body(t, carry):
            r0s = [pl.multiple_of((HGRN_CHUNKS_PER_TRIP * t + j) * C, C) for j in range(HGRN_CHUNKS_PER_TRIP)]
            outs = run([(0, r0) for r0 in r0s])
            for r0, o in zip(r0s, outs):
                og_ref[pl.ds(r0, C), :] = o.astype(og_ref.dtype)
            return carry

        lax.fori_loop(0, nchunk // HGRN_CHUNKS_PER_TRIP, body, 0)

    @pl.when(step == pl.num_programs(1) - 1)
    def _():
        for i in range(nb):
            for h in range(H):
                so_ref[i, h] = st_ref[i, h].T if state_t else st_ref[i, h]


def hgrn2(z_hg, lb, gn, s0, *, row0, B, L, C, nb, tok_step, name):
    H, DK, DV = HG_HEADS, HG_DK, HG_DV
    W = H * DK
    steps = L // tok_step
    rows = nb * tok_step
    nchunk = tok_step // C
    blk0 = row0 // rows
    has_s0 = s0 is not None

    def zspec(col):
        return pl.BlockSpec((rows, W), lambda bb, s: (blk0 + bb * steps + s, col))

    in_specs = [zspec(0), zspec(1), zspec(2), zspec(3),
                pl.BlockSpec((1, W), lambda bb, s: (0, 0)),
                pl.BlockSpec((1, DV), lambda bb, s: (0, 0))]
    args = [z_hg, z_hg, z_hg, z_hg, lb.reshape(1, W), gn.reshape(1, DV)]
    if has_s0:
        in_specs.append(pl.BlockSpec((nb, H, DK, DV), lambda bb, s: (bb, 0, 0, 0)))
        args.append(s0)
    state_t = L > C
    kern = functools.partial(_hgrn_kernel, C=C, nchunk=nchunk, nb=nb, tok_step=tok_step, has_s0=has_s0,
                             state_t=state_t)
    return pl.pallas_call(
        kern,
        grid=(B // nb, steps),
        in_specs=in_specs,
        out_specs=[pl.BlockSpec((rows, W), lambda bb, s: (bb * steps + s, 0)),
                   pl.BlockSpec((nb, H, DK, DV), lambda bb, s: (bb, 0, 0, 0))],
        out_shape=[jax.ShapeDtypeStruct((B * L, W), BF16),
                   jax.ShapeDtypeStruct((B, H, DK, DV), F32)],
        scratch_shapes=[pltpu.VMEM((nb, H, DV, DK), F32)],
        compiler_params=_cparams(("arbitrary", "arbitrary")),
        name=name,
    )(*args)


def _s5_step(ar, ai, xr, xi, br, bi):
    return ar * xr - ai * xi + br, ar * xi + ai * xr + bi


S5_STEPS_PER_TRIP = 4
S5_GROUPS_PER_BLOCK = 16


def _s5_prompt_kernel(u_ref, bd_ref, cd_ref, a_ref, d_ref, y_ref, xr_ref, xi_ref, X_ref, st_ref, *, B, TC):
    NP = X_ref.shape[0] // 2
    assert 2 * B == 8
    tc = pl.program_id(1)

    def rows_of(b, s):
        return pl.ds(2 * b + s, TC, stride=8)

    @pl.when(tc == 0)
    def _():
        st_ref[...] = jnp.zeros_like(st_ref)

    bd = bd_ref[...].astype(BF16)
    for b in range(B):
        bu = _dot(u_ref[b].astype(BF16), bd)
        for c in range(4 * NP):
            X_ref[c // 2, rows_of(b, c % 2), :] = bu[:, c * LANES:(c + 1) * LANES]

    odd = lax.broadcasted_iota(jnp.int32, (8, LANES), 0) % 2 == 1

    def pair_const(row, p):
        lo = a_ref[row:row + 1, (2 * p) * LANES:(2 * p + 1) * LANES]
        hi = a_ref[row:row + 1, (2 * p + 1) * LANES:(2 * p + 2) * LANES]
        return jnp.where(odd, hi, lo)

    ars = [pair_const(0, p) for p in range(NP)]
    ais = [pair_const(1, p) for p in range(NP)]

    def body(i, carry):
        carry = list(carry)
        for j in range(S5_STEPS_PER_TRIP):
            rows = pl.ds(pl.multiple_of((i * S5_STEPS_PER_TRIP + j) * 8, 8), 8)
            for p in range(NP):
                xr, xi = _s5_step(ars[p], ais[p], carry[p][0], carry[p][1], X_ref[p, rows, :], X_ref[NP + p, rows, :])
                X_ref[p, rows, :] = xr
                X_ref[NP + p, rows, :] = xi
                carry[p] = (xr, xi)
        return tuple(carry)

    init = tuple((st_ref[p], st_ref[NP + p]) for p in range(NP))
    fin = lax.fori_loop(0, TC // S5_STEPS_PER_TRIP, body, init)
    for p in range(NP):
        st_ref[p] = fin[p][0]
        st_ref[NP + p] = fin[p][1]

    cd = cd_ref[...].astype(BF16)
    dv = d_ref[...]
    for b in range(B):
        xb = jnp.concatenate([X_ref[c // 2, rows_of(b, c % 2), :] for c in range(4 * NP)], axis=1).astype(BF16)
        y_ref[b] = _dot(xb, cd) + dv * u_ref[b]

    @pl.when(tc == pl.num_programs(1) - 1)
    def _():
        for c in range(2 * NP):
            cols = slice(c * LANES, (c + 1) * LANES)
            xr_ref[:, cols] = st_ref[c // 2, pl.ds(c % 2, B, stride=2), :]
            xi_ref[:, cols] = st_ref[NP + c // 2, pl.ds(c % 2, B, stride=2), :]


def s5_prompt(u, bd, cd, ab, dskip, *, TC, name):
    B, L, W = u.shape
    GB, UB, SW = bd.shape
    half = SW // 2
    kern = functools.partial(_s5_prompt_kernel, B=B, TC=TC)
    return pl.pallas_call(
        kern,
        grid=(GB, L // TC),
        in_specs=[
            pl.BlockSpec((B, TC, UB), lambda g, t: (0, t, g)),
            pl.BlockSpec((None, UB, SW), lambda g, t: (g, 0, 0)),
            pl.BlockSpec((None, SW, UB), lambda g, t: (g, 0, 0)),
            pl.BlockSpec((None, 2, half), lambda g, t: (g, 0, 0)),
            pl.BlockSpec((1, UB), lambda g, t: (0, g)),
        ],
        out_specs=[
            pl.BlockSpec((B, TC, UB), lambda g, t: (0, t, g)),
            pl.BlockSpec((B, half), lambda g, t: (0, g)),
            pl.BlockSpec((B, half), lambda g, t: (0, g)),
        ],
        out_shape=[jax.ShapeDtypeStruct((B, L, W), F32),
                   jax.ShapeDtypeStruct((B, GB * half), F32),
                   jax.ShapeDtypeStruct((B, GB * half), F32)],
        scratch_shapes=[pltpu.VMEM((SW // LANES // 2, 8 * TC, LANES), F32),
                        pltpu.VMEM((SW // LANES // 2, 8, LANES), F32)],
        compiler_params=_cparams(("arbitrary", "arbitrary")),
        name=name,
    )(u, bd, cd, ab, dskip.reshape(1, W))


def _s5_sample_kernel(u_ref, bd_ref, cd_ref, a_ref, d_ref, x0r_ref, x0i_ref, y_ref, xr_ref, xi_ref, X_ref, *, B, L):
    NC = X_ref.shape[0]
    NH = NC // 2
    u = u_ref[...]
    bu = _dot_x3(u, _split2(bd_ref[...]))
    for c in range(NC):
        X_ref[c] = bu[:, c * LANES:(c + 1) * LANES]
    for c in range(NH):
        cols = slice(c * LANES, (c + 1) * LANES)
        ar = a_ref[0:1, cols]
        ai = a_ref[1:2, cols]
        xr = x0r_ref[:, cols]
        xi = x0i_ref[:, cols]
        for t in range(L):
            rows = pl.ds(t, B, stride=L)
            xr, xi = _s5_step(ar, ai, xr, xi, X_ref[c, rows, :], X_ref[NH + c, rows, :])
            X_ref[c, rows, :] = xr
            X_ref[NH + c, rows, :] = xi
        xr_ref[:, cols] = xr
        xi_ref[:, cols] = xi
    xs = jnp.concatenate([X_ref[c] for c in range(NC)], axis=1).astype(BF16)
    y_ref[...] = _dot(xs, cd_ref[...].astype(BF16)) + d_ref[...] * u


def s5_sample(u, bd, cd, ab, dskip, x0r, x0i, *, B, L, name):
    T, W = u.shape
    GB, UB, SW = bd.shape
    half = SW // 2
    kern = functools.partial(_s5_sample_kernel, B=B, L=L)
    return pl.pallas_call(
        kern,
        grid=(GB,),
        in_specs=[
            pl.BlockSpec((T, UB), lambda g: (0, g)),
            pl.BlockSpec((None, UB, SW), lambda g: (g, 0, 0)),
            pl.BlockSpec((None, SW, UB), lambda g: (g, 0, 0)),
            pl.BlockSpec((None, 2, half), lambda g: (g, 0, 0)),
            pl.BlockSpec((1, UB), lambda g: (0, g)),
            pl.BlockSpec((B, half), lambda g: (0, g)),
            pl.BlockSpec((B, half), lambda g: (0, g)),
        ],
        out_specs=[
            pl.BlockSpec((T, UB), lambda g: (0, g)),
            pl.BlockSpec((B, half), lambda g: (0, g)),
            pl.BlockSpec((B, half), lambda g: (0, g)),
        ],
        out_shape=[jax.ShapeDtypeStruct((T, W), F32),
                   jax.ShapeDtypeStruct((B, GB * half), F32),
                   jax.ShapeDtypeStruct((B, GB * half), F32)],
        scratch_shapes=[pltpu.VMEM((SW // LANES, T, LANES), F32)],
        compiler_params=_cparams(("arbitrary",)),
        name=name,
    )(u, bd, cd, ab, dskip.reshape(1, W), x0r, x0i)


def s5_params(a_re, a_im, log_dt, b_re, b_im, c_re, c_im):
    G, P = a_re.shape
    gpb = S5_GROUPS_PER_BLOCK
    GB = G // gpb
    A = lax.complex(a_re, a_im)
    dt = jnp.exp(log_dt)[:, None]
    A_bar = jnp.exp(A * dt)
    Bm = lax.complex(b_re, b_im)
    B_bar = ((A_bar - 1.0) / A)[..., None] * Bm
    eye = jnp.eye(gpb, dtype=F32)

    def in_map(m):
        m = m.reshape(GB, gpb, P, SSM_GROUP)
        return jnp.einsum('bgpc,gh->bgchp', m, eye).reshape(GB, gpb * SSM_GROUP, gpb * P)

    def out_map(m):
        m = m.reshape(GB, gpb, SSM_GROUP, P)
        return jnp.einsum('bgcp,gh->bgphc', m, eye).reshape(GB, gpb * P, gpb * SSM_GROUP)

    bd = jnp.concatenate([in_map(B_bar.real), in_map(B_bar.imag)], axis=2)
    cd = jnp.concatenate([out_map(c_re), out_map(-c_im)], axis=1)
    ab = jnp.stack([A_bar.real.reshape(GB, gpb * P), A_bar.imag.reshape(GB, gpb * P)], axis=1)
    return bd, cd, ab


def _glu_kernel(ya_ref, yb_ref, w_ref, o_ref, wb_ref, *, nfirst):
    i = pl.program_id(0)

    @pl.when(i == 0)
    def _():
        wb_ref[...] = w_ref[...].astype(BF16)

    zg = jax.nn.gelu(_pick(i, nfirst, ya_ref, yb_ref))
    o_ref[...] = (zg * jax.nn.sigmoid(_dot(zg.astype(BF16), wb_ref[...]))).astype(o_ref.dtype)


def gelu_glu(ya, yb, w, tm=512, name="gelu_glu"):
    (Ta, W), Tb = ya.shape, yb.shape[0]
    na = Ta // tm
    return pl.pallas_call(
        functools.partial(_glu_kernel, nfirst=na),
        grid=((Ta + Tb) // tm,),
        in_specs=[*_cat_specs((tm, W), na), pl.BlockSpec((W, W), lambda i: (0, 0))],
        out_specs=pl.BlockSpec((tm, W), lambda i: (i, 0)),
        out_shape=jax.ShapeDtypeStruct((Ta + Tb, W), BF16),
        scratch_shapes=[pltpu.VMEM((W, W), BF16)],
        compiler_params=_cparams(("arbitrary",)),
        name=name,
    )(ya, yb, w)


def _merge_kernel(oa1_ref, oa2_ref, ob_ref, ga_ref, gb_ref, wa_ref, wb_ref, o_ref, wab_ref, wbb_ref, *, nfirst):
    i = pl.program_id(1)

    @pl.when(i == 0)
    def _():
        wab_ref[...] = wa_ref[...].astype(BF16)
        wbb_ref[...] = wb_ref[...].astype(BF16)

    ya = _dot(_pick(i, nfirst, oa1_ref, oa2_ref), wab_ref[...])
    yb = _dot(ob_ref[...], wbb_ref[...])
    ga = jax.nn.sigmoid(ga_ref[...].astype(F32))
    gb = jax.nn.sigmoid(gb_ref[...].astype(F32))
    o_ref[...] = (ga * ya + gb * yb).astype(o_ref.dtype)


def gated_merge(oa1, oa2, ob, z_gate, wa, wb, tm=1024, tn=1024, name="gated_merge"):
    T, K = ob.shape
    N = wa.shape[1]
    nj = N // tn
    na = oa1.shape[0] // tm
    return pl.pallas_call(
        functools.partial(_merge_kernel, nfirst=na),
        grid=(nj, T // tm),
        in_specs=[
            *_cat_specs((tm, K), na, row_axis=1),
            pl.BlockSpec((tm, K), lambda j, i: (i, 0)),
            pl.BlockSpec((tm, tn), lambda j, i: (i, j)),
            pl.BlockSpec((tm, tn), lambda j, i: (i, nj + j)),
            pl.BlockSpec((K, tn), lambda j, i: (0, j)),
            pl.BlockSpec((K, tn), lambda j, i: (0, j)),
        ],
        out_specs=pl.BlockSpec((tm, tn), lambda j, i: (i, j)),
        out_shape=jax.ShapeDtypeStruct((T, N), BF16),
        scratch_shapes=[pltpu.VMEM((K, tn), BF16), pltpu.VMEM((K, tn), BF16)],
        compiler_params=_cparams(("arbitrary", "arbitrary")),
        name=name,
    )(oa1, oa2, ob, z_gate, z_gate, wa, wb)


def _outproj_router_kernel(m_ref, h1_ref, h2_ref, w_hbm, nw_ref, wr_ref, br_ref, ho_ref, xn_ref, route_ref, cnt_ref,
                           wb_ref, tri_ref, carry_ref, stage_ref, sem, *, tm, nfirst):
    i = pl.program_id(0)

    @pl.when(i == 0)
    def _():
        _load_weight_bf16(w_hbm, wb_ref, stage_ref, sem)
        rr = lax.broadcasted_iota(jnp.int32, (tm, tm), 0)
        cc = lax.broadcasted_iota(jnp.int32, (tm, tm), 1)
        tri_ref[...] = (rr > cc).astype(BF16)
        carry_ref[...] = jnp.zeros_like(carry_ref)

    x = _pick(i, nfirst, h1_ref, h2_ref) + _dot(m_ref[...], wb_ref[...])
    ho_ref[...] = x
    ms = jnp.mean(x * x, axis=-1, keepdims=True)
    xn = x * lax.rsqrt(ms + EPS) * nw_ref[...]
    xn_ref[...] = _pack_bf16_pairs(xn)
    logits = _dot(xn.astype(BF16), wr_ref[...].astype(BF16)) + br_ref[...]
    lane = lax.broadcasted_iota(jnp.int32, (tm, LANES), 1)
    lanef = lane.astype(F32)
    cur = logits
    vals, hots, eids = [], [], []
    for _ in range(TOP_K):
        m = jnp.max(cur, axis=-1, keepdims=True)
        idx = jnp.min(jnp.where(cur == m, lanef, float(LANES)), axis=-1, keepdims=True)
        hot = lanef == idx
        vals.append(m)
        hots.append(hot)
        eids.append(idx)
        cur = jnp.where(hot, -jnp.inf, cur)
    es = [jnp.exp(v - vals[0]) for v in vals]
    den = es[0] + es[1] + es[2] + es[3]
    multi = jnp.zeros((tm, LANES), F32)
    for hot in hots:
        multi = multi + hot.astype(F32)
    base = carry_ref[0:1, :] + _dot(tri_ref[...], multi.astype(BF16))
    route = jnp.zeros((tm, LANES), F32)
    for k in range(TOP_K):
        e_k = eids[k]
        r_k = jnp.sum(jnp.where(hots[k], base, 0.0), axis=-1, keepdims=True)
        w_k = es[k] / den
        route = route + jnp.where(lane == k, e_k, 0.0) + jnp.where(lane == TOP_K + k, r_k, 0.0) \
            + jnp.where(lane == 2 * TOP_K + k, w_k, 0.0)
    route_ref[...] = route
    carry = carry_ref[0:1, :] + jnp.sum(multi, axis=0, keepdims=True)
    carry_ref[...] = jnp.broadcast_to(carry, carry_ref.shape)
    cnt_ref[...] = jnp.broadcast_to(carry, cnt_ref.shape)


def out_proj_router(merged, h1, h2, w_out, norm_w, w_router, b_router, tm=512, name="out_proj_router"):
    T, K = merged.shape
    D = w_out.shape[1]
    E = w_router.shape[1]
    na = h1.shape[0] // tm
    wr = jnp.pad(w_router, ((0, 0), (0, LANES - E)))
    br = jnp.pad(b_router.reshape(1, E), ((0, 0), (0, LANES - E)), constant_values=NEG_BIG)
    kern = functools.partial(_outproj_router_kernel, tm=tm, nfirst=na)
    return pl.pallas_call(
        kern,
        grid=(T // tm,),
        in_specs=[
            pl.BlockSpec((tm, K), lambda i: (i, 0)),
            *_cat_specs((tm, D), na),
            pl.BlockSpec(memory_space=pl.ANY),
            pl.BlockSpec((1, D), lambda i: (0, 0)),
            pl.BlockSpec((D, LANES), lambda i: (0, 0)),
            pl.BlockSpec((1, LANES), lambda i: (0, 0)),
        ],
        out_specs=[
            pl.BlockSpec((tm, D), lambda i: (i, 0)),
            pl.BlockSpec((tm, D // 2), lambda i: (i, 0)),
            pl.BlockSpec((tm, LANES), lambda i: (i, 0)),
            pl.BlockSpec((8, LANES), lambda i: (0, 0)),
        ],
        out_shape=[jax.ShapeDtypeStruct((T, D), F32),
                   jax.ShapeDtypeStruct((T, D // 2), jnp.uint32),
                   jax.ShapeDtypeStruct((T, LANES), F32),
                   jax.ShapeDtypeStruct((8, LANES), F32)],
        scratch_shapes=[pltpu.VMEM((K, D), BF16), pltpu.VMEM((tm, tm), BF16), pltpu.VMEM((8, LANES), F32),
                        pltpu.VMEM((2, WEIGHT_STAGE_ROWS, D), F32), pltpu.SemaphoreType.DMA((2,))],
        compiler_params=_cparams(("arbitrary",)),
        name=name,
    )(merged, h1, h2, w_out, norm_w.reshape(1, D), wr, br)


ROW_UNROLL = 8


def _pack_bf16_pairs(x):
    h = x.shape[1] // 2
    lo = lax.bitcast_convert_type(x[:, :h].astype(BF16).astype(F32), jnp.uint32)
    hi = lax.bitcast_convert_type(x[:, h:].astype(BF16).astype(F32), jnp.uint32)
    return (hi & jnp.uint32(0xFFFF0000)) | (lo >> 16)


def _unpack_bf16_pairs(w):
    lo = lax.bitcast_convert_type(w << 16, F32).astype(BF16)
    hi = lax.bitcast_convert_type(w & jnp.uint32(0xFFFF0000), F32).astype(BF16)
    return jnp.concatenate([lo, hi], axis=1)


def _dispatch_kernel(dest_ref, zrow_ref, x_ref, buf_ref, zero_ref, xs_ref, sem, zsem, *, tm, n_tail):
    i = pl.program_id(0)
    E = (zrow_ref.shape[0] - 1) // 2
    ZR = zero_ref.shape[0]
    R = buf_ref.shape[0]

    @pl.when(i == 0)
    def _():
        zero_ref[...] = jnp.zeros_like(zero_ref)

        def zero_copy(row, n):
            return pltpu.make_async_copy(zero_ref.at[pl.ds(0, n)], buf_ref.at[pl.ds(pl.multiple_of(row, 8), n)], zsem)

        def zero_rows(act):
            for e in range(E):
                off = zrow_ref[e]
                n = ZR
                while n >= 8:
                    @pl.when((zrow_ref[E + e] & n) != 0)
                    def _():
                        act(zero_copy(off, n))
                    off = off + (zrow_ref[E + e] & n)
                    n //= 2
            for j in range(n_tail):
                row = zrow_ref[2 * E] + j * ZR

                @pl.when(row < R)
                def _():
                    act(zero_copy(row, ZR))

        zero_rows(lambda c: c.start())
        zero_rows(lambda c: c.wait())

    par = i % 2
    xs_ref[par] = x_ref[...]

    def start(rb, c):
        for j in range(ROW_UNROLL):
            r = rb * ROW_UNROLL + j
            for k in range(TOP_K):
                d = dest_ref[(i * tm + r) * TOP_K + k]
                pltpu.make_async_copy(xs_ref.at[par, pl.ds(r, 1)], buf_ref.at[pl.ds(d, 1)],
                                      sem.at[par]).start(priority=k % 2)
        return c

    lax.fori_loop(0, tm // ROW_UNROLL, start, 0)

    def wait_step(s):
        for k in range(TOP_K):
            pltpu.make_async_copy(xs_ref.at[s], buf_ref.at[pl.ds(0, tm)], sem.at[s]).wait()

    @pl.when(i > 0)
    def _():
        wait_step(1 - par)

    @pl.when(i == pl.num_programs(0) - 1)
    def _():
        wait_step(par)


def moe_dispatch(xn, dest_flat, zrows, n_rows, tm=256, name="moe_dispatch"):
    T, H = xn.shape
    n_tail = (n_rows - T * TOP_K) // MOE_TM
    kern = functools.partial(_dispatch_kernel, tm=tm, n_tail=n_tail)
    return pl.pallas_call(
        kern,
        grid_spec=pltpu.PrefetchScalarGridSpec(
            num_scalar_prefetch=2,
            grid=(T // tm,),
            in_specs=[pl.BlockSpec((tm, H), lambda i, d, z: (i, 0))],
            out_specs=pl.BlockSpec(memory_space=pl.ANY),
            scratch_shapes=[pltpu.VMEM((MOE_TM, H), jnp.uint32), pltpu.VMEM((2, tm, H), jnp.uint32),
                            pltpu.SemaphoreType.DMA((2,)), pltpu.SemaphoreType.DMA(())],
        ),
        out_shape=jax.ShapeDtypeStruct((n_rows, H), jnp.uint32),
        compiler_params=_cparams(("arbitrary",)),
        name=name,
    )(dest_flat, zrows, xn)


def _experts_kernel(ue_ref, r0_ref, nt_ref, na_ref, x_hbm, wg_ref, wu_ref, bg_ref, bu_ref, wd_ref, bd_ref,
                    o_hbm, xb_ref, acc_ref, zero_ref, sem_in, sem_out, zsem, *, TM, UT, NF, n_tail):
    TPS = -(-UT // NF)
    u = pl.program_id(0)
    f = pl.program_id(1)
    na = na_ref[0]
    nt = nt_ref[u]
    r0 = r0_ref[u]
    slot = u % 2
    un = jnp.minimum(u + 1, na - 1)

    def prefetch(j):
        return jnp.logical_and(jnp.logical_and(u + 1 < na, j < nt_ref[un]), f == j // TPS)

    def x_copy(base, i, s):
        rows = pl.ds(pl.multiple_of(base + i * TM, TM), TM)
        return pltpu.make_async_copy(x_hbm.at[rows], xb_ref.at[s, i], sem_in.at[i])

    def o_copy(base, i):
        rows = pl.ds(pl.multiple_of(base + i * TM, TM), TM)
        return pltpu.make_async_copy(acc_ref.at[i], o_hbm.at[rows], sem_out.at[i])

    def wait_outputs(unit):
        for i in range(UT):
            @pl.when(i < nt_ref[unit])
            def _():
                o_copy(r0_ref[unit], i).wait()

    @pl.when(u < na)
    def _():
        @pl.when(jnp.logical_and(f == 0, u == 0))
        def _():
            for act in (lambda c: c.start(), lambda c: c.wait()):
                for i in range(UT):
                    @pl.when(i < nt)
                    def _():
                        act(x_copy(r0, i, 0))

        for j in range(UT):
            @pl.when(prefetch(j))
            def _():
                x_copy(r0_ref[un], j, 1 - slot).start()

        up = jnp.maximum(u - 1, 0)

        def wait_prev_output(j):
            @pl.when(jnp.logical_and(u > 0, j < nt_ref[up]))
            def _():
                o_copy(r0_ref[up], j).wait()

        bg = bg_ref[...]
        bu = bu_ref[...]

        def tile_out(xt):
            g = _dot(xt, wg_ref[...].astype(BF16)) + bg
            up = _dot(xt, wu_ref[...].astype(BF16)) + bu
            gate = jnp.minimum(g, SWIGLU_LIMIT)
            lin = jnp.clip(up, -SWIGLU_LIMIT, SWIGLU_LIMIT)
            act = (gate * jax.nn.sigmoid(SWIGLU_ALPHA * gate) * (lin + 1.0)).astype(BF16)
            return _dot(act, wd_ref[...].astype(BF16))

        D = acc_ref.shape[2]

        def update(i, n, first):
            tiles = pl.ds(i, n)
            if first:
                for t in range(n):
                    wait_prev_output(i + t)
            out = tile_out(_unpack_bf16_pairs(xb_ref[slot, tiles].reshape(n * TM, D // 2)))
            if first:
                out = out + bd_ref[...]
            else:
                out = out + acc_ref[tiles].reshape(n * TM, D)
            acc_ref[tiles] = out.reshape(n, TM, D)

            @pl.when(f == NF - 1)
            def _():
                for t in range(n):
                    o_copy(r0, i + t).start()

        def all_tiles(first):
            G = MOE_GROUP

            def body(j, c):
                update(pl.multiple_of(G * j, G), G, first)
                return c
            lax.fori_loop(0, nt // G, body, 0)
            done = (nt // G) * G
            n = G // 2
            while n >= 1:
                @pl.when((nt & n) != 0)
                def _():
                    update(pl.multiple_of(done, n), n, first)
                done = done + (nt & n)
                n //= 2

        @pl.when(f == 0)
        def _():
            all_tiles(True)
            for j in range(UT):
                @pl.when(j >= nt)
                def _():
                    wait_prev_output(j)

        @pl.when(f > 0)
        def _():
            all_tiles(False)

        for j in range(UT):
            @pl.when(prefetch(j))
            def _():
                x_copy(r0_ref[un], j, 1 - slot).wait()

        @pl.when(jnp.logical_and(f == NF - 1, u == na - 1))
        def _():
            zero_ref[...] = jnp.zeros_like(zero_ref)
            for act in (lambda c: c.start(), lambda c: c.wait()):
                for j in range(n_tail):
                    row = na_ref[1] + j * TM

                    @pl.when(row < o_hbm.shape[0])
                    def _():
                        act(pltpu.make_async_copy(zero_ref, o_hbm.at[pl.ds(pl.multiple_of(row, TM), TM)], zsem))
            wait_outputs(u)


def moe_experts(x_buf, plan, n_assign, w_gate_up, b_gate_up, w_down, b_down, name="moe_experts"):
    R, H = x_buf.shape
    D = 2 * H
    n_tail = (R - n_assign) // MOE_TM
    E, _, F2 = w_gate_up.shape
    DF = F2 // 2
    TM, UT, TF = MOE_TM, MOE_UNIT_TILES, MOE_TF
    NF = DF // TF
    ue, r0, nt, na = plan
    U = ue.shape[0]

    def fsel(u, f, na_ref):
        return jnp.where(u < na_ref[0], f, NF - 1)

    kern = functools.partial(_experts_kernel, TM=TM, UT=UT, NF=NF, n_tail=n_tail)
    return pl.pallas_call(
        kern,
        grid_spec=pltpu.PrefetchScalarGridSpec(
            num_scalar_prefetch=4,
            grid=(U, NF),
            in_specs=[
                pl.BlockSpec(memory_space=pl.ANY),
                pl.BlockSpec((None, D, TF), lambda u, f, ue, r0, nt, na: (ue[u], 0, fsel(u, f, na))),
                pl.BlockSpec((None, D, TF), lambda u, f, ue, r0, nt, na: (ue[u], 0, NF + fsel(u, f, na))),
                pl.BlockSpec((None, 1, TF), lambda u, f, ue, r0, nt, na: (ue[u], 0, fsel(u, f, na))),
                pl.BlockSpec((None, 1, TF), lambda u, f, ue, r0, nt, na: (ue[u], 0, NF + fsel(u, f, na))),
                pl.BlockSpec((None, TF, D), lambda u, f, ue, r0, nt, na: (ue[u], fsel(u, f, na), 0)),
                pl.BlockSpec((None, 1, D), lambda u, f, ue, r0, nt, na: (ue[u], 0, 0)),
            ],
            out_specs=pl.BlockSpec(memory_space=pl.ANY),
            scratch_shapes=[
                pltpu.VMEM((2, UT, TM, H), jnp.uint32),
                pltpu.VMEM((UT, TM, D), F32),
                pltpu.VMEM((TM, D), F32),
                pltpu.SemaphoreType.DMA((UT,)),
                pltpu.SemaphoreType.DMA((UT,)),
                pltpu.SemaphoreType.DMA(()),
            ],
        ),
        out_shape=jax.ShapeDtypeStruct((R, D), F32),
        compiler_params=_cparams(("arbitrary", "arbitrary"), VMEM_LIMIT_EXPERTS_BYTES),
        name=name,
    )(ue, r0, nt, na, x_buf, w_gate_up, w_gate_up, b_gate_up.reshape(E, 1, F2), b_gate_up.reshape(E, 1, F2),
      w_down, b_down.reshape(E, 1, D))


def moe_plan(counts, n_tokens):
    TM, UT = MOE_TM, MOE_UNIT_TILES
    E = counts.shape[0]
    nt = (counts + TM - 1) // TM
    padded = nt * TM
    pad_start = jnp.cumsum(padded) - padded
    nu = (nt + UT - 1) // UT
    cu = jnp.cumsum(nu)
    n_units = cu[-1]
    U = E + (n_tokens * TOP_K) // (TM * UT)
    u = jnp.arange(U, dtype=jnp.int32)
    uc = jnp.minimum(u, n_units - 1)
    ue = jnp.clip(jnp.searchsorted(cu, uc, side='right'), 0, E - 1).astype(jnp.int32)
    j = uc - (cu[ue] - nu[ue])
    r0 = (pad_start[ue] + j * (UT * TM)).astype(jnp.int32)
    ntl = jnp.where(u < n_units, jnp.clip(nt[ue] - j * UT, 0, UT), 0).astype(jnp.int32)
    zstart = (pad_start + counts) // 8 * 8
    pad_end = pad_start + padded
    zrows = jnp.concatenate([zstart, pad_end - zstart, pad_end[-1:]]).astype(jnp.int32)
    meta = jnp.stack([n_units, pad_end[-1]]).astype(jnp.int32)
    return pad_start, zrows, (ue, r0, ntl, meta)


def _combine_kernel(dest_ref, h_ref, route_ref, o_hbm, h2_ref, g_ref, sem, *, tm):
    i = pl.program_id(0)
    par = i % 2

    def gather_tile(t, slot):
        def start(rb, c):
            for j in range(ROW_UNROLL):
                r = rb * ROW_UNROLL + j
                for k in range(TOP_K):
                    d = dest_ref[(t * tm + r) * TOP_K + k]
                    pltpu.make_async_copy(o_hbm.at[pl.ds(d, 1)], g_ref.at[slot, k, pl.ds(r, 1)],
                                          sem.at[slot]).start(priority=k % 2)
            return c

        lax.fori_loop(0, tm // ROW_UNROLL, start, 0)

    @pl.when(i == 0)
    def _():
        gather_tile(i, par)

    @pl.when(i + 1 < pl.num_programs(0))
    def _():
        gather_tile(i + 1, 1 - par)

    for k in range(TOP_K):
        pltpu.make_async_copy(o_hbm.at[pl.ds(0, tm)], g_ref.at[par, k], sem.at[par]).wait()

    route = route_ref[...]
    y = g_ref[par, 0] * route[:, 2 * TOP_K:2 * TOP_K + 1]
    for k in range(1, TOP_K):
        y = y + g_ref[par, k] * route[:, 2 * TOP_K + k:2 * TOP_K + k + 1]
    h2_ref[...] = h_ref[...] + y


def moe_combine(h, route, dest_flat, out_buf, tm=256, name="moe_combine"):
    T, D = h.shape
    kern = functools.partial(_combine_kernel, tm=tm)
    return pl.pallas_call(
        kern,
        grid_spec=pltpu.PrefetchScalarGridSpec(
            num_scalar_prefetch=1,
            grid=(T // tm,),
            in_specs=[pl.BlockSpec((tm, D), lambda i, d: (i, 0)),
                      pl.BlockSpec((tm, LANES), lambda i, d: (i, 0)),
                      pl.BlockSpec(memory_space=pl.ANY)],
            out_specs=pl.BlockSpec((tm, D), lambda i, d: (i, 0)),
            scratch_shapes=[pltpu.VMEM((2, TOP_K, tm, D), F32), pltpu.SemaphoreType.DMA((2,))],
        ),
        out_shape=jax.ShapeDtypeStruct((T, D), F32),
        compiler_params=_cparams(("arbitrary",)),
        name=name,
    )(dest_flat, h, route, out_buf)


def _ple_kernel(p1_ref, p2_ref, h_ref, wg_hbm, wp_hbm, np_ref, nf_ref, oa_ref, ob_ref, wgb_ref, wpb_ref, stage_ref, sem,
                *, nfirst):
    i = pl.program_id(0)

    @pl.when(i == 0)
    def _():
        _load_weight_bf16(wg_hbm, wgb_ref, stage_ref, sem)
        _load_weight_bf16(wp_hbm, wpb_ref, stage_ref, sem)

    h = h_ref[...]
    gate = jax.nn.sigmoid(_dot(_rms(h, np_ref[...]).astype(BF16), wgb_ref[...]))
    p = _pick(i, nfirst, p1_ref, p2_ref).astype(BF16)
    y = _rms(h + _dot(p, wpb_ref[...]) * gate, nf_ref[...])

    @pl.when(i < nfirst)
    def _():
        oa_ref[...] = y

    @pl.when(i >= nfirst)
    def _():
        ob_ref[...] = y


def ple_final(p1, p2, h, w_gate, w_ple, norm_ple_w, norm_w, tm=512, name="ple_final"):
    T, D = h.shape
    Ta, P = p1.shape
    na = Ta // tm
    return pl.pallas_call(
        functools.partial(_ple_kernel, nfirst=na),
        grid=(T // tm,),
        in_specs=[
            *_cat_specs((tm, P), na),
            pl.BlockSpec((tm, D), lambda i: (i, 0)),
            pl.BlockSpec(memory_space=pl.ANY),
            pl.BlockSpec(memory_space=pl.ANY),
            pl.BlockSpec((1, D), lambda i: (0, 0)),
            pl.BlockSpec((1, D), lambda i: (0, 0)),
        ],
        out_specs=list(_cat_specs((tm, D), na)),
        out_shape=[jax.ShapeDtypeStruct((Ta, D), F32), jax.ShapeDtypeStruct((T - Ta, D), F32)],
        scratch_shapes=[pltpu.VMEM((D, D), BF16), pltpu.VMEM((P, D), BF16),
                        pltpu.VMEM((2, WEIGHT_STAGE_ROWS, D), F32), pltpu.SemaphoreType.DMA((2,))],
        compiler_params=_cparams(("arbitrary",)),
        name=name,
    )(p1, p2, h, w_gate, w_ple, norm_ple_w.reshape(1, D), norm_w.reshape(1, D))


def kernel(x_prompt, x_sample, p_prompt, p_sample, state_hgrn, state_ssm_re, state_ssm_im, norm_mix, w_in, hg_lb, hg_gnorm, w_branch_a, ssm_a_re, ssm_a_im, ssm_log_dt, ssm_b_re, ssm_b_im, ssm_c_re, ssm_c_im, ssm_d, w_glu, w_branch_b, w_out, norm_moe, w_router, b_router, w_gate_up, b_gate_up, w_down, b_down, norm_ple, w_ple, w_ple_gate, norm_final):
    BP, LP, D = x_prompt.shape
    BS, LS, _ = x_sample.shape
    depth = w_in.shape[0]
    assert depth == 1
    TP, TS = BP * LP, BS * LS
    T = TP + TS
    G, P = ssm_a_re.shape[1:]
    W = SSM_GROUP * G

    xp = x_prompt.reshape(TP, D)
    xs = x_sample.reshape(TS, D)

    xn = rmsnorm_cat(xp, xs, norm_mix[0], BF16, name="norm_mix")
    tm = 1024
    w_in0 = w_in[0]
    z_hg = project(xn, w_in0, 0, T // tm, 0, 4, tm, 1024, "proj_hgrn")
    u_p = project(xn, w_in0, 0, TP // tm, 4, 1, tm, 1024, "proj_u_prompt")
    u_s = project(xn, w_in0, TP // tm, TS // tm, 4, 1, tm, 1024, "proj_u_sample")
    z_gate = project(xn, w_in0, 0, T // tm, 5, 4, tm, 1024, "proj_gates", out_dtype=BF16)

    lb = jax.nn.softmax(hg_lb.astype(F32), axis=0)[0]
    og_p, hg_p = hgrn2(z_hg, lb, hg_gnorm[0], None, row0=0, B=BP, L=LP, C=HG_CHUNK, nb=1, tok_step=256,
                       name="hgrn_prompt")
    og_s, hg_s = hgrn2(z_hg, lb, hg_gnorm[0], state_hgrn[0], row0=TP, B=BS, L=LS, C=LS, nb=8, tok_step=LS,
                       name="hgrn_sample")

    bd, cd, ab = s5_params(ssm_a_re[0], ssm_a_im[0], ssm_log_dt[0], ssm_b_re[0], ssm_b_im[0],
                           ssm_c_re[0], ssm_c_im[0])
    y_p, re_p, im_p = s5_prompt(u_p.reshape(BP, LP, W), bd, cd, ab, ssm_d[0], TC=256, name="s5_prompt")
    y_s, re_s, im_s = s5_sample(u_s, bd, cd, ab, ssm_d[0], state_ssm_re[0].reshape(BS, G * P),
                                state_ssm_im[0].reshape(BS, G * P), B=BS, L=LS, name="s5_sample")

    glu = gelu_glu(y_p.reshape(TP, W), y_s, w_glu[0])
    merged = gated_merge(og_p, og_s, glu, z_gate, w_branch_a[0], w_branch_b[0])

    h1, xn2, route, cnt = out_proj_router(merged, xp, xs, w_out[0], norm_moe[0], w_router[0], b_router[0])
    e_idx = route[:, 0:TOP_K].astype(jnp.int32)
    rank = route[:, TOP_K:2 * TOP_K].astype(jnp.int32)
    counts = cnt[0, :N_EXPERTS].astype(jnp.int32)
    pad_start, zrows, plan = moe_plan(counts, T)
    onehot = e_idx[..., None] == jnp.arange(N_EXPERTS, dtype=jnp.int32)
    dest = (jnp.sum(jnp.where(onehot, pad_start, 0), axis=-1) + rank).astype(jnp.int32).reshape(T * TOP_K)
    n_rows = T * TOP_K + N_EXPERTS * MOE_TM
    x_buf = moe_dispatch(xn2, dest, zrows, n_rows)
    out_buf = moe_experts(x_buf, plan, T * TOP_K, w_gate_up[0], b_gate_up[0], w_down[0], b_down[0])
    h2 = moe_combine(h1, route, dest, out_buf)

    y_p2, y_s2 = ple_final(p_prompt[0].reshape(TP, -1), p_sample[0].reshape(TS, -1), h2, w_ple_gate[0],
                           w_ple[0], norm_ple[0], norm_final)

    y_prompt = y_p2.reshape(BP, LP, D)
    y_sample = y_s2.reshape(BS, LS, D)
    hs = (1, BP, HG_HEADS, HG_DK, HG_DV)
    return (y_prompt, y_sample,
            hg_p.reshape(hs), re_p.reshape(1, BP, G, P), im_p.reshape(1, BP, G, P),
            hg_s.reshape(1, BS, HG_HEADS, HG_DK, HG_DV), re_s.reshape(1, BS, G, P), im_s.reshape(1, BS, G, P))
```

```python
import functools

import jax
import jax.numpy as jnp
from jax import lax
from jax.experimental import pallas as pl
from jax.experimental.pallas import tpu as pltpu

F32 = jnp.float32
BF16 = jnp.bfloat16
EPS = 1e-6

HG_HEADS = 8
HG_DK = 128
HG_DV = 128
HG_CHUNK = 32
SSM_GROUP = 16
N_EXPERTS = 32
TOP_K = 4
SWIGLU_LIMIT = 7.0
SWIGLU_ALPHA = 1.702

LANES = 128
VMEM_LIMIT_BYTES = 56 * 1024 * 1024
VMEM_LIMIT_EXPERTS_BYTES = 61 * 1024 * 1024

MOE_TM = 128
MOE_UNIT_TILES = 12
MOE_GROUP = 8
MOE_TF = 512
NEG_BIG = -1e30


def _cparams(sem, vmem_limit=VMEM_LIMIT_BYTES):
    return pltpu.CompilerParams(dimension_semantics=sem, vmem_limit_bytes=vmem_limit)


def _dot(a, b):
    return jnp.dot(a, b, preferred_element_type=F32)


def _split2(a):
    hi = a.astype(BF16)
    lo = (a - hi.astype(F32)).astype(BF16)
    return hi, lo


def _dot_x3(a, b_split):
    a1, a2 = _split2(a)
    b1, b2 = b_split
    return _dot(a1, b1) + (_dot(a1, b2) + _dot(a2, b1))


WEIGHT_STAGE_ROWS = 256


def _load_weight_bf16(w_hbm, wb_ref, stage_ref, sem):
    K = w_hbm.shape[0]
    rows = min(stage_ref.shape[1], K)
    n = K // rows

    def copy(c):
        return pltpu.make_async_copy(w_hbm.at[pl.ds(c * rows, rows)], stage_ref.at[c % 2, pl.ds(0, rows)],
                                     sem.at[c % 2])

    copy(0).start()
    for c in range(n):
        if c + 1 < n:
            copy(c + 1).start()
        copy(c).wait()
        wb_ref[pl.ds(c * rows, rows), :] = stage_ref[c % 2, pl.ds(0, rows), :].astype(BF16)


def _cat_specs(shape, nfirst, row_axis=0, col_axis=None):
    def col(ids):
        return 0 if col_axis is None else ids[col_axis]

    first = pl.BlockSpec(shape, lambda *ids: (jnp.minimum(ids[row_axis], nfirst - 1), col(ids)))
    second = pl.BlockSpec(shape, lambda *ids: (jnp.maximum(ids[row_axis] - nfirst, 0), col(ids)))
    return first, second


def _pick(i, nfirst, a_ref, b_ref):
    return jnp.where(i < nfirst, a_ref[...], b_ref[...])


def _rms(x, w):
    ms = jnp.mean(x * x, axis=-1, keepdims=True)
    return x * lax.rsqrt(ms + EPS) * w


def _rms_in2_kernel(a_ref, b_ref, w_ref, o_ref, *, nfirst):
    x = _pick(pl.program_id(0), nfirst, a_ref, b_ref)
    o_ref[...] = _rms(x, w_ref[...]).astype(o_ref.dtype)


def rmsnorm_cat(xa, xb, w, out_dtype, tm=512, name="rmsnorm"):
    (Ta, D), Tb = xa.shape, xb.shape[0]
    na = Ta // tm
    return pl.pallas_call(
        functools.partial(_rms_in2_kernel, nfirst=na),
        grid=((Ta + Tb) // tm,),
        in_specs=[*_cat_specs((tm, D), na), pl.BlockSpec((1, D), lambda i: (0, 0))],
        out_specs=pl.BlockSpec((tm, D), lambda i: (i, 0)),
        out_shape=jax.ShapeDtypeStruct((Ta + Tb, D), out_dtype),
        compiler_params=_cparams(("arbitrary",)),
        name=name,
    )(xa, xb, w.reshape(1, D))


def _proj_kernel(x_ref, w_ref, o_ref, wb_ref):
    @pl.when(pl.program_id(1) == 0)
    def _():
        wb_ref[...] = w_ref[...].astype(BF16)

    o_ref[...] = _dot(x_ref[...], wb_ref[...]).astype(o_ref.dtype)


def project(xn, w, row_blk0, n_row_blks, col_blk0, n_col_blks, tm, tn, name, out_dtype=F32):
    K = xn.shape[1]
    return pl.pallas_call(
        _proj_kernel,
        grid=(n_col_blks, n_row_blks),
        in_specs=[
            pl.BlockSpec((tm, K), lambda j, i: (row_blk0 + i, 0)),
            pl.BlockSpec((K, tn), lambda j, i: (0, col_blk0 + j)),
        ],
        out_specs=pl.BlockSpec((tm, tn), lambda j, i: (i, j)),
        out_shape=jax.ShapeDtypeStruct((n_row_blks * tm, n_col_blks * tn), out_dtype),
        scratch_shapes=[pltpu.VMEM((K, tn), BF16)],
        compiler_params=_cparams(("arbitrary", "arbitrary")),
        name=name,
    )(xn, w)


HGRN_CHUNKS_PER_TRIP = 4


def _hgrn_kernel(*refs, C, nchunk, nb, tok_step, has_s0, state_t):
    if has_s0:
        q_ref, f_ref, v_ref, g_ref, lb_ref, gn_ref, s0_ref, og_ref, so_ref, st_ref = refs
    else:
        q_ref, f_ref, v_ref, g_ref, lb_ref, gn_ref, og_ref, so_ref, st_ref = refs
        s0_ref = None
    H, DK, DV = HG_HEADS, HG_DK, HG_DV
    step = pl.program_id(1)

    @pl.when(step == 0)
    def _():
        for i in range(nb):
            for h in range(H):
                if has_s0:
                    st_ref[i, h] = s0_ref[i, h].T if state_t else s0_ref[i, h]
                else:
                    st_ref[i, h] = jnp.zeros(st_ref.shape[2:], F32)

    lbv = lb_ref[...]
    gnv = gn_ref[...]
    rr = lax.broadcasted_iota(jnp.int32, (C, C), 0)
    cc = lax.broadcasted_iota(jnp.int32, (C, C), 1)
    causal = rr >= cc
    tri = causal.astype(BF16)
    sls =[slice(h * DK, (h + 1) * DK) for h in range(H)]
    nt_dims = (((1,), (1,)), ((), ()))
    tn_dims = (((0,), (0,)), ((), ()))

    def tn(a, b):
        return lax.dot_general(a, b, tn_dims, preferred_element_type=F32)

    def prepare(r0):
        rows = pl.ds(r0, C)
        zq = q_ref[rows, :]
        zf = f_ref[rows, :]
        q = jax.nn.silu(zq)
        fe = lbv + (1.0 - lbv) * jax.nn.sigmoid(zf)
        k = 1.0 - fe
        gl = jnp.log(fe)
        g1 = gl.astype(BF16)
        r1 = gl - g1.astype(F32)
        g2 = r1.astype(BF16)
        g3 = (r1 - g2.astype(F32)).astype(BF16)
        b = _dot(tri, g1) + (_dot(tri, g2) + _dot(tri, g3))
        blast = b[C - 1:C, :]
        return dict(
            qt=(q * jnp.exp(b)).astype(BF16),
            kt=(k * jnp.exp(-b)).astype(BF16),
            kend=(k * jnp.exp(blast - b)).astype(BF16),
            eblast=jnp.exp(blast),
            vb=v_ref[rows, :].astype(BF16),
            gate=jax.nn.silu(g_ref[rows, :]),
        )

    def run(jobs):
        P = [prepare(r0) for _, r0 in jobs]
        atts = [[lax.dot_general(p["qt"][:, sl], p["kt"][:, sl], nt_dims, preferred_element_type=F32)
                 for sl in sls] for p in P]
        if state_t:
            upds = [[tn(p["vb"][:, sl], p["kend"][:, sl]) for sl in sls] for p in P]
            decs = [[p["eblast"][:, sl] for sl in sls] for p in P]
        else:
            upds = [[tn(p["kend"][:, sl], p["vb"][:, sl]) for sl in sls] for p in P]
            decs = [[jnp.broadcast_to(p["eblast"][:, sl], (8, DK)).T[:, 0:1] for sl in sls] for p in P]
        state = {}
        inters = []
        for j, (i, _) in enumerate(jobs):
            row = []
            for h, sl in enumerate(sls):
                s = state[(i, h)] if (i, h) in state else st_ref[i, h]
                qth = P[j]["qt"][:, sl]
                if state_t:
                    row.append(lax.dot_general(qth, s.astype(BF16), nt_dims, preferred_element_type=F32))
                else:
                    row.append(_dot(qth, s.astype(BF16)))
                state[(i, h)] = s * decs[j][h] + upds[j][h]
            inters.append(row)
        for (i, h), s in state.items():
            st_ref[i, h] = s
        outs = []
        for j, p in enumerate(P):
            cols = []
            for h, sl in enumerate(sls):
                att = jnp.where(causal, atts[j][h], 0.0).astype(BF16)
                o = _dot(att, p["vb"][:, sl]) + inters[j][h]
                ms = jnp.mean(o * o, axis=-1, keepdims=True)
                cols.append(o * lax.rsqrt(ms + EPS) * gnv * p["gate"][:, sl])
            outs.append(jnp.concatenate(cols, axis=1))
        return outs

    if nchunk == 1:
        outs = run([(i, i * tok_step) for i in range(nb)])
        og_ref[...] = jnp.concatenate(outs, axis=0).astype(og_ref.dtype)
    else:
        assert nb == 1 and nchunk % HGRN_CHUNKS_PER_TRIP == 0

        def body(t, carry):
            r0s = [pl.multiple_of((HGRN_CHUNKS_PER_TRIP * t + j) * C, C) for j in range(HGRN_CHUNKS_PER_TRIP)]
            outs = run([(0, r0) for r0 in r0s])
            for r0, o in zip(r0s, outs):
                og_ref[pl.ds(r0, C), :] = o.astype(og_ref.dtype)
            return carry

        lax.fori_loop(0, nchunk // HGRN_CHUNKS_PER_TRIP, body, 0)

    @pl.when(step == pl.num_programs(1) - 1)
    def _():
        for i in range(nb):
            for h in range(H):
                so_ref[i, h] = st_ref[i, h].T if state_t else st_ref[i, h]


def hgrn2(z_hg, lb, gn, s0, *, row0, B, L, C, nb, tok_step, name):
    H, DK, DV = HG_HEADS, HG_DK, HG_DV
    W = H * DK
    steps = L // tok_step
    rows = nb * tok_step
    nchunk = tok_step // C
    blk0 = row0 // rows
    has_s0 = s0 is not None

    def zspec(col):
        return pl.BlockSpec((rows, W), lambda bb, s: (blk0 + bb * steps + s, col))

    in_specs = [zspec(0), zspec(1), zspec(2), zspec(3),
                pl.BlockSpec((1, W), lambda bb, s: (0, 0)),
                pl.BlockSpec((1, DV), lambda bb, s: (0, 0))]
    args = [z_hg, z_hg, z_hg, z_hg, lb.reshape(1, W), gn.reshape(1, DV)]
    if has_s0:
        in_specs.append(pl.BlockSpec((nb, H, DK, DV), lambda bb, s: (bb, 0, 0, 0)))
        args.append(s0)
    state_t = L > C
    kern = functools.partial(_hgrn_kernel, C=C, nchunk=nchunk, nb=nb, tok_step=tok_step, has_s0=has_s0,
                             state_t=state_t)
    return pl.pallas_call(
        kern,
        grid=(B // nb, steps),
        in_specs=in_specs,
        out_specs=[pl.BlockSpec((rows, W), lambda bb, s: (bb * steps + s, 0)),
                   pl.BlockSpec((nb, H, DK, DV), lambda bb, s: (bb, 0, 0, 0))],
        out_shape=[jax.ShapeDtypeStruct((B * L, W), BF16),
                   jax.ShapeDtypeStruct((B, H, DK, DV), F32)],
        scratch_shapes=[pltpu.VMEM((nb, H, DV, DK), F32)],
        compiler_params=_cparams(("arbitrary", "arbitrary")),
        name=name,
    )(*args)


def _s5_step(ar, ai, xr, xi, br, bi):
    return ar * xr - ai * xi + br, ar * xi + ai * xr + bi


S5_STEPS_PER_TRIP = 4
S5_GROUPS_PER_BLOCK = 16


def _s5_prompt_kernel(u_ref, bd_ref, cd_ref, a_ref, d_ref, y_ref, xr_ref, xi_ref, X_ref, st_ref, *, B, TC):
    NP = X_ref.shape[0] // 2
    assert 2 * B == 8
    tc = pl.program_id(1)

    def rows_of(b, s):
        return pl.ds(2 * b + s, TC, stride=8)

    @pl.when(tc == 0)
    def _():
        st_ref[...] = jnp.zeros_like(st_ref)

    bd = bd_ref[...].astype(BF16)
    for b in range(B):
        bu = _dot(u_ref[b].astype(BF16), bd)
        for c in range(4 * NP):
            X_ref[c // 2, rows_of(b, c % 2), :] = bu[:, c * LANES:(c + 1) * LANES]

    odd = lax.broadcasted_iota(jnp.int32, (8, LANES), 0) % 2 == 1

    def pair_const(row, p):
        lo = a_ref[row:row + 1, (2 * p) * LANES:(2 * p + 1) * LANES]
        hi = a_ref[row:row + 1, (2 * p + 1) * LANES:(2 * p + 2) * LANES]
        return jnp.where(odd, hi, lo)

    ars = [pair_const(0, p) for p in range(NP)]
    ais = [pair_const(1, p) for p in range(NP)]

    def body(i, carry):
        carry = list(carry)
        for j in range(S5_STEPS_PER_TRIP):
            rows = pl.ds(pl.multiple_of((i * S5_STEPS_PER_TRIP + j) * 8, 8), 8)
            for p in range(NP):
                xr, xi = _s5_step(ars[p], ais[p], carry[p][0], carry[p][1], X_ref[p, rows, :], X_ref[NP + p, rows, :])
                X_ref[p, rows, :] = xr
                X_ref[NP + p, rows, :] = xi
                carry[p] = (xr, xi)
        return tuple(carry)

    init = tuple((st_ref[p], st_ref[NP + p]) for p in range(NP))
    fin = lax.fori_loop(0, TC // S5_STEPS_PER_TRIP, body, init)
    for p in range(NP):
        st_ref[p] = fin[p][0]
        st_ref[NP + p] = fin[p][1]

    cd = cd_ref[...].astype(BF16)
    dv = d_ref[...]
    for b in range(B):
        xb = jnp.concatenate([X_ref[c // 2, rows_of(b, c % 2), :] for c in range(4 * NP)], axis=1).astype(BF16)
        y_ref[b] = _dot(xb, cd) + dv * u_ref[b]

    @pl.when(tc == pl.num_programs(1) - 1)
    def _():
        for c in range(2 * NP):
            cols = slice(c * LANES, (c + 1) * LANES)
            xr_ref[:, cols] = st_ref[c // 2, pl.ds(c % 2, B, stride=2), :]
            xi_ref[:, cols] = st_ref[NP + c // 2, pl.ds(c % 2, B, stride=2), :]


def s5_prompt(u, bd, cd, ab, dskip, *, TC, name):
    B, L, W = u.shape
    GB, UB, SW = bd.shape
    half = SW // 2
    kern = functools.partial(_s5_prompt_kernel, B=B, TC=TC)
    return pl.pallas_call(
        kern,
        grid=(GB, L // TC),
        in_specs=[
            pl.BlockSpec((B, TC, UB), lambda g, t: (0, t, g)),
            pl.BlockSpec((None, UB, SW), lambda g, t: (g, 0, 0)),
            pl.BlockSpec((None, SW, UB), lambda g, t: (g, 0, 0)),
            pl.BlockSpec((None, 2, half), lambda g, t: (g, 0, 0)),
            pl.BlockSpec((1, UB), lambda g, t: (0, g)),
        ],
        out_specs=[
            pl.BlockSpec((B, TC, UB), lambda g, t: (0, t, g)),
            pl.BlockSpec((B, half), lambda g, t: (0, g)),
            pl.BlockSpec((B, half), lambda g, t: (0, g)),
        ],
        out_shape=[jax.ShapeDtypeStruct((B, L, W), F32),
                   jax.ShapeDtypeStruct((B, GB * half), F32),
                   jax.ShapeDtypeStruct((B, GB * half), F32)],
        scratch_shapes=[pltpu.VMEM((SW // LANES // 2, 8 * TC, LANES), F32),
                        pltpu.VMEM((SW // LANES // 2, 8, LANES), F32)],
        compiler_params=_cparams(("arbitrary", "arbitrary")),
        name=name,
    )(u, bd, cd, ab, dskip.reshape(1, W))


def _s5_sample_kernel(u_ref, bd_ref, cd_ref, a_ref, d_ref, x0r_ref, x0i_ref, y_ref, xr_ref, xi_ref, X_ref, *, B, L):
    NC = X_ref.shape[0]
    NH = NC // 2
    u = u_ref[...]
    bu = _dot_x3(u, _split2(bd_ref[...]))
    for c in range(NC):
        X_ref[c] = bu[:, c * LANES:(c + 1) * LANES]
    for c in range(NH):
        cols = slice(c * LANES, (c + 1) * LANES)
        ar = a_ref[0:1, cols]
        ai = a_ref[1:2, cols]
        xr = x0r_ref[:, cols]
        xi = x0i_ref[:, cols]
        for t in range(L):
            rows = pl.ds(t, B, stride=L)
            xr, xi = _s5_step(ar, ai, xr, xi, X_ref[c, rows, :], X_ref[NH + c, rows, :])
            X_ref[c, rows, :] = xr
            X_ref[NH + c, rows, :] = xi
        xr_ref[:, cols] = xr
        xi_ref[:, cols] = xi
    xs = jnp.concatenate([X_ref[c] for c in range(NC)], axis=1).astype(BF16)
    y_ref[...] = _dot(xs, cd_ref[...].astype(BF16)) + d_ref[...] * u


def s5_sample(u, bd, cd, ab, dskip, x0r, x0i, *, B, L, name):
    T, W = u.shape
    GB, UB, SW = bd.shape
    half = SW // 2
    kern = functools.partial(_s5_sample_kernel, B=B, L=L)
    return pl.pallas_call(
        kern,
        grid=(GB,),
        in_specs=[
            pl.BlockSpec((T, UB), lambda g: (0, g)),
            pl.BlockSpec((None, UB, SW), lambda g: (g, 0, 0)),
            pl.BlockSpec((None, SW, UB), lambda g: (g, 0, 0)),
            pl.BlockSpec((None, 2, half), lambda g: (g, 0, 0)),
            pl.BlockSpec((1, UB), lambda g: (0, g)),
            pl.BlockSpec((B, half), lambda g: (0, g)),
            pl.BlockSpec((B, half), lambda g: (0, g)),
        ],
        out_specs=[
            pl.BlockSpec((T, UB), lambda g: (0, g)),
            pl.BlockSpec((B, half), lambda g: (0, g)),
            pl.BlockSpec((B, half), lambda g: (0, g)),
        ],
        out_shape=[jax.ShapeDtypeStruct((T, W), F32),
                   jax.ShapeDtypeStruct((B, GB * half), F32),
                   jax.ShapeDtypeStruct((B, GB * half), F32)],
        scratch_shapes=[pltpu.VMEM((SW // LANES, T, LANES), F32)],
        compiler_params=_cparams(("arbitrary",)),
        name=name,
    )(u, bd, cd, ab, dskip.reshape(1, W), x0r, x0i)


def s5_params(a_re, a_im, log_dt, b_re, b_im, c_re, c_im):
    G, P = a_re.shape
    gpb = S5_GROUPS_PER_BLOCK
    GB = G // gpb
    A = lax.complex(a_re, a_im)
    dt = jnp.exp(log_dt)[:, None]
    A_bar = jnp.exp(A * dt)
    Bm = lax.complex(b_re, b_im)
    B_bar = ((A_bar - 1.0) / A)[..., None] * Bm
    same_group = jnp.eye(gpb, dtype=bool)[None, :, None, :, None]

    def block_diag(m):
        r, c = m.shape[1:]
        m = m.reshape(GB, gpb, r, 1, c)
        return jnp.where(same_group, m, 0.0).reshape(GB, gpb * r, gpb * c)

    def in_map(m):
        return block_diag(jnp.swapaxes(m, 1, 2))

    def out_map(m):
        return block_diag(jnp.swapaxes(m, 1, 2))

    bd = jnp.concatenate([in_map(B_bar.real), in_map(B_bar.imag)], axis=2)
    cd = jnp.concatenate([out_map(c_re), out_map(-c_im)], axis=1)
    ab = jnp.stack([A_bar.real.reshape(GB, gpb * P), A_bar.imag.reshape(GB, gpb * P)], axis=1)
    return bd, cd, ab


def _glu_kernel(ya_ref, yb_ref, w_ref, o_ref, wb_ref, *, nfirst):
    i = pl.program_id(0)

    @pl.when(i == 0)
    def _():
        wb_ref[...] = w_ref[...].astype(BF16)

    zg = jax.nn.gelu(_pick(i, nfirst, ya_ref, yb_ref))
    o_ref[...] = (zg * jax.nn.sigmoid(_dot(zg.astype(BF16), wb_ref[...]))).astype(o_ref.dtype)


def gelu_glu(ya, yb, w, tm=512, name="gelu_glu"):
    (Ta, W), Tb = ya.shape, yb.shape[0]
    na = Ta // tm
    return pl.pallas_call(
        functools.partial(_glu_kernel, nfirst=na),
        grid=((Ta + Tb) // tm,),
        in_specs=[*_cat_specs((tm, W), na), pl.BlockSpec((W, W), lambda i: (0, 0))],
        out_specs=pl.BlockSpec((tm, W), lambda i: (i, 0)),
        out_shape=jax.ShapeDtypeStruct((Ta + Tb, W), BF16),
        scratch_shapes=[pltpu.VMEM((W, W), BF16)],
        compiler_params=_cparams(("arbitrary",)),
        name=name,
    )(ya, yb, w)


def _merge_kernel(oa1_ref, oa2_ref, ob_ref, ga_ref, gb_ref, wa_ref, wb_ref, o_ref, wab_ref, wbb_ref, *, nfirst):
    i = pl.program_id(1)

    @pl.when(i == 0)
    def _():
        wab_ref[...] = wa_ref[...].astype(BF16)
        wbb_ref[...] = wb_ref[...].astype(BF16)

    ya = _dot(_pick(i, nfirst, oa1_ref, oa2_ref), wab_ref[...])
    yb = _dot(ob_ref[...], wbb_ref[...])
    ga = jax.nn.sigmoid(ga_ref[...].astype(F32))
    gb = jax.nn.sigmoid(gb_ref[...].astype(F32))
    o_ref[...] = (ga * ya + gb * yb).astype(o_ref.dtype)


def gated_merge(oa1, oa2, ob, z_gate, wa, wb, tm=1024, tn=1024, name="gated_merge"):
    T, K = ob.shape
    N = wa.shape[1]
    nj = N // tn
    na = oa1.shape[0] // tm
    return pl.pallas_call(
        functools.partial(_merge_kernel, nfirst=na),
        grid=(nj, T // tm),
        in_specs=[
            *_cat_specs((tm, K), na, row_axis=1),
            pl.BlockSpec((tm, K), lambda j, i: (i, 0)),
            pl.BlockSpec((tm, tn), lambda j, i: (i, j)),
            pl.BlockSpec((tm, tn), lambda j, i: (i, nj + j)),
            pl.BlockSpec((K, tn), lambda j, i: (0, j)),
            pl.BlockSpec((K, tn), lambda j, i: (0, j)),
        ],
        out_specs=pl.BlockSpec((tm, tn), lambda j, i: (i, j)),
        out_shape=jax.ShapeDtypeStruct((T, N), BF16),
        scratch_shapes=[pltpu.VMEM((K, tn), BF16), pltpu.VMEM((K, tn), BF16)],
        compiler_params=_cparams(("arbitrary", "arbitrary")),
        name=name,
    )(oa1, oa2, ob, z_gate, z_gate, wa, wb)


def _outproj_router_kernel(m_ref, h1_ref, h2_ref, w_hbm, nw_ref, wr_ref, br_ref, ho_ref, xn_ref, route_ref, cnt_ref,
                           wb_ref, tri_ref, carry_ref, stage_ref, sem, *, tm, nfirst):
    i = pl.program_id(0)

    @pl.when(i == 0)
    def _():
        _load_weight_bf16(w_hbm, wb_ref, stage_ref, sem)
        rr = lax.broadcasted_iota(jnp.int32, (tm, tm), 0)
        cc = lax.broadcasted_iota(jnp.int32, (tm, tm), 1)
        tri_ref[...] = (rr > cc).astype(BF16)
        carry_ref[...] = jnp.zeros_like(carry_ref)

    x = _pick(i, nfirst, h1_ref, h2_ref) + _dot(m_ref[...], wb_ref[...])
    ho_ref[...] = x
    ms = jnp.mean(x * x, axis=-1, keepdims=True)
    xn = x * lax.rsqrt(ms + EPS) * nw_ref[...]
    xn_ref[...] = _pack_bf16_pairs(xn)
    logits = _dot(xn.astype(BF16), wr_ref[...].astype(BF16)) + br_ref[...]
    lane = lax.broadcasted_iota(jnp.int32, (tm, LANES), 1)
    lanef = lane.astype(F32)
    cur = logits
    vals, hots, eids = [], [], []
    for _ in range(TOP_K):
        m = jnp.max(cur, axis=-1, keepdims=True)
        idx = jnp.min(jnp.where(cur == m, lanef, float(LANES)), axis=-1, keepdims=True)
        hot = lanef == idx
        vals.append(m)
        hots.append(hot)
        eids.append(idx)
        cur = jnp.where(hot, -jnp.inf, cur)
    es = [jnp.exp(v - vals[0]) for v in vals]
    den = es[0] + es[1] + es[2] + es[3]
    multi = jnp.zeros((tm, LANES), F32)
    for hot in hots:
        multi = multi + hot.astype(F32)
    base = carry_ref[0:1, :] + _dot(tri_ref[...], multi.astype(BF16))
    route = jnp.zeros((tm, LANES), F32)
    for k in range(TOP_K):
        e_k = eids[k]
        r_k = jnp.sum(jnp.where(hots[k], base, 0.0), axis=-1, keepdims=True)
        w_k = es[k] / den
        route = route + jnp.where(lane == k, e_k, 0.0) + jnp.where(lane == TOP_K + k, r_k, 0.0) \
            + jnp.where(lane == 2 * TOP_K + k, w_k, 0.0)
    route_ref[...] = route
    carry = carry_ref[0:1, :] + jnp.sum(multi, axis=0, keepdims=True)
    carry_ref[...] = jnp.broadcast_to(carry, carry_ref.shape)
    cnt_ref[...] = jnp.broadcast_to(carry, cnt_ref.shape)


def out_proj_router(merged, h1, h2, w_out, norm_w, w_router, b_router, tm=512, name="out_proj_router"):
    T, K = merged.shape
    D = w_out.shape[1]
    E = w_router.shape[1]
    na = h1.shape[0] // tm
    wr = jnp.pad(w_router, ((0, 0), (0, LANES - E)))
    br = jnp.pad(b_router.reshape(1, E), ((0, 0), (0, LANES - E)), constant_values=NEG_BIG)
    kern = functools.partial(_outproj_router_kernel, tm=tm, nfirst=na)
    return pl.pallas_call(
        kern,
        grid=(T // tm,),
        in_specs=[
            pl.BlockSpec((tm, K), lambda i: (i, 0)),
            *_cat_specs((tm, D), na),
            pl.BlockSpec(memory_space=pl.ANY),
            pl.BlockSpec((1, D), lambda i: (0, 0)),
            pl.BlockSpec((D, LANES), lambda i: (0, 0)),
            pl.BlockSpec((1, LANES), lambda i: (0, 0)),
        ],
        out_specs=[
            pl.BlockSpec((tm, D), lambda i: (i, 0)),
            pl.BlockSpec((tm, D // 2), lambda i: (i, 0)),
            pl.BlockSpec((tm, LANES), lambda i: (i, 0)),
            pl.BlockSpec((8, LANES), lambda i: (0, 0)),
        ],
        out_shape=[jax.ShapeDtypeStruct((T, D), F32),
                   jax.ShapeDtypeStruct((T, D // 2), jnp.uint32),
                   jax.ShapeDtypeStruct((T, LANES), F32),
                   jax.ShapeDtypeStruct((8, LANES), F32)],
        scratch_shapes=[pltpu.VMEM((K, D), BF16), pltpu.VMEM((tm, tm), BF16), pltpu.VMEM((8, LANES), F32),
                        pltpu.VMEM((2, WEIGHT_STAGE_ROWS, D), F32), pltpu.SemaphoreType.DMA((2,))],
        compiler_params=_cparams(("arbitrary",)),
        name=name,
    )(merged, h1, h2, w_out, norm_w.reshape(1, D), wr, br)


ROW_UNROLL = 8


def _pack_bf16_pairs(x):
    h = x.shape[1] // 2
    lo = lax.bitcast_convert_type(x[:, :h].astype(BF16).astype(F32), jnp.uint32)
    hi = lax.bitcast_convert_type(x[:, h:].astype(BF16).astype(F32), jnp.uint32)
    return (hi & jnp.uint32(0xFFFF0000)) | (lo >> 16)


def _unpack_bf16_pairs(w):
    lo = lax.bitcast_convert_type(w << 16, F32).astype(BF16)
    hi = lax.bitcast_convert_type(w & jnp.uint32(0xFFFF0000), F32).astype(BF16)
    return jnp.concatenate([lo, hi], axis=1)


def _dispatch_kernel(dest_ref, zrow_ref, x_ref, buf_ref, zero_ref, xs_ref, sem, zsem, *, tm, n_tail):
    i = pl.program_id(0)
    E = (zrow_ref.shape[0] - 1) // 2
    ZR = zero_ref.shape[0]
    R = buf_ref.shape[0]

    @pl.when(i == 0)
    def _():
        zero_ref[...] = jnp.zeros_like(zero_ref)

        def zero_copy(row, n):
            return pltpu.make_async_copy(zero_ref.at[pl.ds(0, n)], buf_ref.at[pl.ds(pl.multiple_of(row, 8), n)], zsem)

        def zero_rows(act):
            for e in range(E):
                off = zrow_ref[e]
                n = ZR
                while n >= 8:
                    @pl.when((zrow_ref[E + e] & n) != 0)
                    def _():
                        act(zero_copy(off, n))
                    off = off + (zrow_ref[E + e] & n)
                    n //= 2
            for j in range(n_tail):
                row = zrow_ref[2 * E] + j * ZR

                @pl.when(row < R)
                def _():
                    act(zero_copy(row, ZR))

        zero_rows(lambda c: c.start())
        zero_rows(lambda c: c.wait())

    par = i % 2
    xs_ref[par] = x_ref[...]

    def start(rb, c):
        for j in range(ROW_UNROLL):
            r = rb * ROW_UNROLL + j
            for k in range(TOP_K):
                d = dest_ref[(i * tm + r) * TOP_K + k]
                pltpu.make_async_copy(xs_ref.at[par, pl.ds(r, 1)], buf_ref.at[pl.ds(d, 1)],
                                      sem.at[par]).start(priority=k % 2)
        return c

    lax.fori_loop(0, tm // ROW_UNROLL, start, 0)

    def wait_step(s):
        for k in range(TOP_K):
            pltpu.make_async_copy(xs_ref.at[s], buf_ref.at[pl.ds(0, tm)], sem.at[s]).wait()

    @pl.when(i > 0)
    def _():
        wait_step(1 - par)

    @pl.when(i == pl.num_programs(0) - 1)
    def _():
        wait_step(par)


def moe_dispatch(xn, dest_flat, zrows, n_rows, tm=256, name="moe_dispatch"):
    T, H = xn.shape
    n_tail = (n_rows - T * TOP_K) // MOE_TM
    kern = functools.partial(_dispatch_kernel, tm=tm, n_tail=n_tail)
    return pl.pallas_call(
        kern,
        grid_spec=pltpu.PrefetchScalarGridSpec(
            num_scalar_prefetch=2,
            grid=(T // tm,),
            in_specs=[pl.BlockSpec((tm, H), lambda i, d, z: (i, 0))],
            out_specs=pl.BlockSpec(memory_space=pl.ANY),
            scratch_shapes=[pltpu.VMEM((MOE_TM, H), jnp.uint32), pltpu.VMEM((2, tm, H), jnp.uint32),
                            pltpu.SemaphoreType.DMA((2,)), pltpu.SemaphoreType.DMA(())],
        ),
        out_shape=jax.ShapeDtypeStruct((n_rows, H), jnp.uint32),
        compiler_params=_cparams(("arbitrary",)),
        name=name,
    )(dest_flat, zrows, xn)


def _experts_kernel(ue_ref, r0_ref, nt_ref, na_ref, x_hbm, wg_ref, wu_ref, bg_ref, bu_ref, wd_ref, bd_ref,
                    o_hbm, xb_ref, acc_ref, zero_ref, sem_in, sem_out, zsem, *, TM, UT, NF, n_tail):
    TPS = -(-UT // NF)
    u = pl.program_id(0)
    f = pl.program_id(1)
    na = na_ref[0]
    nt = nt_ref[u]
    r0 = r0_ref[u]
    slot = u % 2
    un = jnp.minimum(u + 1, na - 1)

    def prefetch(j):
        return jnp.logical_and(jnp.logical_and(u + 1 < na, j < nt_ref[un]), f == j // TPS)

    def x_copy(base, i, s):
        rows = pl.ds(pl.multiple_of(base + i * TM, TM), TM)
        return pltpu.make_async_copy(x_hbm.at[rows], xb_ref.at[s, i], sem_in.at[i])

    def o_copy(base, i):
        rows = pl.ds(pl.multiple_of(base + i * TM, TM), TM)
        return pltpu.make_async_copy(acc_ref.at[i], o_hbm.at[rows], sem_out.at[i])

    def wait_outputs(unit):
        for i in range(UT):
            @pl.when(i < nt_ref[unit])
            def _():
                o_copy(r0_ref[unit], i).wait()

    @pl.when(u < na)
    def _():
        @pl.when(jnp.logical_and(f == 0, u == 0))
        def _():
            for act in (lambda c: c.start(), lambda c: c.wait()):
                for i in range(UT):
                    @pl.when(i < nt)
                    def _():
                        act(x_copy(r0, i, 0))

        for j in range(UT):
            @pl.when(prefetch(j))
            def _():
                x_copy(r0_ref[un], j, 1 - slot).start()

        up = jnp.maximum(u - 1, 0)

        def wait_prev_output(j):
            @pl.when(jnp.logical_and(u > 0, j < nt_ref[up]))
            def _():
                o_copy(r0_ref[up], j).wait()

        bg = bg_ref[...]
        bu = bu_ref[...]

        def tile_out(xt):
            g = _dot(xt, wg_ref[...].astype(BF16)) + bg
            up = _dot(xt, wu_ref[...].astype(BF16)) + bu
            gate = jnp.minimum(g, SWIGLU_LIMIT)
            lin = jnp.clip(up, -SWIGLU_LIMIT, SWIGLU_LIMIT)
            act = (gate * jax.nn.sigmoid(SWIGLU_ALPHA * gate) * (lin + 1.0)).astype(BF16)
            return _dot(act, wd_ref[...].astype(BF16))

        D = acc_ref.shape[2]

        def update(i, n, first):
            tiles = pl.ds(i, n)
            if first:
                for t in range(n):
                    wait_prev_output(i + t)
            out = tile_out(_unpack_bf16_pairs(xb_ref[slot, tiles].reshape(n * TM, D // 2)))
            if first:
                out = out + bd_ref[...]
            else:
                out = out + acc_ref[tiles].reshape(n * TM, D)
            acc_ref[tiles] = out.reshape(n, TM, D)

            @pl.when(f == NF - 1)
            def _():
                for t in range(n):
                    o_copy(r0, i + t).start()

        def all_tiles(first):
            G = MOE_GROUP

            def body(j, c):
                update(pl.multiple_of(G * j, G), G, first)
                return c
            lax.fori_loop(0, nt // G, body, 0)
            done = (nt // G) * G
            n = G // 2
            while n >= 1:
                @pl.when((nt & n) != 0)
                def _():
                    update(pl.multiple_of(done, n), n, first)
                done = done + (nt & n)
                n //= 2

        @pl.when(f == 0)
        def _():
            all_tiles(True)
            for j in range(UT):
                @pl.when(j >= nt)
                def _():
                    wait_prev_output(j)

        @pl.when(f > 0)
        def _():
            all_tiles(False)

        for j in range(UT):
            @pl.when(prefetch(j))
            def _():
                x_copy(r0_ref[un], j, 1 - slot).wait()

        @pl.when(jnp.logical_and(f == NF - 1, u == na - 1))
        def _():
            zero_ref[...] = jnp.zeros_like(zero_ref)
            for act in (lambda c: c.start(), lambda c: c.wait()):
                for j in range(n_tail):
                    row = na_ref[1] + j * TM

                    @pl.when(row < o_hbm.shape[0])
                    def _():
                        act(pltpu.make_async_copy(zero_ref, o_hbm.at[pl.ds(pl.multiple_of(row, TM), TM)], zsem))
            wait_outputs(u)


def moe_experts(x_buf, plan, n_assign, w_gate_up, b_gate_up, w_down, b_down, name="moe_experts"):
    R, H = x_buf.shape
    D = 2 * H
    n_tail = (R - n_assign) // MOE_TM
    E, _, F2 = w_gate_up.shape
    DF = F2 // 2
    TM, UT, TF = MOE_TM, MOE_UNIT_TILES, MOE_TF
    NF = DF // TF
    ue, r0, nt, na = plan
    U = ue.shape[0]

    def fsel(u, f, na_ref):
        return jnp.where(u < na_ref[0], f, NF - 1)

    kern = functools.partial(_experts_kernel, TM=TM, UT=UT, NF=NF, n_tail=n_tail)
    return pl.pallas_call(
        kern,
        grid_spec=pltpu.PrefetchScalarGridSpec(
            num_scalar_prefetch=4,
            grid=(U, NF),
            in_specs=[
                pl.BlockSpec(memory_space=pl.ANY),
                pl.BlockSpec((None, D, TF), lambda u, f, ue, r0, nt, na: (ue[u], 0, fsel(u, f, na))),
                pl.BlockSpec((None, D, TF), lambda u, f, ue, r0, nt, na: (ue[u], 0, NF + fsel(u, f, na))),
                pl.BlockSpec((None, 1, TF), lambda u, f, ue, r0, nt, na: (ue[u], 0, fsel(u, f, na))),
                pl.BlockSpec((None, 1, TF), lambda u, f, ue, r0, nt, na: (ue[u], 0, NF + fsel(u, f, na))),
                pl.BlockSpec((None, TF, D), lambda u, f, ue, r0, nt, na: (ue[u], fsel(u, f, na), 0)),
                pl.BlockSpec((None, 1, D), lambda u, f, ue, r0, nt, na: (ue[u], 0, 0)),
            ],
            out_specs=pl.BlockSpec(memory_space=pl.ANY),
            scratch_shapes=[
                pltpu.VMEM((2, UT, TM, H), jnp.uint32),
                pltpu.VMEM((UT, TM, D), F32),
                pltpu.VMEM((TM, D), F32),
                pltpu.SemaphoreType.DMA((UT,)),
                pltpu.SemaphoreType.DMA((UT,)),
                pltpu.SemaphoreType.DMA(()),
            ],
        ),
        out_shape=jax.ShapeDtypeStruct((R, D), F32),
        compiler_params=_cparams(("arbitrary", "arbitrary"), VMEM_LIMIT_EXPERTS_BYTES),
        name=name,
    )(ue, r0, nt, na, x_buf, w_gate_up, w_gate_up, b_gate_up.reshape(E, 1, F2), b_gate_up.reshape(E, 1, F2),
      w_down, b_down.reshape(E, 1, D))


def moe_plan(counts, n_tokens):
    TM, UT = MOE_TM, MOE_UNIT_TILES
    E = counts.shape[0]
    nt = (counts + TM - 1) // TM
    padded = nt * TM
    pad_start = jnp.cumsum(padded) - padded
    nu = (nt + UT - 1) // UT
    cu = jnp.cumsum(nu)
    n_units = cu[-1]
    U = E + (n_tokens * TOP_K) // (TM * UT)
    u = jnp.arange(U, dtype=jnp.int32)
    uc = jnp.minimum(u, n_units - 1)
    ue = jnp.clip(jnp.searchsorted(cu, uc, side='right'), 0, E - 1).astype(jnp.int32)
    j = uc - (cu[ue] - nu[ue])
    r0 = (pad_start[ue] + j * (UT * TM)).astype(jnp.int32)
    ntl = jnp.where(u < n_units, jnp.clip(nt[ue] - j * UT, 0, UT), 0).astype(jnp.int32)
    zstart = (pad_start + counts) // 8 * 8
    pad_end = pad_start + padded
    zrows = jnp.concatenate([zstart, pad_end - zstart, pad_end[-1:]]).astype(jnp.int32)
    meta = jnp.stack([n_units, pad_end[-1]]).astype(jnp.int32)
    return pad_start, zrows, (ue, r0, ntl, meta)


def _combine_kernel(dest_ref, h_ref, route_ref, o_hbm, h2_ref, g_ref, sem, *, tm):
    i = pl.program_id(0)
    par = i % 2

    def gather_tile(t, slot):
        def start(rb, c):
            for j in range(ROW_UNROLL):
                r = rb * ROW_UNROLL + j
                for k in range(TOP_K):
                    d = dest_ref[(t * tm + r) * TOP_K + k]
                    pltpu.make_async_copy(o_hbm.at[pl.ds(d, 1)], g_ref.at[slot, k, pl.ds(r, 1)],
                                          sem.at[slot]).start(priority=k % 2)
            return c

        lax.fori_loop(0, tm // ROW_UNROLL, start, 0)

    @pl.when(i == 0)
    def _():
        gather_tile(i, par)

    @pl.when(i + 1 < pl.num_programs(0))
    def _():
        gather_tile(i + 1, 1 - par)

    for k in range(TOP_K):
        pltpu.make_async_copy(o_hbm.at[pl.ds(0, tm)], g_ref.at[par, k], sem.at[par]).wait()

    route = route_ref[...]
    y = g_ref[par, 0] * route[:, 2 * TOP_K:2 * TOP_K + 1]
    for k in range(1, TOP_K):
        y = y + g_ref[par, k] * route[:, 2 * TOP_K + k:2 * TOP_K + k + 1]
    h2_ref[...] = h_ref[...] + y


def moe_combine(h, route, dest_flat, out_buf, tm=256, name="moe_combine"):
    T, D = h.shape
    kern = functools.partial(_combine_kernel, tm=tm)
    return pl.pallas_call(
        kern,
        grid_spec=pltpu.PrefetchScalarGridSpec(
            num_scalar_prefetch=1,
            grid=(T // tm,),
            in_specs=[pl.BlockSpec((tm, D), lambda i, d: (i, 0)),
                      pl.BlockSpec((tm, LANES), lambda i, d: (i, 0)),
                      pl.BlockSpec(memory_space=pl.ANY)],
            out_specs=pl.BlockSpec((tm, D), lambda i, d: (i, 0)),
            scratch_shapes=[pltpu.VMEM((2, TOP_K, tm, D), F32), pltpu.SemaphoreType.DMA((2,))],
        ),
        out_shape=jax.ShapeDtypeStruct((T, D), F32),
        compiler_params=_cparams(("arbitrary",)),
        name=name,
    )(dest_flat, h, route, out_buf)


def _ple_kernel(p1_ref, p2_ref, h_ref, wg_hbm, wp_hbm, np_ref, nf_ref, oa_ref, ob_ref, wgb_ref, wpb_ref, stage_ref, sem,
                *, nfirst):
    i = pl.program_id(0)

    @pl.when(i == 0)
    def _():
        _load_weight_bf16(wg_hbm, wgb_ref, stage_ref, sem)
        _load_weight_bf16(wp_hbm, wpb_ref, stage_ref, sem)

    h = h_ref[...]
    gate = jax.nn.sigmoid(_dot(_rms(h, np_ref[...]).astype(BF16), wgb_ref[...]))
    p = _pick(i, nfirst, p1_ref, p2_ref).astype(BF16)
    y = _rms(h + _dot(p, wpb_ref[...]) * gate, nf_ref[...])

    @pl.when(i < nfirst)
    def _():
        oa_ref[...] = y

    @pl.when(i >= nfirst)
    def _():
        ob_ref[...] = y


def ple_final(p1, p2, h, w_gate, w_ple, norm_ple_w, norm_w, tm=512, name="ple_final"):
    T, D = h.shape
    Ta, P = p1.shape
    na = Ta // tm
    return pl.pallas_call(
        functools.partial(_ple_kernel, nfirst=na),
        grid=(T // tm,),
        in_specs=[
            *_cat_specs((tm, P), na),
            pl.BlockSpec((tm, D), lambda i: (i, 0)),
            pl.BlockSpec(memory_space=pl.ANY),
            pl.BlockSpec(memory_space=pl.ANY),
            pl.BlockSpec((1, D), lambda i: (0, 0)),
            pl.BlockSpec((1, D), lambda i: (0, 0)),
        ],
        out_specs=list(_cat_specs((tm, D), na)),
        out_shape=[jax.ShapeDtypeStruct((Ta, D), F32), jax.ShapeDtypeStruct((T - Ta, D), F32)],
        scratch_shapes=[pltpu.VMEM((D, D), BF16), pltpu.VMEM((P, D), BF16),
                        pltpu.VMEM((2, WEIGHT_STAGE_ROWS, D), F32), pltpu.SemaphoreType.DMA((2,))],
        compiler_params=_cparams(("arbitrary",)),
        name=name,
    )(p1, p2, h, w_gate, w_ple, norm_ple_w.reshape(1, D), norm_w.reshape(1, D))


def kernel(x_prompt, x_sample, p_prompt, p_sample, state_hgrn, state_ssm_re, state_ssm_im, norm_mix, w_in, hg_lb, hg_gnorm, w_branch_a, ssm_a_re, ssm_a_im, ssm_log_dt, ssm_b_re, ssm_b_im, ssm_c_re, ssm_c_im, ssm_d, w_glu, w_branch_b, w_out, norm_moe, w_router, b_router, w_gate_up, b_gate_up, w_down, b_down, norm_ple, w_ple, w_ple_gate, norm_final):
    BP, LP, D = x_prompt.shape
    BS, LS, _ = x_sample.shape
    depth = w_in.shape[0]
    assert depth == 1
    TP, TS = BP * LP, BS * LS
    T = TP + TS
    G, P = ssm_a_re.shape[1:]
    W = SSM_GROUP * G

    xp = x_prompt.reshape(TP, D)
    xs = x_sample.reshape(TS, D)

    xn = rmsnorm_cat(xp, xs, norm_mix[0], BF16, name="norm_mix")
    tm = 1024
    w_in0 = w_in[0]
    z_hg = project(xn, w_in0, 0, T // tm, 0, 4, tm, 1024, "proj_hgrn")
    u_p = project(xn, w_in0, 0, TP // tm, 4, 1, tm, 1024, "proj_u_prompt")
    u_s = project(xn, w_in0, TP // tm, TS // tm, 4, 1, tm, 1024, "proj_u_sample")
    z_gate = project(xn, w_in0, 0, T // tm, 5, 4, tm, 1024, "proj_gates", out_dtype=BF16)

    lb = jax.nn.softmax(hg_lb.astype(F32), axis=0)[0]
    og_p, hg_p = hgrn2(z_hg, lb, hg_gnorm[0], None, row0=0, B=BP, L=LP, C=HG_CHUNK, nb=1, tok_step=256,
                       name="hgrn_prompt")
    og_s, hg_s = hgrn2(z_hg, lb, hg_gnorm[0], state_hgrn[0], row0=TP, B=BS, L=LS, C=LS, nb=8, tok_step=LS,
                       name="hgrn_sample")

    bd, cd, ab = s5_params(ssm_a_re[0], ssm_a_im[0], ssm_log_dt[0], ssm_b_re[0], ssm_b_im[0],
                           ssm_c_re[0], ssm_c_im[0])
    y_p, re_p, im_p = s5_prompt(u_p.reshape(BP, LP, W), bd, cd, ab, ssm_d[0], TC=256, name="s5_prompt")
    y_s, re_s, im_s = s5_sample(u_s, bd, cd, ab, ssm_d[0], state_ssm_re[0].reshape(BS, G * P),
                                state_ssm_im[0].reshape(BS, G * P), B=BS, L=LS, name="s5_sample")

    glu = gelu_glu(y_p.reshape(TP, W), y_s, w_glu[0])
    merged = gated_merge(og_p, og_s, glu, z_gate, w_branch_a[0], w_branch_b[0])

    h1, xn2, route, cnt = out_proj_router(merged, xp, xs, w_out[0], norm_moe[0], w_router[0], b_router[0])
    e_idx = route[:, 0:TOP_K].astype(jnp.int32)
    rank = route[:, TOP_K:2 * TOP_K].astype(jnp.int32)
    counts = cnt[0, :N_EXPERTS].astype(jnp.int32)
    pad_start, zrows, plan = moe_plan(counts, T)
    onehot = e_idx[..., None] == jnp.arange(N_EXPERTS, dtype=jnp.int32)
    dest = (jnp.sum(jnp.where(onehot, pad_start, 0), axis=-1) + rank).astype(jnp.int32).reshape(T * TOP_K)
    n_rows = T * TOP_K + N_EXPERTS * MOE_TM
    x_buf = moe_dispatch(xn2, dest, zrows, n_rows)
    out_buf = moe_experts(x_buf, plan, T * TOP_K, w_gate_up[0], b_gate_up[0], w_down[0], b_down[0])
    h2 = moe_combine(h1, route, dest, out_buf)

    y_p2, y_s2 = ple_final(p_prompt[0].reshape(TP, -1), p_sample[0].reshape(TS, -1), h2, w_ple_gate[0],
                           w_ple[0], norm_ple[0], norm_final)

    y_prompt = y_p2.reshape(BP, LP, D)
    y_sample = y_s2.reshape(BS, LS, D)
    hs = (1, BP, HG_HEADS, HG_DK, HG_DV)
    return (y_prompt, y_sample,
            hg_p.reshape(hs), re_p.reshape(1, BP, G, P), im_p.reshape(1, BP, G, P),
            hg_s.reshape(1, BS, HG_HEADS, HG_DK, HG_DV), re_s.reshape(1, BS, G, P), im_s.reshape(1, BS, G, P))
```

```python
import functools

import jax
import jax.numpy as jnp
from jax import lax
from jax.experimental import pallas as pl
from jax.experimental.pallas import tpu as pltpu

F32 = jnp.float32
BF16 = jnp.bfloat16
EPS = 1e-6

HG_HEADS = 8
HG_DK = 128
HG_DV = 128
HG_CHUNK = 32
SSM_GROUP = 16
N_EXPERTS = 32
TOP_K = 4
SWIGLU_LIMIT = 7.0
SWIGLU_ALPHA = 1.702

LANES = 128
VMEM_LIMIT_BYTES = 56 * 1024 * 1024
VMEM_LIMIT_EXPERTS_BYTES = 61 * 1024 * 1024

MOE_TM = 128
MOE_UNIT_TILES = 12
MOE_GROUP = 8
MOE_TF = 512
NEG_BIG = -1e30


def _cparams(sem, vmem_limit=VMEM_LIMIT_BYTES):
    return pltpu.CompilerParams(dimension_semantics=sem, vmem_limit_bytes=vmem_limit)


def _dot(a, b):
    return jnp.dot(a, b, preferred_element_type=F32)


def _split2(a):
    hi = a.astype(BF16)
    lo = (a - hi.astype(F32)).astype(BF16)
    return hi, lo


def _dot_x3(a, b_split):
    a1, a2 = _split2(a)
    b1, b2 = b_split
    return _dot(a1, b1) + (_dot(a1, b2) + _dot(a2, b1))


WEIGHT_STAGE_ROWS = 256


def _load_weight_bf16(w_hbm, wb_ref, stage_ref, sem):
    K = w_hbm.shape[0]
    rows = min(stage_ref.shape[1], K)
    n = K // rows

    def copy(c):
        return pltpu.make_async_copy(w_hbm.at[pl.ds(c * rows, rows)], stage_ref.at[c % 2, pl.ds(0, rows)],
                                     sem.at[c % 2])

    copy(0).start()
    for c in range(n):
        if c + 1 < n:
            copy(c + 1).start()
        copy(c).wait()
        wb_ref[pl.ds(c * rows, rows), :] = stage_ref[c % 2, pl.ds(0, rows), :].astype(BF16)


def _cat_specs(shape, nfirst, row_axis=0, col_axis=None):
    def col(ids):
        return 0 if col_axis is None else ids[col_axis]

    first = pl.BlockSpec(shape, lambda *ids: (jnp.minimum(ids[row_axis], nfirst - 1), col(ids)))
    second = pl.BlockSpec(shape, lambda *ids: (jnp.maximum(ids[row_axis] - nfirst, 0), col(ids)))
    return first, second


def _pick(i, nfirst, a_ref, b_ref):
    return jnp.where(i < nfirst, a_ref[...], b_ref[...])


def _rms(x, w):
    ms = jnp.mean(x * x, axis=-1, keepdims=True)
    return x * lax.rsqrt(ms + EPS) * w


def _rms_in2_kernel(a_ref, b_ref, w_ref, o_ref, *, nfirst):
    x = _pick(pl.program_id(0), nfirst, a_ref, b_ref)
    o_ref[...] = _rms(x, w_ref[...]).astype(o_ref.dtype)


def rmsnorm_cat(xa, xb, w, out_dtype, tm=512, name="rmsnorm"):
    (Ta, D), Tb = xa.shape, xb.shape[0]
    na = Ta // tm
    return pl.pallas_call(
        functools.partial(_rms_in2_kernel, nfirst=na),
        grid=((Ta + Tb) // tm,),
        in_specs=[*_cat_specs((tm, D), na), pl.BlockSpec((1, D), lambda i: (0, 0))],
        out_specs=pl.BlockSpec((tm, D), lambda i: (i, 0)),
        out_shape=jax.ShapeDtypeStruct((Ta + Tb, D), out_dtype),
        compiler_params=_cparams(("arbitrary",)),
        name=name,
    )(xa, xb, w.reshape(1, D))


def _proj_kernel(x_ref, w_ref, o_ref, wb_ref):
    @pl.when(pl.program_id(1) == 0)
    def _():
        wb_ref[...] = w_ref[...].astype(BF16)

    o_ref[...] = _dot(x_ref[...], wb_ref[...]).astype(o_ref.dtype)


def project(xn, w, row_blk0, n_row_blks, col_blk0, n_col_blks, tm, tn, name, out_dtype=F32):
    K = xn.shape[1]
    return pl.pallas_call(
        _proj_kernel,
        grid=(n_col_blks, n_row_blks),
        in_specs=[
            pl.BlockSpec((tm, K), lambda j, i: (row_blk0 + i, 0)),
            pl.BlockSpec((K, tn), lambda j, i: (0, col_blk0 + j)),
        ],
        out_specs=pl.BlockSpec((tm, tn), lambda j, i: (i, j)),
        out_shape=jax.ShapeDtypeStruct((n_row_blks * tm, n_col_blks * tn), out_dtype),
        scratch_shapes=[pltpu.VMEM((K, tn), BF16)],
        compiler_params=_cparams(("arbitrary", "arbitrary")),
        name=name,
    )(xn, w)


HGRN_CHUNKS_PER_TRIP = 4


def _hgrn_kernel(*refs, C, nchunk, nb, tok_step, has_s0, state_t):
    if has_s0:
        q_ref, f_ref, v_ref, g_ref, lb_ref, gn_ref, s0_ref, og_ref, so_ref, st_ref = refs
    else:
        q_ref, f_ref, v_ref, g_ref, lb_ref, gn_ref, og_ref, so_ref, st_ref = refs
        s0_ref = None
    H, DK, DV = HG_HEADS, HG_DK, HG_DV
    step = pl.program_id(1)

    @pl.when(step == 0)
    def _():
        for i in range(nb):
            for h in range(H):
                if has_s0:
                    st_ref[i, h] = s0_ref[i, h].T if state_t else s0_ref[i, h]
                else:
                    st_ref[i, h] = jnp.zeros(st_ref.shape[2:], F32)

    lbv = lb_ref[...]
    gnv = gn_ref[...]
    rr = lax.broadcasted_iota(jnp.int32, (C, C), 0)
    cc = lax.broadcasted_iota(jnp.int32, (C, C), 1)
    causal = rr >= cc
    tri = causal.astype(BF16)
    sls =[slice(h * DK, (h + 1) * DK) for h in range(H)]
    nt_dims = (((1,), (1,)), ((), ()))
    tn_dims = (((0,), (0,)), ((), ()))

    def tn(a, b):
        return lax.dot_general(a, b, tn_dims, preferred_element_type=F32)

    def prepare(r0):
        rows = pl.ds(r0, C)
        zq = q_ref[rows, :]
        zf = f_ref[rows, :]
        q = jax.nn.silu(zq)
        fe = lbv + (1.0 - lbv) * jax.nn.sigmoid(zf)
        k = 1.0 - fe
        gl = jnp.log(fe)
        g1 = gl.astype(BF16)
        r1 = gl - g1.astype(F32)
        g2 = r1.astype(BF16)
        g3 = (r1 - g2.astype(F32)).astype(BF16)
        b = _dot(tri, g1) + (_dot(tri, g2) + _dot(tri, g3))
        blast = b[C - 1:C, :]
        return dict(
            qt=(q * jnp.exp(b)).astype(BF16),
            kt=(k * jnp.exp(-b)).astype(BF16),
            kend=(k * jnp.exp(blast - b)).astype(BF16),
            eblast=jnp.exp(blast),
            vb=v_ref[rows, :].astype(BF16),
            gate=jax.nn.silu(g_ref[rows, :]),
        )

    def run(jobs):
        P = [prepare(r0) for _, r0 in jobs]
        atts = [[lax.dot_general(p["qt"][:, sl], p["kt"][:, sl], nt_dims, preferred_element_type=F32)
                 for sl in sls] for p in P]
        if state_t:
            upds = [[tn(p["vb"][:, sl], p["kend"][:, sl]) for sl in sls] for p in P]
            decs = [[p["eblast"][:, sl] for sl in sls] for p in P]
        else:
            upds = [[tn(p["kend"][:, sl], p["vb"][:, sl]) for sl in sls] for p in P]
            decs = [[jnp.broadcast_to(p["eblast"][:, sl], (8, DK)).T[:, 0:1] for sl in sls] for p in P]
        state = {}
        inters = []
        for j, (i, _) in enumerate(jobs):
            row = []
            for h, sl in enumerate(sls):
                s = state[(i, h)] if (i, h) in state else st_ref[i, h]
                qth = P[j]["qt"][:, sl]
                if state_t:
                    row.append(lax.dot_general(qth, s.astype(BF16), nt_dims, preferred_element_type=F32))
                else:
                    row.append(_dot(qth, s.astype(BF16)))
                state[(i, h)] = s * decs[j][h] + upds[j][h]
            inters.append(row)
        for (i, h), s in state.items():
            st_ref[i, h] = s
        outs = []
        for j, p in enumerate(P):
            cols = []
            for h, sl in enumerate(sls):
                att = jnp.where(causal, atts[j][h], 0.0).astype(BF16)
                o = _dot(att, p["vb"][:, sl]) + inters[j][h]
                ms = jnp.mean(o * o, axis=-1, keepdims=True)
                cols.append(o * lax.rsqrt(ms + EPS) * gnv * p["gate"][:, sl])
            outs.append(jnp.concatenate(cols, axis=1))
        return outs

    if nchunk == 1:
        outs = run([(i, i * tok_step) for i in range(nb)])
        og_ref[...] = jnp.concatenate(outs, axis=0).astype(og_ref.dtype)
    else:
        assert nb == 1 and nchunk % HGRN_CHUNKS_PER_TRIP == 0

        def body(t, carry):
            r0s = [pl.multiple_of((HGRN_CHUNKS_PER_TRIP * t + j) * C, C) for j in range(HGRN_CHUNKS_PER_TRIP)]
            outs = run([(0, r0) for r0 in r0s])
            for r0, o in zip(r0s, outs):
                og_ref[pl.ds(r0, C), :] = o.astype(og_ref.dtype)
            return carry

        lax.fori_loop(0, nchunk // HGRN_CHUNKS_PER_TRIP, body, 0)

    @pl.when(step == pl.num_programs(1) - 1)
    def _():
        for i in range(nb):
            for h in range(H):
                so_ref[i, h] = st_ref[i, h].T if state_t else st_ref[i, h]


def hgrn2(z_hg, lb, gn, s0, *, row0, B, L, C, nb, tok_step, name):
    H, DK, DV = HG_HEADS, HG_DK, HG_DV
    W = H * DK
    steps = L // tok_step
    rows = nb * tok_step
    nchunk = tok_step // C
    blk0 = row0 // rows
    has_s0 = s0 is not None

    def zspec(col):
        return pl.BlockSpec((rows, W), lambda bb, s: (blk0 + bb * steps + s, col))

    in_specs = [zspec(0), zspec(1), zspec(2), zspec(3),
                pl.BlockSpec((1, W), lambda bb, s: (0, 0)),
                pl.BlockSpec((1, DV), lambda bb, s: (0, 0))]
    args = [z_hg, z_hg, z_hg, z_hg, lb.reshape(1, W), gn.reshape(1, DV)]
    if has_s0:
        in_specs.append(pl.BlockSpec((nb, H, DK, DV), lambda bb, s: (bb, 0, 0, 0)))
        args.append(s0)
    state_t = L > C
    kern = functools.partial(_hgrn_kernel, C=C, nchunk=nchunk, nb=nb, tok_step=tok_step, has_s0=has_s0,
                             state_t=state_t)
    return pl.pallas_call(
        kern,
        grid=(B // nb, steps),
        in_specs=in_specs,
        out_specs=[pl.BlockSpec((rows, W), lambda bb, s: (bb * steps + s, 0)),
                   pl.BlockSpec((nb, H, DK, DV), lambda bb, s: (bb, 0, 0, 0))],
        out_shape=[jax.ShapeDtypeStruct((B * L, W), BF16),
                   jax.ShapeDtypeStruct((B, H, DK, DV), F32)],
        scratch_shapes=[pltpu.VMEM((nb, H, DV, DK), F32)],
        compiler_params=_cparams(("arbitrary", "arbitrary")),
        name=name,
    )(*args)


def _s5_step(ar, ai, xr, xi, br, bi):
    return ar * xr - ai * xi + br, ar * xi + ai * xr + bi


S5_STEPS_PER_TRIP = 4
S5_GROUPS_PER_BLOCK = 16


def _s5_prompt_kernel(u_ref, bd_ref, cd_ref, a_ref, d_ref, y_ref, xr_ref, xi_ref, X_ref, st_ref, *, B, TC):
    NP = X_ref.shape[0] // 2
    assert 2 * B == 8
    tc = pl.program_id(1)

    def rows_of(b, s):
        return pl.ds(2 * b + s, TC, stride=8)

    @pl.when(tc == 0)
    def _():
        st_ref[...] = jnp.zeros_like(st_ref)

    bd = bd_ref[...].astype(BF16)
    for b in range(B):
        bu = _dot(u_ref[b].astype(BF16), bd)
        for c in range(4 * NP):
            X_ref[c // 2, rows_of(b, c % 2), :] = bu[:, c * LANES:(c + 1) * LANES]

    odd = lax.broadcasted_iota(jnp.int32, (8, LANES), 0) % 2 == 1

    def pair_const(row, p):
        lo = a_ref[row:row + 1, (2 * p) * LANES:(2 * p + 1) * LANES]
        hi = a_ref[row:row + 1, (2 * p + 1) * LANES:(2 * p + 2) * LANES]
        return jnp.where(odd, hi, lo)

    ars = [pair_const(0, p) for p in range(NP)]
    ais = [pair_const(1, p) for p in range(NP)]

    def body(i, carry):
        carry = list(carry)
        for j in range(S5_STEPS_PER_TRIP):
            rows = pl.ds(pl.multiple_of((i * S5_STEPS_PER_TRIP + j) * 8, 8), 8)
            for p in range(NP):
                xr, xi = _s5_step(ars[p], ais[p], carry[p][0], carry[p][1], X_ref[p, rows, :], X_ref[NP + p, rows, :])
                X_ref[p, rows, :] = xr
                X_ref[NP + p, rows, :] = xi
                carry[p] = (xr, xi)
        return tuple(carry)

    init = tuple((st_ref[p], st_ref[NP + p]) for p in range(NP))
    fin = lax.fori_loop(0, TC // S5_STEPS_PER_TRIP, body, init)
    for p in range(NP):
        st_ref[p] = fin[p][0]
        st_ref[NP + p] = fin[p][1]

    cd = cd_ref[...].astype(BF16)
    dv = d_ref[...]
    for b in range(B):
        xb = jnp.concatenate([X_ref[c // 2, rows_of(b, c % 2), :] for c in range(4 * NP)], axis=1).astype(BF16)
        y_ref[b] = _dot(xb, cd) + dv * u_ref[b]

    @pl.when(tc == pl.num_programs(1) - 1)
    def _():
        for c in range(2 * NP):
            cols = slice(c * LANES, (c + 1) * LANES)
            xr_ref[:, cols] = st_ref[c // 2, pl.ds(c % 2, B, stride=2), :]
            xi_ref[:, cols] = st_ref[NP + c // 2, pl.ds(c % 2, B, stride=2), :]


def s5_prompt(u, bd, cd, ab, dskip, *, TC, name):
    B, L, W = u.shape
    GB, UB, SW = bd.shape
    half = SW // 2
    kern = functools.partial(_s5_prompt_kernel, B=B, TC=TC)
    return pl.pallas_call(
        kern,
        grid=(GB, L // TC),
        in_specs=[
            pl.BlockSpec((B, TC, UB), lambda g, t: (0, t, g)),
            pl.BlockSpec((None, UB, SW), lambda g, t: (g, 0, 0)),
            pl.BlockSpec((None, SW, UB), lambda g, t: (g, 0, 0)),
            pl.BlockSpec((None, 2, half), lambda g, t: (g, 0, 0)),
            pl.BlockSpec((1, UB), lambda g, t: (0, g)),
        ],
        out_specs=[
            pl.BlockSpec((B, TC, UB), lambda g, t: (0, t, g)),
            pl.BlockSpec((B, half), lambda g, t: (0, g)),
            pl.BlockSpec((B, half), lambda g, t: (0, g)),
        ],
        out_shape=[jax.ShapeDtypeStruct((B, L, W), F32),
                   jax.ShapeDtypeStruct((B, GB * half), F32),
                   jax.ShapeDtypeStruct((B, GB * half), F32)],
        scratch_shapes=[pltpu.VMEM((SW // LANES // 2, 8 * TC, LANES), F32),
                        pltpu.VMEM((SW // LANES // 2, 8, LANES), F32)],
        compiler_params=_cparams(("arbitrary", "arbitrary")),
        name=name,
    )(u, bd, cd, ab, dskip.reshape(1, W))


def _s5_sample_kernel(u_ref, bd_ref, cd_ref, a_ref, d_ref, x0r_ref, x0i_ref, y_ref, xr_ref, xi_ref, X_ref, *, B, L):
    NC = X_ref.shape[0]
    NH = NC // 2
    u = u_ref[...]
    bu = _dot_x3(u, _split2(bd_ref[...]))
    for c in range(NC):
        X_ref[c] = bu[:, c * LANES:(c + 1) * LANES]
    for c in range(NH):
        cols = slice(c * LANES, (c + 1) * LANES)
        ar = a_ref[0:1, cols]
        ai = a_ref[1:2, cols]
        xr = x0r_ref[:, cols]
        xi = x0i_ref[:, cols]
        for t in range(L):
            rows = pl.ds(t, B, stride=L)
            xr, xi = _s5_step(ar, ai, xr, xi, X_ref[c, rows, :], X_ref[NH + c, rows, :])
            X_ref[c, rows, :] = xr
            X_ref[NH + c, rows, :] = xi
        xr_ref[:, cols] = xr
        xi_ref[:, cols] = xi
    xs = jnp.concatenate([X_ref[c] for c in range(NC)], axis=1).astype(BF16)
    y_ref[...] = _dot(xs, cd_ref[...].astype(BF16)) + d_ref[...] * u


def s5_sample(u, bd, cd, ab, dskip, x0r, x0i, *, B, L, name):
    T, W = u.shape
    GB, UB, SW = bd.shape
    half = SW // 2
    kern = functools.partial(_s5_sample_kernel, B=B, L=L)
    return pl.pallas_call(
        kern,
        grid=(GB,),
        in_specs=[
            pl.BlockSpec((T, UB), lambda g: (0, g)),
            pl.BlockSpec((None, UB, SW), lambda g: (g, 0, 0)),
            pl.BlockSpec((None, SW, UB), lambda g: (g, 0, 0)),
            pl.BlockSpec((None, 2, half), lambda g: (g, 0, 0)),
            pl.BlockSpec((1, UB), lambda g: (0, g)),
            pl.BlockSpec((B, half), lambda g: (0, g)),
            pl.BlockSpec((B, half), lambda g: (0, g)),
        ],
        out_specs=[
            pl.BlockSpec((T, UB), lambda g: (0, g)),
            pl.BlockSpec((B, half), lambda g: (0, g)),
            pl.BlockSpec((B, half), lambda g: (0, g)),
        ],
        out_shape=[jax.ShapeDtypeStruct((T, W), F32),
                   jax.ShapeDtypeStruct((B, GB * half), F32),
                   jax.ShapeDtypeStruct((B, GB * half), F32)],
        scratch_shapes=[pltpu.VMEM((SW // LANES, T, LANES), F32)],
        compiler_params=_cparams(("arbitrary",)),
        name=name,
    )(u, bd, cd, ab, dskip.reshape(1, W), x0r, x0i)


def s5_params(a_re, a_im, log_dt, b_re, b_im, c_re, c_im):
    G, P = a_re.shape
    gpb = S5_GROUPS_PER_BLOCK
    GB = G // gpb
    A = lax.complex(a_re, a_im)
    dt = jnp.exp(log_dt)[:, None]
    A_bar = jnp.exp(A * dt)
    Bm = lax.complex(b_re, b_im)
    B_bar = ((A_bar - 1.0) / A)[..., None] * Bm
    eye = jnp.eye(gpb, dtype=F32)

    def in_map(m):
        m = m.reshape(GB, gpb, P, SSM_GROUP)
        return jnp.einsum('bgpc,gh->bgchp', m, eye).reshape(GB, gpb * SSM_GROUP, gpb * P)

    def out_map(m):
        m = m.reshape(GB, gpb, SSM_GROUP, P)
        return jnp.einsum('bgcp,gh->bgphc', m, eye).reshape(GB, gpb * P, gpb * SSM_GROUP)

    bd = jnp.concatenate([in_map(B_bar.real), in_map(B_bar.imag)], axis=2)
    cd = jnp.concatenate([out_map(c_re), out_map(-c_im)], axis=1)
    ab = jnp.stack([A_bar.real.reshape(GB, gpb * P), A_bar.imag.reshape(GB, gpb * P)], axis=1)
    return bd, cd, ab


def _glu_kernel(ya_ref, yb_ref, w_ref, o_ref, wb_ref, *, nfirst):
    i = pl.program_id(0)

    @pl.when(i == 0)
    def _():
        wb_ref[...] = w_ref[...].astype(BF16)

    zg = jax.nn.gelu(_pick(i, nfirst, ya_ref, yb_ref))
    o_ref[...] = (zg * jax.nn.sigmoid(_dot(zg.astype(BF16), wb_ref[...]))).astype(o_ref.dtype)


def gelu_glu(ya, yb, w, tm=512, name="gelu_glu"):
    (Ta, W), Tb = ya.shape, yb.shape[0]
    na = Ta // tm
    return pl.pallas_call(
        functools.partial(_glu_kernel, nfirst=na),
        grid=((Ta + Tb) // tm,),
        in_specs=[*_cat_specs((tm, W), na), pl.BlockSpec((W, W), lambda i: (0, 0))],
        out_specs=pl.BlockSpec((tm, W), lambda i: (i, 0)),
        out_shape=jax.ShapeDtypeStruct((Ta + Tb, W), BF16),
        scratch_shapes=[pltpu.VMEM((W, W), BF16)],
        compiler_params=_cparams(("arbitrary",)),
        name=name,
    )(ya, yb, w)


def _merge_kernel(oa1_ref, oa2_ref, ob_ref, ga_ref, gb_ref, wa_ref, wb_ref, o_ref, wab_ref, wbb_ref, *, nfirst):
    i = pl.program_id(1)

    @pl.when(i == 0)
    def _():
        wab_ref[...] = wa_ref[...].astype(BF16)
        wbb_ref[...] = wb_ref[...].astype(BF16)

    ya = _dot(_pick(i, nfirst, oa1_ref, oa2_ref), wab_ref[...])
    yb = _dot(ob_ref[...], wbb_ref[...])
    ga = jax.nn.sigmoid(ga_ref[...].astype(F32))
    gb = jax.nn.sigmoid(gb_ref[...].astype(F32))
    o_ref[...] = (ga * ya + gb * yb).astype(o_ref.dtype)


def gated_merge(oa1, oa2, ob, z_gate, wa, wb, tm=1024, tn=1024, name="gated_merge"):
    T, K = ob.shape
    N = wa.shape[1]
    nj = N // tn
    na = oa1.shape[0] // tm
    return pl.pallas_call(
        functools.partial(_merge_kernel, nfirst=na),
        grid=(nj, T // tm),
        in_specs=[
            *_cat_specs((tm, K), na, row_axis=1),
            pl.BlockSpec((tm, K), lambda j, i: (i, 0)),
            pl.BlockSpec((tm, tn), lambda j, i: (i, j)),
            pl.BlockSpec((tm, tn), lambda j, i: (i, nj + j)),
            pl.BlockSpec((K, tn), lambda j, i: (0, j)),
            pl.BlockSpec((K, tn), lambda j, i: (0, j)),
        ],
        out_specs=pl.BlockSpec((tm, tn), lambda j, i: (i, j)),
        out_shape=jax.ShapeDtypeStruct((T, N), BF16),
        scratch_shapes=[pltpu.VMEM((K, tn), BF16), pltpu.VMEM((K, tn), BF16)],
        compiler_params=_cparams(("arbitrary", "arbitrary")),
        name=name,
    )(oa1, oa2, ob, z_gate, z_gate, wa, wb)


def _outproj_router_kernel(m_ref, h1_ref, h2_ref, w_hbm, nw_ref, wr_ref, br_ref, ho_ref, xn_ref, route_ref, cnt_ref,
                           wb_ref, tri_ref, carry_ref, stage_ref, sem, *, tm, nfirst):
    i = pl.program_id(0)

    @pl.when(i == 0)
    def _():
        _load_weight_bf16(w_hbm, wb_ref, stage_ref, sem)
        rr = lax.broadcasted_iota(jnp.int32, (tm, tm), 0)
        cc = lax.broadcasted_iota(jnp.int32, (tm, tm), 1)
        tri_ref[...] = (rr > cc).astype(BF16)
        carry_ref[...] = jnp.zeros_like(carry_ref)

    x = _pick(i, nfirst, h1_ref, h2_ref) + _dot(m_ref[...], wb_ref[...])
    ho_ref[...] = x
    ms = jnp.mean(x * x, axis=-1, keepdims=True)
    xn = x * lax.rsqrt(ms + EPS) * nw_ref[...]
    xn_ref[...] = _pack_bf16_pairs(xn)
    logits = _dot(xn.astype(BF16), wr_ref[...].astype(BF16)) + br_ref[...]
    lane = lax.broadcasted_iota(jnp.int32, (tm, LANES), 1)
    lanef = lane.astype(F32)
    cur = logits
    vals, hots, eids = [], [], []
    for _ in range(TOP_K):
        m = jnp.max(cur, axis=-1, keepdims=True)
        idx = jnp.min(jnp.where(cur == m, lanef, float(LANES)), axis=-1, keepdims=True)
        hot = lanef == idx
        vals.append(m)
        hots.append(hot)
        eids.append(idx)
        cur = jnp.where(hot, -jnp.inf, cur)
    es = [jnp.exp(v - vals[0]) for v in vals]
    den = es[0] + es[1] + es[2] + es[3]
    multi = jnp.zeros((tm, LANES), F32)
    for hot in hots:
        multi = multi + hot.astype(F32)
    base = carry_ref[0:1, :] + _dot(tri_ref[...], multi.astype(BF16))
    route = jnp.zeros((tm, LANES), F32)
    for k in range(TOP_K):
        e_k = eids[k]
        r_k = jnp.sum(jnp.where(hots[k], base, 0.0), axis=-1, keepdims=True)
        w_k = es[k] / den
        route = route + jnp.where(lane == k, e_k, 0.0) + jnp.where(lane == TOP_K + k, r_k, 0.0) \
            + jnp.where(lane == 2 * TOP_K + k, w_k, 0.0)
    route_ref[...] = route
    carry = carry_ref[0:1, :] + jnp.sum(multi, axis=0, keepdims=True)
    carry_ref[...] = jnp.broadcast_to(carry, carry_ref.shape)
    cnt_ref[...] = jnp.broadcast_to(carry, cnt_ref.shape)


def out_proj_router(merged, h1, h2, w_out, norm_w, w_router, b_router, tm=512, name="out_proj_router"):
    T, K = merged.shape
    D = w_out.shape[1]
    E = w_router.shape[1]
    na = h1.shape[0] // tm
    wr = jnp.pad(w_router, ((0, 0), (0, LANES - E)))
    br = jnp.pad(b_router.reshape(1, E), ((0, 0), (0, LANES - E)), constant_values=NEG_BIG)
    kern = functools.partial(_outproj_router_kernel, tm=tm, nfirst=na)
    return pl.pallas_call(
        kern,
        grid=(T // tm,),
        in_specs=[
            pl.BlockSpec((tm, K), lambda i: (i, 0)),
            *_cat_specs((tm, D), na),
            pl.BlockSpec(memory_space=pl.ANY),
            pl.BlockSpec((1, D), lambda i: (0, 0)),
            pl.BlockSpec((D, LANES), lambda i: (0, 0)),
            pl.BlockSpec((1, LANES), lambda i: (0, 0)),
        ],
        out_specs=[
            pl.BlockSpec((tm, D), lambda i: (i, 0)),
            pl.BlockSpec((tm, D // 2), lambda i: (i, 0)),
            pl.BlockSpec((tm, LANES), lambda i: (i, 0)),
            pl.BlockSpec((8, LANES), lambda i: (0, 0)),
        ],
        out_shape=[jax.ShapeDtypeStruct((T, D), F32),
                   jax.ShapeDtypeStruct((T, D // 2), jnp.uint32),
                   jax.ShapeDtypeStruct((T, LANES), F32),
                   jax.ShapeDtypeStruct((8, LANES), F32)],
        scratch_shapes=[pltpu.VMEM((K, D), BF16), pltpu.VMEM((tm, tm), BF16), pltpu.VMEM((8, LANES), F32),
                        pltpu.VMEM((2, WEIGHT_STAGE_ROWS, D), F32), pltpu.SemaphoreType.DMA((2,))],
        compiler_params=_cparams(("arbitrary",)),
        name=name,
    )(merged, h1, h2, w_out, norm_w.reshape(1, D), wr, br)


ROW_UNROLL = 8


def _pack_bf16_pairs(x):
    h = x.shape[1] // 2
    lo = lax.bitcast_convert_type(x[:, :h].astype(BF16).astype(F32), jnp.uint32)
    hi = lax.bitcast_convert_type(x[:, h:].astype(BF16).astype(F32), jnp.uint32)
    return (hi & jnp.uint32(0xFFFF0000)) | (lo >> 16)


def _unpack_bf16_pairs(w):
    lo = lax.bitcast_convert_type(w << 16, F32).astype(BF16)
    hi = lax.bitcast_convert_type(w & jnp.uint32(0xFFFF0000), F32).astype(BF16)
    return jnp.concatenate([lo, hi], axis=1)


def _dispatch_kernel(dest_ref, zrow_ref, x_ref, buf_ref, zero_ref, xs_ref, sem, zsem, *, tm, n_tail):
    i = pl.program_id(0)
    E = (zrow_ref.shape[0] - 1) // 2
    ZR = zero_ref.shape[0]
    R = buf_ref.shape[0]

    @pl.when(i == 0)
    def _():
        zero_ref[...] = jnp.zeros_like(zero_ref)

        def zero_copy(row, n):
            return pltpu.make_async_copy(zero_ref.at[pl.ds(0, n)], buf_ref.at[pl.ds(pl.multiple_of(row, 8), n)], zsem)

        def zero_rows(act):
            for e in range(E):
                off = zrow_ref[e]
                n = ZR
                while n >= 8:
                    @pl.when((zrow_ref[E + e] & n) != 0)
                    def _():
                        act(zero_copy(off, n))
                    off = off + (zrow_ref[E + e] & n)
                    n //= 2
            for j in range(n_tail):
                row = zrow_ref[2 * E] + j * ZR

                @pl.when(row < R)
                def _():
                    act(zero_copy(row, ZR))

        zero_rows(lambda c: c.start())
        zero_rows(lambda c: c.wait())

    par = i % 2
    xs_ref[par] = x_ref[...]

    def start(rb, c):
        for j in range(ROW_UNROLL):
            r = rb * ROW_UNROLL + j
            for k in range(TOP_K):
                d = dest_ref[(i * tm + r) * TOP_K + k]
                pltpu.make_async_copy(xs_ref.at[par, pl.ds(r, 1)], buf_ref.at[pl.ds(d, 1)],
                                      sem.at[par]).start(priority=k % 2)
        return c

    lax.fori_loop(0, tm // ROW_UNROLL, start, 0)

    def wait_step(s):
        for k in range(TOP_K):
            pltpu.make_async_copy(xs_ref.at[s], buf_ref.at[pl.ds(0, tm)], sem.at[s]).wait()

    @pl.when(i > 0)
    def _():
        wait_step(1 - par)

    @pl.when(i == pl.num_programs(0) - 1)
    def _():
        wait_step(par)


def moe_dispatch(xn, dest_flat, zrows, n_rows, tm=256, name="moe_dispatch"):
    T, H = xn.shape
    n_tail = (n_rows - T * TOP_K) // MOE_TM
    kern = functools.partial(_dispatch_kernel, tm=tm, n_tail=n_tail)
    return pl.pallas_call(
        kern,
        grid_spec=pltpu.PrefetchScalarGridSpec(
            num_scalar_prefetch=2,
            grid=(T // tm,),
            in_specs=[pl.BlockSpec((tm, H), lambda i, d, z: (i, 0))],
            out_specs=pl.BlockSpec(memory_space=pl.ANY),
            scratch_shapes=[pltpu.VMEM((MOE_TM, H), jnp.uint32), pltpu.VMEM((2, tm, H), jnp.uint32),
                            pltpu.SemaphoreType.DMA((2,)), pltpu.SemaphoreType.DMA(())],
        ),
        out_shape=jax.ShapeDtypeStruct((n_rows, H), jnp.uint32),
        compiler_params=_cparams(("arbitrary",)),
        name=name,
    )(dest_flat, zrows, xn)


def _experts_kernel(ue_ref, r0_ref, nt_ref, na_ref, x_hbm, wg_ref, wu_ref, bg_ref, bu_ref, wd_ref, bd_ref,
                    o_hbm, xb_ref, acc_ref, zero_ref, sem_in, sem_out, zsem, *, TM, UT, NF, n_tail):
    TPS = -(-UT // NF)
    u = pl.program_id(0)
    f = pl.program_id(1)
    na = na_ref[0]
    nt = nt_ref[u]
    r0 = r0_ref[u]
    slot = u % 2
    un = jnp.minimum(u + 1, na - 1)

    def prefetch(j):
        return jnp.logical_and(jnp.logical_and(u + 1 < na, j < nt_ref[un]), f == j // TPS)

    def x_copy(base, i, s):
        rows = pl.ds(pl.multiple_of(base + i * TM, TM), TM)
        return pltpu.make_async_copy(x_hbm.at[rows], xb_ref.at[s, i], sem_in.at[i])

    def o_copy(base, i):
        rows = pl.ds(pl.multiple_of(base + i * TM, TM), TM)
        return pltpu.make_async_copy(acc_ref.at[i], o_hbm.at[rows], sem_out.at[i])

    def wait_outputs(unit):
        for i in range(UT):
            @pl.when(i < nt_ref[unit])
            def _():
                o_copy(r0_ref[unit], i).wait()

    @pl.when(u < na)
    def _():
        @pl.when(jnp.logical_and(f == 0, u == 0))
        def _():
            for act in (lambda c: c.start(), lambda c: c.wait()):
                for i in range(UT):
                    @pl.when(i < nt)
                    def _():
                        act(x_copy(r0, i, 0))

        for j in range(UT):
            @pl.when(prefetch(j))
            def _():
                x_copy(r0_ref[un], j, 1 - slot).start()

        up = jnp.maximum(u - 1, 0)

        def wait_prev_output(j):
            @pl.when(jnp.logical_and(u > 0, j < nt_ref[up]))
            def _():
                o_copy(r0_ref[up], j).wait()

        bg = bg_ref[...]
        bu = bu_ref[...]

        def tile_out(xt):
            g = _dot(xt, wg_ref[...].astype(BF16)) + bg
            up = _dot(xt, wu_ref[...].astype(BF16)) + bu
            gate = jnp.minimum(g, SWIGLU_LIMIT)
            lin = jnp.clip(up, -SWIGLU_LIMIT, SWIGLU_LIMIT)
            act = (gate * jax.nn.sigmoid(SWIGLU_ALPHA * gate) * (lin + 1.0)).astype(BF16)
            return _dot(act, wd_ref[...].astype(BF16))

        D = acc_ref.shape[2]

        def update(i, n, first):
            tiles = pl.ds(i, n)
            if first:
                for t in range(n):
                    wait_prev_output(i + t)
            out = tile_out(_unpack_bf16_pairs(xb_ref[slot, tiles].reshape(n * TM, D // 2)))
            if first:
                out = out + bd_ref[...]
            else:
                out = out + acc_ref[tiles].reshape(n * TM, D)
            acc_ref[tiles] = out.reshape(n, TM, D)

            @pl.when(f == NF - 1)
            def _():
                for t in range(n):
                    o_copy(r0, i + t).start()

        def all_tiles(first):
            G = MOE_GROUP

            def body(j, c):
                update(pl.multiple_of(G * j, G), G, first)
                return c
            lax.fori_loop(0, nt // G, body, 0)
            done = (nt // G) * G
            n = G // 2
            while n >= 1:
                @pl.when((nt & n) != 0)
                def _():
                    update(pl.multiple_of(done, n), n, first)
                done = done + (nt & n)
                n //= 2

        @pl.when(f == 0)
        def _():
            all_tiles(True)
            for j in range(UT):
                @pl.when(j >= nt)
                def _():
                    wait_prev_output(j)

        @pl.when(f > 0)
        def _():
            all_tiles(False)

        for j in range(UT):
            @pl.when(prefetch(j))
            def _():
                x_copy(r0_ref[un], j, 1 - slot).wait()

        @pl.when(jnp.logical_and(f == NF - 1, u == na - 1))
        def _():
            zero_ref[...] = jnp.zeros_like(zero_ref)
            for act in (lambda c: c.start(), lambda c: c.wait()):
                for j in range(n_tail):
                    row = na_ref[1] + j * TM

                    @pl.when(row < o_hbm.shape[0])
                    def _():
                        act(pltpu.make_async_copy(zero_ref, o_hbm.at[pl.ds(pl.multiple_of(row, TM), TM)], zsem))
            wait_outputs(u)


def moe_experts(x_buf, plan, n_assign, w_gate_up, b_gate_up, w_down, b_down, name="moe_experts"):
    R, H = x_buf.shape
    D = 2 * H
    n_tail = (R - n_assign) // MOE_TM
    E, _, F2 = w_gate_up.shape
    DF = F2 // 2
    TM, UT, TF = MOE_TM, MOE_UNIT_TILES, MOE_TF
    NF = DF // TF
    ue, r0, nt, na = plan
    U = ue.shape[0]

    def fsel(u, f, na_ref):
        return jnp.where(u < na_ref[0], f, NF - 1)

    kern = functools.partial(_experts_kernel, TM=TM, UT=UT, NF=NF, n_tail=n_tail)
    return pl.pallas_call(
        kern,
        grid_spec=pltpu.PrefetchScalarGridSpec(
            num_scalar_prefetch=4,
            grid=(U, NF),
            in_specs=[
                pl.BlockSpec(memory_space=pl.ANY),
                pl.BlockSpec((None, D, TF), lambda u, f, ue, r0, nt, na: (ue[u], 0, fsel(u, f, na))),
                pl.BlockSpec((None, D, TF), lambda u, f, ue, r0, nt, na: (ue[u], 0, NF + fsel(u, f, na))),
                pl.BlockSpec((None, 1, TF), lambda u, f, ue, r0, nt, na: (ue[u], 0, fsel(u, f, na))),
                pl.BlockSpec((None, 1, TF), lambda u, f, ue, r0, nt, na: (ue[u], 0, NF + fsel(u, f, na))),
                pl.BlockSpec((None, TF, D), lambda u, f, ue, r0, nt, na: (ue[u], fsel(u, f, na), 0)),
                pl.BlockSpec((None, 1, D), lambda u, f, ue, r0, nt, na: (ue[u], 0, 0)),
            ],
            out_specs=pl.BlockSpec(memory_space=pl.ANY),
            scratch_shapes=[
                pltpu.VMEM((2, UT, TM, H), jnp.uint32),
                pltpu.VMEM((UT, TM, D), F32),
                pltpu.VMEM((TM, D), F32),
                pltpu.SemaphoreType.DMA((UT,)),
                pltpu.SemaphoreType.DMA((UT,)),
                pltpu.SemaphoreType.DMA(()),
            ],
        ),
        out_shape=jax.ShapeDtypeStruct((R, D), F32),
        compiler_params=_cparams(("arbitrary", "arbitrary"), VMEM_LIMIT_EXPERTS_BYTES),
        name=name,
    )(ue, r0, nt, na, x_buf, w_gate_up, w_gate_up, b_gate_up.reshape(E, 1, F2), b_gate_up.reshape(E, 1, F2),
      w_down, b_down.reshape(E, 1, D))


def moe_plan(counts, n_tokens):
    TM, UT = MOE_TM, MOE_UNIT_TILES
    E = counts.shape[0]
    nt = (counts + TM - 1) // TM
    padded = nt * TM
    pad_start = jnp.cumsum(padded) - padded
    nu = (nt + UT - 1) // UT
    cu = jnp.cumsum(nu)
    n_units = cu[-1]
    U = E + (n_tokens * TOP_K) // (TM * UT)
    u = jnp.arange(U, dtype=jnp.int32)
    uc = jnp.minimum(u, n_units - 1)
    ue = jnp.clip(jnp.searchsorted(cu, uc, side='right'), 0, E - 1).astype(jnp.int32)
    j = uc - (cu[ue] - nu[ue])
    r0 = (pad_start[ue] + j * (UT * TM)).astype(jnp.int32)
    ntl = jnp.where(u < n_units, jnp.clip(nt[ue] - j * UT, 0, UT), 0).astype(jnp.int32)
    zstart = (pad_start + counts) // 8 * 8
    pad_end = pad_start + padded
    zrows = jnp.concatenate([zstart, pad_end - zstart, pad_end[-1:]]).astype(jnp.int32)
    meta = jnp.stack([n_units, pad_end[-1]]).astype(jnp.int32)
    return pad_start, zrows, (ue, r0, ntl, meta)


def _combine_kernel(dest_ref, h_ref, route_ref, nw_ref, o_hbm, h2_ref, xn_ref, g_ref, sem, *, tm):
    i = pl.program_id(0)
    par = i % 2

    def gather_tile(t, slot):
        def start(rb, c):
            for j in range(ROW_UNROLL):
                r = rb * ROW_UNROLL + j
                for k in range(TOP_K):
                    d = dest_ref[(t * tm + r) * TOP_K + k]
                    pltpu.make_async_copy(o_hbm.at[pl.ds(d, 1)], g_ref.at[slot, k, pl.ds(r, 1)],
                                          sem.at[slot]).start(priority=k % 2)
            return c

        lax.fori_loop(0, tm // ROW_UNROLL, start, 0)

    @pl.when(i == 0)
    def _():
        gather_tile(i, par)

    @pl.when(i + 1 < pl.num_programs(0))
    def _():
        gather_tile(i + 1, 1 - par)

    for k in range(TOP_K):
        pltpu.make_async_copy(o_hbm.at[pl.ds(0, tm)], g_ref.at[par, k], sem.at[par]).wait()

    route = route_ref[...]
    y = g_ref[par, 0] * route[:, 2 * TOP_K:2 * TOP_K + 1]
    for k in range(1, TOP_K):
        y = y + g_ref[par, k] * route[:, 2 * TOP_K + k:2 * TOP_K + k + 1]
    h2 = h_ref[...] + y
    h2_ref[...] = h2
    ms = jnp.mean(h2 * h2, axis=-1, keepdims=True)
    xn_ref[...] = (h2 * lax.rsqrt(ms + EPS) * nw_ref[...]).astype(xn_ref.dtype)


def moe_combine(h, route, dest_flat, out_buf, norm_w, tm=256, name="moe_combine"):
    T, D = h.shape
    kern = functools.partial(_combine_kernel, tm=tm)
    return pl.pallas_call(
        kern,
        grid_spec=pltpu.PrefetchScalarGridSpec(
            num_scalar_prefetch=1,
            grid=(T // tm,),
            in_specs=[pl.BlockSpec((tm, D), lambda i, d: (i, 0)),
                      pl.BlockSpec((tm, LANES), lambda i, d: (i, 0)),
                      pl.BlockSpec((1, D), lambda i, d: (0, 0)),
                      pl.BlockSpec(memory_space=pl.ANY)],
            out_specs=[pl.BlockSpec((tm, D), lambda i, d: (i, 0)),
                       pl.BlockSpec((tm, D), lambda i, d: (i, 0))],
            scratch_shapes=[pltpu.VMEM((2, TOP_K, tm, D), F32), pltpu.SemaphoreType.DMA((2,))],
        ),
        out_shape=[jax.ShapeDtypeStruct((T, D), F32), jax.ShapeDtypeStruct((T, D), BF16)],
        compiler_params=_cparams(("arbitrary",)),
        name=name,
    )(dest_flat, h, route, norm_w.reshape(1, D), out_buf)


def _ple_kernel(xn_ref, p1_ref, p2_ref, h_ref, wg_hbm, wp_hbm, nf_ref, oa_ref, ob_ref, wgb_ref, wpb_ref, stage_ref, sem,
                *, nfirst):
    i = pl.program_id(0)

    @pl.when(i == 0)
    def _():
        _load_weight_bf16(wg_hbm, wgb_ref, stage_ref, sem)
        _load_weight_bf16(wp_hbm, wpb_ref, stage_ref, sem)

    gate = jax.nn.sigmoid(_dot(xn_ref[...], wgb_ref[...]))
    p = _pick(i, nfirst, p1_ref, p2_ref).astype(BF16)
    y = _rms(h_ref[...] + _dot(p, wpb_ref[...]) * gate, nf_ref[...])

    @pl.when(i < nfirst)
    def _():
        oa_ref[...] = y

    @pl.when(i >= nfirst)
    def _():
        ob_ref[...] = y


def ple_final(xn, p1, p2, h, w_gate, w_ple, norm_w, tm=512, name="ple_final"):
    T, D = h.shape
    Ta, P = p1.shape
    na = Ta // tm
    return pl.pallas_call(
        functools.partial(_ple_kernel, nfirst=na),
        grid=(T // tm,),
        in_specs=[
            pl.BlockSpec((tm, D), lambda i: (i, 0)),
            *_cat_specs((tm, P), na),
            pl.BlockSpec((tm, D), lambda i: (i, 0)),
            pl.BlockSpec(memory_space=pl.ANY),
            pl.BlockSpec(memory_space=pl.ANY),
            pl.BlockSpec((1, D), lambda i: (0, 0)),
        ],
        out_specs=list(_cat_specs((tm, D), na)),
        out_shape=[jax.ShapeDtypeStruct((Ta, D), F32), jax.ShapeDtypeStruct((T - Ta, D), F32)],
        scratch_shapes=[pltpu.VMEM((D, D), BF16), pltpu.VMEM((P, D), BF16),
                        pltpu.VMEM((2, WEIGHT_STAGE_ROWS, D), F32), pltpu.SemaphoreType.DMA((2,))],
        compiler_params=_cparams(("arbitrary",)),
        name=name,
    )(xn, p1, p2, h, w_gate, w_ple, norm_w.reshape(1, D))


def kernel(x_prompt, x_sample, p_prompt, p_sample, state_hgrn, state_ssm_re, state_ssm_im, norm_mix, w_in, hg_lb, hg_gnorm, w_branch_a, ssm_a_re, ssm_a_im, ssm_log_dt, ssm_b_re, ssm_b_im, ssm_c_re, ssm_c_im, ssm_d, w_glu, w_branch_b, w_out, norm_moe, w_router, b_router, w_gate_up, b_gate_up, w_down, b_down, norm_ple, w_ple, w_ple_gate, norm_final):
    BP, LP, D = x_prompt.shape
    BS, LS, _ = x_sample.shape
    depth = w_in.shape[0]
    assert depth == 1
    TP, TS = BP * LP, BS * LS
    T = TP + TS
    G, P = ssm_a_re.shape[1:]
    W = SSM_GROUP * G

    xp = x_prompt.reshape(TP, D)
    xs = x_sample.reshape(TS, D)

    xn = rmsnorm_cat(xp, xs, norm_mix[0], BF16, name="norm_mix")
    tm = 1024
    w_in0 = w_in[0]
    z_hg = project(xn, w_in0, 0, T // tm, 0, 4, tm, 1024, "proj_hgrn")
    u_p = project(xn, w_in0, 0, TP // tm, 4, 1, tm, 1024, "proj_u_prompt")
    u_s = project(xn, w_in0, TP // tm, TS // tm, 4, 1, tm, 1024, "proj_u_sample")
    z_gate = project(xn, w_in0, 0, T // tm, 5, 4, tm, 1024, "proj_gates", out_dtype=BF16)

    lb = jax.nn.softmax(hg_lb.astype(F32), axis=0)[0]
    og_p, hg_p = hgrn2(z_hg, lb, hg_gnorm[0], None, row0=0, B=BP, L=LP, C=HG_CHUNK, nb=1, tok_step=512,
                       name="hgrn_prompt")
    og_s, hg_s = hgrn2(z_hg, lb, hg_gnorm[0], state_hgrn[0], row0=TP, B=BS, L=LS, C=LS, nb=8, tok_step=LS,
                       name="hgrn_sample")

    bd, cd, ab = s5_params(ssm_a_re[0], ssm_a_im[0], ssm_log_dt[0], ssm_b_re[0], ssm_b_im[0],
                           ssm_c_re[0], ssm_c_im[0])
    y_p, re_p, im_p = s5_prompt(u_p.reshape(BP, LP, W), bd, cd, ab, ssm_d[0], TC=512, name="s5_prompt")
    y_s, re_s, im_s = s5_sample(u_s, bd, cd, ab, ssm_d[0], state_ssm_re[0].reshape(BS, G * P),
                                state_ssm_im[0].reshape(BS, G * P), B=BS, L=LS, name="s5_sample")

    glu = gelu_glu(y_p.reshape(TP, W), y_s, w_glu[0])
    merged = gated_merge(og_p, og_s, glu, z_gate, w_branch_a[0], w_branch_b[0])

    h1, xn2, route, cnt = out_proj_router(merged, xp, xs, w_out[0], norm_moe[0], w_router[0], b_router[0])
    e_idx = route[:, 0:TOP_K].astype(jnp.int32)
    rank = route[:, TOP_K:2 * TOP_K].astype(jnp.int32)
    counts = cnt[0, :N_EXPERTS].astype(jnp.int32)
    pad_start, zrows, plan = moe_plan(counts, T)
    onehot = e_idx[..., None] == jnp.arange(N_EXPERTS, dtype=jnp.int32)
    dest = (jnp.sum(jnp.where(onehot, pad_start, 0), axis=-1) + rank).astype(jnp.int32).reshape(T * TOP_K)
    n_rows = T * TOP_K + N_EXPERTS * MOE_TM
    x_buf = moe_dispatch(xn2, dest, zrows, n_rows)
    out_buf = moe_experts(x_buf, plan, T * TOP_K, w_gate_up[0], b_gate_up[0], w_down[0], b_down[0])
    h2, xn3 = moe_combine(h1, route, dest, out_buf, norm_ple[0])

    y_p2, y_s2 = ple_final(xn3, p_prompt[0].reshape(TP, -1), p_sample[0].reshape(TS, -1), h2, w_ple_gate[0],
                           w_ple[0], norm_final)

    y_prompt = y_p2.reshape(BP, LP, D)
    y_sample = y_s2.reshape(BS, LS, D)
    hs = (1, BP, HG_HEADS, HG_DK, HG_DV)
    return (y_prompt, y_sample,
            hg_p.reshape(hs), re_p.reshape(1, BP, G, P), im_p.reshape(1, BP, G, P),
            hg_s.reshape(1, BS, HG_HEADS, HG_DK, HG_DV), re_s.reshape(1, BS, G, P), im_s.reshape(1, BS, G, P))
```

```python
import functools

import jax
import jax.numpy as jnp
from jax import lax
from jax.experimental import pallas as pl
from jax.experimental.pallas import tpu as pltpu

F32 = jnp.float32
BF16 = jnp.bfloat16
EPS = 1e-6

HG_HEADS = 8
HG_DK = 128
HG_DV = 128
HG_CHUNK = 32
SSM_GROUP = 16
N_EXPERTS = 32
TOP_K = 4
SWIGLU_LIMIT = 7.0
SWIGLU_ALPHA = 1.702

LANES = 128
VMEM_LIMIT_BYTES = 56 * 1024 * 1024
VMEM_LIMIT_EXPERTS_BYTES = 61 * 1024 * 1024

MOE_TM = 128
MOE_UNIT_TILES = 12
MOE_GROUP = 8
MOE_TF = 512
NEG_BIG = -1e30


def _cparams(sem, vmem_limit=VMEM_LIMIT_BYTES):
    return pltpu.CompilerParams(dimension_semantics=sem, vmem_limit_bytes=vmem_limit)


def _dot(a, b):
    return jnp.dot(a, b, preferred_element_type=F32)


def _split2(a):
    hi = a.astype(BF16)
    lo = (a - hi.astype(F32)).astype(BF16)
    return hi, lo


def _dot_x3(a, b_split):
    a1, a2 = _split2(a)
    b1, b2 = b_split
    return _dot(a1, b1) + (_dot(a1, b2) + _dot(a2, b1))


WEIGHT_STAGE_ROWS = 256


def _load_weight_bf16(w_hbm, wb_ref, stage_ref, sem):
    K = w_hbm.shape[0]
    rows = min(stage_ref.shape[1], K)
    n = K // rows

    def copy(c):
        return pltpu.make_async_copy(w_hbm.at[pl.ds(c * rows, rows)], stage_ref.at[c % 2, pl.ds(0, rows)],
                                     sem.at[c % 2])

    copy(0).start()
    for c in range(n):
        if c + 1 < n:
            copy(c + 1).start()
        copy(c).wait()
        wb_ref[pl.ds(c * rows, rows), :] = stage_ref[c % 2, pl.ds(0, rows), :].astype(BF16)


def _cat_specs(shape, nfirst, row_axis=0, col_axis=None):
    def col(ids):
        return 0 if col_axis is None else ids[col_axis]

    first = pl.BlockSpec(shape, lambda *ids: (jnp.minimum(ids[row_axis], nfirst - 1), col(ids)))
    second = pl.BlockSpec(shape, lambda *ids: (jnp.maximum(ids[row_axis] - nfirst, 0), col(ids)))
    return first, second


def _pick(i, nfirst, a_ref, b_ref):
    return jnp.where(i < nfirst, a_ref[...], b_ref[...])


def _rms(x, w):
    ms = jnp.mean(x * x, axis=-1, keepdims=True)
    return x * lax.rsqrt(ms + EPS) * w


def _rms_in2_kernel(a_ref, b_ref, w_ref, o_ref, *, nfirst):
    x = _pick(pl.program_id(0), nfirst, a_ref, b_ref)
    o_ref[...] = _rms(x, w_ref[...]).astype(o_ref.dtype)


def rmsnorm_cat(xa, xb, w, out_dtype, tm=512, name="rmsnorm"):
    (Ta, D), Tb = xa.shape, xb.shape[0]
    na = Ta // tm
    return pl.pallas_call(
        functools.partial(_rms_in2_kernel, nfirst=na),
        grid=((Ta + Tb) // tm,),
        in_specs=[*_cat_specs((tm, D), na), pl.BlockSpec((1, D), lambda i: (0, 0))],
        out_specs=pl.BlockSpec((tm, D), lambda i: (i, 0)),
        out_shape=jax.ShapeDtypeStruct((Ta + Tb, D), out_dtype),
        compiler_params=_cparams(("arbitrary",)),
        name=name,
    )(xa, xb, w.reshape(1, D))


def _proj_kernel(x_ref, w_ref, o_ref, wb_ref):
    @pl.when(pl.program_id(1) == 0)
    def _():
        wb_ref[...] = w_ref[...].astype(BF16)

    o_ref[...] = _dot(x_ref[...], wb_ref[...]).astype(o_ref.dtype)


def project(xn, w, row_blk0, n_row_blks, col_blk0, n_col_blks, tm, tn, name, out_dtype=F32):
    K = xn.shape[1]
    return pl.pallas_call(
        _proj_kernel,
        grid=(n_col_blks, n_row_blks),
        in_specs=[
            pl.BlockSpec((tm, K), lambda j, i: (row_blk0 + i, 0)),
            pl.BlockSpec((K, tn), lambda j, i: (0, col_blk0 + j)),
        ],
        out_specs=pl.BlockSpec((tm, tn), lambda j, i: (i, j)),
        out_shape=jax.ShapeDtypeStruct((n_row_blks * tm, n_col_blks * tn), out_dtype),
        scratch_shapes=[pltpu.VMEM((K, tn), BF16)],
        compiler_params=_cparams(("arbitrary", "arbitrary")),
        name=name,
    )(xn, w)


HGRN_CHUNKS_PER_TRIP = 4


def _hgrn_kernel(*refs, C, nchunk, nb, tok_step, has_s0, state_t):
    if has_s0:
        q_ref, f_ref, v_ref, g_ref, lb_ref, gn_ref, s0_ref, og_ref, so_ref, st_ref = refs
    else:
        q_ref, f_ref, v_ref, g_ref, lb_ref, gn_ref, og_ref, so_ref, st_ref = refs
        s0_ref = None
    H, DK, DV = HG_HEADS, HG_DK, HG_DV
    step = pl.program_id(1)

    @pl.when(step == 0)
    def _():
        for i in range(nb):
            for h in range(H):
                if has_s0:
                    st_ref[i, h] = s0_ref[i, h].T if state_t else s0_ref[i, h]
                else:
                    st_ref[i, h] = jnp.zeros(st_ref.shape[2:], F32)

    lbv = lb_ref[...]
    gnv = gn_ref[...]
    rr = lax.broadcasted_iota(jnp.int32, (C, C), 0)
    cc = lax.broadcasted_iota(jnp.int32, (C, C), 1)
    causal = rr >= cc
    tri = causal.astype(BF16)
    sls =[slice(h * DK, (h + 1) * DK) for h in range(H)]
    nt_dims = (((1,), (1,)), ((), ()))
    tn_dims = (((0,), (0,)), ((), ()))

    def tn(a, b):
        return lax.dot_general(a, b, tn_dims, preferred_element_type=F32)

    def prepare(r0):
        rows = pl.ds(r0, C)
        zq = q_ref[rows, :]
        zf = f_ref[rows, :]
        q = jax.nn.silu(zq)
        fe = lbv + (1.0 - lbv) * jax.nn.sigmoid(zf)
        k = 1.0 - fe
        gl = jnp.log(fe)
        g1 = gl.astype(BF16)
        r1 = gl - g1.astype(F32)
        g2 = r1.astype(BF16)
        g3 = (r1 - g2.astype(F32)).astype(BF16)
        b = _dot(tri, g1) + (_dot(tri, g2) + _dot(tri, g3))
        blast = b[C - 1:C, :]
        return dict(
            qt=(q * jnp.exp(b)).astype(BF16),
            kt=(k * jnp.exp(-b)).astype(BF16),
            kend=(k * jnp.exp(blast - b)).astype(BF16),
            eblast=jnp.exp(blast),
            vb=v_ref[rows, :].astype(BF16),
            gate=jax.nn.silu(g_ref[rows, :]),
        )

    def run(jobs):
        P = [prepare(r0) for _, r0 in jobs]
        atts = [[lax.dot_general(p["qt"][:, sl], p["kt"][:, sl], nt_dims, preferred_element_type=F32)
                 for sl in sls] for p in P]
        if state_t:
            upds = [[tn(p["vb"][:, sl], p["kend"][:, sl]) for sl in sls] for p in P]
            decs = [[p["eblast"][:, sl] for sl in sls] for p in P]
        else:
            upds = [[tn(p["kend"][:, sl], p["vb"][:, sl]) for sl in sls] for p in P]
            decs = [[jnp.broadcast_to(p["eblast"][:, sl], (8, DK)).T[:, 0:1] for sl in sls] for p in P]
        state = {}
        inters = []
        for j, (i, _) in enumerate(jobs):
            row = []
            for h, sl in enumerate(sls):
                s = state[(i, h)] if (i, h) in state else st_ref[i, h]
                qth = P[j]["qt"][:, sl]
                if state_t:
                    row.append(lax.dot_general(qth, s.astype(BF16), nt_dims, preferred_element_type=F32))
                else:
                    row.append(_dot(qth, s.astype(BF16)))
                state[(i, h)] = s * decs[j][h] + upds[j][h]
            inters.append(row)
        for (i, h), s in state.items():
            st_ref[i, h] = s
        outs = []
        for j, p in enumerate(P):
            cols = []
            for h, sl in enumerate(sls):
                att = jnp.where(causal, atts[j][h], 0.0).astype(BF16)
                o = _dot(att, p["vb"][:, sl]) + inters[j][h]
                ms = jnp.mean(o * o, axis=-1, keepdims=True)
                cols.append(o * lax.rsqrt(ms + EPS) * gnv * p["gate"][:, sl])
            outs.append(jnp.concatenate(cols, axis=1))
        return outs

    if nchunk == 1:
        outs = run([(i, i * tok_step) for i in range(nb)])
        og_ref[...] = jnp.concatenate(outs, axis=0).astype(og_ref.dtype)
    else:
        assert nb == 1 and nchunk % HGRN_CHUNKS_PER_TRIP == 0

        def body(t, carry):
            r0s = [pl.multiple_of((HGRN_CHUNKS_PER_TRIP * t + j) * C, C) for j in range(HGRN_CHUNKS_PER_TRIP)]
            outs = run([(0, r0) for r0 in r0s])
            for r0, o in zip(r0s, outs):
                og_ref[pl.ds(r0, C), :] = o.astype(og_ref.dtype)
            return carry

        lax.fori_loop(0, nchunk // HGRN_CHUNKS_PER_TRIP, body, 0)

    @pl.when(step == pl.num_programs(1) - 1)
    def _():
        for i in range(nb):
            for h in range(H):
                so_ref[i, h] = st_ref[i, h].T if state_t else st_ref[i, h]


def hgrn2(z_hg, lb, gn, s0, *, row0, B, L, C, nb, tok_step, name):
    H, DK, DV = HG_HEADS, HG_DK, HG_DV
    W = H * DK
    steps = L // tok_step
    rows = nb * tok_step
    nchunk = tok_step // C
    blk0 = row0 // rows
    has_s0 = s0 is not None

    def zspec(col):
        return pl.BlockSpec((rows, W), lambda bb, s: (blk0 + bb * steps + s, col))

    in_specs = [zspec(0), zspec(1), zspec(2), zspec(3),
                pl.BlockSpec((1, W), lambda bb, s: (0, 0)),
                pl.BlockSpec((1, DV), lambda bb, s: (0, 0))]
    args = [z_hg, z_hg, z_hg, z_hg, lb.reshape(1, W), gn.reshape(1, DV)]
    if has_s0:
        in_specs.append(pl.BlockSpec((nb, H, DK, DV), lambda bb, s: (bb, 0, 0, 0)))
        args.append(s0)
    state_t = L > C
    kern = functools.partial(_hgrn_kernel, C=C, nchunk=nchunk, nb=nb, tok_step=tok_step, has_s0=has_s0,
                             state_t=state_t)
    return pl.pallas_call(
        kern,
        grid=(B // nb, steps),
        in_specs=in_specs,
        out_specs=[pl.BlockSpec((rows, W), lambda bb, s: (bb * steps + s, 0)),
                   pl.BlockSpec((nb, H, DK, DV), lambda bb, s: (bb, 0, 0, 0))],
        out_shape=[jax.ShapeDtypeStruct((B * L, W), BF16),
                   jax.ShapeDtypeStruct((B, H, DK, DV), F32)],
        scratch_shapes=[pltpu.VMEM((nb, H, DV, DK), F32)],
        compiler_params=_cparams(("arbitrary", "arbitrary")),
        name=name,
    )(*args)


def _s5_step(ar, ai, xr, xi, br, bi):
    return ar * xr - ai * xi + br, ar * xi + ai * xr + bi


S5_STEPS_PER_TRIP = 8
S5_GROUPS_PER_BLOCK = 16


def _s5_prompt_kernel(u_ref, bd_ref, cd_ref, a_ref, d_ref, y_ref, xr_ref, xi_ref, X_ref, st_ref, *, B, TC):
    NP = X_ref.shape[0] // 2
    assert 2 * B == 8
    tc = pl.program_id(1)

    def rows_of(b, s):
        return pl.ds(2 * b + s, TC, stride=8)

    @pl.when(tc == 0)
    def _():
        st_ref[...] = jnp.zeros_like(st_ref)

    bd = bd_ref[...].astype(BF16)
    for b in range(B):
        bu = _dot(u_ref[b].astype(BF16), bd)
        for c in range(4 * NP):
            X_ref[c // 2, rows_of(b, c % 2), :] = bu[:, c * LANES:(c + 1) * LANES]

    odd = lax.broadcasted_iota(jnp.int32, (8, LANES), 0) % 2 == 1

    def pair_const(row, p):
        lo = a_ref[row:row + 1, (2 * p) * LANES:(2 * p + 1) * LANES]
        hi = a_ref[row:row + 1, (2 * p + 1) * LANES:(2 * p + 2) * LANES]
        return jnp.where(odd, hi, lo)

    ars = [pair_const(0, p) for p in range(NP)]
    ais = [pair_const(1, p) for p in range(NP)]

    def body(i, carry):
        carry = list(carry)
        for j in range(S5_STEPS_PER_TRIP):
            rows = pl.ds(pl.multiple_of((i * S5_STEPS_PER_TRIP + j) * 8, 8), 8)
            for p in range(NP):
                xr, xi = _s5_step(ars[p], ais[p], carry[p][0], carry[p][1], X_ref[p, rows, :], X_ref[NP + p, rows, :])
                X_ref[p, rows, :] = xr
                X_ref[NP + p, rows, :] = xi
                carry[p] = (xr, xi)
        return tuple(carry)

    init = tuple((st_ref[p], st_ref[NP + p]) for p in range(NP))
    fin = lax.fori_loop(0, TC // S5_STEPS_PER_TRIP, body, init)
    for p in range(NP):
        st_ref[p] = fin[p][0]
        st_ref[NP + p] = fin[p][1]

    cd = cd_ref[...].astype(BF16)
    dv = d_ref[...]
    for b in range(B):
        xb = jnp.concatenate([X_ref[c // 2, rows_of(b, c % 2), :] for c in range(4 * NP)], axis=1).astype(BF16)
        y_ref[b] = _dot(xb, cd) + dv * u_ref[b]

    @pl.when(tc == pl.num_programs(1) - 1)
    def _():
        for c in range(2 * NP):
            cols = slice(c * LANES, (c + 1) * LANES)
            xr_ref[:, cols] = st_ref[c // 2, pl.ds(c % 2, B, stride=2), :]
            xi_ref[:, cols] = st_ref[NP + c // 2, pl.ds(c % 2, B, stride=2), :]


def s5_prompt(u, bd, cd, ab, dskip, *, TC, name):
    B, L, W = u.shape
    GB, UB, SW = bd.shape
    half = SW // 2
    kern = functools.partial(_s5_prompt_kernel, B=B, TC=TC)
    return pl.pallas_call(
        kern,
        grid=(GB, L // TC),
        in_specs=[
            pl.BlockSpec((B, TC, UB), lambda g, t: (0, t, g)),
            pl.BlockSpec((None, UB, SW), lambda g, t: (g, 0, 0)),
            pl.BlockSpec((None, SW, UB), lambda g, t: (g, 0, 0)),
            pl.BlockSpec((None, 2, half), lambda g, t: (g, 0, 0)),
            pl.BlockSpec((1, UB), lambda g, t: (0, g)),
        ],
        out_specs=[
            pl.BlockSpec((B, TC, UB), lambda g, t: (0, t, g)),
            pl.BlockSpec((B, half), lambda g, t: (0, g)),
            pl.BlockSpec((B, half), lambda g, t: (0, g)),
        ],
        out_shape=[jax.ShapeDtypeStruct((B, L, W), F32),
                   jax.ShapeDtypeStruct((B, GB * half), F32),
                   jax.ShapeDtypeStruct((B, GB * half), F32)],
        scratch_shapes=[pltpu.VMEM((SW // LANES // 2, 8 * TC, LANES), F32),
                        pltpu.VMEM((SW // LANES // 2, 8, LANES), F32)],
        compiler_params=_cparams(("arbitrary", "arbitrary")),
        name=name,
    )(u, bd, cd, ab, dskip.reshape(1, W))


def _s5_sample_kernel(u_ref, bd_ref, cd_ref, a_ref, d_ref, x0r_ref, x0i_ref, y_ref, xr_ref, xi_ref, X_ref, *, B, L):
    NC = X_ref.shape[0]
    NH = NC // 2
    u = u_ref[...]
    bu = _dot_x3(u, _split2(bd_ref[...]))
    for c in range(NC):
        X_ref[c] = bu[:, c * LANES:(c + 1) * LANES]
    for c in range(NH):
        cols = slice(c * LANES, (c + 1) * LANES)
        ar = a_ref[0:1, cols]
        ai = a_ref[1:2, cols]
        xr = x0r_ref[:, cols]
        xi = x0i_ref[:, cols]
        for t in range(L):
            rows = pl.ds(t, B, stride=L)
            xr, xi = _s5_step(ar, ai, xr, xi, X_ref[c, rows, :], X_ref[NH + c, rows, :])
            X_ref[c, rows, :] = xr
            X_ref[NH + c, rows, :] = xi
        xr_ref[:, cols] = xr
        xi_ref[:, cols] = xi
    xs = jnp.concatenate([X_ref[c] for c in range(NC)], axis=1).astype(BF16)
    y_ref[...] = _dot(xs, cd_ref[...].astype(BF16)) + d_ref[...] * u


def s5_sample(u, bd, cd, ab, dskip, x0r, x0i, *, B, L, name):
    T, W = u.shape
    GB, UB, SW = bd.shape
    half = SW // 2
    kern = functools.partial(_s5_sample_kernel, B=B, L=L)
    return pl.pallas_call(
        kern,
        grid=(GB,),
        in_specs=[
            pl.BlockSpec((T, UB), lambda g: (0, g)),
            pl.BlockSpec((None, UB, SW), lambda g: (g, 0, 0)),
            pl.BlockSpec((None, SW, UB), lambda g: (g, 0, 0)),
            pl.BlockSpec((None, 2, half), lambda g: (g, 0, 0)),
            pl.BlockSpec((1, UB), lambda g: (0, g)),
            pl.BlockSpec((B, half), lambda g: (0, g)),
            pl.BlockSpec((B, half), lambda g: (0, g)),
        ],
        out_specs=[
            pl.BlockSpec((T, UB), lambda g: (0, g)),
            pl.BlockSpec((B, half), lambda g: (0, g)),
            pl.BlockSpec((B, half), lambda g: (0, g)),
        ],
        out_shape=[jax.ShapeDtypeStruct((T, W), F32),
                   jax.ShapeDtypeStruct((B, GB * half), F32),
                   jax.ShapeDtypeStruct((B, GB * half), F32)],
        scratch_shapes=[pltpu.VMEM((SW // LANES, T, LANES), F32)],
        compiler_params=_cparams(("arbitrary",)),
        name=name,
    )(u, bd, cd, ab, dskip.reshape(1, W), x0r, x0i)


def s5_params(a_re, a_im, log_dt, b_re, b_im, c_re, c_im):
    G, P = a_re.shape
    gpb = S5_GROUPS_PER_BLOCK
    GB = G // gpb
    A = lax.complex(a_re, a_im)
    dt = jnp.exp(log_dt)[:, None]
    A_bar = jnp.exp(A * dt)
    Bm = lax.complex(b_re, b_im)
    B_bar = ((A_bar - 1.0) / A)[..., None] * Bm
    eye = jnp.eye(gpb, dtype=F32)

    def in_map(m):
        m = m.reshape(GB, gpb, P, SSM_GROUP)
        return jnp.einsum('bgpc,gh->bgchp', m, eye).reshape(GB, gpb * SSM_GROUP, gpb * P)

    def out_map(m):
        m = m.reshape(GB, gpb, SSM_GROUP, P)
        return jnp.einsum('bgcp,gh->bgphc', m, eye).reshape(GB, gpb * P, gpb * SSM_GROUP)

    bd = jnp.concatenate([in_map(B_bar.real), in_map(B_bar.imag)], axis=2)
    cd = jnp.concatenate([out_map(c_re), out_map(-c_im)], axis=1)
    ab = jnp.stack([A_bar.real.reshape(GB, gpb * P), A_bar.imag.reshape(GB, gpb * P)], axis=1)
    return bd, cd, ab


def _glu_kernel(ya_ref, yb_ref, w_ref, o_ref, wb_ref, *, nfirst):
    i = pl.program_id(0)

    @pl.when(i == 0)
    def _():
        wb_ref[...] = w_ref[...].astype(BF16)

    zg = jax.nn.gelu(_pick(i, nfirst, ya_ref, yb_ref))
    o_ref[...] = (zg * jax.nn.sigmoid(_dot(zg.astype(BF16), wb_ref[...]))).astype(o_ref.dtype)


def gelu_glu(ya, yb, w, tm=1024, name="gelu_glu"):
    (Ta, W), Tb = ya.shape, yb.shape[0]
    na = Ta // tm
    return pl.pallas_call(
        functools.partial(_glu_kernel, nfirst=na),
        grid=((Ta + Tb) // tm,),
        in_specs=[*_cat_specs((tm, W), na), pl.BlockSpec((W, W), lambda i: (0, 0))],
        out_specs=pl.BlockSpec((tm, W), lambda i: (i, 0)),
        out_shape=jax.ShapeDtypeStruct((Ta + Tb, W), BF16),
        scratch_shapes=[pltpu.VMEM((W, W), BF16)],
        compiler_params=_cparams(("arbitrary",)),
        name=name,
    )(ya, yb, w)


def _merge_kernel(oa1_ref, oa2_ref, ob_ref, ga_ref, gb_ref, wa_ref, wb_ref, o_ref, wab_ref, wbb_ref, *, nfirst):
    i = pl.program_id(1)

    @pl.when(i == 0)
    def _():
        wab_ref[...] = wa_ref[...].astype(BF16)
        wbb_ref[...] = wb_ref[...].astype(BF16)

    ya = _dot(_pick(i, nfirst, oa1_ref, oa2_ref), wab_ref[...])
    yb = _dot(ob_ref[...], wbb_ref[...])
    ga = jax.nn.sigmoid(ga_ref[...].astype(F32))
    gb = jax.nn.sigmoid(gb_ref[...].astype(F32))
    o_ref[...] = (ga * ya + gb * yb).astype(o_ref.dtype)


def gated_merge(oa1, oa2, ob, z_gate, wa, wb, tm=1024, tn=1024, name="gated_merge"):
    T, K = ob.shape
    N = wa.shape[1]
    nj = N // tn
    na = oa1.shape[0] // tm
    return pl.pallas_call(
        functools.partial(_merge_kernel, nfirst=na),
        grid=(nj, T // tm),
        in_specs=[
            *_cat_specs((tm, K), na, row_axis=1),
            pl.BlockSpec((tm, K), lambda j, i: (i, 0)),
            pl.BlockSpec((tm, tn), lambda j, i: (i, j)),
            pl.BlockSpec((tm, tn), lambda j, i: (i, nj + j)),
            pl.BlockSpec((K, tn), lambda j, i: (0, j)),
            pl.BlockSpec((K, tn), lambda j, i: (0, j)),
        ],
        out_specs=pl.BlockSpec((tm, tn), lambda j, i: (i, j)),
        out_shape=jax.ShapeDtypeStruct((T, N), BF16),
        scratch_shapes=[pltpu.VMEM((K, tn), BF16), pltpu.VMEM((K, tn), BF16)],
        compiler_params=_cparams(("arbitrary", "arbitrary")),
        name=name,
    )(oa1, oa2, ob, z_gate, z_gate, wa, wb)


def _outproj_router_kernel(m_ref, h1_ref, h2_ref, w_hbm, nw_ref, wr_ref, br_ref, ho_ref, xn_ref, route_ref, cnt_ref,
                           wb_ref, tri_ref, carry_ref, stage_ref, sem, *, tm, nfirst):
    i = pl.program_id(0)

    @pl.when(i == 0)
    def _():
        _load_weight_bf16(w_hbm, wb_ref, stage_ref, sem)
        rr = lax.broadcasted_iota(jnp.int32, (tm, tm), 0)
        cc = lax.broadcasted_iota(jnp.int32, (tm, tm), 1)
        tri_ref[...] = (rr > cc).astype(BF16)
        carry_ref[...] = jnp.zeros_like(carry_ref)

    x = _pick(i, nfirst, h1_ref, h2_ref) + _dot(m_ref[...], wb_ref[...])
    ho_ref[...] = x
    ms = jnp.mean(x * x, axis=-1, keepdims=True)
    xn = x * lax.rsqrt(ms + EPS) * nw_ref[...]
    xn_ref[...] = _pack_bf16_pairs(xn)
    logits = _dot(xn.astype(BF16), wr_ref[...].astype(BF16)) + br_ref[...]
    lane = lax.broadcasted_iota(jnp.int32, (tm, LANES), 1)
    lanef = lane.astype(F32)
    cur = logits
    vals, hots, eids = [], [], []
    for _ in range(TOP_K):
        m = jnp.max(cur, axis=-1, keepdims=True)
        idx = jnp.min(jnp.where(cur == m, lanef, float(LANES)), axis=-1, keepdims=True)
        hot = lanef == idx
        vals.append(m)
        hots.append(hot)
        eids.append(idx)
        cur = jnp.where(hot, -jnp.inf, cur)
    es = [jnp.exp(v - vals[0]) for v in vals]
    den = es[0] + es[1] + es[2] + es[3]
    multi = jnp.zeros((tm, LANES), F32)
    for hot in hots:
        multi = multi + hot.astype(F32)
    base = carry_ref[0:1, :] + _dot(tri_ref[...], multi.astype(BF16))
    route = jnp.zeros((tm, LANES), F32)
    for k in range(TOP_K):
        e_k = eids[k]
        r_k = jnp.sum(jnp.where(hots[k], base, 0.0), axis=-1, keepdims=True)
        w_k = es[k] / den
        route = route + jnp.where(lane == k, e_k, 0.0) + jnp.where(lane == TOP_K + k, r_k, 0.0) \
            + jnp.where(lane == 2 * TOP_K + k, w_k, 0.0)
    route_ref[...] = route
    carry = carry_ref[0:1, :] + jnp.sum(multi, axis=0, keepdims=True)
    carry_ref[...] = jnp.broadcast_to(carry, carry_ref.shape)
    cnt_ref[...] = jnp.broadcast_to(carry, cnt_ref.shape)


def out_proj_router(merged, h1, h2, w_out, norm_w, w_router, b_router, tm=512, name="out_proj_router"):
    T, K = merged.shape
    D = w_out.shape[1]
    E = w_router.shape[1]
    na = h1.shape[0] // tm
    wr = jnp.pad(w_router, ((0, 0), (0, LANES - E)))
    br = jnp.pad(b_router.reshape(1, E), ((0, 0), (0, LANES - E)), constant_values=NEG_BIG)
    kern = functools.partial(_outproj_router_kernel, tm=tm, nfirst=na)
    return pl.pallas_call(
        kern,
        grid=(T // tm,),
        in_specs=[
            pl.BlockSpec((tm, K), lambda i: (i, 0)),
            *_cat_specs((tm, D), na),
            pl.BlockSpec(memory_space=pl.ANY),
            pl.BlockSpec((1, D), lambda i: (0, 0)),
            pl.BlockSpec((D, LANES), lambda i: (0, 0)),
            pl.BlockSpec((1, LANES), lambda i: (0, 0)),
        ],
        out_specs=[
            pl.BlockSpec((tm, D), lambda i: (i, 0)),
            pl.BlockSpec((tm, D // 2), lambda i: (i, 0)),
            pl.BlockSpec((tm, LANES), lambda i: (i, 0)),
            pl.BlockSpec((8, LANES), lambda i: (0, 0)),
        ],
        out_shape=[jax.ShapeDtypeStruct((T, D), F32),
                   jax.ShapeDtypeStruct((T, D // 2), jnp.uint32),
                   jax.ShapeDtypeStruct((T, LANES), F32),
                   jax.ShapeDtypeStruct((8, LANES), F32)],
        scratch_shapes=[pltpu.VMEM((K, D), BF16), pltpu.VMEM((tm, tm), BF16), pltpu.VMEM((8, LANES), F32),
                        pltpu.VMEM((2, WEIGHT_STAGE_ROWS, D), F32), pltpu.SemaphoreType.DMA((2,))],
        compiler_params=_cparams(("arbitrary",)),
        name=name,
    )(merged, h1, h2, w_out, norm_w.reshape(1, D), wr, br)


ROW_UNROLL = 8


def _pack_bf16_pairs(x):
    h = x.shape[1] // 2
    lo = lax.bitcast_convert_type(x[:, :h].astype(BF16).astype(F32), jnp.uint32)
    hi = lax.bitcast_convert_type(x[:, h:].astype(BF16).astype(F32), jnp.uint32)
    return (hi & jnp.uint32(0xFFFF0000)) | (lo >> 16)


def _unpack_bf16_pairs(w):
    lo = lax.bitcast_convert_type(w << 16, F32).astype(BF16)
    hi = lax.bitcast_convert_type(w & jnp.uint32(0xFFFF0000), F32).astype(BF16)
    return jnp.concatenate([lo, hi], axis=1)


def _dispatch_kernel(dest_ref, zrow_ref, x_ref, buf_ref, zero_ref, xs_ref, sem, zsem, *, tm, n_tail):
    i = pl.program_id(0)
    E = (zrow_ref.shape[0] - 1) // 2
    ZR = zero_ref.shape[0]
    R = buf_ref.shape[0]

    @pl.when(i == 0)
    def _():
        zero_ref[...] = jnp.zeros_like(zero_ref)

        def zero_copy(row, n):
            return pltpu.make_async_copy(zero_ref.at[pl.ds(0, n)], buf_ref.at[pl.ds(pl.multiple_of(row, 8), n)], zsem)

        def zero_rows(act):
            for e in range(E):
                off = zrow_ref[e]
                n = ZR
                while n >= 8:
                    @pl.when((zrow_ref[E + e] & n) != 0)
                    def _():
                        act(zero_copy(off, n))
                    off = off + (zrow_ref[E + e] & n)
                    n //= 2
            for j in range(n_tail):
                row = zrow_ref[2 * E] + j * ZR

                @pl.when(row < R)
                def _():
                    act(zero_copy(row, ZR))

        zero_rows(lambda c: c.start())
        zero_rows(lambda c: c.wait())

    par = i % 2
    xs_ref[par] = x_ref[...]

    def start(rb, c):
        for j in range(ROW_UNROLL):
            r = rb * ROW_UNROLL + j
            for k in range(TOP_K):
                d = dest_ref[(i * tm + r) * TOP_K + k]
                pltpu.make_async_copy(xs_ref.at[par, pl.ds(r, 1)], buf_ref.at[pl.ds(d, 1)],
                                      sem.at[par]).start(priority=k % 2)
        return c

    lax.fori_loop(0, tm // ROW_UNROLL, start, 0)

    def wait_step(s):
        for k in range(TOP_K):
            pltpu.make_async_copy(xs_ref.at[s], buf_ref.at[pl.ds(0, tm)], sem.at[s]).wait()

    @pl.when(i > 0)
    def _():
        wait_step(1 - par)

    @pl.when(i == pl.num_programs(0) - 1)
    def _():
        wait_step(par)


def moe_dispatch(xn, dest_flat, zrows, n_rows, tm=256, name="moe_dispatch"):
    T, H = xn.shape
    n_tail = (n_rows - T * TOP_K) // MOE_TM
    kern = functools.partial(_dispatch_kernel, tm=tm, n_tail=n_tail)
    return pl.pallas_call(
        kern,
        grid_spec=pltpu.PrefetchScalarGridSpec(
            num_scalar_prefetch=2,
            grid=(T // tm,),
            in_specs=[pl.BlockSpec((tm, H), lambda i, d, z: (i, 0))],
            out_specs=pl.BlockSpec(memory_space=pl.ANY),
            scratch_shapes=[pltpu.VMEM((MOE_TM, H), jnp.uint32), pltpu.VMEM((2, tm, H), jnp.uint32),
                            pltpu.SemaphoreType.DMA((2,)), pltpu.SemaphoreType.DMA(())],
        ),
        out_shape=jax.ShapeDtypeStruct((n_rows, H), jnp.uint32),
        compiler_params=_cparams(("arbitrary",)),
        name=name,
    )(dest_flat, zrows, xn)


def _experts_kernel(ue_ref, r0_ref, nt_ref, na_ref, x_hbm, wg_ref, wu_ref, bg_ref, bu_ref, wd_ref, bd_ref,
                    o_hbm, xb_ref, acc_ref, zero_ref, sem_in, sem_out, zsem, *, TM, UT, NF, n_tail):
    TPS = -(-UT // NF)
    u = pl.program_id(0)
    f = pl.program_id(1)
    na = na_ref[0]
    nt = nt_ref[u]
    r0 = r0_ref[u]
    slot = u % 2
    un = jnp.minimum(u + 1, na - 1)

    def prefetch(j):
        return jnp.logical_and(jnp.logical_and(u + 1 < na, j < nt_ref[un]), f == j // TPS)

    def x_copy(base, i, s):
        rows = pl.ds(pl.multiple_of(base + i * TM, TM), TM)
        return pltpu.make_async_copy(x_hbm.at[rows], xb_ref.at[s, i], sem_in.at[i])

    def o_copy(base, i):
        rows = pl.ds(pl.multiple_of(base + i * TM, TM), TM)
        return pltpu.make_async_copy(acc_ref.at[i], o_hbm.at[rows], sem_out.at[i])

    def wait_outputs(unit):
        for i in range(UT):
            @pl.when(i < nt_ref[unit])
            def _():
                o_copy(r0_ref[unit], i).wait()

    @pl.when(u < na)
    def _():
        @pl.when(jnp.logical_and(f == 0, u == 0))
        def _():
            for act in (lambda c: c.start(), lambda c: c.wait()):
                for i in range(UT):
                    @pl.when(i < nt)
                    def _():
                        act(x_copy(r0, i, 0))

        for j in range(UT):
            @pl.when(prefetch(j))
            def _():
                x_copy(r0_ref[un], j, 1 - slot).start()

        up = jnp.maximum(u - 1, 0)

        def wait_prev_output(j):
            @pl.when(jnp.logical_and(u > 0, j < nt_ref[up]))
            def _():
                o_copy(r0_ref[up], j).wait()

        bg = bg_ref[...]
        bu = bu_ref[...]

        def tile_out(xt):
            g = _dot(xt, wg_ref[...].astype(BF16)) + bg
            up = _dot(xt, wu_ref[...].astype(BF16)) + bu
            gate = jnp.minimum(g, SWIGLU_LIMIT)
            lin = jnp.clip(up, -SWIGLU_LIMIT, SWIGLU_LIMIT)
            act = (gate * jax.nn.sigmoid(SWIGLU_ALPHA * gate) * (lin + 1.0)).astype(BF16)
            return _dot(act, wd_ref[...].astype(BF16))

        D = acc_ref.shape[2]

        def update(i, n, first):
            tiles = pl.ds(i, n)
            if first:
                for t in range(n):
                    wait_prev_output(i + t)
            out = tile_out(_unpack_bf16_pairs(xb_ref[slot, tiles].reshape(n * TM, D // 2)))
            if first:
                out = out + bd_ref[...]
            else:
                out = out + acc_ref[tiles].reshape(n * TM, D)
            acc_ref[tiles] = out.reshape(n, TM, D)

            @pl.when(f == NF - 1)
            def _():
                for t in range(n):
                    o_copy(r0, i + t).start()

        def all_tiles(first):
            G = MOE_GROUP

            def body(j, c):
                update(pl.multiple_of(G * j, G), G, first)
                return c
            lax.fori_loop(0, nt // G, body, 0)
            done = (nt // G) * G
            n = G // 2
            while n >= 1:
                @pl.when((nt & n) != 0)
                def _():
                    update(pl.multiple_of(done, n), n, first)
                done = done + (nt & n)
                n //= 2

        @pl.when(f == 0)
        def _():
            all_tiles(True)
            for j in range(UT):
                @pl.when(j >= nt)
                def _():
                    wait_prev_output(j)

        @pl.when(f > 0)
        def _():
            all_tiles(False)

        for j in range(UT):
            @pl.when(prefetch(j))
            def _():
                x_copy(r0_ref[un], j, 1 - slot).wait()

        @pl.when(jnp.logical_and(f == NF - 1, u == na - 1))
        def _():
            zero_ref[...] = jnp.zeros_like(zero_ref)
            for act in (lambda c: c.start(), lambda c: c.wait()):
                for j in range(n_tail):
                    row = na_ref[1] + j * TM

                    @pl.when(row < o_hbm.shape[0])
                    def _():
                        act(pltpu.make_async_copy(zero_ref, o_hbm.at[pl.ds(pl.multiple_of(row, TM), TM)], zsem))
            wait_outputs(u)


def moe_experts(x_buf, plan, n_assign, w_gate_up, b_gate_up, w_down, b_down, name="moe_experts"):
    R, H = x_buf.shape
    D = 2 * H
    n_tail = (R - n_assign) // MOE_TM
    E, _, F2 = w_gate_up.shape
    DF = F2 // 2
    TM, UT, TF = MOE_TM, MOE_UNIT_TILES, MOE_TF
    NF = DF // TF
    ue, r0, nt, na = plan
    U = ue.shape[0]

    def fsel(u, f, na_ref):
        return jnp.where(u < na_ref[0], f, NF - 1)

    kern = functools.partial(_experts_kernel, TM=TM, UT=UT, NF=NF, n_tail=n_tail)
    return pl.pallas_call(
        kern,
        grid_spec=pltpu.PrefetchScalarGridSpec(
            num_scalar_prefetch=4,
            grid=(U, NF),
            in_specs=[
                pl.BlockSpec(memory_space=pl.ANY),
                pl.BlockSpec((None, D, TF), lambda u, f, ue, r0, nt, na: (ue[u], 0, fsel(u, f, na))),
                pl.BlockSpec((None, D, TF), lambda u, f, ue, r0, nt, na: (ue[u], 0, NF + fsel(u, f, na))),
                pl.BlockSpec((None, 1, TF), lambda u, f, ue, r0, nt, na: (ue[u], 0, fsel(u, f, na))),
                pl.BlockSpec((None, 1, TF), lambda u, f, ue, r0, nt, na: (ue[u], 0, NF + fsel(u, f, na))),
                pl.BlockSpec((None, TF, D), lambda u, f, ue, r0, nt, na: (ue[u], fsel(u, f, na), 0)),
                pl.BlockSpec((None, 1, D), lambda u, f, ue, r0, nt, na: (ue[u], 0, 0)),
            ],
            out_specs=pl.BlockSpec(memory_space=pl.ANY),
            scratch_shapes=[
                pltpu.VMEM((2, UT, TM, H), jnp.uint32),
                pltpu.VMEM((UT, TM, D), F32),
                pltpu.VMEM((TM, D), F32),
                pltpu.SemaphoreType.DMA((UT,)),
                pltpu.SemaphoreType.DMA((UT,)),
                pltpu.SemaphoreType.DMA(()),
            ],
        ),
        out_shape=jax.ShapeDtypeStruct((R, D), F32),
        compiler_params=_cparams(("arbitrary", "arbitrary"), VMEM_LIMIT_EXPERTS_BYTES),
        name=name,
    )(ue, r0, nt, na, x_buf, w_gate_up, w_gate_up, b_gate_up.reshape(E, 1, F2), b_gate_up.reshape(E, 1, F2),
      w_down, b_down.reshape(E, 1, D))


def moe_plan(counts, n_tokens):
    TM, UT = MOE_TM, MOE_UNIT_TILES
    E = counts.shape[0]
    nt = (counts + TM - 1) // TM
    padded = nt * TM
    pad_start = jnp.cumsum(padded) - padded
    nu = (nt + UT - 1) // UT
    cu = jnp.cumsum(nu)
    n_units = cu[-1]
    U = E + (n_tokens * TOP_K) // (TM * UT)
    u = jnp.arange(U, dtype=jnp.int32)
    uc = jnp.minimum(u, n_units - 1)
    ue = jnp.clip(jnp.searchsorted(cu, uc, side='right'), 0, E - 1).astype(jnp.int32)
    j = uc - (cu[ue] - nu[ue])
    r0 = (pad_start[ue] + j * (UT * TM)).astype(jnp.int32)
    ntl = jnp.where(u < n_units, jnp.clip(nt[ue] - j * UT, 0, UT), 0).astype(jnp.int32)
    zstart = (pad_start + counts) // 8 * 8
    pad_end = pad_start + padded
    zrows = jnp.concatenate([zstart, pad_end - zstart, pad_end[-1:]]).astype(jnp.int32)
    meta = jnp.stack([n_units, pad_end[-1]]).astype(jnp.int32)
    return pad_start, zrows, (ue, r0, ntl, meta)


def _combine_kernel(dest_ref, h_ref, route_ref, nw_ref, o_hbm, h2_ref, xn_ref, g_ref, sem, *, tm):
    i = pl.program_id(0)
    par = i % 2

    def gather_tile(t, slot):
        def start(rb, c):
            for j in range(ROW_UNROLL):
                r = rb * ROW_UNROLL + j
                for k in range(TOP_K):
                    d = dest_ref[(t * tm + r) * TOP_K + k]
                    pltpu.make_async_copy(o_hbm.at[pl.ds(d, 1)], g_ref.at[slot, k, pl.ds(r, 1)],
                                          sem.at[slot]).start(priority=k % 2)
            return c

        lax.fori_loop(0, tm // ROW_UNROLL, start, 0)

    @pl.when(i == 0)
    def _():
        gather_tile(i, par)

    @pl.when(i + 1 < pl.num_programs(0))
    def _():
        gather_tile(i + 1, 1 - par)

    for k in range(TOP_K):
        pltpu.make_async_copy(o_hbm.at[pl.ds(0, tm)], g_ref.at[par, k], sem.at[par]).wait()

    route = route_ref[...]
    y = g_ref[par, 0] * route[:, 2 * TOP_K:2 * TOP_K + 1]
    for k in range(1, TOP_K):
        y = y + g_ref[par, k] * route[:, 2 * TOP_K + k:2 * TOP_K + k + 1]
    h2 = h_ref[...] + y
    h2_ref[...] = h2
    ms = jnp.mean(h2 * h2, axis=-1, keepdims=True)
    xn_ref[...] = (h2 * lax.rsqrt(ms + EPS) * nw_ref[...]).astype(xn_ref.dtype)


def moe_combine(h, route, dest_flat, out_buf, norm_w, tm=256, name="moe_combine"):
    T, D = h.shape
    kern = functools.partial(_combine_kernel, tm=tm)
    return pl.pallas_call(
        kern,
        grid_spec=pltpu.PrefetchScalarGridSpec(
            num_scalar_prefetch=1,
            grid=(T // tm,),
            in_specs=[pl.BlockSpec((tm, D), lambda i, d: (i, 0)),
                      pl.BlockSpec((tm, LANES), lambda i, d: (i, 0)),
                      pl.BlockSpec((1, D), lambda i, d: (0, 0)),
                      pl.BlockSpec(memory_space=pl.ANY)],
            out_specs=[pl.BlockSpec((tm, D), lambda i, d: (i, 0)),
                       pl.BlockSpec((tm, D), lambda i, d: (i, 0))],
            scratch_shapes=[pltpu.VMEM((2, TOP_K, tm, D), F32), pltpu.SemaphoreType.DMA((2,))],
        ),
        out_shape=[jax.ShapeDtypeStruct((T, D), F32), jax.ShapeDtypeStruct((T, D), BF16)],
        compiler_params=_cparams(("arbitrary",)),
        name=name,
    )(dest_flat, h, route, norm_w.reshape(1, D), out_buf)


def _ple_kernel(xn_ref, p1_ref, p2_ref, h_ref, wg_hbm, wp_hbm, nf_ref, oa_ref, ob_ref, wgb_ref, wpb_ref, stage_ref, sem,
                *, nfirst):
    i = pl.program_id(0)

    @pl.when(i == 0)
    def _():
        _load_weight_bf16(wg_hbm, wgb_ref, stage_ref, sem)
        _load_weight_bf16(wp_hbm, wpb_ref, stage_ref, sem)

    gate = jax.nn.sigmoid(_dot(xn_ref[...], wgb_ref[...]))
    p = _pick(i, nfirst, p1_ref, p2_ref).astype(BF16)
    y = _rms(h_ref[...] + _dot(p, wpb_ref[...]) * gate, nf_ref[...])

    @pl.when(i < nfirst)
    def _():
        oa_ref[...] = y

    @pl.when(i >= nfirst)
    def _():
        ob_ref[...] = y


def ple_final(xn, p1, p2, h, w_gate, w_ple, norm_w, tm=512, name="ple_final"):
    T, D = h.shape
    Ta, P = p1.shape
    na = Ta // tm
    return pl.pallas_call(
        functools.partial(_ple_kernel, nfirst=na),
        grid=(T // tm,),
        in_specs=[
            pl.BlockSpec((tm, D), lambda i: (i, 0)),
            *_cat_specs((tm, P), na),
            pl.BlockSpec((tm, D), lambda i: (i, 0)),
            pl.BlockSpec(memory_space=pl.ANY),
            pl.BlockSpec(memory_space=pl.ANY),
            pl.BlockSpec((1, D), lambda i: (0, 0)),
        ],
        out_specs=list(_cat_specs((tm, D), na)),
        out_shape=[jax.ShapeDtypeStruct((Ta, D), F32), jax.ShapeDtypeStruct((T - Ta, D), F32)],
        scratch_shapes=[pltpu.VMEM((D, D), BF16), pltpu.VMEM((P, D), BF16),
                        pltpu.VMEM((2, WEIGHT_STAGE_ROWS, D), F32), pltpu.SemaphoreType.DMA((2,))],
        compiler_params=_cparams(("arbitrary",)),
        name=name,
    )(xn, p1, p2, h, w_gate, w_ple, norm_w.reshape(1, D))


def kernel(x_prompt, x_sample, p_prompt, p_sample, state_hgrn, state_ssm_re, state_ssm_im, norm_mix, w_in, hg_lb, hg_gnorm, w_branch_a, ssm_a_re, ssm_a_im, ssm_log_dt, ssm_b_re, ssm_b_im, ssm_c_re, ssm_c_im, ssm_d, w_glu, w_branch_b, w_out, norm_moe, w_router, b_router, w_gate_up, b_gate_up, w_down, b_down, norm_ple, w_ple, w_ple_gate, norm_final):
    BP, LP, D = x_prompt.shape
    BS, LS, _ = x_sample.shape
    depth = w_in.shape[0]
    assert depth == 1
    TP, TS = BP * LP, BS * LS
    T = TP + TS
    G, P = ssm_a_re.shape[1:]
    W = SSM_GROUP * G

    xp = x_prompt.reshape(TP, D)
    xs = x_sample.reshape(TS, D)

    xn = rmsnorm_cat(xp, xs, norm_mix[0], BF16, name="norm_mix")
    tm = 1024
    w_in0 = w_in[0]
    z_hg = project(xn, w_in0, 0, T // tm, 0, 4, tm, 1024, "proj_hgrn")
    u_p = project(xn, w_in0, 0, TP // tm, 4, 1, tm, 1024, "proj_u_prompt")
    u_s = project(xn, w_in0, TP // tm, TS // tm, 4, 1, tm, 1024, "proj_u_sample")
    z_gate = project(xn, w_in0, 0, T // tm, 5, 4, tm, 1024, "proj_gates", out_dtype=BF16)

    lb = jax.nn.softmax(hg_lb.astype(F32), axis=0)[0]
    og_p, hg_p = hgrn2(z_hg, lb, hg_gnorm[0], None, row0=0, B=BP, L=LP, C=HG_CHUNK, nb=1, tok_step=512,
                       name="hgrn_prompt")
    og_s, hg_s = hgrn2(z_hg, lb, hg_gnorm[0], state_hgrn[0], row0=TP, B=BS, L=LS, C=LS, nb=8, tok_step=LS,
                       name="hgrn_sample")

    bd, cd, ab = s5_params(ssm_a_re[0], ssm_a_im[0], ssm_log_dt[0], ssm_b_re[0], ssm_b_im[0],
                           ssm_c_re[0], ssm_c_im[0])
    y_p, re_p, im_p = s5_prompt(u_p.reshape(BP, LP, W), bd, cd, ab, ssm_d[0], TC=512, name="s5_prompt")
    y_s, re_s, im_s = s5_sample(u_s, bd, cd, ab, ssm_d[0], state_ssm_re[0].reshape(BS, G * P),
                                state_ssm_im[0].reshape(BS, G * P), B=BS, L=LS, name="s5_sample")

    glu = gelu_glu(y_p.reshape(TP, W), y_s, w_glu[0])
    merged = gated_merge(og_p, og_s, glu, z_gate, w_branch_a[0], w_branch_b[0])

    h1, xn2, route, cnt = out_proj_router(merged, xp, xs, w_out[0], norm_moe[0], w_router[0], b_router[0])
    e_idx = route[:, 0:TOP_K].astype(jnp.int32)
    rank = route[:, TOP_K:2 * TOP_K].astype(jnp.int32)
    counts = cnt[0, :N_EXPERTS].astype(jnp.int32)
    pad_start, zrows, plan = moe_plan(counts, T)
    onehot = e_idx[..., None] == jnp.arange(N_EXPERTS, dtype=jnp.int32)
    dest = (jnp.sum(jnp.where(onehot, pad_start, 0), axis=-1) + rank).astype(jnp.int32).reshape(T * TOP_K)
    n_rows = T * TOP_K + N_EXPERTS * MOE_TM
    x_buf = moe_dispatch(xn2, dest, zrows, n_rows)
    out_buf = moe_experts(x_buf, plan, T * TOP_K, w_gate_up[0], b_gate_up[0], w_down[0], b_down[0])
    h2, xn3 = moe_combine(h1, route, dest, out_buf, norm_ple[0])

    y_p2, y_s2 = ple_final(xn3, p_prompt[0].reshape(TP, -1), p_sample[0].reshape(TS, -1), h2, w_ple_gate[0],
                           w_ple[0], norm_final)

    y_prompt = y_p2.reshape(BP, LP, D)
    y_sample = y_s2.reshape(BS, LS, D)
    hs = (1, BP, HG_HEADS, HG_DK, HG_DV)
    return (y_prompt, y_sample,
            hg_p.reshape(hs), re_p.reshape(1, BP, G, P), im_p.reshape(1, BP, G, P),
            hg_s.reshape(1, BS, HG_HEADS, HG_DK, HG_DV), re_s.reshape(1, BS, G, P), im_s.reshape(1, BS, G, P))
```

```python
import functools

import jax
import jax.numpy as jnp
from jax import lax
from jax.experimental import pallas as pl
from jax.experimental.pallas import tpu as pltpu

F32 = jnp.float32
BF16 = jnp.bfloat16
EPS = 1e-6

HG_HEADS = 8
HG_DK = 128
HG_DV = 128
HG_CHUNK = 32
SSM_GROUP = 16
N_EXPERTS = 32
TOP_K = 4
SWIGLU_LIMIT = 7.0
SWIGLU_ALPHA = 1.702

LANES = 128
VMEM_LIMIT_BYTES = 56 * 1024 * 1024
VMEM_LIMIT_EXPERTS_BYTES = 61 * 1024 * 1024

MOE_TM = 128
MOE_UNIT_TILES = 12
MOE_GROUP = 8
MOE_TF = 512
NEG_BIG = -1e30


def _cparams(sem, vmem_limit=VMEM_LIMIT_BYTES):
    return pltpu.CompilerParams(dimension_semantics=sem, vmem_limit_bytes=vmem_limit)


def _dot(a, b):
    return jnp.dot(a, b, preferred_element_type=F32)


def _split2(a):
    hi = a.astype(BF16)
    lo = (a - hi.astype(F32)).astype(BF16)
    return hi, lo


def _dot_x3(a, b_split):
    a1, a2 = _split2(a)
    b1, b2 = b_split
    return _dot(a1, b1) + (_dot(a1, b2) + _dot(a2, b1))


WEIGHT_STAGE_ROWS = 256


def _load_weight_bf16(w_hbm, wb_ref, stage_ref, sem):
    K = w_hbm.shape[0]
    rows = min(stage_ref.shape[1], K)
    n = K // rows

    def copy(c):
        return pltpu.make_async_copy(w_hbm.at[pl.ds(c * rows, rows)], stage_ref.at[c % 2, pl.ds(0, rows)],
                                     sem.at[c % 2])

    copy(0).start()
    for c in range(n):
        if c + 1 < n:
            copy(c + 1).start()
        copy(c).wait()
        wb_ref[pl.ds(c * rows, rows), :] = stage_ref[c % 2, pl.ds(0, rows), :].astype(BF16)


def _cat_specs(shape, nfirst, row_axis=0, col_axis=None):
    def col(ids):
        return 0 if col_axis is None else ids[col_axis]

    first = pl.BlockSpec(shape, lambda *ids: (jnp.minimum(ids[row_axis], nfirst - 1), col(ids)))
    second = pl.BlockSpec(shape, lambda *ids: (jnp.maximum(ids[row_axis] - nfirst, 0), col(ids)))
    return first, second


def _pick(i, nfirst, a_ref, b_ref):
    return jnp.where(i < nfirst, a_ref[...], b_ref[...])


def _rms(x, w):
    ms = jnp.mean(x * x, axis=-1, keepdims=True)
    return x * lax.rsqrt(ms + EPS) * w


def _rms_in2_kernel(a_ref, b_ref, w_ref, o_ref, *, nfirst):
    x = _pick(pl.program_id(0), nfirst, a_ref, b_ref)
    o_ref[...] = _rms(x, w_ref[...]).astype(o_ref.dtype)


def rmsnorm_cat(xa, xb, w, out_dtype, tm=512, name="rmsnorm"):
    (Ta, D), Tb = xa.shape, xb.shape[0]
    na = Ta // tm
    return pl.pallas_call(
        functools.partial(_rms_in2_kernel, nfirst=na),
        grid=((Ta + Tb) // tm,),
        in_specs=[*_cat_specs((tm, D), na), pl.BlockSpec((1, D), lambda i: (0, 0))],
        out_specs=pl.BlockSpec((tm, D), lambda i: (i, 0)),
        out_shape=jax.ShapeDtypeStruct((Ta + Tb, D), out_dtype),
        compiler_params=_cparams(("arbitrary",)),
        name=name,
    )(xa, xb, w.reshape(1, D))


def _proj_kernel(x_ref, w_ref, o_ref, wb_ref):
    @pl.when(pl.program_id(1) == 0)
    def _():
        wb_ref[...] = w_ref[...].astype(BF16)

    o_ref[...] = _dot(x_ref[...], wb_ref[...]).astype(o_ref.dtype)


def project(xn, w, row_blk0, n_row_blks, col_blk0, n_col_blks, tm, tn, name, out_dtype=F32):
    K = xn.shape[1]
    return pl.pallas_call(
        _proj_kernel,
        grid=(n_col_blks, n_row_blks),
        in_specs=[
            pl.BlockSpec((tm, K), lambda j, i: (row_blk0 + i, 0)),
            pl.BlockSpec((K, tn), lambda j, i: (0, col_blk0 + j)),
        ],
        out_specs=pl.BlockSpec((tm, tn), lambda j, i: (i, j)),
        out_shape=jax.ShapeDtypeStruct((n_row_blks * tm, n_col_blks * tn), out_dtype),
        scratch_shapes=[pltpu.VMEM((K, tn), BF16)],
        compiler_params=_cparams(("arbitrary", "arbitrary")),
        name=name,
    )(xn, w)


HGRN_CHUNKS_PER_TRIP = 4


def _hgrn_kernel(*refs, C, nchunk, nb, tok_step, has_s0, state_t):
    if has_s0:
        q_ref, f_ref, v_ref, g_ref, lb_ref, gn_ref, s0_ref, og_ref, so_ref, st_ref = refs
    else:
        q_ref, f_ref, v_ref, g_ref, lb_ref, gn_ref, og_ref, so_ref, st_ref = refs
        s0_ref = None
    H, DK, DV = HG_HEADS, HG_DK, HG_DV
    step = pl.program_id(1)

    @pl.when(step == 0)
    def _():
        for i in range(nb):
            for h in range(H):
                if has_s0:
                    st_ref[i, h] = s0_ref[i, h].T if state_t else s0_ref[i, h]
                else:
                    st_ref[i, h] = jnp.zeros(st_ref.shape[2:], F32)

    lbv = lb_ref[...]
    gnv = gn_ref[...]
    rr = lax.broadcasted_iota(jnp.int32, (C, C), 0)
    cc = lax.broadcasted_iota(jnp.int32, (C, C), 1)
    causal = rr >= cc
    tri = causal.astype(BF16)
    sls =[slice(h * DK, (h + 1) * DK) for h in range(H)]
    nt_dims = (((1,), (1,)), ((), ()))
    tn_dims = (((0,), (0,)), ((), ()))

    def tn(a, b):
        return lax.dot_general(a, b, tn_dims, preferred_element_type=F32)

    def prepare(r0):
        rows = pl.ds(r0, C)
        zq = q_ref[rows, :]
        zf = f_ref[rows, :]
        q = jax.nn.silu(zq)
        fe = lbv + (1.0 - lbv) * jax.nn.sigmoid(zf)
        k = 1.0 - fe
        gl = jnp.log(fe)
        g1 = gl.astype(BF16)
        r1 = gl - g1.astype(F32)
        g2 = r1.astype(BF16)
        g3 = (r1 - g2.astype(F32)).astype(BF16)
        b = _dot(tri, g1) + (_dot(tri, g2) + _dot(tri, g3))
        blast = b[C - 1:C, :]
        return dict(
            qt=(q * jnp.exp(b)).astype(BF16),
            kt=(k * jnp.exp(-b)).astype(BF16),
            kend=(k * jnp.exp(blast - b)).astype(BF16),
            eblast=jnp.exp(blast),
            vb=v_ref[rows, :].astype(BF16),
            gate=jax.nn.silu(g_ref[rows, :]),
        )

    def run(jobs):
        P = [prepare(r0) for _, r0 in jobs]
        atts = [[lax.dot_general(p["qt"][:, sl], p["kt"][:, sl], nt_dims, preferred_element_type=F32)
                 for sl in sls] for p in P]
        if state_t:
            upds = [[tn(p["vb"][:, sl], p["kend"][:, sl]) for sl in sls] for p in P]
            decs = [[p["eblast"][:, sl] for sl in sls] for p in P]
        else:
            upds = [[tn(p["kend"][:, sl], p["vb"][:, sl]) for sl in sls] for p in P]
            decs = [[jnp.broadcast_to(p["eblast"][:, sl], (8, DK)).T[:, 0:1] for sl in sls] for p in P]
        state = {}
        inters = []
        for j, (i, _) in enumerate(jobs):
            row = []
            for h, sl in enumerate(sls):
                s = state[(i, h)] if (i, h) in state else st_ref[i, h]
                qth = P[j]["qt"][:, sl]
                if state_t:
                    row.append(lax.dot_general(qth, s.astype(BF16), nt_dims, preferred_element_type=F32))
                else:
                    row.append(_dot(qth, s.astype(BF16)))
                state[(i, h)] = s * decs[j][h] + upds[j][h]
            inters.append(row)
        for (i, h), s in state.items():
            st_ref[i, h] = s
        outs = []
        for j, p in enumerate(P):
            cols = []
            for h, sl in enumerate(sls):
                att = jnp.where(causal, atts[j][h], 0.0).astype(BF16)
                o = _dot(att, p["vb"][:, sl]) + inters[j][h]
                ms = jnp.mean(o * o, axis=-1, keepdims=True)
                cols.append(o * lax.rsqrt(ms + EPS) * gnv * p["gate"][:, sl])
            outs.append(jnp.concatenate(cols, axis=1))
        return outs

    if nchunk == 1:
        outs = run([(i, i * tok_step) for i in range(nb)])
        og_ref[...] = jnp.concatenate(outs, axis=0).astype(og_ref.dtype)
    else:
        assert nb == 1 and nchunk % HGRN_CHUNKS_PER_TRIP == 0

        def body(t, carry):
            r0s = [pl.multiple_of((HGRN_CHUNKS_PER_TRIP * t + j) * C, C) for j in range(HGRN_CHUNKS_PER_TRIP)]
            outs = run([(0, r0) for r0 in r0s])
            for r0, o in zip(r0s, outs):
                og_ref[pl.ds(r0, C), :] = o.astype(og_ref.dtype)
            return carry

        lax.fori_loop(0, nchunk // HGRN_CHUNKS_PER_TRIP, body, 0)

    @pl.when(step == pl.num_programs(1) - 1)
    def _():
        for i in range(nb):
            for h in range(H):
                so_ref[i, h] = st_ref[i, h].T if state_t else st_ref[i, h]


def hgrn2(z_hg, lb, gn, s0, *, row0, B, L, C, nb, tok_step, name):
    H, DK, DV = HG_HEADS, HG_DK, HG_DV
    W = H * DK
    steps = L // tok_step
    rows = nb * tok_step
    nchunk = tok_step // C
    blk0 = row0 // rows
    has_s0 = s0 is not None

    def zspec(col):
        return pl.BlockSpec((rows, W), lambda bb, s: (blk0 + bb * steps + s, col))

    in_specs = [zspec(0), zspec(1), zspec(2), zspec(3),
                pl.BlockSpec((1, W), lambda bb, s: (0, 0)),
                pl.BlockSpec((1, DV), lambda bb, s: (0, 0))]
    args = [z_hg, z_hg, z_hg, z_hg, lb.reshape(1, W), gn.reshape(1, DV)]
    if has_s0:
        in_specs.append(pl.BlockSpec((nb, H, DK, DV), lambda bb, s: (bb, 0, 0, 0)))
        args.append(s0)
    state_t = L > C
    kern = functools.partial(_hgrn_kernel, C=C, nchunk=nchunk, nb=nb, tok_step=tok_step, has_s0=has_s0,
                             state_t=state_t)
    return pl.pallas_call(
        kern,
        grid=(B // nb, steps),
        in_specs=in_specs,
        out_specs=[pl.BlockSpec((rows, W), lambda bb, s: (bb * steps + s, 0)),
                   pl.BlockSpec((nb, H, DK, DV), lambda bb, s: (bb, 0, 0, 0))],
        out_shape=[jax.ShapeDtypeStruct((B * L, W), BF16),
                   jax.ShapeDtypeStruct((B, H, DK, DV), F32)],
        scratch_shapes=[pltpu.VMEM((nb, H, DV, DK), F32)],
        compiler_params=_cparams(("arbitrary", "arbitrary")),
        name=name,
    )(*args)


def _s5_step(ar, ai, xr, xi, br, bi):
    return ar * xr - ai * xi + br, ar * xi + ai * xr + bi


S5_STEPS_PER_TRIP = 8
S5_GROUPS_PER_BLOCK = 16


def _s5_prompt_kernel(u_ref, bd_ref, cd_ref, a_ref, d_ref, y_ref, xr_ref, xi_ref, X_ref, st_ref, *, B, TC):
    NP = X_ref.shape[0] // 2
    assert 2 * B == 8
    tc = pl.program_id(1)

    def rows_of(b, s):
        return pl.ds(2 * b + s, TC, stride=8)

    @pl.when(tc == 0)
    def _():
        st_ref[...] = jnp.zeros_like(st_ref)

    bd = bd_ref[...].astype(BF16)
    for b in range(B):
        bu = _dot(u_ref[b].astype(BF16), bd)
        for c in range(4 * NP):
            X_ref[c // 2, rows_of(b, c % 2), :] = bu[:, c * LANES:(c + 1) * LANES]

    odd = lax.broadcasted_iota(jnp.int32, (8, LANES), 0) % 2 == 1

    def pair_const(row, p):
        lo = a_ref[row:row + 1, (2 * p) * LANES:(2 * p + 1) * LANES]
        hi = a_ref[row:row + 1, (2 * p + 1) * LANES:(2 * p + 2) * LANES]
        return jnp.where(odd, hi, lo)

    ars = [pair_const(0, p) for p in range(NP)]
    ais = [pair_const(1, p) for p in range(NP)]

    def body(i, carry):
        carry = list(carry)
        for j in range(S5_STEPS_PER_TRIP):
            rows = pl.ds(pl.multiple_of((i * S5_STEPS_PER_TRIP + j) * 8, 8), 8)
            for p in range(NP):
                xr, xi = _s5_step(ars[p], ais[p], carry[p][0], carry[p][1], X_ref[p, rows, :], X_ref[NP + p, rows, :])
                X_ref[p, rows, :] = xr
                X_ref[NP + p, rows, :] = xi
                carry[p] = (xr, xi)
        return tuple(carry)

    init = tuple((st_ref[p], st_ref[NP + p]) for p in range(NP))
    fin = lax.fori_loop(0, TC // S5_STEPS_PER_TRIP, body, init)
    for p in range(NP):
        st_ref[p] = fin[p][0]
        st_ref[NP + p] = fin[p][1]

    cd = cd_ref[...].astype(BF16)
    dv = d_ref[...]
    for b in range(B):
        xb = jnp.concatenate([X_ref[c // 2, rows_of(b, c % 2), :] for c in range(4 * NP)], axis=1).astype(BF16)
        y_ref[b] = _dot(xb, cd) + dv * u_ref[b]

    @pl.when(tc == pl.num_programs(1) - 1)
    def _():
        for c in range(2 * NP):
            cols = slice(c * LANES, (c + 1) * LANES)
            xr_ref[:, cols] = st_ref[c // 2, pl.ds(c % 2, B, stride=2), :]
            xi_ref[:, cols] = st_ref[NP + c // 2, pl.ds(c % 2, B, stride=2), :]


def s5_prompt(u, bd, cd, ab, dskip, *, TC, name):
    B, L, W = u.shape
    GB, UB, SW = bd.shape
    half = SW // 2
    kern = functools.partial(_s5_prompt_kernel, B=B, TC=TC)
    return pl.pallas_call(
        kern,
        grid=(GB, L // TC),
        in_specs=[
            pl.BlockSpec((B, TC, UB), lambda g, t: (0, t, g)),
            pl.BlockSpec((None, UB, SW), lambda g, t: (g, 0, 0)),
            pl.BlockSpec((None, SW, UB), lambda g, t: (g, 0, 0)),
            pl.BlockSpec((None, 2, half), lambda g, t: (g, 0, 0)),
            pl.BlockSpec((1, UB), lambda g, t: (0, g)),
        ],
        out_specs=[
            pl.BlockSpec((B, TC, UB), lambda g, t: (0, t, g)),
            pl.BlockSpec((B, half), lambda g, t: (0, g)),
            pl.BlockSpec((B, half), lambda g, t: (0, g)),
        ],
        out_shape=[jax.ShapeDtypeStruct((B, L, W), F32),
                   jax.ShapeDtypeStruct((B, GB * half), F32),
                   jax.ShapeDtypeStruct((B, GB * half), F32)],
        scratch_shapes=[pltpu.VMEM((SW // LANES // 2, 8 * TC, LANES), F32),
                        pltpu.VMEM((SW // LANES // 2, 8, LANES), F32)],
        compiler_params=_cparams(("arbitrary", "arbitrary")),
        name=name,
    )(u, bd, cd, ab, dskip.reshape(1, W))


def _s5_sample_kernel(u_ref, bd_ref, cd_ref, a_ref, d_ref, x0r_ref, x0i_ref, y_ref, xr_ref, xi_ref, X_ref, *, B, L):
    NC = X_ref.shape[0]
    NH = NC // 2
    u = u_ref[...]
    bu = _dot_x3(u, _split2(bd_ref[...]))
    for c in range(NC):
        X_ref[c] = bu[:, c * LANES:(c + 1) * LANES]
    for c in range(NH):
        cols = slice(c * LANES, (c + 1) * LANES)
        ar = a_ref[0:1, cols]
        ai = a_ref[1:2, cols]
        xr = x0r_ref[:, cols]
        xi = x0i_ref[:, cols]
        for t in range(L):
            rows = pl.ds(t, B, stride=L)
            xr, xi = _s5_step(ar, ai, xr, xi, X_ref[c, rows, :], X_ref[NH + c, rows, :])
            X_ref[c, rows, :] = xr
            X_ref[NH + c, rows, :] = xi
        xr_ref[:, cols] = xr
        xi_ref[:, cols] = xi
    xs = jnp.concatenate([X_ref[c] for c in range(NC)], axis=1).astype(BF16)
    y_ref[...] = _dot(xs, cd_ref[...].astype(BF16)) + d_ref[...] * u


def s5_sample(u, bd, cd, ab, dskip, x0r, x0i, *, B, L, name):
    T, W = u.shape
    GB, UB, SW = bd.shape
    half = SW // 2
    kern = functools.partial(_s5_sample_kernel, B=B, L=L)
    return pl.pallas_call(
        kern,
        grid=(GB,),
        in_specs=[
            pl.BlockSpec((T, UB), lambda g: (0, g)),
            pl.BlockSpec((None, UB, SW), lambda g: (g, 0, 0)),
            pl.BlockSpec((None, SW, UB), lambda g: (g, 0, 0)),
            pl.BlockSpec((None, 2, half), lambda g: (g, 0, 0)),
            pl.BlockSpec((1, UB), lambda g: (0, g)),
            pl.BlockSpec((B, half), lambda g: (0, g)),
            pl.BlockSpec((B, half), lambda g: (0, g)),
        ],
        out_specs=[
            pl.BlockSpec((T, UB), lambda g: (0, g)),
            pl.BlockSpec((B, half), lambda g: (0, g)),
            pl.BlockSpec((B, half), lambda g: (0, g)),
        ],
        out_shape=[jax.ShapeDtypeStruct((T, W), F32),
                   jax.ShapeDtypeStruct((B, GB * half), F32),
                   jax.ShapeDtypeStruct((B, GB * half), F32)],
        scratch_shapes=[pltpu.VMEM((SW // LANES, T, LANES), F32)],
        compiler_params=_cparams(("arbitrary",)),
        name=name,
    )(u, bd, cd, ab, dskip.reshape(1, W), x0r, x0i)


def s5_params(a_re, a_im, log_dt, b_re, b_im, c_re, c_im):
    G, P = a_re.shape
    gpb = S5_GROUPS_PER_BLOCK
    GB = G // gpb
    A = lax.complex(a_re, a_im)
    dt = jnp.exp(log_dt)[:, None]
    A_bar = jnp.exp(A * dt)
    Bm = lax.complex(b_re, b_im)
    B_bar = ((A_bar - 1.0) / A)[..., None] * Bm
    C16 = SSM_GROUP

    def spread(m, width):
        k = m.shape[-1]
        sel = (jnp.arange(k)[:, None] == jnp.arange(width)[None, :] % k).astype(F32)
        return jnp.matmul(m, sel, precision=lax.Precision.HIGHEST)

    bt = jnp.swapaxes(B_bar, 1, 2).reshape(GB, gpb * C16, P)
    rows = jnp.arange(gpb * C16)[:, None] // C16
    cols = jnp.arange(gpb * P)[None, :] // P
    keep = rows == cols
    bd = jnp.concatenate([jnp.where(keep, spread(bt.real, gpb * P), 0.0),
                          jnp.where(keep, spread(bt.imag, gpb * P), 0.0)], axis=2)
    keep_t = keep.T

    def out_map(m):
        mt = jnp.swapaxes(m, 1, 2).reshape(GB, gpb * P, C16)
        return jnp.where(keep_t, spread(mt, gpb * C16), 0.0)

    cd = jnp.concatenate([out_map(c_re), out_map(-c_im)], axis=1)
    ab = jnp.stack([A_bar.real.reshape(GB, gpb * P), A_bar.imag.reshape(GB, gpb * P)], axis=1)
    return bd, cd, ab


def _glu_kernel(ya_ref, yb_ref, w_ref, o_ref, wb_ref, *, nfirst):
    i = pl.program_id(0)

    @pl.when(i == 0)
    def _():
        wb_ref[...] = w_ref[...].astype(BF16)

    zg = jax.nn.gelu(_pick(i, nfirst, ya_ref, yb_ref))
    o_ref[...] = (zg * jax.nn.sigmoid(_dot(zg.astype(BF16), wb_ref[...]))).astype(o_ref.dtype)


def gelu_glu(ya, yb, w, tm=1024, name="gelu_glu"):
    (Ta, W), Tb = ya.shape, yb.shape[0]
    na = Ta // tm
    return pl.pallas_call(
        functools.partial(_glu_kernel, nfirst=na),
        grid=((Ta + Tb) // tm,),
        in_specs=[*_cat_specs((tm, W), na), pl.BlockSpec((W, W), lambda i: (0, 0))],
        out_specs=pl.BlockSpec((tm, W), lambda i: (i, 0)),
        out_shape=jax.ShapeDtypeStruct((Ta + Tb, W), BF16),
        scratch_shapes=[pltpu.VMEM((W, W), BF16)],
        compiler_params=_cparams(("arbitrary",)),
        name=name,
    )(ya, yb, w)


def _merge_kernel(oa1_ref, oa2_ref, ob_ref, ga_ref, gb_ref, wa_ref, wb_ref, o_ref, wab_ref, wbb_ref, *, nfirst):
    i = pl.program_id(1)

    @pl.when(i == 0)
    def _():
        wab_ref[...] = wa_ref[...].astype(BF16)
        wbb_ref[...] = wb_ref[...].astype(BF16)

    ya = _dot(_pick(i, nfirst, oa1_ref, oa2_ref), wab_ref[...])
    yb = _dot(ob_ref[...], wbb_ref[...])
    ga = jax.nn.sigmoid(ga_ref[...].astype(F32))
    gb = jax.nn.sigmoid(gb_ref[...].astype(F32))
    o_ref[...] = (ga * ya + gb * yb).astype(o_ref.dtype)


def gated_merge(oa1, oa2, ob, z_gate, wa, wb, tm=1024, tn=1024, name="gated_merge"):
    T, K = ob.shape
    N = wa.shape[1]
    nj = N // tn
    na = oa1.shape[0] // tm
    return pl.pallas_call(
        functools.partial(_merge_kernel, nfirst=na),
        grid=(nj, T // tm),
        in_specs=[
            *_cat_specs((tm, K), na, row_axis=1),
            pl.BlockSpec((tm, K), lambda j, i: (i, 0)),
            pl.BlockSpec((tm, tn), lambda j, i: (i, j)),
            pl.BlockSpec((tm, tn), lambda j, i: (i, nj + j)),
            pl.BlockSpec((K, tn), lambda j, i: (0, j)),
            pl.BlockSpec((K, tn), lambda j, i: (0, j)),
        ],
        out_specs=pl.BlockSpec((tm, tn), lambda j, i: (i, j)),
        out_shape=jax.ShapeDtypeStruct((T, N), BF16),
        scratch_shapes=[pltpu.VMEM((K, tn), BF16), pltpu.VMEM((K, tn), BF16)],
        compiler_params=_cparams(("arbitrary", "arbitrary")),
        name=name,
    )(oa1, oa2, ob, z_gate, z_gate, wa, wb)


def _outproj_router_kernel(m_ref, h1_ref, h2_ref, w_hbm, nw_ref, wr_ref, br_ref, ho_ref, xn_ref, route_ref, cnt_ref,
                           wb_ref, tri_ref, carry_ref, stage_ref, sem, *, tm, nfirst):
    i = pl.program_id(0)

    @pl.when(i == 0)
    def _():
        _load_weight_bf16(w_hbm, wb_ref, stage_ref, sem)
        rr = lax.broadcasted_iota(jnp.int32, (tm, tm), 0)
        cc = lax.broadcasted_iota(jnp.int32, (tm, tm), 1)
        tri_ref[...] = (rr > cc).astype(BF16)
        carry_ref[...] = jnp.zeros_like(carry_ref)

    x = _pick(i, nfirst, h1_ref, h2_ref) + _dot(m_ref[...], wb_ref[...])
    ho_ref[...] = x
    ms = jnp.mean(x * x, axis=-1, keepdims=True)
    xn = x * lax.rsqrt(ms + EPS) * nw_ref[...]
    xn_ref[...] = _pack_bf16_pairs(xn)
    logits = _dot(xn.astype(BF16), wr_ref[...].astype(BF16)) + br_ref[...]
    lane = lax.broadcasted_iota(jnp.int32, (tm, LANES), 1)
    lanef = lane.astype(F32)
    cur = logits
    vals, hots, eids = [], [], []
    for _ in range(TOP_K):
        m = jnp.max(cur, axis=-1, keepdims=True)
        idx = jnp.min(jnp.where(cur == m, lanef, float(LANES)), axis=-1, keepdims=True)
        hot = lanef == idx
        vals.append(m)
        hots.append(hot)
        eids.append(idx)
        cur = jnp.where(hot, -jnp.inf, cur)
    es = [jnp.exp(v - vals[0]) for v in vals]
    den = es[0] + es[1] + es[2] + es[3]
    multi = jnp.zeros((tm, LANES), F32)
    for hot in hots:
        multi = multi + hot.astype(F32)
    base = carry_ref[0:1, :] + _dot(tri_ref[...], multi.astype(BF16))
    route = jnp.zeros((tm, LANES), F32)
    for k in range(TOP_K):
        e_k = eids[k]
        r_k = jnp.sum(jnp.where(hots[k], base, 0.0), axis=-1, keepdims=True)
        w_k = es[k] / den
        route = route + jnp.where(lane == k, e_k, 0.0) + jnp.where(lane == TOP_K + k, r_k, 0.0) \
            + jnp.where(lane == 2 * TOP_K + k, w_k, 0.0)
    route_ref[...] = route
    carry = carry_ref[0:1, :] + jnp.sum(multi, axis=0, keepdims=True)
    carry_ref[...] = jnp.broadcast_to(carry, carry_ref.shape)
    cnt_ref[...] = jnp.broadcast_to(carry, cnt_ref.shape)


def out_proj_router(merged, h1, h2, w_out, norm_w, w_router, b_router, tm=512, name="out_proj_router"):
    T, K = merged.shape
    D = w_out.shape[1]
    E = w_router.shape[1]
    na = h1.shape[0] // tm
    wr = jnp.pad(w_router, ((0, 0), (0, LANES - E)))
    br = jnp.pad(b_router.reshape(1, E), ((0, 0), (0, LANES - E)), constant_values=NEG_BIG)
    kern = functools.partial(_outproj_router_kernel, tm=tm, nfirst=na)
    return pl.pallas_call(
        kern,
        grid=(T // tm,),
        in_specs=[
            pl.BlockSpec((tm, K), lambda i: (i, 0)),
            *_cat_specs((tm, D), na),
            pl.BlockSpec(memory_space=pl.ANY),
            pl.BlockSpec((1, D), lambda i: (0, 0)),
            pl.BlockSpec((D, LANES), lambda i: (0, 0)),
            pl.BlockSpec((1, LANES), lambda i: (0, 0)),
        ],
        out_specs=[
            pl.BlockSpec((tm, D), lambda i: (i, 0)),
            pl.BlockSpec((tm, D // 2), lambda i: (i, 0)),
            pl.BlockSpec((tm, LANES), lambda i: (i, 0)),
            pl.BlockSpec((8, LANES), lambda i: (0, 0)),
        ],
        out_shape=[jax.ShapeDtypeStruct((T, D), F32),
                   jax.ShapeDtypeStruct((T, D // 2), jnp.uint32),
                   jax.ShapeDtypeStruct((T, LANES), F32),
                   jax.ShapeDtypeStruct((8, LANES), F32)],
        scratch_shapes=[pltpu.VMEM((K, D), BF16), pltpu.VMEM((tm, tm), BF16), pltpu.VMEM((8, LANES), F32),
                        pltpu.VMEM((2, WEIGHT_STAGE_ROWS, D), F32), pltpu.SemaphoreType.DMA((2,))],
        compiler_params=_cparams(("arbitrary",)),
        name=name,
    )(merged, h1, h2, w_out, norm_w.reshape(1, D), wr, br)


ROW_UNROLL = 8


def _pack_bf16_pairs(x):
    h = x.shape[1] // 2
    lo = lax.bitcast_convert_type(x[:, :h].astype(BF16).astype(F32), jnp.uint32)
    hi = lax.bitcast_convert_type(x[:, h:].astype(BF16).astype(F32), jnp.uint32)
    return (hi & jnp.uint32(0xFFFF0000)) | (lo >> 16)


def _unpack_bf16_pairs(w):
    lo = lax.bitcast_convert_type(w << 16, F32).astype(BF16)
    hi = lax.bitcast_convert_type(w & jnp.uint32(0xFFFF0000), F32).astype(BF16)
    return jnp.concatenate([lo, hi], axis=1)


def _dispatch_kernel(dest_ref, zrow_ref, x_ref, buf_ref, zero_ref, xs_ref, sem, zsem, *, tm, n_tail):
    i = pl.program_id(0)
    E = (zrow_ref.shape[0] - 1) // 2
    ZR = zero_ref.shape[0]
    R = buf_ref.shape[0]

    @pl.when(i == 0)
    def _():
        zero_ref[...] = jnp.zeros_like(zero_ref)

        def zero_copy(row, n):
            return pltpu.make_async_copy(zero_ref.at[pl.ds(0, n)], buf_ref.at[pl.ds(pl.multiple_of(row, 8), n)], zsem)

        def zero_rows(act):
            for e in range(E):
                off = zrow_ref[e]
                n = ZR
                while n >= 8:
                    @pl.when((zrow_ref[E + e] & n) != 0)
                    def _():
                        act(zero_copy(off, n))
                    off = off + (zrow_ref[E + e] & n)
                    n //= 2
            for j in range(n_tail):
                row = zrow_ref[2 * E] + j * ZR

                @pl.when(row < R)
                def _():
                    act(zero_copy(row, ZR))

        zero_rows(lambda c: c.start())
        zero_rows(lambda c: c.wait())

    par = i % 2
    xs_ref[par] = x_ref[...]

    def start(rb, c):
        for j in range(ROW_UNROLL):
            r = rb * ROW_UNROLL + j
            for k in range(TOP_K):
                d = dest_ref[(i * tm + r) * TOP_K + k]
                pltpu.make_async_copy(xs_ref.at[par, pl.ds(r, 1)], buf_ref.at[pl.ds(d, 1)],
                                      sem.at[par]).start(priority=k % 2)
        return c

    lax.fori_loop(0, tm // ROW_UNROLL, start, 0)

    def wait_step(s):
        for k in range(TOP_K):
            pltpu.make_async_copy(xs_ref.at[s], buf_ref.at[pl.ds(0, tm)], sem.at[s]).wait()

    @pl.when(i > 0)
    def _():
        wait_step(1 - par)

    @pl.when(i == pl.num_programs(0) - 1)
    def _():
        wait_step(par)


def moe_dispatch(xn, dest_flat, zrows, n_rows, tm=256, name="moe_dispatch"):
    T, H = xn.shape
    n_tail = (n_rows - T * TOP_K) // MOE_TM
    kern = functools.partial(_dispatch_kernel, tm=tm, n_tail=n_tail)
    return pl.pallas_call(
        kern,
        grid_spec=pltpu.PrefetchScalarGridSpec(
            num_scalar_prefetch=2,
            grid=(T // tm,),
            in_specs=[pl.BlockSpec((tm, H), lambda i, d, z: (i, 0))],
            out_specs=pl.BlockSpec(memory_space=pl.ANY),
            scratch_shapes=[pltpu.VMEM((MOE_TM, H), jnp.uint32), pltpu.VMEM((2, tm, H), jnp.uint32),
                            pltpu.SemaphoreType.DMA((2,)), pltpu.SemaphoreType.DMA(())],
        ),
        out_shape=jax.ShapeDtypeStruct((n_rows, H), jnp.uint32),
        compiler_params=_cparams(("arbitrary",)),
        name=name,
    )(dest_flat, zrows, xn)


def _experts_kernel(ue_ref, r0_ref, nt_ref, na_ref, x_hbm, wg_ref, wu_ref, bg_ref, bu_ref, wd_ref, bd_ref,
                    o_hbm, xb_ref, acc_ref, zero_ref, sem_in, sem_out, zsem, *, TM, UT, NF, n_tail):
    TPS = -(-UT // NF)
    u = pl.program_id(0)
    f = pl.program_id(1)
    na = na_ref[0]
    nt = nt_ref[u]
    r0 = r0_ref[u]
    slot = u % 2
    un = jnp.minimum(u + 1, na - 1)

    def prefetch(j):
        return jnp.logical_and(jnp.logical_and(u + 1 < na, j < nt_ref[un]), f == j // TPS)

    def x_copy(base, i, s):
        rows = pl.ds(pl.multiple_of(base + i * TM, TM), TM)
        return pltpu.make_async_copy(x_hbm.at[rows], xb_ref.at[s, i], sem_in.at[i])

    def o_copy(base, i):
        rows = pl.ds(pl.multiple_of(base + i * TM, TM), TM)
        return pltpu.make_async_copy(acc_ref.at[i], o_hbm.at[rows], sem_out.at[i])

    def wait_outputs(unit):
        for i in range(UT):
            @pl.when(i < nt_ref[unit])
            def _():
                o_copy(r0_ref[unit], i).wait()

    @pl.when(u < na)
    def _():
        @pl.when(jnp.logical_and(f == 0, u == 0))
        def _():
            for act in (lambda c: c.start(), lambda c: c.wait()):
                for i in range(UT):
                    @pl.when(i < nt)
                    def _():
                        act(x_copy(r0, i, 0))

        for j in range(UT):
            @pl.when(prefetch(j))
            def _():
                x_copy(r0_ref[un], j, 1 - slot).start()

        up = jnp.maximum(u - 1, 0)

        def wait_prev_output(j):
            @pl.when(jnp.logical_and(u > 0, j < nt_ref[up]))
            def _():
                o_copy(r0_ref[up], j).wait()

        bg = bg_ref[...]
        bu = bu_ref[...]

        def tile_out(xt):
            g = _dot(xt, wg_ref[...].astype(BF16)) + bg
            up = _dot(xt, wu_ref[...].astype(BF16)) + bu
            gate = jnp.minimum(g, SWIGLU_LIMIT)
            lin = jnp.clip(up, -SWIGLU_LIMIT, SWIGLU_LIMIT)
            act = (gate * jax.nn.sigmoid(SWIGLU_ALPHA * gate) * (lin + 1.0)).astype(BF16)
            return _dot(act, wd_ref[...].astype(BF16))

        D = acc_ref.shape[2]

        def update(i, n, first):
            tiles = pl.ds(i, n)
            if first:
                for t in range(n):
                    wait_prev_output(i + t)
            out = tile_out(_unpack_bf16_pairs(xb_ref[slot, tiles].reshape(n * TM, D // 2)))
            if first:
                out = out + bd_ref[...]
            else:
                out = out + acc_ref[tiles].reshape(n * TM, D)
            acc_ref[tiles] = out.reshape(n, TM, D)

            @pl.when(f == NF - 1)
            def _():
                for t in range(n):
                    o_copy(r0, i + t).start()

        def all_tiles(first):
            G = MOE_GROUP

            def body(j, c):
                update(pl.multiple_of(G * j, G), G, first)
                return c
            lax.fori_loop(0, nt // G, body, 0)
            done = (nt // G) * G
            n = G // 2
            while n >= 1:
                @pl.when((nt & n) != 0)
                def _():
                    update(pl.multiple_of(done, n), n, first)
                done = done + (nt & n)
                n //= 2

        @pl.when(f == 0)
        def _():
            all_tiles(True)
            for j in range(UT):
                @pl.when(j >= nt)
                def _():
                    wait_prev_output(j)

        @pl.when(f > 0)
        def _():
            all_tiles(False)

        for j in range(UT):
            @pl.when(prefetch(j))
            def _():
                x_copy(r0_ref[un], j, 1 - slot).wait()

        @pl.when(jnp.logical_and(f == NF - 1, u == na - 1))
        def _():
            zero_ref[...] = jnp.zeros_like(zero_ref)
            for act in (lambda c: c.start(), lambda c: c.wait()):
                for j in range(n_tail):
                    row = na_ref[1] + j * TM

                    @pl.when(row < o_hbm.shape[0])
                    def _():
                        act(pltpu.make_async_copy(zero_ref, o_hbm.at[pl.ds(pl.multiple_of(row, TM), TM)], zsem))
            wait_outputs(u)


def moe_experts(x_buf, plan, n_assign, w_gate_up, b_gate_up, w_down, b_down, name="moe_experts"):
    R, H = x_buf.shape
    D = 2 * H
    n_tail = (R - n_assign) // MOE_TM
    E, _, F2 = w_gate_up.shape
    DF = F2 // 2
    TM, UT, TF = MOE_TM, MOE_UNIT_TILES, MOE_TF
    NF = DF // TF
    ue, r0, nt, na = plan
    U = ue.shape[0]

    def fsel(u, f, na_ref):
        return jnp.where(u < na_ref[0], f, NF - 1)

    kern = functools.partial(_experts_kernel, TM=TM, UT=UT, NF=NF, n_tail=n_tail)
    return pl.pallas_call(
        kern,
        grid_spec=pltpu.PrefetchScalarGridSpec(
            num_scalar_prefetch=4,
            grid=(U, NF),
            in_specs=[
                pl.BlockSpec(memory_space=pl.ANY),
                pl.BlockSpec((None, D, TF), lambda u, f, ue, r0, nt, na: (ue[u], 0, fsel(u, f, na))),
                pl.BlockSpec((None, D, TF), lambda u, f, ue, r0, nt, na: (ue[u], 0, NF + fsel(u, f, na))),
                pl.BlockSpec((None, 1, TF), lambda u, f, ue, r0, nt, na: (ue[u], 0, fsel(u, f, na))),
                pl.BlockSpec((None, 1, TF), lambda u, f, ue, r0, nt, na: (ue[u], 0, NF + fsel(u, f, na))),
                pl.BlockSpec((None, TF, D), lambda u, f, ue, r0, nt, na: (ue[u], fsel(u, f, na), 0)),
                pl.BlockSpec((None, 1, D), lambda u, f, ue, r0, nt, na: (ue[u], 0, 0)),
            ],
            out_specs=pl.BlockSpec(memory_space=pl.ANY),
            scratch_shapes=[
                pltpu.VMEM((2, UT, TM, H), jnp.uint32),
                pltpu.VMEM((UT, TM, D), F32),
                pltpu.VMEM((TM, D), F32),
                pltpu.SemaphoreType.DMA((UT,)),
                pltpu.SemaphoreType.DMA((UT,)),
                pltpu.SemaphoreType.DMA(()),
            ],
        ),
        out_shape=jax.ShapeDtypeStruct((R, D), F32),
        compiler_params=_cparams(("arbitrary", "arbitrary"), VMEM_LIMIT_EXPERTS_BYTES),
        name=name,
    )(ue, r0, nt, na, x_buf, w_gate_up, w_gate_up, b_gate_up.reshape(E, 1, F2), b_gate_up.reshape(E, 1, F2),
      w_down, b_down.reshape(E, 1, D))


def moe_plan(counts, n_tokens):
    TM, UT = MOE_TM, MOE_UNIT_TILES
    E = counts.shape[0]
    nt = (counts + TM - 1) // TM
    padded = nt * TM
    pad_start = jnp.cumsum(padded) - padded
    nu = (nt + UT - 1) // UT
    cu = jnp.cumsum(nu)
    n_units = cu[-1]
    U = E + (n_tokens * TOP_K) // (TM * UT)
    u = jnp.arange(U, dtype=jnp.int32)
    uc = jnp.minimum(u, n_units - 1)
    ue = jnp.clip(jnp.searchsorted(cu, uc, side='right'), 0, E - 1).astype(jnp.int32)
    j = uc - (cu[ue] - nu[ue])
    r0 = (pad_start[ue] + j * (UT * TM)).astype(jnp.int32)
    ntl = jnp.where(u < n_units, jnp.clip(nt[ue] - j * UT, 0, UT), 0).astype(jnp.int32)
    zstart = (pad_start + counts) // 8 * 8
    pad_end = pad_start + padded
    zrows = jnp.concatenate([zstart, pad_end - zstart, pad_end[-1:]]).astype(jnp.int32)
    meta = jnp.stack([n_units, pad_end[-1]]).astype(jnp.int32)
    return pad_start, zrows, (ue, r0, ntl, meta)


def _combine_kernel(dest_ref, h_ref, route_ref, nw_ref, o_hbm, h2_ref, xn_ref, g_ref, sem, *, tm):
    i = pl.program_id(0)
    par = i % 2

    def gather_tile(t, slot):
        def start(rb, c):
            for j in range(ROW_UNROLL):
                r = rb * ROW_UNROLL + j
                for k in range(TOP_K):
                    d = dest_ref[(t * tm + r) * TOP_K + k]
                    pltpu.make_async_copy(o_hbm.at[pl.ds(d, 1)], g_ref.at[slot, k, pl.ds(r, 1)],
                                          sem.at[slot]).start(priority=k % 2)
            return c

        lax.fori_loop(0, tm // ROW_UNROLL, start, 0)

    @pl.when(i == 0)
    def _():
        gather_tile(i, par)

    @pl.when(i + 1 < pl.num_programs(0))
    def _():
        gather_tile(i + 1, 1 - par)

    for k in range(TOP_K):
        pltpu.make_async_copy(o_hbm.at[pl.ds(0, tm)], g_ref.at[par, k], sem.at[par]).wait()

    route = route_ref[...]
    y = g_ref[par, 0] * route[:, 2 * TOP_K:2 * TOP_K + 1]
    for k in range(1, TOP_K):
        y = y + g_ref[par, k] * route[:, 2 * TOP_K + k:2 * TOP_K + k + 1]
    h2 = h_ref[...] + y
    h2_ref[...] = h2
    ms = jnp.mean(h2 * h2, axis=-1, keepdims=True)
    xn_ref[...] = (h2 * lax.rsqrt(ms + EPS) * nw_ref[...]).astype(xn_ref.dtype)


def moe_combine(h, route, dest_flat, out_buf, norm_w, tm=256, name="moe_combine"):
    T, D = h.shape
    kern = functools.partial(_combine_kernel, tm=tm)
    return pl.pallas_call(
        kern,
        grid_spec=pltpu.PrefetchScalarGridSpec(
            num_scalar_prefetch=1,
            grid=(T // tm,),
            in_specs=[pl.BlockSpec((tm, D), lambda i, d: (i, 0)),
                      pl.BlockSpec((tm, LANES), lambda i, d: (i, 0)),
                      pl.BlockSpec((1, D), lambda i, d: (0, 0)),
                      pl.BlockSpec(memory_space=pl.ANY)],
            out_specs=[pl.BlockSpec((tm, D), lambda i, d: (i, 0)),
                       pl.BlockSpec((tm, D), lambda i, d: (i, 0))],
            scratch_shapes=[pltpu.VMEM((2, TOP_K, tm, D), F32), pltpu.SemaphoreType.DMA((2,))],
        ),
        out_shape=[jax.ShapeDtypeStruct((T, D), F32), jax.ShapeDtypeStruct((T, D), BF16)],
        compiler_params=_cparams(("arbitrary",)),
        name=name,
    )(dest_flat, h, route, norm_w.reshape(1, D), out_buf)


def _ple_kernel(xn_ref, p1_ref, p2_ref, h_ref, wg_hbm, wp_hbm, nf_ref, oa_ref, ob_ref, wgb_ref, wpb_ref, stage_ref, sem,
                *, nfirst):
    i = pl.program_id(0)

    @pl.when(i == 0)
    def _():
        _load_weight_bf16(wg_hbm, wgb_ref, stage_ref, sem)
        _load_weight_bf16(wp_hbm, wpb_ref, stage_ref, sem)

    gate = jax.nn.sigmoid(_dot(xn_ref[...], wgb_ref[...]))
    p = _pick(i, nfirst, p1_ref, p2_ref).astype(BF16)
    y = _rms(h_ref[...] + _dot(p, wpb_ref[...]) * gate, nf_ref[...])

    @pl.when(i < nfirst)
    def _():
        oa_ref[...] = y

    @pl.when(i >= nfirst)
    def _():
        ob_ref[...] = y


def ple_final(xn, p1, p2, h, w_gate, w_ple, norm_w, tm=512, name="ple_final"):
    T, D = h.shape
    Ta, P = p1.shape
    na = Ta // tm
    return pl.pallas_call(
        functools.partial(_ple_kernel, nfirst=na),
        grid=(T // tm,),
        in_specs=[
            pl.BlockSpec((tm, D), lambda i: (i, 0)),
            *_cat_specs((tm, P), na),
            pl.BlockSpec((tm, D), lambda i: (i, 0)),
            pl.BlockSpec(memory_space=pl.ANY),
            pl.BlockSpec(memory_space=pl.ANY),
            pl.BlockSpec((1, D), lambda i: (0, 0)),
        ],
        out_specs=list(_cat_specs((tm, D), na)),
        out_shape=[jax.ShapeDtypeStruct((Ta, D), F32), jax.ShapeDtypeStruct((T - Ta, D), F32)],
        scratch_shapes=[pltpu.VMEM((D, D), BF16), pltpu.VMEM((P, D), BF16),
                        pltpu.VMEM((2, WEIGHT_STAGE_ROWS, D), F32), pltpu.SemaphoreType.DMA((2,))],
        compiler_params=_cparams(("arbitrary",)),
        name=name,
    )(xn, p1, p2, h, w_gate, w_ple, norm_w.reshape(1, D))


def kernel(x_prompt, x_sample, p_prompt, p_sample, state_hgrn, state_ssm_re, state_ssm_im, norm_mix, w_in, hg_lb, hg_gnorm, w_branch_a, ssm_a_re, ssm_a_im, ssm_log_dt, ssm_b_re, ssm_b_im, ssm_c_re, ssm_c_im, ssm_d, w_glu, w_branch_b, w_out, norm_moe, w_router, b_router, w_gate_up, b_gate_up, w_down, b_down, norm_ple, w_ple, w_ple_gate, norm_final):
    BP, LP, D = x_prompt.shape
    BS, LS, _ = x_sample.shape
    depth = w_in.shape[0]
    assert depth == 1
    TP, TS = BP * LP, BS * LS
    T = TP + TS
    G, P = ssm_a_re.shape[1:]
    W = SSM_GROUP * G

    xp = x_prompt.reshape(TP, D)
    xs = x_sample.reshape(TS, D)

    xn = rmsnorm_cat(xp, xs, norm_mix[0], BF16, name="norm_mix")
    tm = 1024
    w_in0 = w_in[0]
    z_hg = project(xn, w_in0, 0, T // tm, 0, 4, tm, 1024, "proj_hgrn")
    u_p = project(xn, w_in0, 0, TP // tm, 4, 1, tm, 1024, "proj_u_prompt")
    u_s = project(xn, w_in0, TP // tm, TS // tm, 4, 1, tm, 1024, "proj_u_sample")
    z_gate = project(xn, w_in0, 0, T // tm, 5, 4, tm, 1024, "proj_gates", out_dtype=BF16)

    lb = jax.nn.softmax(hg_lb.astype(F32), axis=0)[0]
    og_p, hg_p = hgrn2(z_hg, lb, hg_gnorm[0], None, row0=0, B=BP, L=LP, C=HG_CHUNK, nb=1, tok_step=512,
                       name="hgrn_prompt")
    og_s, hg_s = hgrn2(z_hg, lb, hg_gnorm[0], state_hgrn[0], row0=TP, B=BS, L=LS, C=LS, nb=8, tok_step=LS,
                       name="hgrn_sample")

    bd, cd, ab = s5_params(ssm_a_re[0], ssm_a_im[0], ssm_log_dt[0], ssm_b_re[0], ssm_b_im[0],
                           ssm_c_re[0], ssm_c_im[0])
    y_p, re_p, im_p = s5_prompt(u_p.reshape(BP, LP, W), bd, cd, ab, ssm_d[0], TC=512, name="s5_prompt")
    y_s, re_s, im_s = s5_sample(u_s, bd, cd, ab, ssm_d[0], state_ssm_re[0].reshape(BS, G * P),
                                state_ssm_im[0].reshape(BS, G * P), B=BS, L=LS, name="s5_sample")

    glu = gelu_glu(y_p.reshape(TP, W), y_s, w_glu[0])
    merged = gated_merge(og_p, og_s, glu, z_gate, w_branch_a[0], w_branch_b[0])

    h1, xn2, route, cnt = out_proj_router(merged, xp, xs, w_out[0], norm_moe[0], w_router[0], b_router[0])
    e_idx = route[:, 0:TOP_K].astype(jnp.int32)
    rank = route[:, TOP_K:2 * TOP_K].astype(jnp.int32)
    counts = cnt[0, :N_EXPERTS].astype(jnp.int32)
    pad_start, zrows, plan = moe_plan(counts, T)
    onehot = e_idx[..., None] == jnp.arange(N_EXPERTS, dtype=jnp.int32)
    dest = (jnp.sum(jnp.where(onehot, pad_start, 0), axis=-1) + rank).astype(jnp.int32).reshape(T * TOP_K)
    n_rows = T * TOP_K + N_EXPERTS * MOE_TM
    x_buf = moe_dispatch(xn2, dest, zrows, n_rows)
    out_buf = moe_experts(x_buf, plan, T * TOP_K, w_gate_up[0], b_gate_up[0], w_down[0], b_down[0])
    h2, xn3 = moe_combine(h1, route, dest, out_buf, norm_ple[0])

    y_p2, y_s2 = ple_final(xn3, p_prompt[0].reshape(TP, -1), p_sample[0].reshape(TS, -1), h2, w_ple_gate[0],
                           w_ple[0], norm_final)

    y_prompt = y_p2.reshape(BP, LP, D)
    y_sample = y_s2.reshape(BS, LS, D)
    hs = (1, BP, HG_HEADS, HG_DK, HG_DV)
    return (y_prompt, y_sample,
            hg_p.reshape(hs), re_p.reshape(1, BP, G, P), im_p.reshape(1, BP, G, P),
            hg_s.reshape(1, BS, HG_HEADS, HG_DK, HG_DV), re_s.reshape(1, BS, G, P), im_s.reshape(1, BS, G, P))
```
